```python
import jax, jax.numpy as jnp
from jax import lax
import numpy as np

D_MODEL = 2048
BATCH = 8
SEQ = 8192
DEPTH = 4

GRID_W = 64
EPS = 1e-6
D_SSM = D_MODEL
SSM_HEAD_DIM = 64
SSM_HEADS = D_SSM // SSM_HEAD_DIM
SSM_GROUPS = 8
SSM_STATE = 128
SSM_CONV = 5
SSM_CHUNK = 128
SSM_CONV_CH = D_SSM + 2 * SSM_GROUPS * SSM_STATE
D_NA = D_MODEL
NA_HEAD_DIM = 128
NA_HEADS = D_NA // NA_HEAD_DIM
NA_WIN_ROWS = 8
NA_WIN_COLS = 16
D_CONV = 2 * D_MODEL
CONV_WIDTH = 31
PLE_DIM = 256

N_EVEN = (DEPTH + 1) // 2
N_ODD = DEPTH // 2
EVEN_SPLITS = [D_SSM, SSM_CONV_CH, 2 * SSM_HEADS, D_NA, D_NA, D_NA, D_NA]
EVEN_IN = sum(EVEN_SPLITS)
ODD_IN = 3 * D_CONV

kernel_name = "hybrid_ssd_natten_conformer_encoder"


def rmsnorm(x, w, eps=EPS):
    xf = x.astype(jnp.float32)
    y = xf * lax.rsqrt(jnp.mean(xf * xf, axis=-1, keepdims=True) + eps)
    return (y * w.astype(jnp.float32)).astype(x.dtype)


def layernorm(x, w, b, eps=EPS):
    xf = x.astype(jnp.float32)
    mu = jnp.mean(xf, axis=-1, keepdims=True)
    xc = xf - mu
    y = xc * lax.rsqrt(jnp.mean(xc * xc, axis=-1, keepdims=True) + eps)
    return (y * w.astype(jnp.float32) + b.astype(jnp.float32)).astype(x.dtype)


def depthwise_conv(x, w, b):
    k = w.shape[0]
    y = lax.conv_general_dilated(
        x, w[:, None, :].astype(x.dtype), window_strides=(1,),
        padding=[(k // 2, k - 1 - k // 2)],
        dimension_numbers=("NWC", "WIO", "NWC"),
        feature_group_count=x.shape[-1])
    return y + b.astype(x.dtype)


def segsum(a):
    t = a.shape[-1]
    a_rep = jnp.broadcast_to(a[..., :, None], a.shape + (t,))
    strict = jnp.tril(jnp.ones((t, t), bool), -1)
    cs = jnp.cumsum(jnp.where(strict, a_rep, 0.0), axis=-2)
    return jnp.where(jnp.tril(jnp.ones((t, t), bool)), cs, -jnp.inf)


def ssd_chunked(x, dt, a, bm, cm):
    b, l, h, p = x.shape
    g, n = bm.shape[2], bm.shape[3]
    j = h // g
    q = SSM_CHUNK
    c = l // q
    xdt = (x * dt[..., None]).reshape(b, c, q, g, j, p)
    da = (dt * a).reshape(b, c, q, g, j)
    a_cs = jnp.cumsum(da, axis=2)
    bc = bm.reshape(b, c, q, g, n)
    cc = cm.reshape(b, c, q, g, n)
    decay_in = jnp.exp(segsum(da.transpose(0, 3, 4, 1, 2)))
    scores = jnp.einsum("bclgn,bcsgn->bgcls", cc, bc)[:, :, None] * decay_in
    y_diag = jnp.einsum("bgjcls,bcsgjp->bclgjp", scores, xdt)
    decay_states = jnp.exp(a_cs[:, :, -1:] - a_cs)
    states = jnp.einsum("bcsgn,bcsgjp->cbgjpn", bc, xdt * decay_states[..., None])
    chunk_decay = jnp.exp(jnp.moveaxis(a_cs[:, :, -1], 1, 0))

    def step(carry, inp):
        st, dec = inp
        return carry * dec[..., None, None] + st, carry

    _, prev = lax.scan(step, jnp.zeros(states.shape[1:], states.dtype), (states, chunk_decay))
    y_off = jnp.einsum("bclgn,cbgjpn->bclgjp", cc, prev) * jnp.exp(a_cs)[..., None]
    return (y_diag + y_off).reshape(b, l, h, p)


def ssd_branch(z, xbc, dt_raw, conv_w, conv_b, dt_bias_f, dt_bias_b, a_log_f, a_log_b, d_skip, gnorm_w):
    b, l, _ = z.shape
    xbc = jax.nn.silu(depthwise_conv(xbc, conv_w, conv_b)).astype(jnp.float32)
    xs, bm, cm = jnp.split(xbc, [D_SSM, D_SSM + SSM_GROUPS * SSM_STATE], axis=-1)
    xs = xs.reshape(b, l, SSM_HEADS, SSM_HEAD_DIM)
    bm = bm.reshape(b, l, SSM_GROUPS, SSM_STATE)
    cm = cm.reshape(b, l, SSM_GROUPS, SSM_STATE)
    dt_f, dt_b = jnp.split(dt_raw.astype(jnp.float32), 2, axis=-1)
    dt_f = jax.nn.softplus(dt_f + dt_bias_f.astype(jnp.float32))
    dt_b = jax.nn.softplus(dt_b + dt_bias_b.astype(jnp.float32))
    a_f = -jnp.exp(a_log_f.astype(jnp.float32))
    a_b = -jnp.exp(a_log_b.astype(jnp.float32))
    flip = lambda t: jnp.flip(t, axis=1)
    y_f = ssd_chunked(xs, dt_f, a_f, bm, cm)
    y_b = flip(ssd_chunked(flip(xs), flip(dt_b), a_b, flip(bm), flip(cm)))
    y = y_f + y_b + d_skip.astype(jnp.float32)[:, None] * xs
    y = y.reshape(b, l, D_SSM) * jax.nn.silu(z.astype(jnp.float32))
    yg = y.reshape(b, l, SSM_GROUPS, D_SSM // SSM_GROUPS)
    yg = yg * lax.rsqrt(jnp.mean(yg * yg, axis=-1, keepdims=True) + EPS)
    return (yg.reshape(b, l, D_SSM) * gnorm_w.astype(jnp.float32)).astype(z.dtype)


def neighbourhood_attention(q, k, v, rpb):
    b, s_len, _ = q.shape
    rows = s_len // GRID_W
    kr = min(NA_WIN_ROWS, rows)
    grid = (b, rows, GRID_W, NA_HEADS, NA_HEAD_DIM)
    qg, kg, vg = q.reshape(grid), k.reshape(grid), v.reshape(grid)
    cols = jnp.arange(GRID_W)
    col_start = jnp.clip(cols - NA_WIN_COLS // 2, 0, GRID_W - NA_WIN_COLS)
    col_valid = (cols[None, :] >= col_start[:, None]) & (cols[None, :] < col_start[:, None] + NA_WIN_COLS)
    col_off = jnp.clip(cols[None, :] - cols[:, None], -(NA_WIN_COLS - 1), NA_WIN_COLS - 1) + NA_WIN_COLS - 1
    rpb_cols = rpb.astype(jnp.float32)[:, :, col_off]
    scale = NA_HEAD_DIM ** -0.5

    def row_attend(args):
        q_row, r = args
        rs = jnp.clip(r - kr // 2, 0, rows - kr)
        k_rows = lax.dynamic_slice_in_dim(kg, rs, kr, axis=1).astype(jnp.float32)
        v_rows = lax.dynamic_slice_in_dim(vg, rs, kr, axis=1).astype(jnp.float32)
        s = jnp.einsum("bqhd,bikhd->bhqik", q_row.astype(jnp.float32), k_rows) * scale
        row_off = rs + jnp.arange(kr) - r + NA_WIN_ROWS - 1
        bias = jnp.take(rpb_cols, row_off, axis=1).transpose(0, 2, 1, 3)
        s = jnp.where(col_valid[None, None, :, None, :], s + bias[None], -jnp.inf)
        pr = jax.nn.softmax(s.reshape(b, NA_HEADS, GRID_W, kr * GRID_W), axis=-1).reshape(s.shape)
        return jnp.einsum("bhqik,bikhd->bqhd", pr, v_rows).astype(q_row.dtype)

    out = lax.map(row_attend, (jnp.moveaxis(qg, 1, 0), jnp.arange(rows)))
    return jnp.moveaxis(out, 0, 1).reshape(b, s_len, D_NA)


def even_mixer(hn, w_in, conv_w, conv_b, dt_bias_f, dt_bias_b, a_log_f, a_log_b, d_skip, gnorm_w, rpb, w_out):
    u = hn @ w_in
    z, xbc, dt_raw, q, k, v, g = jnp.split(u, np.cumsum(EVEN_SPLITS)[:-1].tolist(), axis=-1)
    y_ssd = ssd_branch(z, xbc, dt_raw, conv_w, conv_b, dt_bias_f, dt_bias_b, a_log_f, a_log_b, d_skip, gnorm_w)
    y_na = neighbourhood_attention(q, k, v, rpb) * jax.nn.silu(g)
    return jnp.concatenate([y_ssd, y_na], axis=-1) @ w_out


def conv_module(hn, w_in, dw_w, dw_b, ln_w, ln_b, w_out):
    a, a_gate, g = jnp.split(hn @ w_in, 3, axis=-1)
    v = a * jax.nn.sigmoid(a_gate)
    v = depthwise_conv(v, dw_w, dw_b)
    v = layernorm(v, ln_w, ln_b)
    v = jax.nn.silu(v) * jax.nn.silu(g)
    return v @ w_out


def per_layer_embedding(h, p_i, norm_w, w_gate, w_proj):
    gate = jax.nn.sigmoid(rmsnorm(h, norm_w) @ w_gate)
    return gate * (p_i @ w_proj)


def _fwd_setup_inputs(seed: int = 0) -> dict:
    key = jax.random.key(seed)
    ks = iter(jax.random.split(key, 32))
    nrm = lambda shape, scale: jax.random.normal(next(ks), shape, jnp.float32) * scale
    gain = lambda shape: 1.0 + nrm(shape, 0.02)
    dt = jnp.exp(jax.random.uniform(next(ks), (2, N_EVEN, SSM_HEADS), jnp.float32,
                                    jnp.log(0.001), jnp.log(0.1)))
    dt_bias = dt + jnp.log(-jnp.expm1(-dt))
    a_log = jnp.log(jax.random.uniform(next(ks), (2, N_EVEN, SSM_HEADS), jnp.float32, 1.0, 16.0))
    return {
        "x": nrm((BATCH, SEQ, D_MODEL), 1.0),
        "p": nrm((DEPTH, BATCH, SEQ, PLE_DIM), 1.0),
        "ev_norm_w": gain((N_EVEN, D_MODEL)),
        "ev_w_in": nrm((N_EVEN, D_MODEL, EVEN_IN), D_MODEL ** -0.5),
        "ev_conv_w": nrm((N_EVEN, SSM_CONV, SSM_CONV_CH), SSM_CONV ** -0.5),
        "ev_conv_b": nrm((N_EVEN, SSM_CONV_CH), 0.02),
        "ev_dt_bias_f": dt_bias[0],
        "ev_dt_bias_b": dt_bias[1],
        "ev_a_log_f": a_log[0],
        "ev_a_log_b": a_log[1],
        "ev_d_skip": gain((N_EVEN, SSM_HEADS)),
        "ev_gnorm_w": gain((N_EVEN, D_SSM)),
        "ev_rpb": nrm((N_EVEN, NA_HEADS, 2 * NA_WIN_ROWS - 1, 2 * NA_WIN_COLS - 1), 0.1),
        "ev_w_out": nrm((N_EVEN, D_SSM + D_NA, D_MODEL), (D_SSM + D_NA) ** -0.5),
        "od_norm_w": gain((N_ODD, D_MODEL)),
        "od_w_in": nrm((N_ODD, D_MODEL, ODD_IN), D_MODEL ** -0.5),
        "od_dw_w": nrm((N_ODD, CONV_WIDTH, D_CONV), CONV_WIDTH ** -0.5),
        "od_dw_b": nrm((N_ODD, D_CONV), 0.02),
        "od_ln_w": gain((N_ODD, D_CONV)),
        "od_ln_b": nrm((N_ODD, D_CONV), 0.02),
        "od_w_out": nrm((N_ODD, D_CONV, D_MODEL), D_CONV ** -0.5),
        "ple_norm_w": gain((DEPTH, D_MODEL)),
        "ple_w_gate": nrm((DEPTH, D_MODEL, D_MODEL), D_MODEL ** -0.5),
        "ple_w_proj": nrm((DEPTH, PLE_DIM, D_MODEL), PLE_DIM ** -0.5),
        "final_norm_w": gain((D_MODEL,)),
    }


def _fwd_reference(x, p, ev_norm_w, ev_w_in, ev_conv_w, ev_conv_b, ev_dt_bias_f, ev_dt_bias_b,
              ev_a_log_f, ev_a_log_b, ev_d_skip, ev_gnorm_w, ev_rpb, ev_w_out,
              od_norm_w, od_w_in, od_dw_w, od_dw_b, od_ln_w, od_ln_b, od_w_out,
              ple_norm_w, ple_w_gate, ple_w_proj, final_norm_w):
    h = x
    for i in range(DEPTH):
        e = i // 2
        if i % 2 == 0:
            h = h + even_mixer(rmsnorm(h, ev_norm_w[e]), ev_w_in[e], ev_conv_w[e], ev_conv_b[e],
                               ev_dt_bias_f[e], ev_dt_bias_b[e], ev_a_log_f[e], ev_a_log_b[e],
                               ev_d_skip[e], ev_gnorm_w[e], ev_rpb[e], ev_w_out[e])
        else:
            h = h + conv_module(rmsnorm(h, od_norm_w[e]), od_w_in[e], od_dw_w[e], od_dw_b[e],
                                od_ln_w[e], od_ln_b[e], od_w_out[e])
        h = h + per_layer_embedding(h, p[i], ple_norm_w[i], ple_w_gate[i], ple_w_proj[i])
    return rmsnorm(h, final_norm_w)


import jax as _jax
import jax.numpy as _jnp

TWIN_FORMAT = 'train_step'
FWD_PARAMS = ['x', 'p', 'ev_norm_w', 'ev_w_in', 'ev_conv_w', 'ev_conv_b', 'ev_dt_bias_f', 'ev_dt_bias_b', 'ev_a_log_f', 'ev_a_log_b', 'ev_d_skip', 'ev_gnorm_w', 'ev_rpb', 'ev_w_out', 'od_norm_w', 'od_w_in', 'od_dw_w', 'od_dw_b', 'od_ln_w', 'od_ln_b', 'od_w_out', 'ple_norm_w', 'ple_w_gate', 'ple_w_proj', 'final_norm_w']
TWIN_WEIGHTS = ['ev_norm_w', 'ev_w_in', 'ev_conv_w', 'ev_conv_b', 'ev_dt_bias_f', 'ev_dt_bias_b', 'ev_a_log_f', 'ev_a_log_b', 'ev_d_skip', 'ev_gnorm_w', 'ev_rpb', 'ev_w_out', 'od_norm_w', 'od_w_in', 'od_dw_w', 'od_dw_b', 'od_ln_w', 'od_ln_b', 'od_w_out', 'ple_norm_w', 'ple_w_gate', 'ple_w_proj', 'final_norm_w']
TWIN_DIFF_INPUT = 'x'
TWIN_INPUTS = ['x', 'p', 'ev_norm_w', 'ev_w_in', 'ev_conv_w', 'ev_conv_b', 'ev_dt_bias_f', 'ev_dt_bias_b', 'ev_a_log_f', 'ev_a_log_b', 'ev_d_skip', 'ev_gnorm_w', 'ev_rpb', 'ev_w_out', 'od_norm_w', 'od_w_in', 'od_dw_w', 'od_dw_b', 'od_ln_w', 'od_ln_b', 'od_w_out', 'ple_norm_w', 'ple_w_gate', 'ple_w_proj', 'final_norm_w', 'loss_target', 'm_ev_norm_w', 'm_ev_w_in', 'm_ev_conv_w', 'm_ev_conv_b', 'm_ev_dt_bias_f', 'm_ev_dt_bias_b', 'm_ev_a_log_f', 'm_ev_a_log_b', 'm_ev_d_skip', 'm_ev_gnorm_w', 'm_ev_rpb', 'm_ev_w_out', 'm_od_norm_w', 'm_od_w_in', 'm_od_dw_w', 'm_od_dw_b', 'm_od_ln_w', 'm_od_ln_b', 'm_od_w_out', 'm_ple_norm_w', 'm_ple_w_gate', 'm_ple_w_proj', 'm_final_norm_w', 'v_ev_norm_w', 'v_ev_w_in', 'v_ev_conv_w', 'v_ev_conv_b', 'v_ev_dt_bias_f', 'v_ev_dt_bias_b', 'v_ev_a_log_f', 'v_ev_a_log_b', 'v_ev_d_skip', 'v_ev_gnorm_w', 'v_ev_rpb', 'v_ev_w_out', 'v_od_norm_w', 'v_od_w_in', 'v_od_dw_w', 'v_od_dw_b', 'v_od_ln_w', 'v_od_ln_b', 'v_od_w_out', 'v_ple_norm_w', 'v_ple_w_gate', 'v_ple_w_proj', 'v_final_norm_w']
TWIN_OUTPUTS = ['loss', 'grad_x', 'grad_ev_norm_w', 'grad_ev_w_in', 'grad_ev_conv_w', 'grad_ev_conv_b', 'grad_ev_dt_bias_f', 'grad_ev_dt_bias_b', 'grad_ev_a_log_f', 'grad_ev_a_log_b', 'grad_ev_d_skip', 'grad_ev_gnorm_w', 'grad_ev_rpb', 'grad_ev_w_out', 'grad_od_norm_w', 'grad_od_w_in', 'grad_od_dw_w', 'grad_od_dw_b', 'grad_od_ln_w', 'grad_od_ln_b', 'grad_od_w_out', 'grad_ple_norm_w', 'grad_ple_w_gate', 'grad_ple_w_proj', 'grad_final_norm_w', 'delta_ev_norm_w', 'delta_ev_w_in', 'delta_ev_conv_w', 'delta_ev_conv_b', 'delta_ev_dt_bias_f', 'delta_ev_dt_bias_b', 'delta_ev_a_log_f', 'delta_ev_a_log_b', 'delta_ev_d_skip', 'delta_ev_gnorm_w', 'delta_ev_rpb', 'delta_ev_w_out', 'delta_od_norm_w', 'delta_od_w_in', 'delta_od_dw_w', 'delta_od_dw_b', 'delta_od_ln_w', 'delta_od_ln_b', 'delta_od_w_out', 'delta_ple_norm_w', 'delta_ple_w_gate', 'delta_ple_w_proj', 'delta_final_norm_w', 'new_m_ev_norm_w', 'new_m_ev_w_in', 'new_m_ev_conv_w', 'new_m_ev_conv_b', 'new_m_ev_dt_bias_f', 'new_m_ev_dt_bias_b', 'new_m_ev_a_log_f', 'new_m_ev_a_log_b', 'new_m_ev_d_skip', 'new_m_ev_gnorm_w', 'new_m_ev_rpb', 'new_m_ev_w_out', 'new_m_od_norm_w', 'new_m_od_w_in', 'new_m_od_dw_w', 'new_m_od_dw_b', 'new_m_od_ln_w', 'new_m_od_ln_b', 'new_m_od_w_out', 'new_m_ple_norm_w', 'new_m_ple_w_gate', 'new_m_ple_w_proj', 'new_m_final_norm_w', 'new_v_ev_norm_w', 'new_v_ev_w_in', 'new_v_ev_conv_w', 'new_v_ev_conv_b', 'new_v_ev_dt_bias_f', 'new_v_ev_dt_bias_b', 'new_v_ev_a_log_f', 'new_v_ev_a_log_b', 'new_v_ev_d_skip', 'new_v_ev_gnorm_w', 'new_v_ev_rpb', 'new_v_ev_w_out', 'new_v_od_norm_w', 'new_v_od_w_in', 'new_v_od_dw_w', 'new_v_od_dw_b', 'new_v_od_ln_w', 'new_v_od_ln_b', 'new_v_od_w_out', 'new_v_ple_norm_w', 'new_v_ple_w_gate', 'new_v_ple_w_proj', 'new_v_final_norm_w']
TWIN_LEAF_KINDS = {'loss': 'loss', 'grad_x': 'grad_x', 'grad_ev_norm_w': 'grad_w', 'grad_ev_w_in': 'grad_w', 'grad_ev_conv_w': 'grad_w', 'grad_ev_conv_b': 'grad_w', 'grad_ev_dt_bias_f': 'grad_w', 'grad_ev_dt_bias_b': 'grad_w', 'grad_ev_a_log_f': 'grad_w', 'grad_ev_a_log_b': 'grad_w', 'grad_ev_d_skip': 'grad_w', 'grad_ev_gnorm_w': 'grad_w', 'grad_ev_rpb': 'grad_w', 'grad_ev_w_out': 'grad_w', 'grad_od_norm_w': 'grad_w', 'grad_od_w_in': 'grad_w', 'grad_od_dw_w': 'grad_w', 'grad_od_dw_b': 'grad_w', 'grad_od_ln_w': 'grad_w', 'grad_od_ln_b': 'grad_w', 'grad_od_w_out': 'grad_w', 'grad_ple_norm_w': 'grad_w', 'grad_ple_w_gate': 'grad_w', 'grad_ple_w_proj': 'grad_w', 'grad_final_norm_w': 'grad_w', 'delta_ev_norm_w': 'delta_w', 'delta_ev_w_in': 'delta_w', 'delta_ev_conv_w': 'delta_w', 'delta_ev_conv_b': 'delta_w', 'delta_ev_dt_bias_f': 'delta_w', 'delta_ev_dt_bias_b': 'delta_w', 'delta_ev_a_log_f': 'delta_w', 'delta_ev_a_log_b': 'delta_w', 'delta_ev_d_skip': 'delta_w', 'delta_ev_gnorm_w': 'delta_w', 'delta_ev_rpb': 'delta_w', 'delta_ev_w_out': 'delta_w', 'delta_od_norm_w': 'delta_w', 'delta_od_w_in': 'delta_w', 'delta_od_dw_w': 'delta_w', 'delta_od_dw_b': 'delta_w', 'delta_od_ln_w': 'delta_w', 'delta_od_ln_b': 'delta_w', 'delta_od_w_out': 'delta_w', 'delta_ple_norm_w': 'delta_w', 'delta_ple_w_gate': 'delta_w', 'delta_ple_w_proj': 'delta_w', 'delta_final_norm_w': 'delta_w', 'new_m_ev_norm_w': 'new_m', 'new_m_ev_w_in': 'new_m', 'new_m_ev_conv_w': 'new_m', 'new_m_ev_conv_b': 'new_m', 'new_m_ev_dt_bias_f': 'new_m', 'new_m_ev_dt_bias_b': 'new_m', 'new_m_ev_a_log_f': 'new_m', 'new_m_ev_a_log_b': 'new_m', 'new_m_ev_d_skip': 'new_m', 'new_m_ev_gnorm_w': 'new_m', 'new_m_ev_rpb': 'new_m', 'new_m_ev_w_out': 'new_m', 'new_m_od_norm_w': 'new_m', 'new_m_od_w_in': 'new_m', 'new_m_od_dw_w': 'new_m', 'new_m_od_dw_b': 'new_m', 'new_m_od_ln_w': 'new_m', 'new_m_od_ln_b': 'new_m', 'new_m_od_w_out': 'new_m', 'new_m_ple_norm_w': 'new_m', 'new_m_ple_w_gate': 'new_m', 'new_m_ple_w_proj': 'new_m', 'new_m_final_norm_w': 'new_m', 'new_v_ev_norm_w': 'new_v', 'new_v_ev_w_in': 'new_v', 'new_v_ev_conv_w': 'new_v', 'new_v_ev_conv_b': 'new_v', 'new_v_ev_dt_bias_f': 'new_v', 'new_v_ev_dt_bias_b': 'new_v', 'new_v_ev_a_log_f': 'new_v', 'new_v_ev_a_log_b': 'new_v', 'new_v_ev_d_skip': 'new_v', 'new_v_ev_gnorm_w': 'new_v', 'new_v_ev_rpb': 'new_v', 'new_v_ev_w_out': 'new_v', 'new_v_od_norm_w': 'new_v', 'new_v_od_w_in': 'new_v', 'new_v_od_dw_w': 'new_v', 'new_v_od_dw_b': 'new_v', 'new_v_od_ln_w': 'new_v', 'new_v_od_ln_b': 'new_v', 'new_v_od_w_out': 'new_v', 'new_v_ple_norm_w': 'new_v', 'new_v_ple_w_gate': 'new_v', 'new_v_ple_w_proj': 'new_v', 'new_v_final_norm_w': 'new_v'}


def _forward(args):
    return _fwd_reference(*[args[k] for k in FWD_PARAMS])


def _output_shape():
    def fwd():
        inp = _fwd_setup_inputs(0)
        return _fwd_reference(*[inp[k] for k in FWD_PARAMS])
    out = _jax.eval_shape(fwd)
    return out.shape, out.dtype

N_MICROBATCH = 1
ADAM_LR = 0.001
ADAM_B1 = 0.9
ADAM_B2 = 0.999
ADAM_EPS = 1e-08
ADAM_WD = 0.01
ADAM_STEP = 10
PER_EXAMPLE_BATCH_AXIS = {'x': 0, 'p': 1, 'loss_target': 0}
SHARED_INPUTS = []
_WEIGHT_DTYPES = {'ev_norm_w': _jnp.float32, 'ev_w_in': _jnp.float32, 'ev_conv_w': _jnp.float32, 'ev_conv_b': _jnp.float32, 'ev_dt_bias_f': _jnp.float32, 'ev_dt_bias_b': _jnp.float32, 'ev_a_log_f': _jnp.float32, 'ev_a_log_b': _jnp.float32, 'ev_d_skip': _jnp.float32, 'ev_gnorm_w': _jnp.float32, 'ev_rpb': _jnp.float32, 'ev_w_out': _jnp.float32, 'od_norm_w': _jnp.float32, 'od_w_in': _jnp.float32, 'od_dw_w': _jnp.float32, 'od_dw_b': _jnp.float32, 'od_ln_w': _jnp.float32, 'od_ln_b': _jnp.float32, 'od_w_out': _jnp.float32, 'ple_norm_w': _jnp.float32, 'ple_w_gate': _jnp.float32, 'ple_w_proj': _jnp.float32, 'final_norm_w': _jnp.float32}
MOMENT_SCALE = {'ev_norm_w': 9.233544e-02, 'ev_w_in': 3.495137e-02, 'ev_conv_w': 4.495485e-02, 'ev_conv_b': 7.304451e-02, 'ev_dt_bias_f': 9.575372e-02, 'ev_dt_bias_b': 9.124370e-02, 'ev_a_log_f': 2.210795e-01, 'ev_a_log_b': 1.955598e-01, 'ev_d_skip': 2.457941e-01, 'ev_gnorm_w': 6.473228e-02, 'ev_rpb': 3.529560e-03, 'ev_w_out': 6.140501e-02, 'od_norm_w': 4.403966e-02, 'od_w_in': 1.759862e-02, 'od_dw_w': 2.079470e-02, 'od_dw_b': 3.929200e-02, 'od_ln_w': 2.405552e-02, 'od_ln_b': 2.085127e-02, 'od_w_out': 2.850456e-02, 'ple_norm_w': 1.781553e-02, 'ple_w_gate': 1.742689e-02, 'ple_w_proj': 4.430069e-02, 'final_norm_w': 3.195686e+01}


def _to_microbatches(a, axis):
    t = _jnp.moveaxis(a, axis, 0)
    t = t.reshape((N_MICROBATCH, t.shape[0] // N_MICROBATCH) + t.shape[1:])
    return _jnp.moveaxis(t, 1, axis + 1)


def setup_inputs(seed: int = 0) -> dict:
    inp = _fwd_setup_inputs(seed)
    key = _jax.random.fold_in(_jax.random.key(seed), 7919)
    shape, _ = _output_shape()
    out = dict(inp)
    out["loss_target"] = _jax.random.normal(_jax.random.fold_in(key, 0), shape, _jnp.float32)
    for i, name in enumerate(TWIN_WEIGHTS):
        w = inp[name].astype(_jnp.float32)
        if MOMENT_SCALE is None:
            s = _jnp.sqrt(_jnp.mean(_jnp.square(w)) + 1e-30)
        else:
            s = MOMENT_SCALE[name]
        km, kv = _jax.random.split(_jax.random.fold_in(key, i + 1))
        out[name] = w
        out["m_" + name] = s * _jax.random.normal(km, w.shape, _jnp.float32)
        out["v_" + name] = (s * s) * _jax.random.uniform(kv, w.shape, _jnp.float32, 0.5, 1.5)
    if N_MICROBATCH > 1:
        for name, axis in PER_EXAMPLE_BATCH_AXIS.items():
            out[name] = _to_microbatches(out[name], axis)
    return {'x': out['x'], 'p': out['p'], 'ev_norm_w': out['ev_norm_w'], 'ev_w_in': out['ev_w_in'], 'ev_conv_w': out['ev_conv_w'], 'ev_conv_b': out['ev_conv_b'], 'ev_dt_bias_f': out['ev_dt_bias_f'], 'ev_dt_bias_b': out['ev_dt_bias_b'], 'ev_a_log_f': out['ev_a_log_f'], 'ev_a_log_b': out['ev_a_log_b'], 'ev_d_skip': out['ev_d_skip'], 'ev_gnorm_w': out['ev_gnorm_w'], 'ev_rpb': out['ev_rpb'], 'ev_w_out': out['ev_w_out'], 'od_norm_w': out['od_norm_w'], 'od_w_in': out['od_w_in'], 'od_dw_w': out['od_dw_w'], 'od_dw_b': out['od_dw_b'], 'od_ln_w': out['od_ln_w'], 'od_ln_b': out['od_ln_b'], 'od_w_out': out['od_w_out'], 'ple_norm_w': out['ple_norm_w'], 'ple_w_gate': out['ple_w_gate'], 'ple_w_proj': out['ple_w_proj'], 'final_norm_w': out['final_norm_w'], 'loss_target': out['loss_target'], 'm_ev_norm_w': out['m_ev_norm_w'], 'm_ev_w_in': out['m_ev_w_in'], 'm_ev_conv_w': out['m_ev_conv_w'], 'm_ev_conv_b': out['m_ev_conv_b'], 'm_ev_dt_bias_f': out['m_ev_dt_bias_f'], 'm_ev_dt_bias_b': out['m_ev_dt_bias_b'], 'm_ev_a_log_f': out['m_ev_a_log_f'], 'm_ev_a_log_b': out['m_ev_a_log_b'], 'm_ev_d_skip': out['m_ev_d_skip'], 'm_ev_gnorm_w': out['m_ev_gnorm_w'], 'm_ev_rpb': out['m_ev_rpb'], 'm_ev_w_out': out['m_ev_w_out'], 'm_od_norm_w': out['m_od_norm_w'], 'm_od_w_in': out['m_od_w_in'], 'm_od_dw_w': out['m_od_dw_w'], 'm_od_dw_b': out['m_od_dw_b'], 'm_od_ln_w': out['m_od_ln_w'], 'm_od_ln_b': out['m_od_ln_b'], 'm_od_w_out': out['m_od_w_out'], 'm_ple_norm_w': out['m_ple_norm_w'], 'm_ple_w_gate': out['m_ple_w_gate'], 'm_ple_w_proj': out['m_ple_w_proj'], 'm_final_norm_w': out['m_final_norm_w'], 'v_ev_norm_w': out['v_ev_norm_w'], 'v_ev_w_in': out['v_ev_w_in'], 'v_ev_conv_w': out['v_ev_conv_w'], 'v_ev_conv_b': out['v_ev_conv_b'], 'v_ev_dt_bias_f': out['v_ev_dt_bias_f'], 'v_ev_dt_bias_b': out['v_ev_dt_bias_b'], 'v_ev_a_log_f': out['v_ev_a_log_f'], 'v_ev_a_log_b': out['v_ev_a_log_b'], 'v_ev_d_skip': out['v_ev_d_skip'], 'v_ev_gnorm_w': out['v_ev_gnorm_w'], 'v_ev_rpb': out['v_ev_rpb'], 'v_ev_w_out': out['v_ev_w_out'], 'v_od_norm_w': out['v_od_norm_w'], 'v_od_w_in': out['v_od_w_in'], 'v_od_dw_w': out['v_od_dw_w'], 'v_od_dw_b': out['v_od_dw_b'], 'v_od_ln_w': out['v_od_ln_w'], 'v_od_ln_b': out['v_od_ln_b'], 'v_od_w_out': out['v_od_w_out'], 'v_ple_norm_w': out['v_ple_norm_w'], 'v_ple_w_gate': out['v_ple_w_gate'], 'v_ple_w_proj': out['v_ple_w_proj'], 'v_final_norm_w': out['v_final_norm_w']}


def _loss(weights, diff, rest, loss_target):
    with _jax.named_scope("forward"):
        args = {**rest, TWIN_DIFF_INPUT: diff, **{k: w.astype(_WEIGHT_DTYPES[k]) for k, w in weights.items()}}
        y = _forward(args)
    with _jax.named_scope("loss_head"):
        err = _jnp.square(y.astype(_jnp.float32) - loss_target)
        return 0.5 * _jnp.sum(_jnp.mean(err, axis=-1)) if err.ndim else 0.5 * err


def _adamw(w, g, m, v):
    m = ADAM_B1 * m + (1.0 - ADAM_B1) * g
    v = ADAM_B2 * v + (1.0 - ADAM_B2) * _jnp.square(g)
    m_hat = m / (1.0 - ADAM_B1 ** ADAM_STEP)
    v_hat = v / (1.0 - ADAM_B2 ** ADAM_STEP)
    delta = -ADAM_LR * (m_hat / (_jnp.sqrt(v_hat) + ADAM_EPS) + ADAM_WD * w)
    return delta, m, v


def reference(x, p, ev_norm_w, ev_w_in, ev_conv_w, ev_conv_b, ev_dt_bias_f, ev_dt_bias_b, ev_a_log_f, ev_a_log_b, ev_d_skip, ev_gnorm_w, ev_rpb, ev_w_out, od_norm_w, od_w_in, od_dw_w, od_dw_b, od_ln_w, od_ln_b, od_w_out, ple_norm_w, ple_w_gate, ple_w_proj, final_norm_w, loss_target, m_ev_norm_w, m_ev_w_in, m_ev_conv_w, m_ev_conv_b, m_ev_dt_bias_f, m_ev_dt_bias_b, m_ev_a_log_f, m_ev_a_log_b, m_ev_d_skip, m_ev_gnorm_w, m_ev_rpb, m_ev_w_out, m_od_norm_w, m_od_w_in, m_od_dw_w, m_od_dw_b, m_od_ln_w, m_od_ln_b, m_od_w_out, m_ple_norm_w, m_ple_w_gate, m_ple_w_proj, m_final_norm_w, v_ev_norm_w, v_ev_w_in, v_ev_conv_w, v_ev_conv_b, v_ev_dt_bias_f, v_ev_dt_bias_b, v_ev_a_log_f, v_ev_a_log_b, v_ev_d_skip, v_ev_gnorm_w, v_ev_rpb, v_ev_w_out, v_od_norm_w, v_od_w_in, v_od_dw_w, v_od_dw_b, v_od_ln_w, v_od_ln_b, v_od_w_out, v_ple_norm_w, v_ple_w_gate, v_ple_w_proj, v_final_norm_w):
    given = dict(x=x, p=p, ev_norm_w=ev_norm_w, ev_w_in=ev_w_in, ev_conv_w=ev_conv_w, ev_conv_b=ev_conv_b, ev_dt_bias_f=ev_dt_bias_f, ev_dt_bias_b=ev_dt_bias_b, ev_a_log_f=ev_a_log_f, ev_a_log_b=ev_a_log_b, ev_d_skip=ev_d_skip, ev_gnorm_w=ev_gnorm_w, ev_rpb=ev_rpb, ev_w_out=ev_w_out, od_norm_w=od_norm_w, od_w_in=od_w_in, od_dw_w=od_dw_w, od_dw_b=od_dw_b, od_ln_w=od_ln_w, od_ln_b=od_ln_b, od_w_out=od_w_out, ple_norm_w=ple_norm_w, ple_w_gate=ple_w_gate, ple_w_proj=ple_w_proj, final_norm_w=final_norm_w, loss_target=loss_target, m_ev_norm_w=m_ev_norm_w, m_ev_w_in=m_ev_w_in, m_ev_conv_w=m_ev_conv_w, m_ev_conv_b=m_ev_conv_b, m_ev_dt_bias_f=m_ev_dt_bias_f, m_ev_dt_bias_b=m_ev_dt_bias_b, m_ev_a_log_f=m_ev_a_log_f, m_ev_a_log_b=m_ev_a_log_b, m_ev_d_skip=m_ev_d_skip, m_ev_gnorm_w=m_ev_gnorm_w, m_ev_rpb=m_ev_rpb, m_ev_w_out=m_ev_w_out, m_od_norm_w=m_od_norm_w, m_od_w_in=m_od_w_in, m_od_dw_w=m_od_dw_w, m_od_dw_b=m_od_dw_b, m_od_ln_w=m_od_ln_w, m_od_ln_b=m_od_ln_b, m_od_w_out=m_od_w_out, m_ple_norm_w=m_ple_norm_w, m_ple_w_gate=m_ple_w_gate, m_ple_w_proj=m_ple_w_proj, m_final_norm_w=m_final_norm_w, v_ev_norm_w=v_ev_norm_w, v_ev_w_in=v_ev_w_in, v_ev_conv_w=v_ev_conv_w, v_ev_conv_b=v_ev_conv_b, v_ev_dt_bias_f=v_ev_dt_bias_f, v_ev_dt_bias_b=v_ev_dt_bias_b, v_ev_a_log_f=v_ev_a_log_f, v_ev_a_log_b=v_ev_a_log_b, v_ev_d_skip=v_ev_d_skip, v_ev_gnorm_w=v_ev_gnorm_w, v_ev_rpb=v_ev_rpb, v_ev_w_out=v_ev_w_out, v_od_norm_w=v_od_norm_w, v_od_w_in=v_od_w_in, v_od_dw_w=v_od_dw_w, v_od_dw_b=v_od_dw_b, v_od_ln_w=v_od_ln_w, v_od_ln_b=v_od_ln_b, v_od_w_out=v_od_w_out, v_ple_norm_w=v_ple_norm_w, v_ple_w_gate=v_ple_w_gate, v_ple_w_proj=v_ple_w_proj, v_final_norm_w=v_final_norm_w)
    weights = {n: given[n] for n in TWIN_WEIGHTS}
    shared = {n: given[n] for n in SHARED_INPUTS}
    per_example = {n: given[n] for n in ['x', 'p']}
    grad_fn = _jax.value_and_grad(_loss, argnums=(0, 1))

    def one_microbatch(ex, loss_target):
        ex = dict(ex)
        diff = ex.pop(TWIN_DIFF_INPUT)
        return grad_fn(weights, diff, {**shared, **ex}, loss_target)

    if N_MICROBATCH == 1:
        loss, (grad_w, grad_x) = one_microbatch(per_example, given["loss_target"])
    else:
        def body(carry, xs):
            loss_sum, grad_sum = carry
            l_k, (gw_k, gx_k) = one_microbatch(xs[0], xs[1])
            with _jax.named_scope("update"):
                return (loss_sum + l_k, _jax.tree.map(_jnp.add, grad_sum, gw_k)), gx_k

        init = (_jnp.zeros((), _jnp.float32), _jax.tree.map(_jnp.zeros_like, weights))
        (loss, grad_w), grad_x = _jax.lax.scan(body, init, (per_example, given["loss_target"]))
    with _jax.named_scope("update"):
        delta_w, new_m, new_v = {}, {}, {}
        for n in TWIN_WEIGHTS:
            delta_w[n], new_m[n], new_v[n] = _adamw(weights[n], grad_w[n], given["m_" + n], given["v_" + n])
    return (loss, grad_x, *[grad_w[n] for n in TWIN_WEIGHTS], *[delta_w[n] for n in TWIN_WEIGHTS],
            *[new_m[n] for n in TWIN_WEIGHTS], *[new_v[n] for n in TWIN_WEIGHTS])
```

```python
import functools

import numpy as np
import jax
import jax.numpy as jnp
from jax import lax
from jax.experimental import pallas as pl
from jax.experimental.pallas import tpu as pltpu

f32 = jnp.float32
bf16 = jnp.bfloat16
HIGHEST = lax.Precision.HIGHEST

N_DEV = 8
EPS = 1e-6
GRID_W = 64
HEAD_P = 64
HEADS_PER_GROUP = 4
GROUP_W = HEAD_P * HEADS_PER_GROUP
N_STATE = 128
CHUNK = 128
K_SSM = 5
K_CONV = 31
NA_D = 128
NA_WR = 8
NA_WC = 16
PLE_DIM = 256
LANES = 128
CONV_PAD = 16
VMEM_LIMIT = 56 * 1024 * 1024

ADAM_LR = 0.001
ADAM_B1 = 0.9
ADAM_B2 = 0.999
ADAM_EPS = 1e-08
ADAM_WD = 0.01
ADAM_STEP = 10

NEG_BIG = -1e30


def _params(sem=None):
    if sem is None:
        return pltpu.CompilerParams(vmem_limit_bytes=VMEM_LIMIT)
    return pltpu.CompilerParams(vmem_limit_bytes=VMEM_LIMIT, dimension_semantics=sem)


def _pick(n, prefs):
    for t in prefs:
        if n % t == 0:
            return t
    return n


def _sigmoid(x):
    return jax.nn.sigmoid(x)


def _silu(x):
    return x * jax.nn.sigmoid(x)


_DN = {"nn": (((1,), (0,)), ((), ())), "nt": (((1,), (1,)), ((), ())), "tn": (((0,), (0,)), ((), ()))}


def mm(a, b, mode, name, out_dtype=f32, add=None):
    if mode == "nn":
        (M, K), (K2, N) = a.shape, b.shape
    elif mode == "nt":
        (M, K), (N, K2) = a.shape, b.shape
    else:
        (K, M), (K2, N) = a.shape, b.shape
    assert K == K2, (a.shape, b.shape, mode)
    tm = _pick(M, (512, 256, 128))
    tn = _pick(N, (1024, 512, 256, 128))
    tk = _pick(K, (512, 256, 128))
    nk = K // tk
    dn = _DN[mode]

    def body(a_ref, b_ref, *rest):
        if add is not None:
            add_ref, o_ref, acc = rest
        else:
            o_ref, acc = rest
        k = pl.program_id(2)

        @pl.when(k == 0)
        def _():
            acc[...] = jnp.zeros_like(acc)

        acc[...] += lax.dot_general(a_ref[...].astype(bf16), b_ref[...].astype(bf16), dn, preferred_element_type=f32)

        @pl.when(k == nk - 1)
        def _():
            r = acc[...]
            if add is not None:
                r = r + add_ref[...].astype(f32)
            o_ref[...] = r.astype(out_dtype)

    a_spec = pl.BlockSpec((tk, tm), lambda i, j, k: (k, i)) if mode == "tn" else pl.BlockSpec((tm, tk), lambda i, j, k: (i, k))
    b_spec = pl.BlockSpec((tn, tk), lambda i, j, k: (j, k)) if mode == "nt" else pl.BlockSpec((tk, tn), lambda i, j, k: (k, j))
    o_spec = pl.BlockSpec((tm, tn), lambda i, j, k: (i, j))
    in_specs = [a_spec, b_spec] + ([o_spec] if add is not None else [])
    args = (a, b) + ((add,) if add is not None else ())
    return pl.pallas_call(
        body, name=name, grid=(M // tm, N // tn, nk), in_specs=in_specs, out_specs=o_spec,
        out_shape=jax.ShapeDtypeStruct((M, N), out_dtype), scratch_shapes=[pltpu.VMEM((tm, tn), f32)],
        compiler_params=_params(("parallel", "parallel", "arbitrary")))(*args)


def _row_spec(T, w, base):
    return pl.BlockSpec((T, w), lambda g, i: (i, base + g))


def _const_spec(r, w, base):
    return pl.BlockSpec((r, w), lambda g, i: (0, base + g))


def rowop_fwd(fn, name, rows, consts, outs, T, ncb=1):
    L = rows[0][0].shape[0]
    nr, nc = len(rows), len(consts)

    def body(*refs):
        ins = [r[...].astype(f32) for r in refs[:nr + nc]]
        res = fn(*ins)
        for o_ref, r in zip(refs[nr + nc:], res):
            o_ref[...] = r.astype(o_ref.dtype)

    in_specs = [_row_spec(T, w, b) for (_, w, b) in rows] + [_const_spec(a.shape[0], w, b) for (a, w, b) in consts]
    out_specs = [_row_spec(T, w, 0) for (w, _) in outs]
    out_shape = [jax.ShapeDtypeStruct((L, ncb * w), dt) for (w, dt) in outs]
    return pl.pallas_call(
        body, name=name, grid=(ncb, L // T), in_specs=in_specs, out_specs=out_specs, out_shape=out_shape,
        compiler_params=_params(("parallel", "parallel")))(*[a for (a, _, _) in rows], *[a for (a, _, _) in consts])


def rowop_bwd(fn, name, rows, consts, cts, row_grads, const_grads, T, ncb=1, add=None):
    L = rows[0][0].shape[0]
    nr, nc, nct = len(rows), len(consts), len(cts)
    n_in = nr + nc + nct + (1 if add is not None else 0)
    rg_idx = [i for i, d in enumerate(row_grads) if d is not None]
    cg_idx = [i for i, d in enumerate(const_grads) if d]

    def body(*refs):
        ins = [r[...].astype(f32) for r in refs[:nr + nc]]
        ct = tuple(r[...].astype(f32) for r in refs[nr + nc:nr + nc + nct])
        _, vjp = jax.vjp(fn, *ins)
        grads = vjp(ct)
        outs = refs[n_in:]
        for n, i in enumerate(rg_idx):
            gi = grads[i]
            if add is not None and n == 0:
                gi = gi + refs[n_in - 1][...].astype(f32)
            outs[n][...] = gi.astype(outs[n].dtype)
        first = pl.program_id(1) == 0
        for n, i in enumerate(cg_idx):
            o = outs[len(rg_idx) + n]

            @pl.when(first)
            def _(o=o):
                o[...] = jnp.zeros_like(o)

            o[...] += grads[nr + i]

    in_specs = ([_row_spec(T, w, b) for (_, w, b) in rows] + [_const_spec(a.shape[0], w, b) for (a, w, b) in consts]
                + [_row_spec(T, w, b) for (_, w, b) in cts] + ([_row_spec(T, add[1], add[2])] if add is not None else []))
    out_specs = ([_row_spec(T, rows[i][1], 0) for i in rg_idx] + [_const_spec(consts[i][0].shape[0], consts[i][1], 0) for i in cg_idx])
    out_shape = ([jax.ShapeDtypeStruct((L, ncb * rows[i][1]), row_grads[i]) for i in rg_idx]
                 + [jax.ShapeDtypeStruct((consts[i][0].shape[0], ncb * consts[i][1]), f32) for i in cg_idx])
    args = [a for (a, _, _) in rows] + [a for (a, _, _) in consts] + [a for (a, _, _) in cts] + ([add[0]] if add is not None else [])
    return pl.pallas_call(
        body, name=name, grid=(ncb, L // T), in_specs=in_specs, out_specs=out_specs, out_shape=out_shape,
        compiler_params=_params(("parallel", "arbitrary")))(*args)


def _rms_fn(x, w):
    return (x * lax.rsqrt(jnp.mean(x * x, axis=-1, keepdims=True) + EPS) * w,)


def _glu_fn(a, ag):
    return (a * _sigmoid(ag),)


def _gate_fn(o, g):
    return (o * _silu(g),)


def _ple_fn(gpre, pp):
    return (_sigmoid(gpre) * pp,)


def _ple_fwd_fn(h1, gpre, pp):
    return (h1 + _sigmoid(gpre) * pp,)


def _lngate_fn(vc, g, w, b):
    mu = jnp.mean(vc, axis=-1, keepdims=True)
    xc = vc - mu
    y = xc * lax.rsqrt(jnp.mean(xc * xc, axis=-1, keepdims=True) + EPS) * w + b
    return (_silu(y) * _silu(g),)


def _post_fn(yf, yb, xs, z, dsk, gw):
    y = (yf + yb + dsk * xs) * _silu(z)
    return (y * lax.rsqrt(jnp.mean(y * y, axis=-1, keepdims=True) + EPS) * gw,)


def _make_prep_fn(n_heads):
    def prep_fn(u, bias, alog):
        lane = lax.broadcasted_iota(jnp.int32, (1, LANES), 1)
        dt = jnp.where(lane < 2 * n_heads, jax.nn.softplus(u + bias), 0.0)
        da = dt * (-jnp.exp(alog))
        li = lax.broadcasted_iota(jnp.int32, (CHUNK, CHUNK), 0)
        si = lax.broadcasted_iota(jnp.int32, (CHUNK, CHUNK), 1)
        tril = (si <= li).astype(f32)
        triu = (si >= li).astype(f32)
        csf = jnp.dot(tril, da, precision=HIGHEST, preferred_element_type=f32)
        csb = jnp.dot(triu, da, precision=HIGHEST, preferred_element_type=f32)
        return dt, jnp.where(lane < n_heads, csf, csb)

    return prep_fn


def final_loss(h, w, tgt, name="final_loss"):
    L, D = h.shape
    T = _pick(L, (256, 128))

    def fn(x, w_, t):
        y = x * lax.rsqrt(jnp.mean(x * x, axis=-1, keepdims=True) + EPS) * w_
        err = jnp.square(y - t)
        return 0.5 * jnp.sum(jnp.mean(err, axis=-1, keepdims=True))

    def body(h_ref, w_ref, t_ref, loss_ref, dh_ref, dw_ref):
        @pl.when(pl.program_id(0) == 0)
        def _():
            loss_ref[...] = jnp.zeros_like(loss_ref)
            dw_ref[...] = jnp.zeros_like(dw_ref)

        t = t_ref[...]
        val, vjp = jax.vjp(lambda x, w_: fn(x, w_, t), h_ref[...], w_ref[...])
        dx, dw = vjp(jnp.ones((), f32))
        dh_ref[...] = dx
        dw_ref[...] += dw
        loss_ref[...] += jnp.broadcast_to(val, loss_ref.shape)

    return pl.pallas_call(
        body, name=name, grid=(L // T,),
        in_specs=[pl.BlockSpec((T, D), lambda i: (i, 0)), pl.BlockSpec((1, D), lambda i: (0, 0)), pl.BlockSpec((T, D), lambda i: (i, 0))],
        out_specs=[pl.BlockSpec((8, LANES), lambda i: (0, 0)), pl.BlockSpec((T, D), lambda i: (i, 0)), pl.BlockSpec((1, D), lambda i: (0, 0))],
        out_shape=[jax.ShapeDtypeStruct((8, LANES), f32), jax.ShapeDtypeStruct((L, D), f32), jax.ShapeDtypeStruct((1, D), f32)],
        compiler_params=_params(("arbitrary",)))(h, w, tgt)


CONV_ROWS = 128


def conv_fwd(x_arr, xbase, C, w, b, K, act, name):
    L = x_arr.shape[0]
    R = CONV_ROWS
    lo = CONV_PAD - K // 2

    def body(x_ref, w_ref, b_ref, *rest):
        xpad = rest[-1]
        outs = rest[:-1]
        xpad[0:CONV_PAD, :] = jnp.zeros((CONV_PAD, LANES), f32)
        xpad[CONV_PAD + L:CONV_PAD + L + CONV_PAD, :] = jnp.zeros((CONV_PAD, LANES), f32)
        xpad[CONV_PAD:CONV_PAD + L, :] = x_ref[...]

        def chunk(ci, carry):
            r0 = pl.multiple_of(ci * R, R)
            acc = jnp.broadcast_to(b_ref[...], (R, LANES))
            for k in range(K):
                acc = acc + w_ref[k:k + 1, :] * xpad[pl.ds(r0 + (lo + k), R), :]
            outs[0][pl.ds(r0, R), :] = acc
            if act:
                outs[1][pl.ds(r0, R), :] = _silu(acc)
            return carry

        lax.fori_loop(0, L // R, chunk, 0)

    col = pl.BlockSpec((L, LANES), lambda j: (0, j))
    n_out = 2 if act else 1
    return pl.pallas_call(
        body, name=name, grid=(C // LANES,),
        in_specs=[pl.BlockSpec((L, LANES), lambda j: (0, xbase + j)), pl.BlockSpec((K, LANES), lambda j: (0, j)),
                  pl.BlockSpec((1, LANES), lambda j: (0, j))],
        out_specs=[col] * n_out, out_shape=[jax.ShapeDtypeStruct((L, C), f32)] * n_out,
        scratch_shapes=[pltpu.VMEM((L + 2 * CONV_PAD, LANES), f32)],
        compiler_params=_params(("parallel",)))(x_arr, w, b)


def conv_bwd(dy, x_arr, xbase, C, w, K, pre, out_dtype, name):
    L = x_arr.shape[0]
    R = CONV_ROWS
    lo = CONV_PAD - K // 2
    act = pre is not None

    def body(*refs):
        if act:
            dy_ref, pre_ref, x_ref, w_ref, dx_ref, dw_ref, db_ref, dpad, xpad, accw, accb = refs
        else:
            dy_ref, x_ref, w_ref, dx_ref, dw_ref, db_ref, dpad, xpad, accw, accb = refs
        zeros = jnp.zeros((CONV_PAD, LANES), f32)
        for pad in (dpad, xpad):
            pad[0:CONV_PAD, :] = zeros
            pad[CONV_PAD + L:CONV_PAD + L + CONV_PAD, :] = zeros
        xpad[CONV_PAD:CONV_PAD + L, :] = x_ref[...]
        if act:
            p = pre_ref[...]
            s = _sigmoid(p)
            dpad[CONV_PAD:CONV_PAD + L, :] = dy_ref[...].astype(f32) * (s * (1.0 + p * (1.0 - s)))
        else:
            dpad[CONV_PAD:CONV_PAD + L, :] = dy_ref[...].astype(f32)
        accw[...] = jnp.zeros_like(accw)
        accb[...] = jnp.zeros_like(accb)

        def chunk(ci, carry):
            r0 = pl.multiple_of(ci * R, R)
            acc = jnp.zeros((R, LANES), f32)
            for k in range(K):
                acc = acc + w_ref[k:k + 1, :] * dpad[pl.ds(r0 + (CONV_PAD + K // 2 - k), R), :]
            dx_ref[pl.ds(r0, R), :] = acc.astype(out_dtype)
            d = dpad[pl.ds(r0 + CONV_PAD, R), :]
            accb[...] += jnp.sum(d.reshape(R // 8, 8, LANES), axis=0)
            for k in range(K):
                prod = d * xpad[pl.ds(r0 + (lo + k), R), :]
                accw[k] += jnp.sum(prod.reshape(R // 8, 8, LANES), axis=0)
            return carry

        lax.fori_loop(0, L // R, chunk, 0)
        dw_ref[...] = jnp.sum(accw[...], axis=1)
        db_ref[...] = jnp.sum(accb[...], axis=0, keepdims=True)

    col = pl.BlockSpec((L, LANES), lambda j: (0, j))
    in_specs = [col] + ([col] if act else []) + [pl.BlockSpec((L, LANES), lambda j: (0, xbase + j)), pl.BlockSpec((K, LANES), lambda j: (0, j))]
    args = (dy,) + ((pre,) if act else ()) + (x_arr, w)
    return pl.pallas_call(
        body, name=name, grid=(C // LANES,), in_specs=in_specs,
        out_specs=[col, pl.BlockSpec((K, LANES), lambda j: (0, j)), pl.BlockSpec((1, LANES), lambda j: (0, j))],
        out_shape=[jax.ShapeDtypeStruct((L, C), out_dtype), jax.ShapeDtypeStruct((K, C), f32), jax.ShapeDtypeStruct((1, C), f32)],
        scratch_shapes=[pltpu.VMEM((L + 2 * CONV_PAD, LANES), f32), pltpu.VMEM((L + 2 * CONV_PAD, LANES), f32),
                        pltpu.VMEM((K, 8, LANES), f32), pltpu.VMEM((8, LANES), f32)],
        compiler_params=_params(("parallel",)))(*args)


def _ssd_chunk(X, Bm, Cm, dtc, csc, csr, S, lane_off, dirn):
    Qn = X.shape[0]
    ci = lax.broadcasted_iota(jnp.int32, (LANES, GROUP_W), 0)
    hh = lax.shift_right_logical(lax.broadcasted_iota(jnp.int32, (LANES, GROUP_W), 1), 6)
    E = (ci == lane_off + hh).astype(f32)
    dtx = jnp.dot(dtc, E, precision=HIGHEST, preferred_element_type=f32)
    csx = jnp.dot(csc, E, precision=HIGHEST, preferred_element_type=f32)
    Xd = X * dtx
    edge = Qn - 1 if dirn == 0 else 0
    cs_edge = csx[edge:edge + 1, :]
    li = lax.broadcasted_iota(jnp.int32, (Qn, Qn), 0)
    si = lax.broadcasted_iota(jnp.int32, (Qn, Qn), 1)
    mask = (si <= li) if dirn == 0 else (si >= li)
    Cb = Cm.astype(bf16)
    Bb = Bm.astype(bf16)
    G = lax.dot_general(Cb, Bb, _DN["nt"], preferred_element_type=f32)
    Y = jnp.exp(csx) * jnp.dot(Cb, S.astype(bf16), preferred_element_type=f32)
    head = lax.shift_right_logical(lax.broadcasted_iota(jnp.int32, (1, GROUP_W), 1), 6)
    for j in range(HEADS_PER_GROUP):
        col = csx[:, HEAD_P * j:HEAD_P * j + 1]
        row = csr[j:j + 1, :]
        Lm = jnp.exp(jnp.where(mask, col - row, NEG_BIG))
        Wj = (G * Lm).astype(bf16)
        Xj = jnp.where(head == j, Xd, 0.0).astype(bf16)
        Y = Y + jnp.dot(Wj, Xj, preferred_element_type=f32)
    S_new = S * jnp.exp(cs_edge) + lax.dot_general(Bb, (Xd * jnp.exp(cs_edge - csx)).astype(bf16), _DN["tn"], preferred_element_type=f32)
    return Y, S_new


def _ssd_specs(D, G, nchunk, dirn, descending):
    def cidx(c):
        return nchunk - 1 - c if descending else c

    x_spec = pl.BlockSpec((CHUNK, GROUP_W), lambda c, g: (cidx(c), g))
    b_spec = pl.BlockSpec((CHUNK, N_STATE), lambda c, g: (cidx(c), D // N_STATE + g))
    c_spec = pl.BlockSpec((CHUNK, N_STATE), lambda c, g: (cidx(c), D // N_STATE + G + g))
    n_spec = pl.BlockSpec((CHUNK, N_STATE), lambda c, g: (cidx(c), g))
    lane_spec = pl.BlockSpec((CHUNK, LANES), lambda c, g: (cidx(c), 0))
    csr_spec = pl.BlockSpec((None, 8, CHUNK), lambda c, g: (dirn * G + g, 0, cidx(c)))
    dcsr_spec = pl.BlockSpec((None, 8, CHUNK), lambda c, g: (g, 0, cidx(c)))
    s_spec = pl.BlockSpec((None, None, N_STATE, GROUP_W), lambda c, g: (g, cidx(c), 0, 0))
    return x_spec, b_spec, c_spec, n_spec, lane_spec, csr_spec, dcsr_spec, s_spec


def ssd_fwd(act, dt128, cs128, csr, dirn, n_heads, name):
    L = act.shape[0]
    D = act.shape[1] // 2
    G = D // GROUP_W
    nchunk = L // CHUNK
    x_spec, b_spec, c_spec, _, lane_spec, csr_spec, _, s_spec = _ssd_specs(D, G, nchunk, dirn, dirn == 1)

    def body(x_ref, b_ref, c_ref, dt_ref, cs_ref, csr_ref, y_ref, ssave_ref, S):
        c, g = pl.program_id(0), pl.program_id(1)

        @pl.when(c == 0)
        def _():
            S[g] = jnp.zeros((N_STATE, GROUP_W), f32)

        s_in = S[g]
        ssave_ref[...] = s_in
        y, s_new = _ssd_chunk(x_ref[...], b_ref[...], c_ref[...], dt_ref[...], cs_ref[...], csr_ref[...], s_in,
                              dirn * n_heads + HEADS_PER_GROUP * g, dirn)
        y_ref[...] = y
        S[g] = s_new

    return pl.pallas_call(
        body, name=name, grid=(nchunk, G),
        in_specs=[x_spec, b_spec, c_spec, lane_spec, lane_spec, csr_spec],
        out_specs=[x_spec, s_spec],
        out_shape=[jax.ShapeDtypeStruct((L, D), f32), jax.ShapeDtypeStruct((G, nchunk, N_STATE, GROUP_W), f32)],
        scratch_shapes=[pltpu.VMEM((G, N_STATE, GROUP_W), f32)],
        compiler_params=_params(("arbitrary", "arbitrary")))(act, act, act, dt128, cs128, csr)


def ssd_bwd(act, dt128, cs128, csr, ssave, dy, dirn, n_heads, name):
    L = act.shape[0]
    D = act.shape[1] // 2
    G = D // GROUP_W
    nchunk = L // CHUNK
    x_spec, b_spec, c_spec, n_spec, lane_spec, csr_in, dcsr_spec, s_spec = _ssd_specs(D, G, nchunk, dirn, dirn == 0)

    def body(x_ref, b_ref, c_ref, dt_ref, cs_ref, csr_ref, s_ref, dy_ref, dx_ref, db_ref, dc_ref, ddt_ref, dcs_ref, dcsr_ref, dS):
        c, g = pl.program_id(0), pl.program_id(1)

        @pl.when(c == 0)
        def _():
            dS[g] = jnp.zeros((N_STATE, GROUP_W), f32)

        lane_off = dirn * n_heads + HEADS_PER_GROUP * g
        _, vjp = jax.vjp(lambda X, Bm, Cm, dtc, csc, csr_, S: _ssd_chunk(X, Bm, Cm, dtc, csc, csr_, S, lane_off, dirn),
                         x_ref[...], b_ref[...], c_ref[...], dt_ref[...], cs_ref[...], csr_ref[...], s_ref[...])
        dX, dB, dC, ddt, dcs, dcsr, dS_in = vjp((dy_ref[...], dS[g]))
        dx_ref[...] = dX
        db_ref[...] = dB
        dc_ref[...] = dC
        dcsr_ref[...] = dcsr
        dS[g] = dS_in

        @pl.when(g == 0)
        def _():
            ddt_ref[...] = jnp.zeros_like(ddt_ref)
            dcs_ref[...] = jnp.zeros_like(dcs_ref)

        ddt_ref[...] += ddt
        dcs_ref[...] += dcs

    return pl.pallas_call(
        body, name=name, grid=(nchunk, G),
        in_specs=[x_spec, b_spec, c_spec, lane_spec, lane_spec, csr_in, s_spec, x_spec],
        out_specs=[x_spec, n_spec, n_spec, lane_spec, lane_spec, dcsr_spec],
        out_shape=[jax.ShapeDtypeStruct((L, D), f32), jax.ShapeDtypeStruct((L, G * N_STATE), f32), jax.ShapeDtypeStruct((L, G * N_STATE), f32),
                   jax.ShapeDtypeStruct((L, LANES), f32), jax.ShapeDtypeStruct((L, LANES), f32), jax.ShapeDtypeStruct((G, 8, L), f32)],
        scratch_shapes=[pltpu.VMEM((G, N_STATE, GROUP_W), f32)],
        compiler_params=_params(("arbitrary", "arbitrary")))(act, act, act, dt128, cs128, csr, ssave, dy)


def _csr_from_cs(cs128, n_heads):
    L = cs128.shape[0]
    t = cs128[:, :2 * n_heads].T.reshape(2 * n_heads // HEADS_PER_GROUP, HEADS_PER_GROUP, L)
    return jnp.pad(t, ((0, 0), (0, 8 - HEADS_PER_GROUP), (0, 0)))


def _cs_from_dcsr(dcsr_f, dcsr_b, n_heads):
    L = dcsr_f.shape[-1]
    t = jnp.concatenate([dcsr_f, dcsr_b], axis=0)[:, :HEADS_PER_GROUP, :].reshape(2 * n_heads, L).T
    return jnp.pad(t, ((0, 0), (0, LANES - 2 * n_heads)))


def na_bias_table(rpb):
    nh = rpb.shape[0]
    cols = np.arange(GRID_W)
    col_start = np.clip(cols - NA_WC // 2, 0, GRID_W - NA_WC)
    col_valid = (cols[None, :] >= col_start[:, None]) & (cols[None, :] < col_start[:, None] + NA_WC)
    col_off = np.clip(cols[None, :] - cols[:, None], -(NA_WC - 1), NA_WC - 1) + NA_WC - 1
    rm = np.zeros((NA_WR, NA_WR, 2 * NA_WR - 1), np.float32)
    for d in range(NA_WR):
        for i in range(NA_WR):
            rm[d, i, i - d + NA_WR - 1] = 1.0
    cm = np.zeros((GRID_W, GRID_W, 2 * NA_WC - 1), np.float32)
    cm[np.arange(GRID_W)[:, None], np.arange(GRID_W)[None, :], col_off] = 1.0
    bt = jnp.einsum("hrc,dir,qkc->hdqik", rpb, jnp.asarray(rm), jnp.asarray(cm), precision=HIGHEST)
    bt = jnp.where(jnp.asarray(col_valid)[None, None, :, None, :], bt, NEG_BIG)
    return bt.reshape(nh, NA_WR, GRID_W, NA_WR * GRID_W)


def _na_window(r, rows):
    rs = jnp.clip(r - NA_WR // 2, 0, rows - NA_WR)
    return rs, r - rs


def _na_probs(q, kw, bias):
    s = lax.dot_general(q.astype(bf16), kw.astype(bf16), _DN["nt"], preferred_element_type=f32) * (NA_D ** -0.5) + bias
    m = jnp.max(s, axis=-1, keepdims=True)
    p = jnp.exp(s - m)
    return p / jnp.sum(p, axis=-1, keepdims=True)


def na_fwd(um, qbase, kbase, vbase, bt, name):
    L = um.shape[0]
    nh = bt.shape[0]
    rows = L // GRID_W
    WK = NA_WR * GRID_W

    def body(q_ref, k_ref, v_ref, bt_ref, o_ref):
        rs, delta = _na_window(pl.program_id(1), rows)
        t0 = pl.multiple_of(rs * GRID_W, GRID_W)
        p = _na_probs(q_ref[...], k_ref[pl.ds(t0, WK), :], bt_ref[delta])
        o_ref[...] = jnp.dot(p.astype(bf16), v_ref[pl.ds(t0, WK), :].astype(bf16), preferred_element_type=f32)

    kv = lambda base: pl.BlockSpec((L, NA_D), lambda h, r: (0, base + h))
    return pl.pallas_call(
        body, name=name, grid=(nh, rows),
        in_specs=[pl.BlockSpec((GRID_W, NA_D), lambda h, r: (r, qbase + h)), kv(kbase), kv(vbase),
                  pl.BlockSpec((None, NA_WR, GRID_W, WK), lambda h, r: (h, 0, 0, 0))],
        out_specs=pl.BlockSpec((GRID_W, NA_D), lambda h, r: (r, h)),
        out_shape=jax.ShapeDtypeStruct((L, nh * NA_D), f32),
        compiler_params=_params(("parallel", "arbitrary")))(um, um, um, bt)


def na_bwd(um, qbase, kbase, vbase, bt, do, name):
    L = um.shape[0]
    nh = bt.shape[0]
    rows = L // GRID_W
    WK = NA_WR * GRID_W
    scale = NA_D ** -0.5

    def body(q_ref, k_ref, v_ref, bt_ref, do_ref, dq_ref, dk_ref, dv_ref, dbt_ref):
        r = pl.program_id(1)

        @pl.when(r == 0)
        def _():
            dk_ref[...] = jnp.zeros_like(dk_ref)
            dv_ref[...] = jnp.zeros_like(dv_ref)
            dbt_ref[...] = jnp.zeros_like(dbt_ref)

        rs, delta = _na_window(r, rows)
        t0 = pl.multiple_of(rs * GRID_W, GRID_W)
        q = q_ref[...]
        kw = k_ref[pl.ds(t0, WK), :]
        vw = v_ref[pl.ds(t0, WK), :]
        do_ = do_ref[...].astype(bf16)
        p = _na_probs(q, kw, bt_ref[delta])
        dp = lax.dot_general(do_, vw.astype(bf16), _DN["nt"], preferred_element_type=f32)
        ds = p * (dp - jnp.sum(dp * p, axis=-1, keepdims=True))
        dbt_ref[delta] += ds
        dsb = ds.astype(bf16)
        dq_ref[...] = jnp.dot(dsb, kw.astype(bf16), preferred_element_type=f32) * scale
        dk_ref[pl.ds(t0, WK), :] += lax.dot_general(dsb, q.astype(bf16), _DN["tn"], preferred_element_type=f32) * scale
        dv_ref[pl.ds(t0, WK), :] += lax.dot_general(p.astype(bf16), do_, _DN["tn"], preferred_element_type=f32)

    kv = lambda base: pl.BlockSpec((L, NA_D), lambda h, r: (0, base + h))
    qs = lambda base: pl.BlockSpec((GRID_W, NA_D), lambda h, r: (r, base + h))
    bts = pl.BlockSpec((None, NA_WR, GRID_W, WK), lambda h, r: (h, 0, 0, 0))
    big = jax.ShapeDtypeStruct((L, nh * NA_D), f32)
    return pl.pallas_call(
        body, name=name, grid=(nh, rows),
        in_specs=[qs(qbase), kv(kbase), kv(vbase), bts, qs(0)],
        out_specs=[qs(0), kv(0), kv(0), bts],
        out_shape=[big, big, big, jax.ShapeDtypeStruct(bt.shape, f32)],
        compiler_params=_params(("parallel", "arbitrary")))(um, um, um, bt, do)


_HBM = pl.BlockSpec(memory_space=pltpu.HBM)


def all_gather(x, name):
    R, C = x.shape

    def body(x_ref, out_ref, send_sems, recv_sems, local_sem):
        mx, my, mc = lax.axis_index("x"), lax.axis_index("y"), lax.axis_index("c")
        me, sibling = (mx, my, mc), (mx, my, 1 - mc)
        chips = [(1 - mx, my), (mx, 1 - my), (1 - mx, 1 - my)]

        def slab(px, py, pc):
            return out_ref.at[4 * px + 2 * py + pc]

        def copy(k, block, to, src=None):
            return pltpu.make_async_remote_copy(
                src_ref=slab(*block) if src is None else src, dst_ref=slab(*block),
                send_sem=send_sems.at[k], recv_sem=recv_sems.at[k], device_id=to, device_id_type=pl.DeviceIdType.MESH)

        mine = pltpu.make_async_copy(x_ref, slab(*me), local_sem)
        mine.start()
        first = [copy(0, me, sibling, src=x_ref)]
        first += [copy(1 + j, me, (*chip, mc), src=x_ref) for j, chip in enumerate(chips)]
        for cp in first:
            cp.start()
        passed = [copy(4 + j, (*chip, mc), sibling) for j, chip in enumerate(chips)]
        for j, chip in enumerate(chips):
            copy(1 + j, (*chip, mc), me).wait_recv()
            passed[j].start()
        copy(0, sibling, me).wait_recv()
        for j, chip in enumerate(chips):
            copy(4 + j, (*chip, 1 - mc), me).wait_recv()
        for cp in first + passed:
            cp.wait_send()
        mine.wait()

    return pl.pallas_call(
        body, name=name, in_specs=[_HBM], out_specs=_HBM, out_shape=jax.ShapeDtypeStruct((N_DEV, R, C), x.dtype),
        scratch_shapes=[pltpu.SemaphoreType.DMA((7,)), pltpu.SemaphoreType.DMA((7,)), pltpu.SemaphoreType.DMA(())])(x)


def all_to_all(x, name):
    _, R, C = x.shape

    def body(x_ref, out_ref, send_sems, recv_sems, local_sem):
        mx, my, mc = lax.axis_index("x"), lax.axis_index("y"), lax.axis_index("c")
        me = 4 * mx + 2 * my + mc
        mine = pltpu.make_async_copy(x_ref.at[me], out_ref.at[me], local_sem)
        mine.start()
        copies = []
        for k in range(1, N_DEV):
            px = 1 - mx if k & 4 else mx
            py = 1 - my if k & 2 else my
            pc = 1 - mc if k & 1 else mc
            peer = 4 * px + 2 * py + pc
            cp = pltpu.make_async_remote_copy(
                src_ref=x_ref.at[peer], dst_ref=out_ref.at[me], send_sem=send_sems.at[k - 1], recv_sem=recv_sems.at[k - 1],
                device_id=(px, py, pc), device_id_type=pl.DeviceIdType.MESH)
            cp.start()
            copies.append(cp)
        for cp in copies:
            cp.wait()
        mine.wait()

    return pl.pallas_call(
        body, name=name, in_specs=[_HBM], out_specs=_HBM, out_shape=jax.ShapeDtypeStruct(x.shape, x.dtype),
        scratch_shapes=[pltpu.SemaphoreType.DMA((7,)), pltpu.SemaphoreType.DMA((7,)), pltpu.SemaphoreType.DMA(())])(x)


def adamw(w, slabs, m, v, own, name):
    R, C = w.shape
    tr = _pick(R, (256, 128, 64, 32, 16, 8))
    c1 = 1.0 - ADAM_B1 ** ADAM_STEP
    c2 = 1.0 - ADAM_B2 ** ADAM_STEP

    def body(*refs):
        if own is not None:
            me_ref, w_ref, s_ref, m_ref, v_ref, own_ref, g_ref, d_ref, nm_ref, nv_ref = refs
        else:
            w_ref, s_ref, m_ref, v_ref, g_ref, d_ref, nm_ref, nv_ref = refs
        g = None
        for s in range(N_DEV):
            t = s_ref[s].astype(f32)
            if own is not None:
                t = jnp.where(me_ref[0] == s, own_ref[...], t)
            g = t if g is None else g + t
        wv = w_ref[...]
        mn = ADAM_B1 * m_ref[...] + (1.0 - ADAM_B1) * g
        vn = ADAM_B2 * v_ref[...] + (1.0 - ADAM_B2) * jnp.square(g)
        m_hat = mn / c1
        v_hat = vn / c2
        g_ref[...] = g
        d_ref[...] = -ADAM_LR * (m_hat / (jnp.sqrt(v_hat) + ADAM_EPS) + ADAM_WD * wv)
        nm_ref[...] = mn
        nv_ref[...] = vn

    out_shape = [jax.ShapeDtypeStruct((R, C), f32)] * 4
    if own is not None:
        me, full = own
        blk = pl.BlockSpec((tr, C), lambda i, me_: (i, 0))
        gs = pltpu.PrefetchScalarGridSpec(
            num_scalar_prefetch=1, grid=(R // tr,),
            in_specs=[blk, pl.BlockSpec((N_DEV, tr, C), lambda i, me_: (0, i, 0)), blk, blk,
                      pl.BlockSpec((None, tr, C), lambda i, me_: (me_[0], i, 0))],
            out_specs=[blk] * 4)
        return pl.pallas_call(body, name=name, grid_spec=gs, out_shape=out_shape, compiler_params=_params(("parallel",)))(
            me.reshape(1).astype(jnp.int32), w, slabs, m, v, full)
    blk = pl.BlockSpec((tr, C), lambda i: (i, 0))
    return pl.pallas_call(
        body, name=name, grid=(R // tr,), in_specs=[blk, pl.BlockSpec((N_DEV, tr, C), lambda i: (0, i, 0)), blk, blk],
        out_specs=[blk] * 4, out_shape=out_shape, compiler_params=_params(("parallel",)))(w, slabs, m, v)


def _packed_rows(n, cols):
    rows = -(-n // cols)
    mult = 256 if rows > 256 else 16
    return -(-rows // mult) * mult


def _pack(arrs, cols, dtype, lead=()):
    nl = len(lead)
    flat = jnp.concatenate([a.reshape(lead + (-1,)).astype(dtype) for a in arrs], axis=-1)
    n = flat.shape[-1]
    rows = _packed_rows(n, cols)
    flat = jnp.pad(flat, [(0, 0)] * nl + [(0, rows * cols - n)])
    return flat.reshape(lead + (rows, cols))


def _unpack(packed, shapes, lead=()):
    flat = packed.reshape(lead + (-1,))
    out, off = [], 0
    for s in shapes:
        n = int(np.prod(s))
        out.append(flat[..., off:off + n].reshape(lead + tuple(s)))
        off += n
    return out


def _to_slabs(full, axis):
    s = full.shape
    r = full.reshape(s[:axis] + (N_DEV, s[axis] // N_DEV) + s[axis + 1:])
    return jnp.moveaxis(r, axis, 0)


def _from_slabs(slabs, axis):
    r = jnp.moveaxis(slabs, 0, axis)
    s = r.shape
    return r.reshape(s[:axis] + (s[axis] * s[axis + 1],) + s[axis + 2:])


BIG = [("ev_w_in", 2), ("ev_w_out", 1), ("od_w_in", 2), ("od_w_out", 1), ("ple_w_gate", 1), ("ple_w_proj", 2)]
SMALL = [("ev_conv_w", 2), ("od_norm_w", 1), ("od_dw_w", 2), ("od_dw_b", 1), ("od_ln_w", 1), ("od_ln_b", 1)]
REPL = ["ev_norm_w", "ev_conv_b", "ev_dt_bias_f", "ev_dt_bias_b", "ev_a_log_f", "ev_a_log_b", "ev_d_skip", "ev_gnorm_w", "ev_rpb",
        "ple_norm_w", "final_norm_w"]
WEIGHTS = ["ev_norm_w", "ev_w_in", "ev_conv_w", "ev_conv_b", "ev_dt_bias_f", "ev_dt_bias_b", "ev_a_log_f", "ev_a_log_b", "ev_d_skip",
           "ev_gnorm_w", "ev_rpb", "ev_w_out", "od_norm_w", "od_w_in", "od_dw_w", "od_dw_b", "od_ln_w", "od_ln_b", "od_w_out",
           "ple_norm_w", "ple_w_gate", "ple_w_proj", "final_norm_w"]
BIG_COLS = 1024
SMALL_COLS = 128


def _row_tile(L, width):
    return _pick(L, (256, 128)) if width <= 2048 else _pick(L, (128,))


def _ple_fwd(h1, p_i, nw, wg, wp, tag):
    L, D = h1.shape
    T = _row_tile(L, D)
    (hn2,) = rowop_fwd(_rms_fn, "ple_rms_" + tag, [(h1, D, 0)], [(nw, D, 0)], [(D, bf16)], T)
    gpre = mm(hn2, wg, "nn", "ple_gate_mm_" + tag)
    pp = mm(p_i, wp, "nn", "ple_proj_mm_" + tag)
    (h2,) = rowop_fwd(_ple_fwd_fn, "ple_comb_" + tag, [(h1, D, 0), (gpre, D, 0), (pp, D, 0)], [], [(D, f32)], T)
    return h2, (h1, p_i, nw, wg, wp, hn2, gpre, pp)


def _ple_bwd(res, dh2, tag):
    h1, p_i, nw, wg, wp, hn2, gpre, pp = res
    L, D = h1.shape
    T = _row_tile(L, D)
    dgpre, dpp = rowop_bwd(_ple_fn, "ple_comb_bwd_" + tag, [(gpre, D, 0), (pp, D, 0)], [], [(dh2, D, 0)], [bf16, bf16], [], T)
    dwg = mm(hn2, dgpre, "tn", "ple_gate_dw_" + tag)
    dwp = mm(p_i, dpp, "tn", "ple_proj_dw_" + tag)
    dhn2 = mm(dgpre, wg, "nt", "ple_gate_dx_" + tag)
    dh1, dnw = rowop_bwd(_rms_fn, "ple_rms_bwd_" + tag, [(h1, D, 0)], [(nw, D, 0)], [(dhn2, D, 0)], [f32], [True], T, add=(dh2, D, 0))
    return dh1, dnw, dwg, dwp


def _even_fwd(h, P, tag):
    L, D = h.shape
    H = D // HEAD_P
    T = _row_tile(L, D)
    DB = D // LANES
    (hn,) = rowop_fwd(_rms_fn, "ev_rms_" + tag, [(h, D, 0)], [(P["norm_w"], D, 0)], [(D, bf16)], T)
    um = mm(hn, P["w_main"], "nn", "ev_in_mm_" + tag)
    udt = mm(hn, P["w_dt"], "nn", "ev_dt_mm_" + tag)
    pre, act = conv_fwd(um, DB, 2 * D, P["conv_w"], P["conv_b"], K_SSM, True, "ev_conv_" + tag)
    prep = _make_prep_fn(H)
    dt128, cs128 = rowop_fwd(prep, "ev_prep_" + tag, [(udt, LANES, 0)], [(P["bias128"], LANES, 0), (P["alog128"], LANES, 0)],
                             [(LANES, f32), (LANES, f32)], CHUNK)
    csr = _csr_from_cs(cs128, H)
    yf, sf = ssd_fwd(act, dt128, cs128, csr, 0, H, "ev_ssd_f_" + tag)
    yb, sb = ssd_fwd(act, dt128, cs128, csr, 1, H, "ev_ssd_b_" + tag)
    G = D // GROUP_W
    (yssd,) = rowop_fwd(_post_fn, "ev_post_" + tag, [(yf, GROUP_W, 0), (yb, GROUP_W, 0), (act, GROUP_W, 0), (um, GROUP_W, 0)],
                        [(P["dskipx"], GROUP_W, 0), (P["gnorm_w"], GROUP_W, 0)], [(GROUP_W, bf16)], T, ncb=G)
    bt = na_bias_table(P["rpb"])
    o = na_fwd(um, 3 * DB, 4 * DB, 5 * DB, bt, "ev_na_" + tag)
    (yna,) = rowop_fwd(_gate_fn, "ev_nagate_" + tag, [(o, D, 0), (um, D, 6)], [], [(D, bf16)], T)
    cat = jnp.concatenate([yssd, yna], axis=-1)
    h1 = mm(cat, P["w_out"], "nn", "ev_out_mm_" + tag, add=h)
    return h1, (h, hn, um, udt, pre, act, dt128, cs128, csr, yf, yb, sf, sb, bt, o, cat)


def _even_bwd(P, res, dh1, tag):
    h, hn, um, udt, pre, act, dt128, cs128, csr, yf, yb, sf, sb, bt, o, cat = res
    L, D = h.shape
    H = D // HEAD_P
    G = D // GROUP_W
    T = _row_tile(L, D)
    DB = D // LANES
    g = {}
    dcat = mm(dh1, P["w_out"], "nt", "ev_out_dx_" + tag)
    g["w_out"] = mm(cat, dh1, "tn", "ev_out_dw_" + tag)
    do, dg = rowop_bwd(_gate_fn, "ev_nagate_bwd_" + tag, [(o, D, 0), (um, D, 6)], [], [(dcat, D, 1)], [f32, bf16], [], T)
    dq, dk, dv, dbt = na_bwd(um, 3 * DB, 4 * DB, 5 * DB, bt, do, "ev_na_bwd_" + tag)
    _, rpb_vjp = jax.vjp(na_bias_table, P["rpb"])
    (g["rpb"],) = rpb_vjp(dbt)
    dy, dxs_a, dz, ddsk, dgn = rowop_bwd(
        _post_fn, "ev_post_bwd_" + tag, [(yf, GROUP_W, 0), (yb, GROUP_W, 0), (act, GROUP_W, 0), (um, GROUP_W, 0)],
        [(P["dskipx"], GROUP_W, 0), (P["gnorm_w"], GROUP_W, 0)], [(dcat, GROUP_W, 0)], [f32, None, f32, bf16], [True, True], T, ncb=G)
    g["d_skip"] = ddsk.reshape(H, HEAD_P).sum(axis=-1)
    g["gnorm_w"] = dgn.reshape(D)
    dxf, dbf, dcf, ddtf, dcsf, dcsrf = ssd_bwd(act, dt128, cs128, csr, sf, dy, 0, H, "ev_ssd_f_bwd_" + tag)
    dxb, dbb, dcb, ddtb, dcsb, dcsrb = ssd_bwd(act, dt128, cs128, csr, sb, dy, 1, H, "ev_ssd_b_bwd_" + tag)
    dact = jnp.concatenate([dxf + dxb + dxs_a, dbf + dbb, dcf + dcb], axis=-1)
    ddt128 = ddtf + ddtb
    dcs128 = dcsf + dcsb + _cs_from_dcsr(dcsrf, dcsrb, H)
    prep = _make_prep_fn(H)
    dudt, dbias, dalog = rowop_bwd(prep, "ev_prep_bwd_" + tag, [(udt, LANES, 0)], [(P["bias128"], LANES, 0), (P["alog128"], LANES, 0)],
                                   [(ddt128, LANES, 0), (dcs128, LANES, 0)], [bf16], [True, True], CHUNK)
    g["dt_bias_f"], g["dt_bias_b"] = dbias[0, :H], dbias[0, H:2 * H]
    g["a_log_f"], g["a_log_b"] = dalog[0, :H], dalog[0, H:2 * H]
    dxbc, g["conv_w"], dcb_ = conv_bwd(dact, um, DB, 2 * D, P["conv_w"], K_SSM, pre, bf16, "ev_conv_bwd_" + tag)
    g["conv_b"] = dcb_.reshape(2 * D)
    dum = jnp.concatenate([dz, dxbc, dq.astype(bf16), dk.astype(bf16), dv.astype(bf16), dg], axis=-1)
    dhn = mm(dum, P["w_main"], "nt", "ev_in_dx_" + tag)
    dhn = mm(dudt, P["w_dt"], "nt", "ev_dt_dx_" + tag, add=dhn)
    g["w_main"] = mm(hn, dum, "tn", "ev_in_dw_" + tag)
    g["w_dt"] = mm(hn, dudt, "tn", "ev_dt_dw_" + tag)
    dh, dnw = rowop_bwd(_rms_fn, "ev_rms_bwd_" + tag, [(h, D, 0)], [(P["norm_w"], D, 0)], [(dhn, D, 0)], [f32], [True], T, add=(dh1, D, 0))
    g["norm_w"] = dnw.reshape(D)
    return dh, g


def _odd_fwd(h, P, tag):
    L, D = h.shape
    C = 2 * D
    T = _row_tile(L, D)
    (hn,) = rowop_fwd(_rms_fn, "od_rms_" + tag, [(h, D, 0)], [(P["norm_w"], D, 0)], [(D, bf16)], T)
    u = mm(hn, P["w_in"], "nn", "od_in_mm_" + tag)
    (vglu,) = rowop_fwd(_glu_fn, "od_glu_" + tag, [(u, D, 0), (u, D, 2)], [], [(D, f32)], T, ncb=2)
    (vc,) = conv_fwd(vglu, 0, C, P["dw_w"], P["dw_b"], K_CONV, False, "od_conv_" + tag)
    TL = _row_tile(L, C)
    (t,) = rowop_fwd(_lngate_fn, "od_lngate_" + tag, [(vc, C, 0), (u, C, 2)], [(P["ln_w"], C, 0), (P["ln_b"], C, 0)], [(C, bf16)], TL)
    h1 = mm(t, P["w_out"], "nn", "od_out_mm_" + tag, add=h)
    return h1, (h, hn, u, vglu, vc, t)


def _odd_bwd(P, res, dh1, tag):
    h, hn, u, vglu, vc, t = res
    L, D = h.shape
    C = 2 * D
    T = _row_tile(L, D)
    TL = _row_tile(L, C)
    g = {}
    dt_ = mm(dh1, P["w_out"], "nt", "od_out_dx_" + tag)
    g["w_out"] = mm(t, dh1, "tn", "od_out_dw_" + tag)
    dvc, dg, dlnw, dlnb = rowop_bwd(_lngate_fn, "od_lngate_bwd_" + tag, [(vc, C, 0), (u, C, 2)], [(P["ln_w"], C, 0), (P["ln_b"], C, 0)],
                                    [(dt_, C, 0)], [f32, bf16], [True, True], TL)
    g["ln_w"], g["ln_b"] = dlnw.reshape(C), dlnb.reshape(C)
    dvglu, g["dw_w"], ddwb = conv_bwd(dvc, vglu, 0, C, P["dw_w"], K_CONV, None, f32, "od_conv_bwd_" + tag)
    g["dw_b"] = ddwb.reshape(C)
    da, dag = rowop_bwd(_glu_fn, "od_glu_bwd_" + tag, [(u, D, 0), (u, D, 2)], [], [(dvglu, D, 0)], [bf16, bf16], [], T, ncb=2)
    du = jnp.concatenate([da, dag, dg], axis=-1)
    dhn = mm(du, P["w_in"], "nt", "od_in_dx_" + tag)
    g["w_in"] = mm(hn, du, "tn", "od_in_dw_" + tag)
    dh, dnw = rowop_bwd(_rms_fn, "od_rms_bwd_" + tag, [(h, D, 0)], [(P["norm_w"], D, 0)], [(dhn, D, 0)], [f32], [True], T, add=(dh1, D, 0))
    g["norm_w"] = dnw.reshape(D)
    return dh, g


def kernel(x, p, ev_norm_w, ev_w_in, ev_conv_w, ev_conv_b, ev_dt_bias_f, ev_dt_bias_b, ev_a_log_f, ev_a_log_b, ev_d_skip, ev_gnorm_w, ev_rpb, ev_w_out, od_norm_w, od_w_in, od_dw_w, od_dw_b, od_ln_w, od_ln_b, od_w_out, ple_norm_w, ple_w_gate, ple_w_proj, final_norm_w, loss_target, m_ev_norm_w, m_ev_w_in, m_ev_conv_w, m_ev_conv_b, m_ev_dt_bias_f, m_ev_dt_bias_b, m_ev_a_log_f, m_ev_a_log_b, m_ev_d_skip, m_ev_gnorm_w, m_ev_rpb, m_ev_w_out, m_od_norm_w, m_od_w_in, m_od_dw_w, m_od_dw_b, m_od_ln_w, m_od_ln_b, m_od_w_out, m_ple_norm_w, m_ple_w_gate, m_ple_w_proj, m_final_norm_w, v_ev_norm_w, v_ev_w_in, v_ev_conv_w, v_ev_conv_b, v_ev_dt_bias_f, v_ev_dt_bias_b, v_ev_a_log_f, v_ev_a_log_b, v_ev_d_skip, v_ev_gnorm_w, v_ev_rpb, v_ev_w_out, v_od_norm_w, v_od_w_in, v_od_dw_w, v_od_dw_b, v_od_ln_w, v_od_ln_b, v_od_w_out, v_ple_norm_w, v_ple_w_gate, v_ple_w_proj, v_final_norm_w):
    W = dict(ev_norm_w=ev_norm_w, ev_w_in=ev_w_in, ev_conv_w=ev_conv_w, ev_conv_b=ev_conv_b, ev_dt_bias_f=ev_dt_bias_f,
             ev_dt_bias_b=ev_dt_bias_b, ev_a_log_f=ev_a_log_f, ev_a_log_b=ev_a_log_b, ev_d_skip=ev_d_skip, ev_gnorm_w=ev_gnorm_w,
             ev_rpb=ev_rpb, ev_w_out=ev_w_out, od_norm_w=od_norm_w, od_w_in=od_w_in, od_dw_w=od_dw_w, od_dw_b=od_dw_b,
             od_ln_w=od_ln_w, od_ln_b=od_ln_b, od_w_out=od_w_out, ple_norm_w=ple_norm_w, ple_w_gate=ple_w_gate,
             ple_w_proj=ple_w_proj, final_norm_w=final_norm_w)
    M = dict(ev_norm_w=m_ev_norm_w, ev_w_in=m_ev_w_in, ev_conv_w=m_ev_conv_w, ev_conv_b=m_ev_conv_b, ev_dt_bias_f=m_ev_dt_bias_f,
             ev_dt_bias_b=m_ev_dt_bias_b, ev_a_log_f=m_ev_a_log_f, ev_a_log_b=m_ev_a_log_b, ev_d_skip=m_ev_d_skip,
             ev_gnorm_w=m_ev_gnorm_w, ev_rpb=m_ev_rpb, ev_w_out=m_ev_w_out, od_norm_w=m_od_norm_w, od_w_in=m_od_w_in,
             od_dw_w=m_od_dw_w, od_dw_b=m_od_dw_b, od_ln_w=m_od_ln_w, od_ln_b=m_od_ln_b, od_w_out=m_od_w_out,
             ple_norm_w=m_ple_norm_w, ple_w_gate=m_ple_w_gate, ple_w_proj=m_ple_w_proj, final_norm_w=m_final_norm_w)
    V = dict(ev_norm_w=v_ev_norm_w, ev_w_in=v_ev_w_in, ev_conv_w=v_ev_conv_w, ev_conv_b=v_ev_conv_b, ev_dt_bias_f=v_ev_dt_bias_f,
             ev_dt_bias_b=v_ev_dt_bias_b, ev_a_log_f=v_ev_a_log_f, ev_a_log_b=v_ev_a_log_b, ev_d_skip=v_ev_d_skip,
             ev_gnorm_w=v_ev_gnorm_w, ev_rpb=v_ev_rpb, ev_w_out=v_ev_w_out, od_norm_w=v_od_norm_w, od_w_in=v_od_w_in,
             od_dw_w=v_od_dw_w, od_dw_b=v_od_dw_b, od_ln_w=v_od_ln_w, od_ln_b=v_od_ln_b, od_w_out=v_od_w_out,
             ple_norm_w=v_ple_norm_w, ple_w_gate=v_ple_w_gate, ple_w_proj=v_ple_w_proj, final_norm_w=v_final_norm_w)

    h0 = x[0]
    L, D = h0.shape
    H = D // HEAD_P
    depth = p.shape[0]
    n_even, n_odd = (depth + 1) // 2, depth // 2
    me = 4 * lax.axis_index("x") + 2 * lax.axis_index("y") + lax.axis_index("c")

    big_shapes = [W[n].shape for n, _ in BIG]
    small_shapes = [W[n].shape for n, _ in SMALL]
    big_g = all_gather(_pack([W[n] for n, _ in BIG], BIG_COLS, bf16), "gather_big")
    small_g = all_gather(_pack([W[n] for n, _ in SMALL], SMALL_COLS, f32), "gather_small")
    F = {n: _from_slabs(a, ax) for (n, ax), a in zip(BIG, _unpack(big_g, big_shapes, (N_DEV,)))}
    F.update({n: _from_slabs(a, ax) for (n, ax), a in zip(SMALL, _unpack(small_g, small_shapes, (N_DEV,)))})

    def even_params(e):
        w_in = F["ev_w_in"][e]
        return dict(
            norm_w=W["ev_norm_w"][e][None], w_main=jnp.concatenate([w_in[:, :3 * D], w_in[:, 3 * D + 2 * H:]], axis=1),
            w_dt=jnp.pad(w_in[:, 3 * D:3 * D + 2 * H], ((0, 0), (0, LANES - 2 * H))),
            conv_w=F["ev_conv_w"][e], conv_b=W["ev_conv_b"][e][None],
            bias128=jnp.pad(jnp.concatenate([W["ev_dt_bias_f"][e], W["ev_dt_bias_b"][e]]), (0, LANES - 2 * H))[None],
            alog128=jnp.pad(jnp.concatenate([W["ev_a_log_f"][e], W["ev_a_log_b"][e]]), (0, LANES - 2 * H))[None],
            dskipx=jnp.repeat(W["ev_d_skip"][e], HEAD_P)[None], gnorm_w=W["ev_gnorm_w"][e][None], rpb=W["ev_rpb"][e],
            w_out=F["ev_w_out"][e])

    def odd_params(e):
        return dict(norm_w=F["od_norm_w"][e][None], w_in=F["od_w_in"][e], dw_w=F["od_dw_w"][e], dw_b=F["od_dw_b"][e][None],
                    ln_w=F["od_ln_w"][e][None], ln_b=F["od_ln_b"][e][None], w_out=F["od_w_out"][e])

    h = h0
    saved = []
    for i in range(depth):
        e = i // 2
        tag = str(i)
        if i % 2 == 0:
            P = even_params(e)
            h1, res = _even_fwd(h, P, tag)
        else:
            P = odd_params(e)
            h1, res = _odd_fwd(h, P, tag)
        h, pres = _ple_fwd(h1, p[i, 0], W["ple_norm_w"][i][None], F["ple_w_gate"][i], F["ple_w_proj"][i], tag)
        saved.append((P, res, pres))
    loss_tile, dh, dfinal = final_loss(h, W["final_norm_w"][None], loss_target[0])

    ev_g = [None] * n_even
    od_g = [None] * n_odd
    ple_g = [None] * depth
    for i in reversed(range(depth)):
        P, res, pres = saved[i]
        tag = str(i)
        dh1, dnw, dwg, dwp = _ple_bwd(pres, dh, tag)
        ple_g[i] = (dnw.reshape(D), dwg, dwp)
        if i % 2 == 0:
            dh, ev_g[i // 2] = _even_bwd(P, res, dh1, tag)
        else:
            dh, od_g[i // 2] = _odd_bwd(P, res, dh1, tag)
    grad_x = dh[None]

    def ev_w_in_grad(g):
        return jnp.concatenate([g["w_main"][:, :3 * D], g["w_dt"][:, :2 * H], g["w_main"][:, 3 * D:]], axis=1)

    full = dict(
        ev_w_in=jnp.stack([ev_w_in_grad(g) for g in ev_g]), ev_w_out=jnp.stack([g["w_out"] for g in ev_g]),
        od_w_in=jnp.stack([g["w_in"] for g in od_g]), od_w_out=jnp.stack([g["w_out"] for g in od_g]),
        ple_w_gate=jnp.stack([g[1] for g in ple_g]), ple_w_proj=jnp.stack([g[2] for g in ple_g]),
        ev_conv_w=jnp.stack([g["conv_w"] for g in ev_g]), od_norm_w=jnp.stack([g["norm_w"] for g in od_g]),
        od_dw_w=jnp.stack([g["dw_w"] for g in od_g]), od_dw_b=jnp.stack([g["dw_b"] for g in od_g]),
        od_ln_w=jnp.stack([g["ln_w"] for g in od_g]), od_ln_b=jnp.stack([g["ln_b"] for g in od_g]),
        ev_norm_w=jnp.stack([g["norm_w"] for g in ev_g]), ev_conv_b=jnp.stack([g["conv_b"] for g in ev_g]),
        ev_dt_bias_f=jnp.stack([g["dt_bias_f"] for g in ev_g]), ev_dt_bias_b=jnp.stack([g["dt_bias_b"] for g in ev_g]),
        ev_a_log_f=jnp.stack([g["a_log_f"] for g in ev_g]), ev_a_log_b=jnp.stack([g["a_log_b"] for g in ev_g]),
        ev_d_skip=jnp.stack([g["d_skip"] for g in ev_g]), ev_gnorm_w=jnp.stack([g["gnorm_w"] for g in ev_g]),
        ev_rpb=jnp.stack([g["rpb"] for g in ev_g]), ple_norm_w=jnp.stack([g[0] for g in ple_g]), final_norm_w=dfinal.reshape(D))

    out = {}
    big_full = _pack([_to_slabs(full[n], ax) for n, ax in BIG], BIG_COLS, f32, (N_DEV,))
    big_recv = all_to_all(big_full.astype(bf16), "scatter_big")
    res_big = adamw(_pack([W[n] for n, _ in BIG], BIG_COLS, f32), big_recv, _pack([M[n] for n, _ in BIG], BIG_COLS, f32),
                    _pack([V[n] for n, _ in BIG], BIG_COLS, f32), (me, big_full), "adamw_big")
    for kind, arr in zip(("grad", "delta", "new_m", "new_v"), res_big):
        for (n, _), a in zip(BIG, _unpack(arr, big_shapes)):
            out[kind + "_" + n] = a
    small_full = _pack([_to_slabs(full[n], ax) for n, ax in SMALL], SMALL_COLS, f32, (N_DEV,))
    small_recv = all_to_all(small_full, "scatter_small")
    res_small = adamw(_pack([W[n] for n, _ in SMALL], SMALL_COLS, f32), small_recv, _pack([M[n] for n, _ in SMALL], SMALL_COLS, f32),
                      _pack([V[n] for n, _ in SMALL], SMALL_COLS, f32), None, "adamw_small")
    for kind, arr in zip(("grad", "delta", "new_m", "new_v"), res_small):
        for (n, _), a in zip(SMALL, _unpack(arr, small_shapes)):
            out[kind + "_" + n] = a
    repl_shapes = [W[n].shape for n in REPL] + [(1,)]
    zero1 = jnp.zeros((1,), f32)
    repl_part = _pack([full[n] for n in REPL] + [loss_tile[0, :1]], SMALL_COLS, f32)
    repl_all = all_gather(repl_part, "gather_repl")
    res_repl = adamw(_pack([W[n] for n in REPL] + [zero1], SMALL_COLS, f32), repl_all, _pack([M[n] for n in REPL] + [zero1], SMALL_COLS, f32),
                     _pack([V[n] for n in REPL] + [zero1], SMALL_COLS, f32), None, "adamw_repl")
    for kind, arr in zip(("grad", "delta", "new_m", "new_v"), res_repl):
        parts = _unpack(arr, repl_shapes)
        for n, a in zip(REPL, parts[:-1]):
            out[kind + "_" + n] = a
        if kind == "grad":
            loss = parts[-1].reshape(())

    return (loss, grad_x, *[out["grad_" + n] for n in WEIGHTS], *[out["delta_" + n] for n in WEIGHTS],
            *[out["new_m_" + n] for n in WEIGHTS], *[out["new_v_" + n] for n in WEIGHTS])
```

```python
import functools

import numpy as np
import jax
import jax.numpy as jnp
from jax import lax
from jax.experimental import pallas as pl
from jax.experimental.pallas import tpu as pltpu

f32 = jnp.float32
bf16 = jnp.bfloat16
HIGHEST = lax.Precision.HIGHEST

N_DEV = 8
EPS = 1e-6
GRID_W = 64
HEAD_P = 64
HEADS_PER_GROUP = 4
GROUP_W = HEAD_P * HEADS_PER_GROUP
N_STATE = 128
CHUNK = 128
K_SSM = 5
K_CONV = 31
NA_D = 128
NA_WR = 8
NA_WC = 16
PLE_DIM = 256
LANES = 128
CONV_PAD = 16
VMEM_LIMIT = 56 * 1024 * 1024
MM_VMEM_BUDGET = 40 * 1024 * 1024
MM_FULL_K = 2048
NA_ROWS_PER_STEP = 4
ADAMW_TILE_BYTES = 1024 * 1024

ADAM_LR = 0.001
ADAM_B1 = 0.9
ADAM_B2 = 0.999
ADAM_EPS = 1e-08
ADAM_WD = 0.01
ADAM_STEP = 10

NEG_BIG = -1e30


def _params(sem=None):
    if sem is None:
        return pltpu.CompilerParams(vmem_limit_bytes=VMEM_LIMIT)
    return pltpu.CompilerParams(vmem_limit_bytes=VMEM_LIMIT, dimension_semantics=sem)


def _pick(n, prefs):
    for t in prefs:
        if n % t == 0:
            return t
    return n


def _sigmoid(x):
    return jax.nn.sigmoid(x)


def _silu(x):
    return x * jax.nn.sigmoid(x)


_DN = {"nn": (((1,), (0,)), ((), ())), "nt": (((1,), (1,)), ((), ())), "tn": (((0,), (0,)), ((), ()))}


def mm(a, b, mode, name, out_dtype=f32, add=None):
    if mode == "nn":
        (M, K), (K2, N) = a.shape, b.shape
    elif mode == "nt":
        (M, K), (N, K2) = a.shape, b.shape
    else:
        (K, M), (K2, N) = a.shape, b.shape
    assert K == K2, (a.shape, b.shape, mode)
    sa, sb, so = a.dtype.itemsize, b.dtype.itemsize, jnp.dtype(out_dtype).itemsize
    tk = K if K <= MM_FULL_K else _pick(K, (1024, 512, 256, 128))
    nk = K // tk
    tms = [t for t in (1024, 512, 256, 128) if M % t == 0] or [M]
    tns = [t for t in (2048, 1024, 512, 256, 128) if N % t == 0] or [N]

    def vmem_bytes(tm, tn):
        n = 2 * (tm * tk * sa + tk * tn * sb) + 2 * tm * tn * so
        n += tm * tn * 4 if nk > 1 else 0
        n += 2 * tm * tn * add.dtype.itemsize if add is not None else 0
        return n

    ti, tj = 0, 0
    while vmem_bytes(tms[ti], tns[tj]) > MM_VMEM_BUDGET:
        if tj + 1 < len(tns) and (tns[tj] >= tms[ti] or ti + 1 >= len(tms)):
            tj += 1
        elif ti + 1 < len(tms):
            ti += 1
        else:
            break
    tm, tn = tms[ti], tns[tj]
    a_bytes, b_bytes = M * K * sa, K * N * sb
    i_outer = a_bytes + b_bytes * (M // tm) <= b_bytes + a_bytes * (N // tn)
    dn = _DN[mode]

    def body(a_ref, b_ref, *rest):
        add_ref = rest[0] if add is not None else None
        o_ref = rest[1] if add is not None else rest[0]

        def finish(r):
            if add is not None:
                r = r + add_ref[...].astype(f32)
            o_ref[...] = r.astype(out_dtype)

        part = lax.dot_general(a_ref[...].astype(bf16), b_ref[...].astype(bf16), dn, preferred_element_type=f32)
        if nk == 1:
            finish(part)
        else:
            acc = rest[-1]
            k = pl.program_id(2)

            @pl.when(k == 0)
            def _():
                acc[...] = part

            @pl.when(k > 0)
            def _():
                acc[...] += part

            @pl.when(k == nk - 1)
            def _():
                finish(acc[...])

    def ij(p, q):
        return (p, q) if i_outer else (q, p)

    if mode == "tn":
        a_spec = pl.BlockSpec((tk, tm), lambda p, q, k: (k, ij(p, q)[0]))
    else:
        a_spec = pl.BlockSpec((tm, tk), lambda p, q, k: (ij(p, q)[0], k))
    if mode == "nt":
        b_spec = pl.BlockSpec((tn, tk), lambda p, q, k: (ij(p, q)[1], k))
    else:
        b_spec = pl.BlockSpec((tk, tn), lambda p, q, k: (k, ij(p, q)[1]))
    o_spec = pl.BlockSpec((tm, tn), lambda p, q, k: ij(p, q))
    in_specs = [a_spec, b_spec] + ([o_spec] if add is not None else [])
    args = (a, b) + ((add,) if add is not None else ())
    grid = (M // tm, N // tn, nk) if i_outer else (N // tn, M // tm, nk)
    return pl.pallas_call(
        body, name=name, grid=grid, in_specs=in_specs, out_specs=o_spec,
        out_shape=jax.ShapeDtypeStruct((M, N), out_dtype), scratch_shapes=[pltpu.VMEM((tm, tn), f32)] if nk > 1 else [],
        compiler_params=_params(("parallel", "parallel", "arbitrary")))(*args)


def _row_spec(T, w, base):
    return pl.BlockSpec((T, w), lambda g, i: (i, base + g))


def _const_spec(r, w, base):
    return pl.BlockSpec((r, w), lambda g, i: (0, base + g))


def rowop_fwd(fn, name, rows, consts, outs, T, ncb=1):
    L = rows[0][0].shape[0]
    nr, nc = len(rows), len(consts)

    def body(*refs):
        ins = [r[...].astype(f32) for r in refs[:nr + nc]]
        res = fn(*ins)
        for o_ref, r in zip(refs[nr + nc:], res):
            o_ref[...] = r.astype(o_ref.dtype)

    in_specs = [_row_spec(T, w, b) for (_, w, b) in rows] + [_const_spec(a.shape[0], w, b) for (a, w, b) in consts]
    out_specs = [_row_spec(T, w, 0) for (w, _) in outs]
    out_shape = [jax.ShapeDtypeStruct((L, ncb * w), dt) for (w, dt) in outs]
    return pl.pallas_call(
        body, name=name, grid=(ncb, L // T), in_specs=in_specs, out_specs=out_specs, out_shape=out_shape,
        compiler_params=_params(("parallel", "parallel")))(*[a for (a, _, _) in rows], *[a for (a, _, _) in consts])


def rowop_bwd(fn, name, rows, consts, cts, row_grads, const_grads, T, ncb=1, add=None):
    L = rows[0][0].shape[0]
    nr, nc, nct = len(rows), len(consts), len(cts)
    n_in = nr + nc + nct + (1 if add is not None else 0)
    rg_idx = [i for i, d in enumerate(row_grads) if d is not None]
    cg_idx = [i for i, d in enumerate(const_grads) if d]

    def body(*refs):
        ins = [r[...].astype(f32) for r in refs[:nr + nc]]
        ct = tuple(r[...].astype(f32) for r in refs[nr + nc:nr + nc + nct])
        _, vjp = jax.vjp(fn, *ins)
        grads = vjp(ct)
        outs = refs[n_in:]
        for n, i in enumerate(rg_idx):
            gi = grads[i]
            if add is not None and n == 0:
                gi = gi + refs[n_in - 1][...].astype(f32)
            outs[n][...] = gi.astype(outs[n].dtype)
        first = pl.program_id(1) == 0
        for n, i in enumerate(cg_idx):
            o = outs[len(rg_idx) + n]

            @pl.when(first)
            def _(o=o):
                o[...] = jnp.zeros_like(o)

            o[...] += grads[nr + i]

    in_specs = ([_row_spec(T, w, b) for (_, w, b) in rows] + [_const_spec(a.shape[0], w, b) for (a, w, b) in consts]
                + [_row_spec(T, w, b) for (_, w, b) in cts] + ([_row_spec(T, add[1], add[2])] if add is not None else []))
    out_specs = ([_row_spec(T, rows[i][1], 0) for i in rg_idx] + [_const_spec(consts[i][0].shape[0], consts[i][1], 0) for i in cg_idx])
    out_shape = ([jax.ShapeDtypeStruct((L, ncb * rows[i][1]), row_grads[i]) for i in rg_idx]
                 + [jax.ShapeDtypeStruct((consts[i][0].shape[0], ncb * consts[i][1]), f32) for i in cg_idx])
    args = [a for (a, _, _) in rows] + [a for (a, _, _) in consts] + [a for (a, _, _) in cts] + ([add[0]] if add is not None else [])
    return pl.pallas_call(
        body, name=name, grid=(ncb, L // T), in_specs=in_specs, out_specs=out_specs, out_shape=out_shape,
        compiler_params=_params(("parallel", "arbitrary")))(*args)


def _rms_fn(x, w):
    return (x * lax.rsqrt(jnp.mean(x * x, axis=-1, keepdims=True) + EPS) * w,)


def _glu_fn(a, ag):
    return (a * _sigmoid(ag),)


def _gate_fn(o, g):
    return (o * _silu(g),)


def _ple_fn(gpre, pp):
    return (_sigmoid(gpre) * pp,)


def _ple_fwd_fn(h1, gpre, pp):
    return (h1 + _sigmoid(gpre) * pp,)


def _lngate_fn(vc, g, w, b):
    mu = jnp.mean(vc, axis=-1, keepdims=True)
    xc = vc - mu
    y = xc * lax.rsqrt(jnp.mean(xc * xc, axis=-1, keepdims=True) + EPS) * w + b
    return (_silu(y) * _silu(g),)


def _post_fn(yf, yb, xs, z, dsk, gw):
    y = (yf + yb + dsk * xs) * _silu(z)
    return (y * lax.rsqrt(jnp.mean(y * y, axis=-1, keepdims=True) + EPS) * gw,)


def _make_prep_fn(n_heads):
    def prep_fn(u, bias, alog):
        lane = lax.broadcasted_iota(jnp.int32, (1, LANES), 1)
        dt = jnp.where(lane < 2 * n_heads, jax.nn.softplus(u + bias), 0.0)
        da = dt * (-jnp.exp(alog))
        li = lax.broadcasted_iota(jnp.int32, (CHUNK, CHUNK), 0)
        si = lax.broadcasted_iota(jnp.int32, (CHUNK, CHUNK), 1)
        tril = (si <= li).astype(f32)
        triu = (si >= li).astype(f32)
        csf = jnp.dot(tril, da, precision=HIGHEST, preferred_element_type=f32)
        csb = jnp.dot(triu, da, precision=HIGHEST, preferred_element_type=f32)
        return dt, jnp.where(lane < n_heads, csf, csb)

    return prep_fn


def final_loss(h, w, tgt, name="final_loss"):
    L, D = h.shape
    T = _pick(L, (256, 128))

    def fn(x, w_, t):
        y = x * lax.rsqrt(jnp.mean(x * x, axis=-1, keepdims=True) + EPS) * w_
        err = jnp.square(y - t)
        return 0.5 * jnp.sum(jnp.mean(err, axis=-1, keepdims=True))

    def body(h_ref, w_ref, t_ref, loss_ref, dh_ref, dw_ref):
        @pl.when(pl.program_id(0) == 0)
        def _():
            loss_ref[...] = jnp.zeros_like(loss_ref)
            dw_ref[...] = jnp.zeros_like(dw_ref)

        t = t_ref[...]
        val, vjp = jax.vjp(lambda x, w_: fn(x, w_, t), h_ref[...], w_ref[...])
        dx, dw = vjp(jnp.ones((), f32))
        dh_ref[...] = dx
        dw_ref[...] += dw
        loss_ref[...] += jnp.broadcast_to(val, loss_ref.shape)

    return pl.pallas_call(
        body, name=name, grid=(L // T,),
        in_specs=[pl.BlockSpec((T, D), lambda i: (i, 0)), pl.BlockSpec((1, D), lambda i: (0, 0)), pl.BlockSpec((T, D), lambda i: (i, 0))],
        out_specs=[pl.BlockSpec((8, LANES), lambda i: (0, 0)), pl.BlockSpec((T, D), lambda i: (i, 0)), pl.BlockSpec((1, D), lambda i: (0, 0))],
        out_shape=[jax.ShapeDtypeStruct((8, LANES), f32), jax.ShapeDtypeStruct((L, D), f32), jax.ShapeDtypeStruct((1, D), f32)],
        compiler_params=_params(("arbitrary",)))(h, w, tgt)


CONV_ROWS = 128


def conv_fwd(x_arr, xbase, C, w, b, K, act, name):
    L = x_arr.shape[0]
    R = CONV_ROWS
    lo = CONV_PAD - K // 2

    def body(x_ref, w_ref, b_ref, *rest):
        xpad = rest[-1]
        outs = rest[:-1]
        xpad[0:CONV_PAD, :] = jnp.zeros((CONV_PAD, LANES), f32)
        xpad[CONV_PAD + L:CONV_PAD + L + CONV_PAD, :] = jnp.zeros((CONV_PAD, LANES), f32)
        xpad[CONV_PAD:CONV_PAD + L, :] = x_ref[...]

        def chunk(ci, carry):
            r0 = pl.multiple_of(ci * R, R)
            acc = jnp.broadcast_to(b_ref[...], (R, LANES))
            for k in range(K):
                acc = acc + w_ref[k:k + 1, :] * xpad[pl.ds(r0 + (lo + k), R), :]
            outs[0][pl.ds(r0, R), :] = acc
            if act:
                outs[1][pl.ds(r0, R), :] = _silu(acc)
            return carry

        lax.fori_loop(0, L // R, chunk, 0)

    col = pl.BlockSpec((L, LANES), lambda j: (0, j))
    n_out = 2 if act else 1
    return pl.pallas_call(
        body, name=name, grid=(C // LANES,),
        in_specs=[pl.BlockSpec((L, LANES), lambda j: (0, xbase + j)), pl.BlockSpec((K, LANES), lambda j: (0, j)),
                  pl.BlockSpec((1, LANES), lambda j: (0, j))],
        out_specs=[col] * n_out, out_shape=[jax.ShapeDtypeStruct((L, C), f32)] * n_out,
        scratch_shapes=[pltpu.VMEM((L + 2 * CONV_PAD, LANES), f32)],
        compiler_params=_params(("parallel",)))(x_arr, w, b)


def conv_bwd(dy, x_arr, xbase, C, w, K, pre, out_dtype, name):
    L = x_arr.shape[0]
    R = CONV_ROWS
    lo = CONV_PAD - K // 2
    act = pre is not None

    def body(*refs):
        if act:
            dy_ref, pre_ref, x_ref, w_ref, dx_ref, dw_ref, db_ref, dpad, xpad, accw, accb = refs
        else:
            dy_ref, x_ref, w_ref, dx_ref, dw_ref, db_ref, dpad, xpad, accw, accb = refs
        zeros = jnp.zeros((CONV_PAD, LANES), f32)
        for pad in (dpad, xpad):
            pad[0:CONV_PAD, :] = zeros
            pad[CONV_PAD + L:CONV_PAD + L + CONV_PAD, :] = zeros
        xpad[CONV_PAD:CONV_PAD + L, :] = x_ref[...]
        if act:
            p = pre_ref[...]
            s = _sigmoid(p)
            dpad[CONV_PAD:CONV_PAD + L, :] = dy_ref[...].astype(f32) * (s * (1.0 + p * (1.0 - s)))
        else:
            dpad[CONV_PAD:CONV_PAD + L, :] = dy_ref[...].astype(f32)
        accw[...] = jnp.zeros_like(accw)
        accb[...] = jnp.zeros_like(accb)

        def chunk(ci, carry):
            r0 = pl.multiple_of(ci * R, R)
            acc = jnp.zeros((R, LANES), f32)
            for k in range(K):
                acc = acc + w_ref[k:k + 1, :] * dpad[pl.ds(r0 + (CONV_PAD + K // 2 - k), R), :]
            dx_ref[pl.ds(r0, R), :] = acc.astype(out_dtype)
            d = dpad[pl.ds(r0 + CONV_PAD, R), :]
            accb[...] += jnp.sum(d.reshape(R // 8, 8, LANES), axis=0)
            for k in range(K):
                prod = d * xpad[pl.ds(r0 + (lo + k), R), :]
                accw[k] += jnp.sum(prod.reshape(R // 8, 8, LANES), axis=0)
            return carry

        lax.fori_loop(0, L // R, chunk, 0)
        dw_ref[...] = jnp.sum(accw[...], axis=1)
        db_ref[...] = jnp.sum(accb[...], axis=0, keepdims=True)

    col = pl.BlockSpec((L, LANES), lambda j: (0, j))
    in_specs = [col] + ([col] if act else []) + [pl.BlockSpec((L, LANES), lambda j: (0, xbase + j)), pl.BlockSpec((K, LANES), lambda j: (0, j))]
    args = (dy,) + ((pre,) if act else ()) + (x_arr, w)
    return pl.pallas_call(
        body, name=name, grid=(C // LANES,), in_specs=in_specs,
        out_specs=[col, pl.BlockSpec((K, LANES), lambda j: (0, j)), pl.BlockSpec((1, LANES), lambda j: (0, j))],
        out_shape=[jax.ShapeDtypeStruct((L, C), out_dtype), jax.ShapeDtypeStruct((K, C), f32), jax.ShapeDtypeStruct((1, C), f32)],
        scratch_shapes=[pltpu.VMEM((L + 2 * CONV_PAD, LANES), f32), pltpu.VMEM((L + 2 * CONV_PAD, LANES), f32),
                        pltpu.VMEM((K, 8, LANES), f32), pltpu.VMEM((8, LANES), f32)],
        compiler_params=_params(("parallel",)))(*args)


def _ssd_chunk(X, Bm, Cm, dtc, csc, csr, S, lane_off, dirn):
    Qn = X.shape[0]
    ci = lax.broadcasted_iota(jnp.int32, (LANES, GROUP_W), 0)
    hh = lax.shift_right_logical(lax.broadcasted_iota(jnp.int32, (LANES, GROUP_W), 1), 6)
    E = (ci == lane_off + hh).astype(f32)
    dtx = jnp.dot(dtc, E, precision=HIGHEST, preferred_element_type=f32)
    csx = jnp.dot(csc, E, precision=HIGHEST, preferred_element_type=f32)
    Xd = X * dtx
    edge = Qn - 1 if dirn == 0 else 0
    cs_edge = csx[edge:edge + 1, :]
    li = lax.broadcasted_iota(jnp.int32, (Qn, Qn), 0)
    si = lax.broadcasted_iota(jnp.int32, (Qn, Qn), 1)
    mask = (si <= li) if dirn == 0 else (si >= li)
    Cb = Cm.astype(bf16)
    Bb = Bm.astype(bf16)
    G = lax.dot_general(Cb, Bb, _DN["nt"], preferred_element_type=f32)
    Y = jnp.exp(csx) * jnp.dot(Cb, S.astype(bf16), preferred_element_type=f32)
    head = lax.shift_right_logical(lax.broadcasted_iota(jnp.int32, (1, GROUP_W), 1), 6)
    for j in range(HEADS_PER_GROUP):
        col = csx[:, HEAD_P * j:HEAD_P * j + 1]
        row = csr[j:j + 1, :]
        Lm = jnp.exp(jnp.where(mask, col - row, NEG_BIG))
        Wj = (G * Lm).astype(bf16)
        Xj = jnp.where(head == j, Xd, 0.0).astype(bf16)
        Y = Y + jnp.dot(Wj, Xj, preferred_element_type=f32)
    S_new = S * jnp.exp(cs_edge) + lax.dot_general(Bb, (Xd * jnp.exp(cs_edge - csx)).astype(bf16), _DN["tn"], preferred_element_type=f32)
    return Y, S_new


def _ssd_specs(D, G, nchunk, dirn, descending):
    def cidx(c):
        return nchunk - 1 - c if descending else c

    x_spec = pl.BlockSpec((CHUNK, GROUP_W), lambda c, g: (cidx(c), g))
    b_spec = pl.BlockSpec((CHUNK, N_STATE), lambda c, g: (cidx(c), D // N_STATE + g))
    c_spec = pl.BlockSpec((CHUNK, N_STATE), lambda c, g: (cidx(c), D // N_STATE + G + g))
    n_spec = pl.BlockSpec((CHUNK, N_STATE), lambda c, g: (cidx(c), g))
    lane_spec = pl.BlockSpec((CHUNK, LANES), lambda c, g: (cidx(c), 0))
    csr_spec = pl.BlockSpec((None, 8, CHUNK), lambda c, g: (dirn * G + g, 0, cidx(c)))
    dcsr_spec = pl.BlockSpec((None, 8, CHUNK), lambda c, g: (g, 0, cidx(c)))
    s_spec = pl.BlockSpec((None, None, N_STATE, GROUP_W), lambda c, g: (g, cidx(c), 0, 0))
    return x_spec, b_spec, c_spec, n_spec, lane_spec, csr_spec, dcsr_spec, s_spec


def ssd_fwd(act, dt128, cs128, csr, dirn, n_heads, name):
    L = act.shape[0]
    D = act.shape[1] // 2
    G = D // GROUP_W
    nchunk = L // CHUNK
    x_spec, b_spec, c_spec, _, lane_spec, csr_spec, _, s_spec = _ssd_specs(D, G, nchunk, dirn, dirn == 1)

    def body(x_ref, b_ref, c_ref, dt_ref, cs_ref, csr_ref, y_ref, ssave_ref, S):
        c, g = pl.program_id(0), pl.program_id(1)

        @pl.when(c == 0)
        def _():
            S[g] = jnp.zeros((N_STATE, GROUP_W), f32)

        s_in = S[g]
        ssave_ref[...] = s_in
        y, s_new = _ssd_chunk(x_ref[...], b_ref[...], c_ref[...], dt_ref[...], cs_ref[...], csr_ref[...], s_in,
                              dirn * n_heads + HEADS_PER_GROUP * g, dirn)
        y_ref[...] = y
        S[g] = s_new

    return pl.pallas_call(
        body, name=name, grid=(nchunk, G),
        in_specs=[x_spec, b_spec, c_spec, lane_spec, lane_spec, csr_spec],
        out_specs=[x_spec, s_spec],
        out_shape=[jax.ShapeDtypeStruct((L, D), f32), jax.ShapeDtypeStruct((G, nchunk, N_STATE, GROUP_W), f32)],
        scratch_shapes=[pltpu.VMEM((G, N_STATE, GROUP_W), f32)],
        compiler_params=_params(("arbitrary", "arbitrary")))(act, act, act, dt128, cs128, csr)


def ssd_bwd(act, dt128, cs128, csr, ssave, dy, dirn, n_heads, name):
    L = act.shape[0]
    D = act.shape[1] // 2
    G = D // GROUP_W
    nchunk = L // CHUNK
    x_spec, b_spec, c_spec, n_spec, lane_spec, csr_in, dcsr_spec, s_spec = _ssd_specs(D, G, nchunk, dirn, dirn == 0)

    def body(x_ref, b_ref, c_ref, dt_ref, cs_ref, csr_ref, s_ref, dy_ref, dx_ref, db_ref, dc_ref, ddt_ref, dcs_ref, dcsr_ref, dS):
        c, g = pl.program_id(0), pl.program_id(1)

        @pl.when(c == 0)
        def _():
            dS[g] = jnp.zeros((N_STATE, GROUP_W), f32)

        lane_off = dirn * n_heads + HEADS_PER_GROUP * g
        _, vjp = jax.vjp(lambda X, Bm, Cm, dtc, csc, csr_, S: _ssd_chunk(X, Bm, Cm, dtc, csc, csr_, S, lane_off, dirn),
                         x_ref[...], b_ref[...], c_ref[...], dt_ref[...], cs_ref[...], csr_ref[...], s_ref[...])
        dX, dB, dC, ddt, dcs, dcsr, dS_in = vjp((dy_ref[...], dS[g]))
        dx_ref[...] = dX
        db_ref[...] = dB
        dc_ref[...] = dC
        dcsr_ref[...] = dcsr
        dS[g] = dS_in

        @pl.when(g == 0)
        def _():
            ddt_ref[...] = jnp.zeros_like(ddt_ref)
            dcs_ref[...] = jnp.zeros_like(dcs_ref)

        ddt_ref[...] += ddt
        dcs_ref[...] += dcs

    return pl.pallas_call(
        body, name=name, grid=(nchunk, G),
        in_specs=[x_spec, b_spec, c_spec, lane_spec, lane_spec, csr_in, s_spec, x_spec],
        out_specs=[x_spec, n_spec, n_spec, lane_spec, lane_spec, dcsr_spec],
        out_shape=[jax.ShapeDtypeStruct((L, D), f32), jax.ShapeDtypeStruct((L, G * N_STATE), f32), jax.ShapeDtypeStruct((L, G * N_STATE), f32),
                   jax.ShapeDtypeStruct((L, LANES), f32), jax.ShapeDtypeStruct((L, LANES), f32), jax.ShapeDtypeStruct((G, 8, L), f32)],
        scratch_shapes=[pltpu.VMEM((G, N_STATE, GROUP_W), f32)],
        compiler_params=_params(("arbitrary", "arbitrary")))(act, act, act, dt128, cs128, csr, ssave, dy)


def _csr_from_cs(cs128, n_heads):
    L = cs128.shape[0]
    t = cs128[:, :2 * n_heads].T.reshape(2 * n_heads // HEADS_PER_GROUP, HEADS_PER_GROUP, L)
    return jnp.pad(t, ((0, 0), (0, 8 - HEADS_PER_GROUP), (0, 0)))


def _cs_from_dcsr(dcsr_f, dcsr_b, n_heads):
    L = dcsr_f.shape[-1]
    t = jnp.concatenate([dcsr_f, dcsr_b], axis=0)[:, :HEADS_PER_GROUP, :].reshape(2 * n_heads, L).T
    return jnp.pad(t, ((0, 0), (0, LANES - 2 * n_heads)))


def na_bias_table(rpb):
    nh = rpb.shape[0]
    cols = np.arange(GRID_W)
    col_start = np.clip(cols - NA_WC // 2, 0, GRID_W - NA_WC)
    col_valid = (cols[None, :] >= col_start[:, None]) & (cols[None, :] < col_start[:, None] + NA_WC)
    col_off = np.clip(cols[None, :] - cols[:, None], -(NA_WC - 1), NA_WC - 1) + NA_WC - 1
    rm = np.zeros((NA_WR, NA_WR, 2 * NA_WR - 1), np.float32)
    for d in range(NA_WR):
        for i in range(NA_WR):
            rm[d, i, i - d + NA_WR - 1] = 1.0
    cm = np.zeros((GRID_W, GRID_W, 2 * NA_WC - 1), np.float32)
    cm[np.arange(GRID_W)[:, None], np.arange(GRID_W)[None, :], col_off] = 1.0
    bt = jnp.einsum("hrc,dir,qkc->hdqik", rpb, jnp.asarray(rm), jnp.asarray(cm), precision=HIGHEST)
    bt = jnp.where(jnp.asarray(col_valid)[None, None, :, None, :], bt, NEG_BIG)
    return bt.reshape(nh, NA_WR, GRID_W, NA_WR * GRID_W)


def _na_window(r, rows):
    rs = jnp.clip(r - NA_WR // 2, 0, rows - NA_WR)
    return rs, r - rs


def _na_probs(q, kw, bias):
    s = lax.dot_general(q.astype(bf16), kw.astype(bf16), _DN["nt"], preferred_element_type=f32) * (NA_D ** -0.5) + bias
    m = jnp.max(s, axis=-1, keepdims=True)
    p = jnp.exp(s - m)
    return p / jnp.sum(p, axis=-1, keepdims=True)


def na_fwd(um, qbase, kbase, vbase, bt, name):
    L = um.shape[0]
    nh = bt.shape[0]
    rows = L // GRID_W
    WK = NA_WR * GRID_W

    RB = _pick(rows, (NA_ROWS_PER_STEP, 2, 1))

    def body(q_ref, k_ref, v_ref, bt_ref, o_ref):
        for rb in range(RB):
            rs, delta = _na_window(pl.program_id(1) * RB + rb, rows)
            t0 = pl.multiple_of(rs * GRID_W, GRID_W)
            qrows = pl.ds(rb * GRID_W, GRID_W)
            p = _na_probs(q_ref[qrows, :], k_ref[pl.ds(t0, WK), :], bt_ref[delta])
            o_ref[qrows, :] = jnp.dot(p.astype(bf16), v_ref[pl.ds(t0, WK), :].astype(bf16), preferred_element_type=f32)

    kv = lambda base: pl.BlockSpec((L, NA_D), lambda h, r: (0, base + h))
    return pl.pallas_call(
        body, name=name, grid=(nh, rows // RB),
        in_specs=[pl.BlockSpec((RB * GRID_W, NA_D), lambda h, r: (r, qbase + h)), kv(kbase), kv(vbase),
                  pl.BlockSpec((None, NA_WR, GRID_W, WK), lambda h, r: (h, 0, 0, 0))],
        out_specs=pl.BlockSpec((RB * GRID_W, NA_D), lambda h, r: (r, h)),
        out_shape=jax.ShapeDtypeStruct((L, nh * NA_D), f32),
        compiler_params=_params(("parallel", "arbitrary")))(um, um, um, bt)


def na_bwd(um, qbase, kbase, vbase, bt, do, name):
    L = um.shape[0]
    nh = bt.shape[0]
    rows = L // GRID_W
    WK = NA_WR * GRID_W
    scale = NA_D ** -0.5

    RB = _pick(rows, (NA_ROWS_PER_STEP, 2, 1))

    def body(q_ref, k_ref, v_ref, bt_ref, do_ref, dq_ref, dk_ref, dv_ref, dbt_ref):
        @pl.when(pl.program_id(1) == 0)
        def _():
            dk_ref[...] = jnp.zeros_like(dk_ref)
            dv_ref[...] = jnp.zeros_like(dv_ref)
            dbt_ref[...] = jnp.zeros_like(dbt_ref)

        for rb in range(RB):
            rs, delta = _na_window(pl.program_id(1) * RB + rb, rows)
            t0 = pl.multiple_of(rs * GRID_W, GRID_W)
            qrows = pl.ds(rb * GRID_W, GRID_W)
            q = q_ref[qrows, :]
            kw = k_ref[pl.ds(t0, WK), :]
            vw = v_ref[pl.ds(t0, WK), :]
            do_ = do_ref[qrows, :].astype(bf16)
            p = _na_probs(q, kw, bt_ref[delta])
            dp = lax.dot_general(do_, vw.astype(bf16), _DN["nt"], preferred_element_type=f32)
            ds = p * (dp - jnp.sum(dp * p, axis=-1, keepdims=True))
            dbt_ref[delta] += ds
            dsb = ds.astype(bf16)
            dq_ref[qrows, :] = jnp.dot(dsb, kw.astype(bf16), preferred_element_type=f32) * scale
            dk_ref[pl.ds(t0, WK), :] += lax.dot_general(dsb, q.astype(bf16), _DN["tn"], preferred_element_type=f32) * scale
            dv_ref[pl.ds(t0, WK), :] += lax.dot_general(p.astype(bf16), do_, _DN["tn"], preferred_element_type=f32)

    kv = lambda base: pl.BlockSpec((L, NA_D), lambda h, r: (0, base + h))
    qs = lambda base: pl.BlockSpec((RB * GRID_W, NA_D), lambda h, r: (r, base + h))
    bts = pl.BlockSpec((None, NA_WR, GRID_W, WK), lambda h, r: (h, 0, 0, 0))
    big = jax.ShapeDtypeStruct((L, nh * NA_D), f32)
    return pl.pallas_call(
        body, name=name, grid=(nh, rows // RB),
        in_specs=[qs(qbase), kv(kbase), kv(vbase), bts, qs(0)],
        out_specs=[qs(0), kv(0), kv(0), bts],
        out_shape=[big, big, big, jax.ShapeDtypeStruct(bt.shape, f32)],
        compiler_params=_params(("parallel", "arbitrary")))(um, um, um, bt, do)


_HBM = pl.BlockSpec(memory_space=pltpu.HBM)


def all_gather(xs, name):
    n = len(xs)

    def body(*refs):
        x_refs, out_refs = refs[:n], refs[n:2 * n]
        send_sems, recv_sems, local_sems = refs[2 * n:]
        mx, my, mc = lax.axis_index("x"), lax.axis_index("y"), lax.axis_index("c")
        me, sibling = (mx, my, mc), (mx, my, 1 - mc)
        chips = [(1 - mx, my), (mx, 1 - my), (1 - mx, 1 - my)]

        def slab(t, px, py, pc):
            return out_refs[t].at[4 * px + 2 * py + pc]

        def copy(t, k, block, to, src=None):
            return pltpu.make_async_remote_copy(
                src_ref=slab(t, *block) if src is None else src, dst_ref=slab(t, *block),
                send_sem=send_sems.at[7 * t + k], recv_sem=recv_sems.at[7 * t + k], device_id=to, device_id_type=pl.DeviceIdType.MESH)

        mine = [pltpu.make_async_copy(x_refs[t], slab(t, *me), local_sems.at[t]) for t in range(n)]
        for cp in mine:
            cp.start()
        first = []
        for t in range(n):
            first.append(copy(t, 0, me, sibling, src=x_refs[t]))
            first += [copy(t, 1 + j, me, (*chip, mc), src=x_refs[t]) for j, chip in enumerate(chips)]
        for cp in first:
            cp.start()
        passed = []
        for t in range(n):
            for j, chip in enumerate(chips):
                copy(t, 1 + j, (*chip, mc), me).wait_recv()
                fwd = copy(t, 4 + j, (*chip, mc), sibling)
                fwd.start()
                passed.append(fwd)
        for t in range(n):
            copy(t, 0, sibling, me).wait_recv()
            for j, chip in enumerate(chips):
                copy(t, 4 + j, (*chip, 1 - mc), me).wait_recv()
        for cp in first + passed:
            cp.wait_send()
        for cp in mine:
            cp.wait()

    return pl.pallas_call(
        body, name=name, in_specs=[_HBM] * n, out_specs=[_HBM] * n,
        out_shape=[jax.ShapeDtypeStruct((N_DEV,) + x.shape, x.dtype) for x in xs],
        scratch_shapes=[pltpu.SemaphoreType.DMA((7 * n,)), pltpu.SemaphoreType.DMA((7 * n,)), pltpu.SemaphoreType.DMA((n,))])(*xs)


def all_to_all(xs, name):
    n = len(xs)

    def body(*refs):
        x_refs, out_refs = refs[:n], refs[n:2 * n]
        send_sems, recv_sems, local_sems = refs[2 * n:]
        mx, my, mc = lax.axis_index("x"), lax.axis_index("y"), lax.axis_index("c")
        me = 4 * mx + 2 * my + mc
        mine = [pltpu.make_async_copy(x_refs[t].at[me], out_refs[t].at[me], local_sems.at[t]) for t in range(n)]
        for cp in mine:
            cp.start()
        copies = []
        for t in range(n):
            for k in range(1, N_DEV):
                px = 1 - mx if k & 4 else mx
                py = 1 - my if k & 2 else my
                pc = 1 - mc if k & 1 else mc
                peer = 4 * px + 2 * py + pc
                cp = pltpu.make_async_remote_copy(
                    src_ref=x_refs[t].at[peer], dst_ref=out_refs[t].at[me], send_sem=send_sems.at[7 * t + k - 1],
                    recv_sem=recv_sems.at[7 * t + k - 1], device_id=(px, py, pc), device_id_type=pl.DeviceIdType.MESH)
                cp.start()
                copies.append(cp)
        for cp in copies:
            cp.wait()
        for cp in mine:
            cp.wait()

    return pl.pallas_call(
        body, name=name, in_specs=[_HBM] * n, out_specs=[_HBM] * n,
        out_shape=[jax.ShapeDtypeStruct(x.shape, x.dtype) for x in xs],
        scratch_shapes=[pltpu.SemaphoreType.DMA((7 * n,)), pltpu.SemaphoreType.DMA((7 * n,)), pltpu.SemaphoreType.DMA((n,))])(*xs)


def adamw(w, slabs, m, v, own, name):
    R, C = w.shape
    tr = _pick(R, tuple(t for t in (256, 128, 64, 32, 16, 8) if t * C * 4 <= ADAMW_TILE_BYTES) or (8,))
    c1 = 1.0 - ADAM_B1 ** ADAM_STEP
    c2 = 1.0 - ADAM_B2 ** ADAM_STEP

    def body(*refs):
        if own is not None:
            me_ref, w_ref, s_ref, m_ref, v_ref, own_ref, g_ref, d_ref, nm_ref, nv_ref = refs
        else:
            w_ref, s_ref, m_ref, v_ref, g_ref, d_ref, nm_ref, nv_ref = refs
        g = None
        for s in range(N_DEV):
            t = s_ref[s].astype(f32)
            if own is not None:
                t = jnp.where(me_ref[0] == s, own_ref[...], t)
            g = t if g is None else g + t
        wv = w_ref[...]
        mn = ADAM_B1 * m_ref[...] + (1.0 - ADAM_B1) * g
        vn = ADAM_B2 * v_ref[...] + (1.0 - ADAM_B2) * jnp.square(g)
        m_hat = mn / c1
        v_hat = vn / c2
        g_ref[...] = g
        d_ref[...] = -ADAM_LR * (m_hat / (jnp.sqrt(v_hat) + ADAM_EPS) + ADAM_WD * wv)
        nm_ref[...] = mn
        nv_ref[...] = vn

    out_shape = [jax.ShapeDtypeStruct((R, C), f32)] * 4
    if own is not None:
        me, part = own
        blk = pl.BlockSpec((tr, C), lambda i, me_: (i, 0))
        gs = pltpu.PrefetchScalarGridSpec(
            num_scalar_prefetch=1, grid=(R // tr,),
            in_specs=[blk, pl.BlockSpec((N_DEV, tr, C), lambda i, me_: (0, i, 0)), blk, blk, blk],
            out_specs=[blk] * 4)
        return pl.pallas_call(body, name=name, grid_spec=gs, out_shape=out_shape, compiler_params=_params(("parallel",)))(
            me.reshape(1).astype(jnp.int32), w, slabs, m, v, part)
    blk = pl.BlockSpec((tr, C), lambda i: (i, 0))
    return pl.pallas_call(
        body, name=name, grid=(R // tr,), in_specs=[blk, pl.BlockSpec((N_DEV, tr, C), lambda i: (0, i, 0)), blk, blk],
        out_specs=[blk] * 4, out_shape=out_shape, compiler_params=_params(("parallel",)))(w, slabs, m, v)


def _packed_rows(n, cols):
    rows = -(-n // cols)
    mult = 256 if rows > 256 else 16
    return -(-rows // mult) * mult


def _pack(arrs, cols, dtype, lead=()):
    nl = len(lead)
    flat = jnp.concatenate([a.reshape(lead + (-1,)).astype(dtype) for a in arrs], axis=-1)
    n = flat.shape[-1]
    rows = _packed_rows(n, cols)
    flat = jnp.pad(flat, [(0, 0)] * nl + [(0, rows * cols - n)])
    return flat.reshape(lead + (rows, cols))


def _unpack(packed, shapes, lead=()):
    flat = packed.reshape(lead + (-1,))
    out, off = [], 0
    for s in shapes:
        n = int(np.prod(s))
        out.append(flat[..., off:off + n].reshape(lead + tuple(s)))
        off += n
    return out


def _to_slabs(full, axis):
    s = full.shape
    r = full.reshape(s[:axis] + (N_DEV, s[axis] // N_DEV) + s[axis + 1:])
    return jnp.moveaxis(r, axis, 0)


def _from_slabs(slabs, axis):
    r = jnp.moveaxis(slabs, 0, axis)
    s = r.shape
    return r.reshape(s[:axis] + (s[axis] * s[axis + 1],) + s[axis + 2:])


BIG = [("ev_w_in", 2), ("ev_w_out", 1), ("od_w_in", 2), ("od_w_out", 1), ("ple_w_gate", 1), ("ple_w_proj", 2)]
SMALL = [("ev_conv_w", 2), ("od_norm_w", 1), ("od_dw_w", 2), ("od_dw_b", 1), ("od_ln_w", 1), ("od_ln_b", 1)]
REPL = ["ev_norm_w", "ev_conv_b", "ev_dt_bias_f", "ev_dt_bias_b", "ev_a_log_f", "ev_a_log_b", "ev_d_skip", "ev_gnorm_w", "ev_rpb",
        "ple_norm_w", "final_norm_w"]
WEIGHTS = ["ev_norm_w", "ev_w_in", "ev_conv_w", "ev_conv_b", "ev_dt_bias_f", "ev_dt_bias_b", "ev_a_log_f", "ev_a_log_b", "ev_d_skip",
           "ev_gnorm_w", "ev_rpb", "ev_w_out", "od_norm_w", "od_w_in", "od_dw_w", "od_dw_b", "od_ln_w", "od_ln_b", "od_w_out",
           "ple_norm_w", "ple_w_gate", "ple_w_proj", "final_norm_w"]
BIG_COLS = 1024
SMALL_COLS = 128


def _row_tile(L, width):
    return _pick(L, (256, 128)) if width <= 2048 else _pick(L, (128,))


def _ple_fwd(h1, p_i, nw, wg, wp, tag):
    L, D = h1.shape
    T = _row_tile(L, D)
    (hn2,) = rowop_fwd(_rms_fn, "ple_rms_" + tag, [(h1, D, 0)], [(nw, D, 0)], [(D, bf16)], T)
    gpre = mm(hn2, wg, "nn", "ple_gate_mm_" + tag)
    pp = mm(p_i, wp, "nn", "ple_proj_mm_" + tag)
    (h2,) = rowop_fwd(_ple_fwd_fn, "ple_comb_" + tag, [(h1, D, 0), (gpre, D, 0), (pp, D, 0)], [], [(D, f32)], T)
    return h2, (h1, p_i, nw, wg, wp, hn2, gpre, pp)


def _ple_bwd(res, dh2, tag):
    h1, p_i, nw, wg, wp, hn2, gpre, pp = res
    L, D = h1.shape
    T = _row_tile(L, D)
    dgpre, dpp = rowop_bwd(_ple_fn, "ple_comb_bwd_" + tag, [(gpre, D, 0), (pp, D, 0)], [], [(dh2, D, 0)], [bf16, bf16], [], T)
    dwg = mm(hn2, dgpre, "tn", "ple_gate_dw_" + tag)
    dwp = mm(p_i, dpp, "tn", "ple_proj_dw_" + tag)
    dhn2 = mm(dgpre, wg, "nt", "ple_gate_dx_" + tag)
    dh1, dnw = rowop_bwd(_rms_fn, "ple_rms_bwd_" + tag, [(h1, D, 0)], [(nw, D, 0)], [(dhn2, D, 0)], [f32], [True], T, add=(dh2, D, 0))
    return dh1, dnw, dwg, dwp


def _even_fwd(h, P, tag):
    L, D = h.shape
    H = D // HEAD_P
    T = _row_tile(L, D)
    DB = D // LANES
    (hn,) = rowop_fwd(_rms_fn, "ev_rms_" + tag, [(h, D, 0)], [(P["norm_w"], D, 0)], [(D, bf16)], T)
    um = mm(hn, P["w_main"], "nn", "ev_in_mm_" + tag)
    udt = mm(hn, P["w_dt"], "nn", "ev_dt_mm_" + tag)
    pre, act = conv_fwd(um, DB, 2 * D, P["conv_w"], P["conv_b"], K_SSM, True, "ev_conv_" + tag)
    prep = _make_prep_fn(H)
    dt128, cs128 = rowop_fwd(prep, "ev_prep_" + tag, [(udt, LANES, 0)], [(P["bias128"], LANES, 0), (P["alog128"], LANES, 0)],
                             [(LANES, f32), (LANES, f32)], CHUNK)
    csr = _csr_from_cs(cs128, H)
    yf, sf = ssd_fwd(act, dt128, cs128, csr, 0, H, "ev_ssd_f_" + tag)
    yb, sb = ssd_fwd(act, dt128, cs128, csr, 1, H, "ev_ssd_b_" + tag)
    G = D // GROUP_W
    (yssd,) = rowop_fwd(_post_fn, "ev_post_" + tag, [(yf, GROUP_W, 0), (yb, GROUP_W, 0), (act, GROUP_W, 0), (um, GROUP_W, 0)],
                        [(P["dskipx"], GROUP_W, 0), (P["gnorm_w"], GROUP_W, 0)], [(GROUP_W, bf16)], T, ncb=G)
    bt = na_bias_table(P["rpb"])
    o = na_fwd(um, 3 * DB, 4 * DB, 5 * DB, bt, "ev_na_" + tag)
    (yna,) = rowop_fwd(_gate_fn, "ev_nagate_" + tag, [(o, D, 0), (um, D, 6)], [], [(D, bf16)], T)
    cat = jnp.concatenate([yssd, yna], axis=-1)
    h1 = mm(cat, P["w_out"], "nn", "ev_out_mm_" + tag, add=h)
    return h1, (h, hn, um, udt, pre, act, dt128, cs128, csr, yf, yb, sf, sb, bt, o, cat)


def _even_bwd(P, res, dh1, tag):
    h, hn, um, udt, pre, act, dt128, cs128, csr, yf, yb, sf, sb, bt, o, cat = res
    L, D = h.shape
    H = D // HEAD_P
    G = D // GROUP_W
    T = _row_tile(L, D)
    DB = D // LANES
    g = {}
    dcat = mm(dh1, P["w_out"], "nt", "ev_out_dx_" + tag)
    g["w_out"] = mm(cat, dh1, "tn", "ev_out_dw_" + tag)
    do, dg = rowop_bwd(_gate_fn, "ev_nagate_bwd_" + tag, [(o, D, 0), (um, D, 6)], [], [(dcat, D, 1)], [f32, bf16], [], T)
    dq, dk, dv, dbt = na_bwd(um, 3 * DB, 4 * DB, 5 * DB, bt, do, "ev_na_bwd_" + tag)
    _, rpb_vjp = jax.vjp(na_bias_table, P["rpb"])
    (g["rpb"],) = rpb_vjp(dbt)
    dy, dxs_a, dz, ddsk, dgn = rowop_bwd(
        _post_fn, "ev_post_bwd_" + tag, [(yf, GROUP_W, 0), (yb, GROUP_W, 0), (act, GROUP_W, 0), (um, GROUP_W, 0)],
        [(P["dskipx"], GROUP_W, 0), (P["gnorm_w"], GROUP_W, 0)], [(dcat, GROUP_W, 0)], [f32, None, f32, bf16], [True, True], T, ncb=G)
    g["d_skip"] = ddsk.reshape(H, HEAD_P).sum(axis=-1)
    g["gnorm_w"] = dgn.reshape(D)
    dxf, dbf, dcf, ddtf, dcsf, dcsrf = ssd_bwd(act, dt128, cs128, csr, sf, dy, 0, H, "ev_ssd_f_bwd_" + tag)
    dxb, dbb, dcb, ddtb, dcsb, dcsrb = ssd_bwd(act, dt128, cs128, csr, sb, dy, 1, H, "ev_ssd_b_bwd_" + tag)
    dact = jnp.concatenate([dxf + dxb + dxs_a, dbf + dbb, dcf + dcb], axis=-1)
    ddt128 = ddtf + ddtb
    dcs128 = dcsf + dcsb + _cs_from_dcsr(dcsrf, dcsrb, H)
    prep = _make_prep_fn(H)
    dudt, dbias, dalog = rowop_bwd(prep, "ev_prep_bwd_" + tag, [(udt, LANES, 0)], [(P["bias128"], LANES, 0), (P["alog128"], LANES, 0)],
                                   [(ddt128, LANES, 0), (dcs128, LANES, 0)], [bf16], [True, True], CHUNK)
    g["dt_bias_f"], g["dt_bias_b"] = dbias[0, :H], dbias[0, H:2 * H]
    g["a_log_f"], g["a_log_b"] = dalog[0, :H], dalog[0, H:2 * H]
    dxbc, g["conv_w"], dcb_ = conv_bwd(dact, um, DB, 2 * D, P["conv_w"], K_SSM, pre, bf16, "ev_conv_bwd_" + tag)
    g["conv_b"] = dcb_.reshape(2 * D)
    dum = jnp.concatenate([dz, dxbc, dq.astype(bf16), dk.astype(bf16), dv.astype(bf16), dg], axis=-1)
    dhn = mm(dum, P["w_main"], "nt", "ev_in_dx_" + tag)
    dhn = mm(dudt, P["w_dt"], "nt", "ev_dt_dx_" + tag, add=dhn)
    g["w_main"] = mm(hn, dum, "tn", "ev_in_dw_" + tag)
    g["w_dt"] = mm(hn, dudt, "tn", "ev_dt_dw_" + tag)
    dh, dnw = rowop_bwd(_rms_fn, "ev_rms_bwd_" + tag, [(h, D, 0)], [(P["norm_w"], D, 0)], [(dhn, D, 0)], [f32], [True], T, add=(dh1, D, 0))
    g["norm_w"] = dnw.reshape(D)
    return dh, g


def _odd_fwd(h, P, tag):
    L, D = h.shape
    C = 2 * D
    T = _row_tile(L, D)
    (hn,) = rowop_fwd(_rms_fn, "od_rms_" + tag, [(h, D, 0)], [(P["norm_w"], D, 0)], [(D, bf16)], T)
    u = mm(hn, P["w_in"], "nn", "od_in_mm_" + tag)
    (vglu,) = rowop_fwd(_glu_fn, "od_glu_" + tag, [(u, D, 0), (u, D, 2)], [], [(D, f32)], T, ncb=2)
    (vc,) = conv_fwd(vglu, 0, C, P["dw_w"], P["dw_b"], K_CONV, False, "od_conv_" + tag)
    TL = _row_tile(L, C)
    (t,) = rowop_fwd(_lngate_fn, "od_lngate_" + tag, [(vc, C, 0), (u, C, 2)], [(P["ln_w"], C, 0), (P["ln_b"], C, 0)], [(C, bf16)], TL)
    h1 = mm(t, P["w_out"], "nn", "od_out_mm_" + tag, add=h)
    return h1, (h, hn, u, vglu, vc, t)


def _odd_bwd(P, res, dh1, tag):
    h, hn, u, vglu, vc, t = res
    L, D = h.shape
    C = 2 * D
    T = _row_tile(L, D)
    TL = _row_tile(L, C)
    g = {}
    dt_ = mm(dh1, P["w_out"], "nt", "od_out_dx_" + tag)
    g["w_out"] = mm(t, dh1, "tn", "od_out_dw_" + tag)
    dvc, dg, dlnw, dlnb = rowop_bwd(_lngate_fn, "od_lngate_bwd_" + tag, [(vc, C, 0), (u, C, 2)], [(P["ln_w"], C, 0), (P["ln_b"], C, 0)],
                                    [(dt_, C, 0)], [f32, bf16], [True, True], TL)
    g["ln_w"], g["ln_b"] = dlnw.reshape(C), dlnb.reshape(C)
    dvglu, g["dw_w"], ddwb = conv_bwd(dvc, vglu, 0, C, P["dw_w"], K_CONV, None, f32, "od_conv_bwd_" + tag)
    g["dw_b"] = ddwb.reshape(C)
    da, dag = rowop_bwd(_glu_fn, "od_glu_bwd_" + tag, [(u, D, 0), (u, D, 2)], [], [(dvglu, D, 0)], [bf16, bf16], [], T, ncb=2)
    du = jnp.concatenate([da, dag, dg], axis=-1)
    dhn = mm(du, P["w_in"], "nt", "od_in_dx_" + tag)
    g["w_in"] = mm(hn, du, "tn", "od_in_dw_" + tag)
    dh, dnw = rowop_bwd(_rms_fn, "od_rms_bwd_" + tag, [(h, D, 0)], [(P["norm_w"], D, 0)], [(dhn, D, 0)], [f32], [True], T, add=(dh1, D, 0))
    g["norm_w"] = dnw.reshape(D)
    return dh, g


def kernel(x, p, ev_norm_w, ev_w_in, ev_conv_w, ev_conv_b, ev_dt_bias_f, ev_dt_bias_b, ev_a_log_f, ev_a_log_b, ev_d_skip, ev_gnorm_w, ev_rpb, ev_w_out, od_norm_w, od_w_in, od_dw_w, od_dw_b, od_ln_w, od_ln_b, od_w_out, ple_norm_w, ple_w_gate, ple_w_proj, final_norm_w, loss_target, m_ev_norm_w, m_ev_w_in, m_ev_conv_w, m_ev_conv_b, m_ev_dt_bias_f, m_ev_dt_bias_b, m_ev_a_log_f, m_ev_a_log_b, m_ev_d_skip, m_ev_gnorm_w, m_ev_rpb, m_ev_w_out, m_od_norm_w, m_od_w_in, m_od_dw_w, m_od_dw_b, m_od_ln_w, m_od_ln_b, m_od_w_out, m_ple_norm_w, m_ple_w_gate, m_ple_w_proj, m_final_norm_w, v_ev_norm_w, v_ev_w_in, v_ev_conv_w, v_ev_conv_b, v_ev_dt_bias_f, v_ev_dt_bias_b, v_ev_a_log_f, v_ev_a_log_b, v_ev_d_skip, v_ev_gnorm_w, v_ev_rpb, v_ev_w_out, v_od_norm_w, v_od_w_in, v_od_dw_w, v_od_dw_b, v_od_ln_w, v_od_ln_b, v_od_w_out, v_ple_norm_w, v_ple_w_gate, v_ple_w_proj, v_final_norm_w):
    W = dict(ev_norm_w=ev_norm_w, ev_w_in=ev_w_in, ev_conv_w=ev_conv_w, ev_conv_b=ev_conv_b, ev_dt_bias_f=ev_dt_bias_f,
             ev_dt_bias_b=ev_dt_bias_b, ev_a_log_f=ev_a_log_f, ev_a_log_b=ev_a_log_b, ev_d_skip=ev_d_skip, ev_gnorm_w=ev_gnorm_w,
             ev_rpb=ev_rpb, ev_w_out=ev_w_out, od_norm_w=od_norm_w, od_w_in=od_w_in, od_dw_w=od_dw_w, od_dw_b=od_dw_b,
             od_ln_w=od_ln_w, od_ln_b=od_ln_b, od_w_out=od_w_out, ple_norm_w=ple_norm_w, ple_w_gate=ple_w_gate,
             ple_w_proj=ple_w_proj, final_norm_w=final_norm_w)
    M = dict(ev_norm_w=m_ev_norm_w, ev_w_in=m_ev_w_in, ev_conv_w=m_ev_conv_w, ev_conv_b=m_ev_conv_b, ev_dt_bias_f=m_ev_dt_bias_f,
             ev_dt_bias_b=m_ev_dt_bias_b, ev_a_log_f=m_ev_a_log_f, ev_a_log_b=m_ev_a_log_b, ev_d_skip=m_ev_d_skip,
             ev_gnorm_w=m_ev_gnorm_w, ev_rpb=m_ev_rpb, ev_w_out=m_ev_w_out, od_norm_w=m_od_norm_w, od_w_in=m_od_w_in,
             od_dw_w=m_od_dw_w, od_dw_b=m_od_dw_b, od_ln_w=m_od_ln_w, od_ln_b=m_od_ln_b, od_w_out=m_od_w_out,
             ple_norm_w=m_ple_norm_w, ple_w_gate=m_ple_w_gate, ple_w_proj=m_ple_w_proj, final_norm_w=m_final_norm_w)
    V = dict(ev_norm_w=v_ev_norm_w, ev_w_in=v_ev_w_in, ev_conv_w=v_ev_conv_w, ev_conv_b=v_ev_conv_b, ev_dt_bias_f=v_ev_dt_bias_f,
             ev_dt_bias_b=v_ev_dt_bias_b, ev_a_log_f=v_ev_a_log_f, ev_a_log_b=v_ev_a_log_b, ev_d_skip=v_ev_d_skip,
             ev_gnorm_w=v_ev_gnorm_w, ev_rpb=v_ev_rpb, ev_w_out=v_ev_w_out, od_norm_w=v_od_norm_w, od_w_in=v_od_w_in,
             od_dw_w=v_od_dw_w, od_dw_b=v_od_dw_b, od_ln_w=v_od_ln_w, od_ln_b=v_od_ln_b, od_w_out=v_od_w_out,
             ple_norm_w=v_ple_norm_w, ple_w_gate=v_ple_w_gate, ple_w_proj=v_ple_w_proj, final_norm_w=v_final_norm_w)

    h0 = x[0]
    L, D = h0.shape
    H = D // HEAD_P
    depth = p.shape[0]
    n_even, n_odd = (depth + 1) // 2, depth // 2
    me = 4 * lax.axis_index("x") + 2 * lax.axis_index("y") + lax.axis_index("c")

    small_shapes = [W[n].shape for n, _ in SMALL]
    big_g = all_gather([W[n].astype(bf16) for n, _ in BIG], "gather_big")
    (small_g,) = all_gather([_pack([W[n] for n, _ in SMALL], SMALL_COLS, f32)], "gather_small")
    F = {n: _from_slabs(a, ax) for (n, ax), a in zip(BIG, big_g)}
    F.update({n: _from_slabs(a, ax) for (n, ax), a in zip(SMALL, _unpack(small_g, small_shapes, (N_DEV,)))})

    def even_params(e):
        w_in = F["ev_w_in"][e]
        return dict(
            norm_w=W["ev_norm_w"][e][None], w_main=jnp.concatenate([w_in[:, :3 * D], w_in[:, 3 * D + 2 * H:]], axis=1),
            w_dt=jnp.pad(w_in[:, 3 * D:3 * D + 2 * H], ((0, 0), (0, LANES - 2 * H))),
            conv_w=F["ev_conv_w"][e], conv_b=W["ev_conv_b"][e][None],
            bias128=jnp.pad(jnp.concatenate([W["ev_dt_bias_f"][e], W["ev_dt_bias_b"][e]]), (0, LANES - 2 * H))[None],
            alog128=jnp.pad(jnp.concatenate([W["ev_a_log_f"][e], W["ev_a_log_b"][e]]), (0, LANES - 2 * H))[None],
            dskipx=jnp.repeat(W["ev_d_skip"][e], HEAD_P)[None], gnorm_w=W["ev_gnorm_w"][e][None], rpb=W["ev_rpb"][e],
            w_out=F["ev_w_out"][e])

    def odd_params(e):
        return dict(norm_w=F["od_norm_w"][e][None], w_in=F["od_w_in"][e], dw_w=F["od_dw_w"][e], dw_b=F["od_dw_b"][e][None],
                    ln_w=F["od_ln_w"][e][None], ln_b=F["od_ln_b"][e][None], w_out=F["od_w_out"][e])

    h = h0
    saved = []
    for i in range(depth):
        e = i // 2
        tag = str(i)
        if i % 2 == 0:
            P = even_params(e)
            h1, res = _even_fwd(h, P, tag)
        else:
            P = odd_params(e)
            h1, res = _odd_fwd(h, P, tag)
        h, pres = _ple_fwd(h1, p[i, 0], W["ple_norm_w"][i][None], F["ple_w_gate"][i], F["ple_w_proj"][i], tag)
        saved.append((P, res, pres))
    loss_tile, dh, dfinal = final_loss(h, W["final_norm_w"][None], loss_target[0])

    ev_g = [None] * n_even
    od_g = [None] * n_odd
    ple_g = [None] * depth
    for i in reversed(range(depth)):
        P, res, pres = saved[i]
        tag = str(i)
        dh1, dnw, dwg, dwp = _ple_bwd(pres, dh, tag)
        ple_g[i] = (dnw.reshape(D), dwg, dwp)
        if i % 2 == 0:
            dh, ev_g[i // 2] = _even_bwd(P, res, dh1, tag)
        else:
            dh, od_g[i // 2] = _odd_bwd(P, res, dh1, tag)
    grad_x = dh[None]

    def ev_w_in_grad(g):
        return jnp.concatenate([g["w_main"][:, :3 * D], g["w_dt"][:, :2 * H], g["w_main"][:, 3 * D:]], axis=1)

    full = dict(
        ev_w_in=jnp.stack([ev_w_in_grad(g) for g in ev_g]), ev_w_out=jnp.stack([g["w_out"] for g in ev_g]),
        od_w_in=jnp.stack([g["w_in"] for g in od_g]), od_w_out=jnp.stack([g["w_out"] for g in od_g]),
        ple_w_gate=jnp.stack([g[1] for g in ple_g]), ple_w_proj=jnp.stack([g[2] for g in ple_g]),
        ev_conv_w=jnp.stack([g["conv_w"] for g in ev_g]), od_norm_w=jnp.stack([g["norm_w"] for g in od_g]),
        od_dw_w=jnp.stack([g["dw_w"] for g in od_g]), od_dw_b=jnp.stack([g["dw_b"] for g in od_g]),
        od_ln_w=jnp.stack([g["ln_w"] for g in od_g]), od_ln_b=jnp.stack([g["ln_b"] for g in od_g]),
        ev_norm_w=jnp.stack([g["norm_w"] for g in ev_g]), ev_conv_b=jnp.stack([g["conv_b"] for g in ev_g]),
        ev_dt_bias_f=jnp.stack([g["dt_bias_f"] for g in ev_g]), ev_dt_bias_b=jnp.stack([g["dt_bias_b"] for g in ev_g]),
        ev_a_log_f=jnp.stack([g["a_log_f"] for g in ev_g]), ev_a_log_b=jnp.stack([g["a_log_b"] for g in ev_g]),
        ev_d_skip=jnp.stack([g["d_skip"] for g in ev_g]), ev_gnorm_w=jnp.stack([g["gnorm_w"] for g in ev_g]),
        ev_rpb=jnp.stack([g["rpb"] for g in ev_g]), ple_norm_w=jnp.stack([g[0] for g in ple_g]), final_norm_w=dfinal.reshape(D))

    out = {}
    big_recv = all_to_all([_to_slabs(full[n].astype(bf16), ax) for n, ax in BIG], "scatter_big")
    for (n, ax), recv in zip(BIG, big_recv):
        shard = W[n].shape
        two_d = (int(np.prod(shard[:-1])), shard[-1])
        part = lax.dynamic_slice_in_dim(full[n], me * shard[ax], shard[ax], axis=ax)
        res = adamw(W[n].reshape(two_d), recv.reshape((N_DEV,) + two_d), M[n].reshape(two_d), V[n].reshape(two_d),
                    (me, part.reshape(two_d)), "adamw_" + n)
        for kind, arr in zip(("grad", "delta", "new_m", "new_v"), res):
            out[kind + "_" + n] = arr.reshape(shard)
    small_full = _pack([_to_slabs(full[n], ax) for n, ax in SMALL], SMALL_COLS, f32, (N_DEV,))
    (small_recv,) = all_to_all([small_full], "scatter_small")
    res_small = adamw(_pack([W[n] for n, _ in SMALL], SMALL_COLS, f32), small_recv, _pack([M[n] for n, _ in SMALL], SMALL_COLS, f32),
                      _pack([V[n] for n, _ in SMALL], SMALL_COLS, f32), None, "adamw_small")
    for kind, arr in zip(("grad", "delta", "new_m", "new_v"), res_small):
        for (n, _), a in zip(SMALL, _unpack(arr, small_shapes)):
            out[kind + "_" + n] = a
    repl_shapes = [W[n].shape for n in REPL] + [(1,)]
    zero1 = jnp.zeros((1,), f32)
    repl_part = _pack([full[n] for n in REPL] + [loss_tile[0, :1]], SMALL_COLS, f32)
    (repl_all,) = all_gather([repl_part], "gather_repl")
    res_repl = adamw(_pack([W[n] for n in REPL] + [zero1], SMALL_COLS, f32), repl_all, _pack([M[n] for n in REPL] + [zero1], SMALL_COLS, f32),
                     _pack([V[n] for n in REPL] + [zero1], SMALL_COLS, f32), None, "adamw_repl")
    for kind, arr in zip(("grad", "delta", "new_m", "new_v"), res_repl):
        parts = _unpack(arr, repl_shapes)
        for n, a in zip(REPL, parts[:-1]):
            out[kind + "_" + n] = a
        if kind == "grad":
            loss = parts[-1].reshape(())

    return (loss, grad_x, *[out["grad_" + n] for n in WEIGHTS], *[out["delta_" + n] for n in WEIGHTS],
            *[out["new_m_" + n] for n in WEIGHTS], *[out["new_v_" + n] for n in WEIGHTS])
```

```python
import functools

import numpy as np
import jax
import jax.numpy as jnp
from jax import lax
from jax.experimental import pallas as pl
from jax.experimental.pallas import tpu as pltpu

f32 = jnp.float32
bf16 = jnp.bfloat16
HIGHEST = lax.Precision.HIGHEST

N_DEV = 8
EPS = 1e-6
GRID_W = 64
HEAD_P = 64
HEADS_PER_GROUP = 4
GROUP_W = HEAD_P * HEADS_PER_GROUP
N_STATE = 128
CHUNK = 128
K_SSM = 5
K_CONV = 31
NA_D = 128
NA_WR = 8
NA_WC = 16
PLE_DIM = 256
LANES = 128
CONV_PAD = 16
VMEM_LIMIT = 56 * 1024 * 1024
MM_VMEM_BUDGET = 40 * 1024 * 1024
MM_FULL_K = 2048
NA_ROWS_PER_STEP = 4
ADAMW_TILE_BYTES = 1024 * 1024

ADAM_LR = 0.001
ADAM_B1 = 0.9
ADAM_B2 = 0.999
ADAM_EPS = 1e-08
ADAM_WD = 0.01
ADAM_STEP = 10

NEG_BIG = -1e30


def _params(sem=None):
    if sem is None:
        return pltpu.CompilerParams(vmem_limit_bytes=VMEM_LIMIT)
    return pltpu.CompilerParams(vmem_limit_bytes=VMEM_LIMIT, dimension_semantics=sem)


def _pick(n, prefs):
    for t in prefs:
        if n % t == 0:
            return t
    return n


def _sigmoid(x):
    return jax.nn.sigmoid(x)


def _silu(x):
    return x * jax.nn.sigmoid(x)


_DN = {"nn": (((1,), (0,)), ((), ())), "nt": (((1,), (1,)), ((), ())), "tn": (((0,), (0,)), ((), ()))}


def mm(a, b, mode, name, out_dtype=f32, add=None, side=None):
    if mode == "nn":
        (M, K), (K2, N) = a.shape, b.shape
    elif mode == "nt":
        (M, K), (N, K2) = a.shape, b.shape
    else:
        (K, M), (K2, N) = a.shape, b.shape
    assert K == K2, (a.shape, b.shape, mode)
    sa, sb, so = a.dtype.itemsize, b.dtype.itemsize, jnp.dtype(out_dtype).itemsize
    tk = K if K <= MM_FULL_K else _pick(K, (1024, 512, 256, 128))
    nk = K // tk
    tms = [t for t in (1024, 512, 256, 128) if M % t == 0] or [M]
    tns = [t for t in (2048, 1024, 512, 256, 128) if N % t == 0] or [N]

    def vmem_bytes(tm, tn):
        n = 2 * (tm * tk * sa + tk * tn * sb) + 2 * tm * tn * so
        n += tm * tn * 4 if nk > 1 else 0
        n += 2 * tm * tn * add.dtype.itemsize if add is not None else 0
        return n

    ti, tj = 0, 0
    while vmem_bytes(tms[ti], tns[tj]) > MM_VMEM_BUDGET:
        if tj + 1 < len(tns) and (tns[tj] >= tms[ti] or ti + 1 >= len(tms)):
            tj += 1
        elif ti + 1 < len(tms):
            ti += 1
        else:
            break
    tm, tn = tms[ti], tns[tj]
    a_bytes, b_bytes = M * K * sa, K * N * sb
    i_outer = a_bytes + b_bytes * (M // tm) <= b_bytes + a_bytes * (N // tn)
    dn = _DN[mode]

    grid = (M // tm, N // tn, nk) if i_outer else (N // tn, M // tm, nk)
    ns = len(side[1]) if side is not None else 0
    n_add = 1 if add is not None else 0

    def body(a_ref, b_ref, *rest):
        add_ref = rest[0] if add is not None else None
        x_refs = rest[n_add:n_add + ns]
        o_ref = rest[n_add + ns]
        out_refs = rest[n_add + ns + 1:n_add + 2 * ns + 1]
        scratch = rest[n_add + 2 * ns + 1:]
        if side is not None:
            step = (pl.program_id(0) * grid[1] + pl.program_id(1)) * nk + pl.program_id(2)

            @pl.when(step == 0)
            def _():
                _exchange_start(side[0], x_refs, out_refs, scratch[-3:])

        def finish(r):
            if add is not None:
                r = r + add_ref[...].astype(f32)
            o_ref[...] = r.astype(out_dtype)

        part = lax.dot_general(a_ref[...].astype(bf16), b_ref[...].astype(bf16), dn, preferred_element_type=f32)
        if nk == 1:
            finish(part)
        else:
            acc = scratch[0]
            k = pl.program_id(2)

            @pl.when(k == 0)
            def _():
                acc[...] = part

            @pl.when(k > 0)
            def _():
                acc[...] += part

            @pl.when(k == nk - 1)
            def _():
                finish(acc[...])

        if side is not None:
            @pl.when(step == grid[0] * grid[1] * nk - 1)
            def _():
                _exchange_finish(side[0], x_refs, out_refs, scratch[-3:])

    def ij(p, q):
        return (p, q) if i_outer else (q, p)

    if mode == "tn":
        a_spec = pl.BlockSpec((tk, tm), lambda p, q, k: (k, ij(p, q)[0]))
    else:
        a_spec = pl.BlockSpec((tm, tk), lambda p, q, k: (ij(p, q)[0], k))
    if mode == "nt":
        b_spec = pl.BlockSpec((tn, tk), lambda p, q, k: (ij(p, q)[1], k))
    else:
        b_spec = pl.BlockSpec((tk, tn), lambda p, q, k: (k, ij(p, q)[1]))
    o_spec = pl.BlockSpec((tm, tn), lambda p, q, k: ij(p, q))
    in_specs = [a_spec, b_spec] + ([o_spec] if add is not None else [])
    args = (a, b) + ((add,) if add is not None else ())
    acc_scratch = [pltpu.VMEM((tm, tn), f32)] if nk > 1 else []
    o_shape = jax.ShapeDtypeStruct((M, N), out_dtype)
    if side is None:
        return pl.pallas_call(
            body, name=name, grid=grid, in_specs=in_specs, out_specs=o_spec, out_shape=o_shape, scratch_shapes=acc_scratch,
            compiler_params=_params(("parallel", "parallel", "arbitrary")))(*args)
    res = pl.pallas_call(
        body, name=name, grid=grid, in_specs=in_specs + [_HBM] * ns, out_specs=[o_spec] + [_HBM] * ns,
        out_shape=[o_shape] + _exchange_out_shapes(side[0], side[1]), scratch_shapes=acc_scratch + _exchange_scratch(ns),
        compiler_params=_params(("arbitrary", "arbitrary", "arbitrary")))(*args, *side[1])
    return res[0], res[1:]


def _row_spec(T, w, base):
    return pl.BlockSpec((T, w), lambda g, i: (i, base + g))


def _const_spec(r, w, base):
    return pl.BlockSpec((r, w), lambda g, i: (0, base + g))


def rowop_fwd(fn, name, rows, consts, outs, T, ncb=1):
    L = rows[0][0].shape[0]
    nr, nc = len(rows), len(consts)

    def body(*refs):
        ins = [r[...].astype(f32) for r in refs[:nr + nc]]
        res = fn(*ins)
        for o_ref, r in zip(refs[nr + nc:], res):
            o_ref[...] = r.astype(o_ref.dtype)

    in_specs = [_row_spec(T, w, b) for (_, w, b) in rows] + [_const_spec(a.shape[0], w, b) for (a, w, b) in consts]
    out_specs = [_row_spec(T, w, 0) for (w, _) in outs]
    out_shape = [jax.ShapeDtypeStruct((L, ncb * w), dt) for (w, dt) in outs]
    return pl.pallas_call(
        body, name=name, grid=(ncb, L // T), in_specs=in_specs, out_specs=out_specs, out_shape=out_shape,
        compiler_params=_params(("parallel", "parallel")))(*[a for (a, _, _) in rows], *[a for (a, _, _) in consts])


def rowop_bwd(fn, name, rows, consts, cts, row_grads, const_grads, T, ncb=1, add=None):
    L = rows[0][0].shape[0]
    nr, nc, nct = len(rows), len(consts), len(cts)
    n_in = nr + nc + nct + (1 if add is not None else 0)
    rg_idx = [i for i, d in enumerate(row_grads) if d is not None]
    cg_idx = [i for i, d in enumerate(const_grads) if d]

    def body(*refs):
        ins = [r[...].astype(f32) for r in refs[:nr + nc]]
        ct = tuple(r[...].astype(f32) for r in refs[nr + nc:nr + nc + nct])
        _, vjp = jax.vjp(fn, *ins)
        grads = vjp(ct)
        outs = refs[n_in:]
        for n, i in enumerate(rg_idx):
            gi = grads[i]
            if add is not None and n == 0:
                gi = gi + refs[n_in - 1][...].astype(f32)
            outs[n][...] = gi.astype(outs[n].dtype)
        first = pl.program_id(1) == 0
        for n, i in enumerate(cg_idx):
            o = outs[len(rg_idx) + n]

            @pl.when(first)
            def _(o=o):
                o[...] = jnp.zeros_like(o)

            o[...] += grads[nr + i]

    in_specs = ([_row_spec(T, w, b) for (_, w, b) in rows] + [_const_spec(a.shape[0], w, b) for (a, w, b) in consts]
                + [_row_spec(T, w, b) for (_, w, b) in cts] + ([_row_spec(T, add[1], add[2])] if add is not None else []))
    out_specs = ([_row_spec(T, rows[i][1], 0) for i in rg_idx] + [_const_spec(consts[i][0].shape[0], consts[i][1], 0) for i in cg_idx])
    out_shape = ([jax.ShapeDtypeStruct((L, ncb * rows[i][1]), row_grads[i]) for i in rg_idx]
                 + [jax.ShapeDtypeStruct((consts[i][0].shape[0], ncb * consts[i][1]), f32) for i in cg_idx])
    args = [a for (a, _, _) in rows] + [a for (a, _, _) in consts] + [a for (a, _, _) in cts] + ([add[0]] if add is not None else [])
    return pl.pallas_call(
        body, name=name, grid=(ncb, L // T), in_specs=in_specs, out_specs=out_specs, out_shape=out_shape,
        compiler_params=_params(("parallel", "arbitrary")))(*args)


def _rms_fn(x, w):
    return (x * lax.rsqrt(jnp.mean(x * x, axis=-1, keepdims=True) + EPS) * w,)


def _glu_fn(a, ag):
    return (a * _sigmoid(ag),)


def _gate_fn(o, g):
    return (o * _silu(g),)


def _ple_fn(gpre, pp):
    return (_sigmoid(gpre) * pp,)


def _ple_fwd_fn(h1, gpre, pp):
    return (h1 + _sigmoid(gpre) * pp,)


def _lngate_fn(vc, g, w, b):
    mu = jnp.mean(vc, axis=-1, keepdims=True)
    xc = vc - mu
    y = xc * lax.rsqrt(jnp.mean(xc * xc, axis=-1, keepdims=True) + EPS) * w + b
    return (_silu(y) * _silu(g),)


def _post_fn(yf, yb, xs, z, dsk, gw):
    y = (yf + yb + dsk * xs) * _silu(z)
    return (y * lax.rsqrt(jnp.mean(y * y, axis=-1, keepdims=True) + EPS) * gw,)


def _make_prep_fn(n_heads):
    def prep_fn(u, bias, alog):
        lane = lax.broadcasted_iota(jnp.int32, (1, LANES), 1)
        dt = jnp.where(lane < 2 * n_heads, jax.nn.softplus(u + bias), 0.0)
        da = dt * (-jnp.exp(alog))
        li = lax.broadcasted_iota(jnp.int32, (CHUNK, CHUNK), 0)
        si = lax.broadcasted_iota(jnp.int32, (CHUNK, CHUNK), 1)
        tril = (si <= li).astype(f32)
        triu = (si >= li).astype(f32)
        csf = jnp.dot(tril, da, precision=HIGHEST, preferred_element_type=f32)
        csb = jnp.dot(triu, da, precision=HIGHEST, preferred_element_type=f32)
        return dt, jnp.where(lane < n_heads, csf, csb)

    return prep_fn


def final_loss(h, w, tgt, name="final_loss"):
    L, D = h.shape
    T = _pick(L, (256, 128))

    def fn(x, w_, t):
        y = x * lax.rsqrt(jnp.mean(x * x, axis=-1, keepdims=True) + EPS) * w_
        err = jnp.square(y - t)
        return 0.5 * jnp.sum(jnp.mean(err, axis=-1, keepdims=True))

    def body(h_ref, w_ref, t_ref, loss_ref, dh_ref, dw_ref):
        @pl.when(pl.program_id(0) == 0)
        def _():
            loss_ref[...] = jnp.zeros_like(loss_ref)
            dw_ref[...] = jnp.zeros_like(dw_ref)

        t = t_ref[...]
        val, vjp = jax.vjp(lambda x, w_: fn(x, w_, t), h_ref[...], w_ref[...])
        dx, dw = vjp(jnp.ones((), f32))
        dh_ref[...] = dx
        dw_ref[...] += dw
        loss_ref[...] += jnp.broadcast_to(val, loss_ref.shape)

    return pl.pallas_call(
        body, name=name, grid=(L // T,),
        in_specs=[pl.BlockSpec((T, D), lambda i: (i, 0)), pl.BlockSpec((1, D), lambda i: (0, 0)), pl.BlockSpec((T, D), lambda i: (i, 0))],
        out_specs=[pl.BlockSpec((8, LANES), lambda i: (0, 0)), pl.BlockSpec((T, D), lambda i: (i, 0)), pl.BlockSpec((1, D), lambda i: (0, 0))],
        out_shape=[jax.ShapeDtypeStruct((8, LANES), f32), jax.ShapeDtypeStruct((L, D), f32), jax.ShapeDtypeStruct((1, D), f32)],
        compiler_params=_params(("arbitrary",)))(h, w, tgt)


CONV_ROWS = 128


def conv_fwd(x_arr, xbase, C, w, b, K, act, name):
    L = x_arr.shape[0]
    R = CONV_ROWS
    lo = CONV_PAD - K // 2

    def body(x_ref, w_ref, b_ref, *rest):
        xpad = rest[-1]
        outs = rest[:-1]
        xpad[0:CONV_PAD, :] = jnp.zeros((CONV_PAD, LANES), f32)
        xpad[CONV_PAD + L:CONV_PAD + L + CONV_PAD, :] = jnp.zeros((CONV_PAD, LANES), f32)
        xpad[CONV_PAD:CONV_PAD + L, :] = x_ref[...]

        def chunk(ci, carry):
            r0 = pl.multiple_of(ci * R, R)
            acc = jnp.broadcast_to(b_ref[...], (R, LANES))
            for k in range(K):
                acc = acc + w_ref[k:k + 1, :] * xpad[pl.ds(r0 + (lo + k), R), :]
            outs[0][pl.ds(r0, R), :] = acc
            if act:
                outs[1][pl.ds(r0, R), :] = _silu(acc)
            return carry

        lax.fori_loop(0, L // R, chunk, 0)

    col = pl.BlockSpec((L, LANES), lambda j: (0, j))
    n_out = 2 if act else 1
    return pl.pallas_call(
        body, name=name, grid=(C // LANES,),
        in_specs=[pl.BlockSpec((L, LANES), lambda j: (0, xbase + j)), pl.BlockSpec((K, LANES), lambda j: (0, j)),
                  pl.BlockSpec((1, LANES), lambda j: (0, j))],
        out_specs=[col] * n_out, out_shape=[jax.ShapeDtypeStruct((L, C), f32)] * n_out,
        scratch_shapes=[pltpu.VMEM((L + 2 * CONV_PAD, LANES), f32)],
        compiler_params=_params(("parallel",)))(x_arr, w, b)


def conv_bwd(dy, x_arr, xbase, C, w, K, pre, out_dtype, name):
    L = x_arr.shape[0]
    R = CONV_ROWS
    lo = CONV_PAD - K // 2
    act = pre is not None

    def body(*refs):
        if act:
            dy_ref, pre_ref, x_ref, w_ref, dx_ref, dw_ref, db_ref, dpad, xpad, accw, accb = refs
        else:
            dy_ref, x_ref, w_ref, dx_ref, dw_ref, db_ref, dpad, xpad, accw, accb = refs
        zeros = jnp.zeros((CONV_PAD, LANES), f32)
        for pad in (dpad, xpad):
            pad[0:CONV_PAD, :] = zeros
            pad[CONV_PAD + L:CONV_PAD + L + CONV_PAD, :] = zeros
        xpad[CONV_PAD:CONV_PAD + L, :] = x_ref[...]
        if act:
            p = pre_ref[...]
            s = _sigmoid(p)
            dpad[CONV_PAD:CONV_PAD + L, :] = dy_ref[...].astype(f32) * (s * (1.0 + p * (1.0 - s)))
        else:
            dpad[CONV_PAD:CONV_PAD + L, :] = dy_ref[...].astype(f32)
        accw[...] = jnp.zeros_like(accw)
        accb[...] = jnp.zeros_like(accb)

        def chunk(ci, carry):
            r0 = pl.multiple_of(ci * R, R)
            acc = jnp.zeros((R, LANES), f32)
            for k in range(K):
                acc = acc + w_ref[k:k + 1, :] * dpad[pl.ds(r0 + (CONV_PAD + K // 2 - k), R), :]
            dx_ref[pl.ds(r0, R), :] = acc.astype(out_dtype)
            d = dpad[pl.ds(r0 + CONV_PAD, R), :]
            accb[...] += jnp.sum(d.reshape(R // 8, 8, LANES), axis=0)
            for k in range(K):
                prod = d * xpad[pl.ds(r0 + (lo + k), R), :]
                accw[k] += jnp.sum(prod.reshape(R // 8, 8, LANES), axis=0)
            return carry

        lax.fori_loop(0, L // R, chunk, 0)
        dw_ref[...] = jnp.sum(accw[...], axis=1)
        db_ref[...] = jnp.sum(accb[...], axis=0, keepdims=True)

    col = pl.BlockSpec((L, LANES), lambda j: (0, j))
    in_specs = [col] + ([col] if act else []) + [pl.BlockSpec((L, LANES), lambda j: (0, xbase + j)), pl.BlockSpec((K, LANES), lambda j: (0, j))]
    args = (dy,) + ((pre,) if act else ()) + (x_arr, w)
    return pl.pallas_call(
        body, name=name, grid=(C // LANES,), in_specs=in_specs,
        out_specs=[col, pl.BlockSpec((K, LANES), lambda j: (0, j)), pl.BlockSpec((1, LANES), lambda j: (0, j))],
        out_shape=[jax.ShapeDtypeStruct((L, C), out_dtype), jax.ShapeDtypeStruct((K, C), f32), jax.ShapeDtypeStruct((1, C), f32)],
        scratch_shapes=[pltpu.VMEM((L + 2 * CONV_PAD, LANES), f32), pltpu.VMEM((L + 2 * CONV_PAD, LANES), f32),
                        pltpu.VMEM((K, 8, LANES), f32), pltpu.VMEM((8, LANES), f32)],
        compiler_params=_params(("parallel",)))(*args)


def _ssd_chunk(X, Bm, Cm, dtc, csc, csr, S, lane_off, dirn):
    Qn = X.shape[0]
    lane = lax.broadcasted_iota(jnp.int32, (1, LANES), 1)
    head = lax.shift_right_logical(lax.broadcasted_iota(jnp.int32, (1, GROUP_W), 1), 6)
    dtx = jnp.zeros((Qn, GROUP_W), f32)
    csx = jnp.zeros((Qn, GROUP_W), f32)
    cols = []
    for j in range(HEADS_PER_GROUP):
        sel = lane == lane_off + j
        dcol = jnp.sum(jnp.where(sel, dtc, 0.0), axis=1, keepdims=True)
        ccol = jnp.sum(jnp.where(sel, csc, 0.0), axis=1, keepdims=True)
        dtx = jnp.where(head == j, dcol, dtx)
        csx = jnp.where(head == j, ccol, csx)
        cols.append(ccol)
    Xd = X * dtx
    edge = Qn - 1 if dirn == 0 else 0
    cs_edge = csx[edge:edge + 1, :]
    li = lax.broadcasted_iota(jnp.int32, (Qn, Qn), 0)
    si = lax.broadcasted_iota(jnp.int32, (Qn, Qn), 1)
    mask = (si <= li) if dirn == 0 else (si >= li)
    Cb = Cm.astype(bf16)
    Bb = Bm.astype(bf16)
    G = lax.dot_general(Cb, Bb, _DN["nt"], preferred_element_type=f32)
    Y = jnp.exp(csx) * jnp.dot(Cb, S.astype(bf16), preferred_element_type=f32)
    for j in range(HEADS_PER_GROUP):
        row = csr[j:j + 1, :]
        Lm = jnp.exp(jnp.where(mask, cols[j] - row, NEG_BIG))
        Wj = (G * Lm).astype(bf16)
        Xj = jnp.where(head == j, Xd, 0.0).astype(bf16)
        Y = Y + jnp.dot(Wj, Xj, preferred_element_type=f32)
    S_new = S * jnp.exp(cs_edge) + lax.dot_general(Bb, (Xd * jnp.exp(cs_edge - csx)).astype(bf16), _DN["tn"], preferred_element_type=f32)
    return Y, S_new


def _ssd_specs(D, G, nchunk, dirn, descending):
    def cidx(c):
        return nchunk - 1 - c if descending else c

    x_spec = pl.BlockSpec((CHUNK, GROUP_W), lambda c, g: (cidx(c), g))
    b_spec = pl.BlockSpec((CHUNK, N_STATE), lambda c, g: (cidx(c), D // N_STATE + g))
    c_spec = pl.BlockSpec((CHUNK, N_STATE), lambda c, g: (cidx(c), D // N_STATE + G + g))
    n_spec = pl.BlockSpec((CHUNK, N_STATE), lambda c, g: (cidx(c), g))
    lane_spec = pl.BlockSpec((CHUNK, LANES), lambda c, g: (cidx(c), 0))
    csr_spec = pl.BlockSpec((None, 8, CHUNK), lambda c, g: (dirn * G + g, 0, cidx(c)))
    dcsr_spec = pl.BlockSpec((None, 8, CHUNK), lambda c, g: (g, 0, cidx(c)))
    s_spec = pl.BlockSpec((None, None, N_STATE, GROUP_W), lambda c, g: (g, cidx(c), 0, 0))
    return x_spec, b_spec, c_spec, n_spec, lane_spec, csr_spec, dcsr_spec, s_spec


def ssd_fwd(act, dt128, cs128, csr, dirn, n_heads, name):
    L = act.shape[0]
    D = act.shape[1] // 2
    G = D // GROUP_W
    nchunk = L // CHUNK
    x_spec, b_spec, c_spec, _, lane_spec, csr_spec, _, s_spec = _ssd_specs(D, G, nchunk, dirn, dirn == 1)

    def body(x_ref, b_ref, c_ref, dt_ref, cs_ref, csr_ref, y_ref, ssave_ref, S):
        c, g = pl.program_id(0), pl.program_id(1)

        @pl.when(c == 0)
        def _():
            S[g] = jnp.zeros((N_STATE, GROUP_W), f32)

        s_in = S[g]
        ssave_ref[...] = s_in
        y, s_new = _ssd_chunk(x_ref[...], b_ref[...], c_ref[...], dt_ref[...], cs_ref[...], csr_ref[...], s_in,
                              dirn * n_heads + HEADS_PER_GROUP * g, dirn)
        y_ref[...] = y
        S[g] = s_new

    return pl.pallas_call(
        body, name=name, grid=(nchunk, G),
        in_specs=[x_spec, b_spec, c_spec, lane_spec, lane_spec, csr_spec],
        out_specs=[x_spec, s_spec],
        out_shape=[jax.ShapeDtypeStruct((L, D), f32), jax.ShapeDtypeStruct((G, nchunk, N_STATE, GROUP_W), f32)],
        scratch_shapes=[pltpu.VMEM((G, N_STATE, GROUP_W), f32)],
        compiler_params=_params(("arbitrary", "arbitrary")))(act, act, act, dt128, cs128, csr)


def ssd_bwd(act, dt128, cs128, csr, ssave, dy, dirn, n_heads, name):
    L = act.shape[0]
    D = act.shape[1] // 2
    G = D // GROUP_W
    nchunk = L // CHUNK
    x_spec, b_spec, c_spec, n_spec, lane_spec, csr_in, dcsr_spec, s_spec = _ssd_specs(D, G, nchunk, dirn, dirn == 0)

    def body(x_ref, b_ref, c_ref, dt_ref, cs_ref, csr_ref, s_ref, dy_ref, dx_ref, db_ref, dc_ref, ddt_ref, dcs_ref, dcsr_ref, dS):
        c, g = pl.program_id(0), pl.program_id(1)

        @pl.when(c == 0)
        def _():
            dS[g] = jnp.zeros((N_STATE, GROUP_W), f32)

        lane_off = dirn * n_heads + HEADS_PER_GROUP * g
        _, vjp = jax.vjp(lambda X, Bm, Cm, dtc, csc, csr_, S: _ssd_chunk(X, Bm, Cm, dtc, csc, csr_, S, lane_off, dirn),
                         x_ref[...], b_ref[...], c_ref[...], dt_ref[...], cs_ref[...], csr_ref[...], s_ref[...])
        dX, dB, dC, ddt, dcs, dcsr, dS_in = vjp((dy_ref[...], dS[g]))
        dx_ref[...] = dX
        db_ref[...] = dB
        dc_ref[...] = dC
        dcsr_ref[...] = dcsr
        dS[g] = dS_in

        @pl.when(g == 0)
        def _():
            ddt_ref[...] = jnp.zeros_like(ddt_ref)
            dcs_ref[...] = jnp.zeros_like(dcs_ref)

        ddt_ref[...] += ddt
        dcs_ref[...] += dcs

    return pl.pallas_call(
        body, name=name, grid=(nchunk, G),
        in_specs=[x_spec, b_spec, c_spec, lane_spec, lane_spec, csr_in, s_spec, x_spec],
        out_specs=[x_spec, n_spec, n_spec, lane_spec, lane_spec, dcsr_spec],
        out_shape=[jax.ShapeDtypeStruct((L, D), f32), jax.ShapeDtypeStruct((L, G * N_STATE), f32), jax.ShapeDtypeStruct((L, G * N_STATE), f32),
                   jax.ShapeDtypeStruct((L, LANES), f32), jax.ShapeDtypeStruct((L, LANES), f32), jax.ShapeDtypeStruct((G, 8, L), f32)],
        scratch_shapes=[pltpu.VMEM((G, N_STATE, GROUP_W), f32)],
        compiler_params=_params(("arbitrary", "arbitrary")))(act, act, act, dt128, cs128, csr, ssave, dy)


def _csr_from_cs(cs128, n_heads):
    L = cs128.shape[0]
    t = cs128[:, :2 * n_heads].T.reshape(2 * n_heads // HEADS_PER_GROUP, HEADS_PER_GROUP, L)
    return jnp.pad(t, ((0, 0), (0, 8 - HEADS_PER_GROUP), (0, 0)))


def _cs_from_dcsr(dcsr_f, dcsr_b, n_heads):
    L = dcsr_f.shape[-1]
    t = jnp.concatenate([dcsr_f, dcsr_b], axis=0)[:, :HEADS_PER_GROUP, :].reshape(2 * n_heads, L).T
    return jnp.pad(t, ((0, 0), (0, LANES - 2 * n_heads)))


def na_bias_table(rpb):
    nh = rpb.shape[0]
    cols = np.arange(GRID_W)
    col_start = np.clip(cols - NA_WC // 2, 0, GRID_W - NA_WC)
    col_valid = (cols[None, :] >= col_start[:, None]) & (cols[None, :] < col_start[:, None] + NA_WC)
    col_off = np.clip(cols[None, :] - cols[:, None], -(NA_WC - 1), NA_WC - 1) + NA_WC - 1
    rm = np.zeros((NA_WR, NA_WR, 2 * NA_WR - 1), np.float32)
    for d in range(NA_WR):
        for i in range(NA_WR):
            rm[d, i, i - d + NA_WR - 1] = 1.0
    cm = np.zeros((GRID_W, GRID_W, 2 * NA_WC - 1), np.float32)
    cm[np.arange(GRID_W)[:, None], np.arange(GRID_W)[None, :], col_off] = 1.0
    bt = jnp.einsum("hrc,dir,qkc->hdqik", rpb, jnp.asarray(rm), jnp.asarray(cm), precision=HIGHEST)
    bt = jnp.where(jnp.asarray(col_valid)[None, None, :, None, :], bt, NEG_BIG)
    return bt.reshape(nh, NA_WR, GRID_W, NA_WR * GRID_W)


def _na_window(r, rows):
    rs = jnp.clip(r - NA_WR // 2, 0, rows - NA_WR)
    return rs, r - rs


def _na_probs(q, kw, bias):
    s = lax.dot_general(q.astype(bf16), kw.astype(bf16), _DN["nt"], preferred_element_type=f32) * (NA_D ** -0.5) + bias
    m = jnp.max(s, axis=-1, keepdims=True)
    p = jnp.exp(s - m)
    return p / jnp.sum(p, axis=-1, keepdims=True)


def na_fwd(um, qbase, kbase, vbase, bt, name):
    L = um.shape[0]
    nh = bt.shape[0]
    rows = L // GRID_W
    WK = NA_WR * GRID_W

    RB = _pick(rows, (NA_ROWS_PER_STEP, 2, 1))

    def body(q_ref, k_ref, v_ref, bt_ref, o_ref):
        win = [_na_window(pl.program_id(1) * RB + rb, rows) for rb in range(RB)]
        t0s = [pl.multiple_of(rs * GRID_W, GRID_W) for rs, _ in win]
        qrows = [pl.ds(rb * GRID_W, GRID_W) for rb in range(RB)]
        ps = [_na_probs(q_ref[qr, :], k_ref[pl.ds(t0, WK), :], bt_ref[delta]) for qr, t0, (_, delta) in zip(qrows, t0s, win)]
        os_ = [jnp.dot(p.astype(bf16), v_ref[pl.ds(t0, WK), :].astype(bf16), preferred_element_type=f32) for p, t0 in zip(ps, t0s)]
        for qr, o in zip(qrows, os_):
            o_ref[qr, :] = o

    kv = lambda base: pl.BlockSpec((L, NA_D), lambda h, r: (0, base + h))
    return pl.pallas_call(
        body, name=name, grid=(nh, rows // RB),
        in_specs=[pl.BlockSpec((RB * GRID_W, NA_D), lambda h, r: (r, qbase + h)), kv(kbase), kv(vbase),
                  pl.BlockSpec((None, NA_WR, GRID_W, WK), lambda h, r: (h, 0, 0, 0))],
        out_specs=pl.BlockSpec((RB * GRID_W, NA_D), lambda h, r: (r, h)),
        out_shape=jax.ShapeDtypeStruct((L, nh * NA_D), f32),
        compiler_params=_params(("parallel", "arbitrary")))(um, um, um, bt)


def na_bwd(um, qbase, kbase, vbase, bt, do, name, side=None):
    L = um.shape[0]
    nh = bt.shape[0]
    rows = L // GRID_W
    WK = NA_WR * GRID_W
    scale = NA_D ** -0.5

    RB = _pick(rows, (NA_ROWS_PER_STEP, 2, 1))

    ns = len(side[1]) if side is not None else 0
    nsteps = rows // RB

    def body(q_ref, k_ref, v_ref, bt_ref, do_ref, *rest):
        x_refs = rest[:ns]
        dq_ref, dk_ref, dv_ref, dbt_ref = rest[ns:ns + 4]
        out_refs = rest[ns + 4:2 * ns + 4]
        sems = rest[2 * ns + 4:]
        if side is not None:
            step = pl.program_id(0) * nsteps + pl.program_id(1)

            @pl.when(step == 0)
            def _():
                _exchange_start(side[0], x_refs, out_refs, sems)

        @pl.when(pl.program_id(1) == 0)
        def _():
            dk_ref[...] = jnp.zeros_like(dk_ref)
            dv_ref[...] = jnp.zeros_like(dv_ref)
            dbt_ref[...] = jnp.zeros_like(dbt_ref)

        win = [_na_window(pl.program_id(1) * RB + rb, rows) for rb in range(RB)]
        t0s = [pl.multiple_of(rs * GRID_W, GRID_W) for rs, _ in win]
        qrows = [pl.ds(rb * GRID_W, GRID_W) for rb in range(RB)]
        qs_ = [q_ref[qr, :] for qr in qrows]
        kws = [k_ref[pl.ds(t0, WK), :].astype(bf16) for t0 in t0s]
        vws = [v_ref[pl.ds(t0, WK), :].astype(bf16) for t0 in t0s]
        dos = [do_ref[qr, :].astype(bf16) for qr in qrows]
        ps = [_na_probs(q, kw, bt_ref[delta]) for q, kw, (_, delta) in zip(qs_, kws, win)]
        dps = [lax.dot_general(do_, vw, _DN["nt"], preferred_element_type=f32) for do_, vw in zip(dos, vws)]
        dss = [p * (dp - jnp.sum(dp * p, axis=-1, keepdims=True)) for p, dp in zip(ps, dps)]
        dsbs = [ds.astype(bf16) for ds in dss]
        dqs = [jnp.dot(dsb, kw, preferred_element_type=f32) * scale for dsb, kw in zip(dsbs, kws)]
        dks = [lax.dot_general(dsb, q.astype(bf16), _DN["tn"], preferred_element_type=f32) * scale for dsb, q in zip(dsbs, qs_)]
        dvs = [lax.dot_general(p.astype(bf16), do_, _DN["tn"], preferred_element_type=f32) for p, do_ in zip(ps, dos)]
        for rb in range(RB):
            dq_ref[qrows[rb], :] = dqs[rb]
            dbt_ref[win[rb][1]] += dss[rb]
            dk_ref[pl.ds(t0s[rb], WK), :] += dks[rb]
            dv_ref[pl.ds(t0s[rb], WK), :] += dvs[rb]

        if side is not None:
            @pl.when(step == nh * nsteps - 1)
            def _():
                _exchange_finish(side[0], x_refs, out_refs, sems)

    kv = lambda base: pl.BlockSpec((L, NA_D), lambda h, r: (0, base + h))
    qs = lambda base: pl.BlockSpec((RB * GRID_W, NA_D), lambda h, r: (r, base + h))
    bts = pl.BlockSpec((None, NA_WR, GRID_W, WK), lambda h, r: (h, 0, 0, 0))
    big = jax.ShapeDtypeStruct((L, nh * NA_D), f32)
    in_specs = [qs(qbase), kv(kbase), kv(vbase), bts, qs(0)]
    out_specs = [qs(0), kv(0), kv(0), bts]
    out_shape = [big, big, big, jax.ShapeDtypeStruct(bt.shape, f32)]
    if side is None:
        return pl.pallas_call(
            body, name=name, grid=(nh, nsteps), in_specs=in_specs, out_specs=out_specs, out_shape=out_shape,
            compiler_params=_params(("parallel", "arbitrary")))(um, um, um, bt, do)
    res = pl.pallas_call(
        body, name=name, grid=(nh, nsteps), in_specs=in_specs + [_HBM] * ns, out_specs=out_specs + [_HBM] * ns,
        out_shape=out_shape + _exchange_out_shapes(side[0], side[1]), scratch_shapes=_exchange_scratch(ns),
        compiler_params=_params(("arbitrary", "arbitrary")))(um, um, um, bt, do, *side[1])
    return tuple(res[:4]) + (res[4:],)


_HBM = pl.BlockSpec(memory_space=pltpu.HBM)


def all_gather(xs, name):
    n = len(xs)

    def body(*refs):
        _exchange_start("gather", refs[:n], refs[n:2 * n], refs[2 * n:])
        _exchange_finish("gather", refs[:n], refs[n:2 * n], refs[2 * n:])

    return pl.pallas_call(
        body, name=name, in_specs=[_HBM] * n, out_specs=[_HBM] * n, out_shape=_exchange_out_shapes("gather", xs),
        scratch_shapes=_exchange_scratch(n))(*xs)


def _exchange_out_shapes(kind, xs):
    return [jax.ShapeDtypeStruct(((N_DEV,) + x.shape) if kind == "gather" else x.shape, x.dtype) for x in xs]


def _exchange_scratch(n):
    return [pltpu.SemaphoreType.DMA((7 * n,)), pltpu.SemaphoreType.DMA((7 * n,)), pltpu.SemaphoreType.DMA((n,))]


def _gather_copies(x_refs, out_refs, sems):
    send_sems, recv_sems, local_sems = sems
    n = len(x_refs)
    mx, my, mc = lax.axis_index("x"), lax.axis_index("y"), lax.axis_index("c")
    me, sibling = (mx, my, mc), (mx, my, 1 - mc)
    chips = [(1 - mx, my), (mx, 1 - my), (1 - mx, 1 - my)]

    def slab(t, px, py, pc):
        return out_refs[t].at[4 * px + 2 * py + pc]

    def copy(t, k, block, to, src=None):
        return pltpu.make_async_remote_copy(
            src_ref=slab(t, *block) if src is None else src, dst_ref=slab(t, *block),
            send_sem=send_sems.at[7 * t + k], recv_sem=recv_sems.at[7 * t + k], device_id=to, device_id_type=pl.DeviceIdType.MESH)

    mine = [pltpu.make_async_copy(x_refs[t], slab(t, *me), local_sems.at[t]) for t in range(n)]
    first = []
    for t in range(n):
        first.append(copy(t, 0, me, sibling, src=x_refs[t]))
        first += [copy(t, 1 + j, me, (*chip, mc), src=x_refs[t]) for j, chip in enumerate(chips)]
    return mine, first, copy, me, sibling, chips, mc


def _exchange_start(kind, x_refs, out_refs, sems):
    if kind == "gather":
        mine, first = _gather_copies(x_refs, out_refs, sems)[:2]
    else:
        mine, first = _a2a_copies(x_refs, out_refs, sems)
    for cp in mine + first:
        cp.start()


def _exchange_finish(kind, x_refs, out_refs, sems):
    n = len(x_refs)
    if kind == "gather":
        mine, first, copy, me, sibling, chips, mc = _gather_copies(x_refs, out_refs, sems)
        passed = []
        for t in range(n):
            for j, chip in enumerate(chips):
                copy(t, 1 + j, (*chip, mc), me).wait_recv()
                fwd = copy(t, 4 + j, (*chip, mc), sibling)
                fwd.start()
                passed.append(fwd)
        for t in range(n):
            copy(t, 0, sibling, me).wait_recv()
            for j, chip in enumerate(chips):
                copy(t, 4 + j, (*chip, 1 - mc), me).wait_recv()
        for cp in first + passed:
            cp.wait_send()
    else:
        mine, first = _a2a_copies(x_refs, out_refs, sems)
        for cp in first:
            cp.wait()
    for cp in mine:
        cp.wait()


def _a2a_copies(x_refs, out_refs, sems):
    send_sems, recv_sems, local_sems = sems
    n = len(x_refs)
    mx, my, mc = lax.axis_index("x"), lax.axis_index("y"), lax.axis_index("c")
    me = 4 * mx + 2 * my + mc
    mine = [pltpu.make_async_copy(x_refs[t].at[me], out_refs[t].at[me], local_sems.at[t]) for t in range(n)]
    copies = []
    for t in range(n):
        for k in range(1, N_DEV):
            px = 1 - mx if k & 4 else mx
            py = 1 - my if k & 2 else my
            pc = 1 - mc if k & 1 else mc
            peer = 4 * px + 2 * py + pc
            copies.append(pltpu.make_async_remote_copy(
                src_ref=x_refs[t].at[peer], dst_ref=out_refs[t].at[me], send_sem=send_sems.at[7 * t + k - 1],
                recv_sem=recv_sems.at[7 * t + k - 1], device_id=(px, py, pc), device_id_type=pl.DeviceIdType.MESH))
    return mine, copies


def all_to_all(xs, name):
    n = len(xs)

    def body(*refs):
        _exchange_start("a2a", refs[:n], refs[n:2 * n], refs[2 * n:])
        _exchange_finish("a2a", refs[:n], refs[n:2 * n], refs[2 * n:])

    return pl.pallas_call(
        body, name=name, in_specs=[_HBM] * n, out_specs=[_HBM] * n, out_shape=_exchange_out_shapes("a2a", xs),
        scratch_shapes=_exchange_scratch(n))(*xs)


def adamw(w, slabs, m, v, own, name):
    R, C = w.shape
    tr = _pick(R, tuple(t for t in (256, 128, 64, 32, 16, 8) if t * C * 4 <= ADAMW_TILE_BYTES) or (8,))
    c1 = 1.0 - ADAM_B1 ** ADAM_STEP
    c2 = 1.0 - ADAM_B2 ** ADAM_STEP

    def body(*refs):
        if own is not None:
            me_ref, w_ref, s_ref, m_ref, v_ref, own_ref, g_ref, d_ref, nm_ref, nv_ref = refs
        else:
            w_ref, s_ref, m_ref, v_ref, g_ref, d_ref, nm_ref, nv_ref = refs
        g = None
        for s in range(N_DEV):
            t = s_ref[s].astype(f32)
            if own is not None:
                t = jnp.where(me_ref[0] == s, own_ref[...], t)
            g = t if g is None else g + t
        wv = w_ref[...]
        mn = ADAM_B1 * m_ref[...] + (1.0 - ADAM_B1) * g
        vn = ADAM_B2 * v_ref[...] + (1.0 - ADAM_B2) * jnp.square(g)
        m_hat = mn / c1
        v_hat = vn / c2
        g_ref[...] = g
        d_ref[...] = -ADAM_LR * (m_hat / (jnp.sqrt(v_hat) + ADAM_EPS) + ADAM_WD * wv)
        nm_ref[...] = mn
        nv_ref[...] = vn

    out_shape = [jax.ShapeDtypeStruct((R, C), f32)] * 4
    if own is not None:
        me, part = own
        blk = pl.BlockSpec((tr, C), lambda i, me_: (i, 0))
        gs = pltpu.PrefetchScalarGridSpec(
            num_scalar_prefetch=1, grid=(R // tr,),
            in_specs=[blk, pl.BlockSpec((N_DEV, tr, C), lambda i, me_: (0, i, 0)), blk, blk, blk],
            out_specs=[blk] * 4)
        return pl.pallas_call(body, name=name, grid_spec=gs, out_shape=out_shape, compiler_params=_params(("parallel",)))(
            me.reshape(1).astype(jnp.int32), w, slabs, m, v, part)
    blk = pl.BlockSpec((tr, C), lambda i: (i, 0))
    return pl.pallas_call(
        body, name=name, grid=(R // tr,), in_specs=[blk, pl.BlockSpec((N_DEV, tr, C), lambda i: (0, i, 0)), blk, blk],
        out_specs=[blk] * 4, out_shape=out_shape, compiler_params=_params(("parallel",)))(w, slabs, m, v)


def _packed_rows(n, cols):
    rows = -(-n // cols)
    mult = 256 if rows > 256 else 16
    return -(-rows // mult) * mult


def _pack(arrs, cols, dtype, lead=()):
    nl = len(lead)
    flat = jnp.concatenate([a.reshape(lead + (-1,)).astype(dtype) for a in arrs], axis=-1)
    n = flat.shape[-1]
    rows = _packed_rows(n, cols)
    flat = jnp.pad(flat, [(0, 0)] * nl + [(0, rows * cols - n)])
    return flat.reshape(lead + (rows, cols))


def _unpack(packed, shapes, lead=()):
    flat = packed.reshape(lead + (-1,))
    out, off = [], 0
    for s in shapes:
        n = int(np.prod(s))
        out.append(flat[..., off:off + n].reshape(lead + tuple(s)))
        off += n
    return out


def _to_slabs(full, axis):
    s = full.shape
    r = full.reshape(s[:axis] + (N_DEV, s[axis] // N_DEV) + s[axis + 1:])
    return jnp.moveaxis(r, axis, 0)


def _from_slabs(slabs, axis):
    r = jnp.moveaxis(slabs, 0, axis)
    s = r.shape
    return r.reshape(s[:axis] + (s[axis] * s[axis + 1],) + s[axis + 2:])


BIG = [("ev_w_in", 2), ("ev_w_out", 1), ("od_w_in", 2), ("od_w_out", 1), ("ple_w_gate", 1), ("ple_w_proj", 2)]
SMALL = [("ev_conv_w", 2), ("od_norm_w", 1), ("od_dw_w", 2), ("od_dw_b", 1), ("od_ln_w", 1), ("od_ln_b", 1)]
REPL = ["ev_norm_w", "ev_conv_b", "ev_dt_bias_f", "ev_dt_bias_b", "ev_a_log_f", "ev_a_log_b", "ev_d_skip", "ev_gnorm_w", "ev_rpb",
        "ple_norm_w", "final_norm_w"]
WEIGHTS = ["ev_norm_w", "ev_w_in", "ev_conv_w", "ev_conv_b", "ev_dt_bias_f", "ev_dt_bias_b", "ev_a_log_f", "ev_a_log_b", "ev_d_skip",
           "ev_gnorm_w", "ev_rpb", "ev_w_out", "od_norm_w", "od_w_in", "od_dw_w", "od_dw_b", "od_ln_w", "od_ln_b", "od_w_out",
           "ple_norm_w", "ple_w_gate", "ple_w_proj", "final_norm_w"]
BIG_COLS = 1024
SMALL_COLS = 128


def _row_tile(L, width):
    return _pick(L, (256, 128)) if width <= 2048 else _pick(L, (128,))


def _ple_fwd(h1, p_i, nw, wg, wp, tag):
    L, D = h1.shape
    T = _row_tile(L, D)
    (hn2,) = rowop_fwd(_rms_fn, "ple_rms_" + tag, [(h1, D, 0)], [(nw, D, 0)], [(D, bf16)], T)
    gpre = mm(hn2, wg, "nn", "ple_gate_mm_" + tag)
    pp = mm(p_i, wp, "nn", "ple_proj_mm_" + tag)
    (h2,) = rowop_fwd(_ple_fwd_fn, "ple_comb_" + tag, [(h1, D, 0), (gpre, D, 0), (pp, D, 0)], [], [(D, f32)], T)
    return h2, (h1, p_i, nw, wg, wp, hn2, gpre, pp)


def _ple_bwd(res, dh2, tag):
    h1, p_i, nw, wg, wp, hn2, gpre, pp = res
    L, D = h1.shape
    T = _row_tile(L, D)
    dgpre, dpp = rowop_bwd(_ple_fn, "ple_comb_bwd_" + tag, [(gpre, D, 0), (pp, D, 0)], [], [(dh2, D, 0)], [bf16, bf16], [], T)
    dwg = mm(hn2, dgpre, "tn", "ple_gate_dw_" + tag)
    dwp = mm(p_i, dpp, "tn", "ple_proj_dw_" + tag)
    dhn2 = mm(dgpre, wg, "nt", "ple_gate_dx_" + tag)
    dh1, dnw = rowop_bwd(_rms_fn, "ple_rms_bwd_" + tag, [(h1, D, 0)], [(nw, D, 0)], [(dhn2, D, 0)], [f32], [True], T, add=(dh2, D, 0))
    return dh1, dnw, dwg, dwp


def _even_fwd(h, P, tag, side=None):
    L, D = h.shape
    H = D // HEAD_P
    T = _row_tile(L, D)
    DB = D // LANES
    (hn,) = rowop_fwd(_rms_fn, "ev_rms_" + tag, [(h, D, 0)], [(P["norm_w"], D, 0)], [(D, bf16)], T)
    um = mm(hn, P["w_main"], "nn", "ev_in_mm_" + tag, side=side)
    um, exchanged = um if side is not None else (um, None)
    udt = mm(hn, P["w_dt"], "nn", "ev_dt_mm_" + tag)
    pre, act = conv_fwd(um, DB, 2 * D, P["conv_w"], P["conv_b"], K_SSM, True, "ev_conv_" + tag)
    prep = _make_prep_fn(H)
    dt128, cs128 = rowop_fwd(prep, "ev_prep_" + tag, [(udt, LANES, 0)], [(P["bias128"], LANES, 0), (P["alog128"], LANES, 0)],
                             [(LANES, f32), (LANES, f32)], CHUNK)
    csr = _csr_from_cs(cs128, H)
    yf, sf = ssd_fwd(act, dt128, cs128, csr, 0, H, "ev_ssd_f_" + tag)
    yb, sb = ssd_fwd(act, dt128, cs128, csr, 1, H, "ev_ssd_b_" + tag)
    G = D // GROUP_W
    (yssd,) = rowop_fwd(_post_fn, "ev_post_" + tag, [(yf, GROUP_W, 0), (yb, GROUP_W, 0), (act, GROUP_W, 0), (um, GROUP_W, 0)],
                        [(P["dskipx"], GROUP_W, 0), (P["gnorm_w"], GROUP_W, 0)], [(GROUP_W, bf16)], T, ncb=G)
    bt = na_bias_table(P["rpb"])
    o = na_fwd(um, 3 * DB, 4 * DB, 5 * DB, bt, "ev_na_" + tag)
    (yna,) = rowop_fwd(_gate_fn, "ev_nagate_" + tag, [(o, D, 0), (um, D, 6)], [], [(D, bf16)], T)
    cat = jnp.concatenate([yssd, yna], axis=-1)
    h1 = mm(cat, P["w_out"], "nn", "ev_out_mm_" + tag, add=h)
    return h1, (h, hn, um, udt, pre, act, dt128, cs128, csr, yf, yb, sf, sb, bt, o, cat), exchanged


def _even_bwd(P, res, dh1, tag, side=None):
    h, hn, um, udt, pre, act, dt128, cs128, csr, yf, yb, sf, sb, bt, o, cat = res
    L, D = h.shape
    H = D // HEAD_P
    G = D // GROUP_W
    T = _row_tile(L, D)
    DB = D // LANES
    g = {}
    dcat = mm(dh1, P["w_out"], "nt", "ev_out_dx_" + tag)
    g["w_out"] = mm(cat, dh1, "tn", "ev_out_dw_" + tag)
    do, dg = rowop_bwd(_gate_fn, "ev_nagate_bwd_" + tag, [(o, D, 0), (um, D, 6)], [], [(dcat, D, 1)], [f32, bf16], [], T)
    dq, dk, dv, dbt, *exchanged = na_bwd(um, 3 * DB, 4 * DB, 5 * DB, bt, do, "ev_na_bwd_" + tag, side=side)
    _, rpb_vjp = jax.vjp(na_bias_table, P["rpb"])
    (g["rpb"],) = rpb_vjp(dbt)
    dy, dxs_a, dz, ddsk, dgn = rowop_bwd(
        _post_fn, "ev_post_bwd_" + tag, [(yf, GROUP_W, 0), (yb, GROUP_W, 0), (act, GROUP_W, 0), (um, GROUP_W, 0)],
        [(P["dskipx"], GROUP_W, 0), (P["gnorm_w"], GROUP_W, 0)], [(dcat, GROUP_W, 0)], [f32, None, f32, bf16], [True, True], T, ncb=G)
    g["d_skip"] = ddsk.reshape(H, HEAD_P).sum(axis=-1)
    g["gnorm_w"] = dgn.reshape(D)
    dxf, dbf, dcf, ddtf, dcsf, dcsrf = ssd_bwd(act, dt128, cs128, csr, sf, dy, 0, H, "ev_ssd_f_bwd_" + tag)
    dxb, dbb, dcb, ddtb, dcsb, dcsrb = ssd_bwd(act, dt128, cs128, csr, sb, dy, 1, H, "ev_ssd_b_bwd_" + tag)
    dact = jnp.concatenate([dxf + dxb + dxs_a, dbf + dbb, dcf + dcb], axis=-1)
    ddt128 = ddtf + ddtb
    dcs128 = dcsf + dcsb + _cs_from_dcsr(dcsrf, dcsrb, H)
    prep = _make_prep_fn(H)
    dudt, dbias, dalog = rowop_bwd(prep, "ev_prep_bwd_" + tag, [(udt, LANES, 0)], [(P["bias128"], LANES, 0), (P["alog128"], LANES, 0)],
                                   [(ddt128, LANES, 0), (dcs128, LANES, 0)], [bf16], [True, True], CHUNK)
    g["dt_bias_f"], g["dt_bias_b"] = dbias[0, :H], dbias[0, H:2 * H]
    g["a_log_f"], g["a_log_b"] = dalog[0, :H], dalog[0, H:2 * H]
    dxbc, g["conv_w"], dcb_ = conv_bwd(dact, um, DB, 2 * D, P["conv_w"], K_SSM, pre, bf16, "ev_conv_bwd_" + tag)
    g["conv_b"] = dcb_.reshape(2 * D)
    dum = jnp.concatenate([dz, dxbc, dq.astype(bf16), dk.astype(bf16), dv.astype(bf16), dg], axis=-1)
    dhn = mm(dum, P["w_main"], "nt", "ev_in_dx_" + tag)
    dhn = mm(dudt, P["w_dt"], "nt", "ev_dt_dx_" + tag, add=dhn)
    g["w_main"] = mm(hn, dum, "tn", "ev_in_dw_" + tag)
    g["w_dt"] = mm(hn, dudt, "tn", "ev_dt_dw_" + tag)
    dh, dnw = rowop_bwd(_rms_fn, "ev_rms_bwd_" + tag, [(h, D, 0)], [(P["norm_w"], D, 0)], [(dhn, D, 0)], [f32], [True], T, add=(dh1, D, 0))
    g["norm_w"] = dnw.reshape(D)
    return dh, g, (exchanged[0] if exchanged else None)


def _odd_fwd(h, P, tag, side=None):
    L, D = h.shape
    C = 2 * D
    T = _row_tile(L, D)
    (hn,) = rowop_fwd(_rms_fn, "od_rms_" + tag, [(h, D, 0)], [(P["norm_w"], D, 0)], [(D, bf16)], T)
    u = mm(hn, P["w_in"], "nn", "od_in_mm_" + tag, side=side)
    u, exchanged = u if side is not None else (u, None)
    (vglu,) = rowop_fwd(_glu_fn, "od_glu_" + tag, [(u, D, 0), (u, D, 2)], [], [(D, f32)], T, ncb=2)
    (vc,) = conv_fwd(vglu, 0, C, P["dw_w"], P["dw_b"], K_CONV, False, "od_conv_" + tag)
    TL = _row_tile(L, C)
    (t,) = rowop_fwd(_lngate_fn, "od_lngate_" + tag, [(vc, C, 0), (u, C, 2)], [(P["ln_w"], C, 0), (P["ln_b"], C, 0)], [(C, bf16)], TL)
    h1 = mm(t, P["w_out"], "nn", "od_out_mm_" + tag, add=h)
    return h1, (h, hn, u, vglu, vc, t), exchanged


def _odd_bwd(P, res, dh1, tag, side=None):
    h, hn, u, vglu, vc, t = res
    L, D = h.shape
    C = 2 * D
    T = _row_tile(L, D)
    TL = _row_tile(L, C)
    g = {}
    dt_ = mm(dh1, P["w_out"], "nt", "od_out_dx_" + tag)
    g["w_out"] = mm(t, dh1, "tn", "od_out_dw_" + tag)
    dvc, dg, dlnw, dlnb = rowop_bwd(_lngate_fn, "od_lngate_bwd_" + tag, [(vc, C, 0), (u, C, 2)], [(P["ln_w"], C, 0), (P["ln_b"], C, 0)],
                                    [(dt_, C, 0)], [f32, bf16], [True, True], TL)
    g["ln_w"], g["ln_b"] = dlnw.reshape(C), dlnb.reshape(C)
    dvglu, g["dw_w"], ddwb = conv_bwd(dvc, vglu, 0, C, P["dw_w"], K_CONV, None, f32, "od_conv_bwd_" + tag)
    g["dw_b"] = ddwb.reshape(C)
    da, dag = rowop_bwd(_glu_fn, "od_glu_bwd_" + tag, [(u, D, 0), (u, D, 2)], [], [(dvglu, D, 0)], [bf16, bf16], [], T, ncb=2)
    du = jnp.concatenate([da, dag, dg], axis=-1)
    exchanged = None
    if side is None:
        dhn = mm(du, P["w_in"], "nt", "od_in_dx_" + tag)
        g["w_in"] = mm(hn, du, "tn", "od_in_dw_" + tag)
    else:
        dhn, ex_a = mm(du, P["w_in"], "nt", "od_in_dx_" + tag, side=(side[0], side[1][:1]))
        g["w_in"], ex_b = mm(hn, du, "tn", "od_in_dw_" + tag, side=(side[0], side[1][1:]))
        exchanged = list(ex_a) + list(ex_b)
    dh, dnw = rowop_bwd(_rms_fn, "od_rms_bwd_" + tag, [(h, D, 0)], [(P["norm_w"], D, 0)], [(dhn, D, 0)], [f32], [True], T, add=(dh1, D, 0))
    g["norm_w"] = dnw.reshape(D)
    return dh, g, exchanged


def kernel(x, p, ev_norm_w, ev_w_in, ev_conv_w, ev_conv_b, ev_dt_bias_f, ev_dt_bias_b, ev_a_log_f, ev_a_log_b, ev_d_skip, ev_gnorm_w, ev_rpb, ev_w_out, od_norm_w, od_w_in, od_dw_w, od_dw_b, od_ln_w, od_ln_b, od_w_out, ple_norm_w, ple_w_gate, ple_w_proj, final_norm_w, loss_target, m_ev_norm_w, m_ev_w_in, m_ev_conv_w, m_ev_conv_b, m_ev_dt_bias_f, m_ev_dt_bias_b, m_ev_a_log_f, m_ev_a_log_b, m_ev_d_skip, m_ev_gnorm_w, m_ev_rpb, m_ev_w_out, m_od_norm_w, m_od_w_in, m_od_dw_w, m_od_dw_b, m_od_ln_w, m_od_ln_b, m_od_w_out, m_ple_norm_w, m_ple_w_gate, m_ple_w_proj, m_final_norm_w, v_ev_norm_w, v_ev_w_in, v_ev_conv_w, v_ev_conv_b, v_ev_dt_bias_f, v_ev_dt_bias_b, v_ev_a_log_f, v_ev_a_log_b, v_ev_d_skip, v_ev_gnorm_w, v_ev_rpb, v_ev_w_out, v_od_norm_w, v_od_w_in, v_od_dw_w, v_od_dw_b, v_od_ln_w, v_od_ln_b, v_od_w_out, v_ple_norm_w, v_ple_w_gate, v_ple_w_proj, v_final_norm_w):
    W = dict(ev_norm_w=ev_norm_w, ev_w_in=ev_w_in, ev_conv_w=ev_conv_w, ev_conv_b=ev_conv_b, ev_dt_bias_f=ev_dt_bias_f,
             ev_dt_bias_b=ev_dt_bias_b, ev_a_log_f=ev_a_log_f, ev_a_log_b=ev_a_log_b, ev_d_skip=ev_d_skip, ev_gnorm_w=ev_gnorm_w,
             ev_rpb=ev_rpb, ev_w_out=ev_w_out, od_norm_w=od_norm_w, od_w_in=od_w_in, od_dw_w=od_dw_w, od_dw_b=od_dw_b,
             od_ln_w=od_ln_w, od_ln_b=od_ln_b, od_w_out=od_w_out, ple_norm_w=ple_norm_w, ple_w_gate=ple_w_gate,
             ple_w_proj=ple_w_proj, final_norm_w=final_norm_w)
    M = dict(ev_norm_w=m_ev_norm_w, ev_w_in=m_ev_w_in, ev_conv_w=m_ev_conv_w, ev_conv_b=m_ev_conv_b, ev_dt_bias_f=m_ev_dt_bias_f,
             ev_dt_bias_b=m_ev_dt_bias_b, ev_a_log_f=m_ev_a_log_f, ev_a_log_b=m_ev_a_log_b, ev_d_skip=m_ev_d_skip,
             ev_gnorm_w=m_ev_gnorm_w, ev_rpb=m_ev_rpb, ev_w_out=m_ev_w_out, od_norm_w=m_od_norm_w, od_w_in=m_od_w_in,
             od_dw_w=m_od_dw_w, od_dw_b=m_od_dw_b, od_ln_w=m_od_ln_w, od_ln_b=m_od_ln_b, od_w_out=m_od_w_out,
             ple_norm_w=m_ple_norm_w, ple_w_gate=m_ple_w_gate, ple_w_proj=m_ple_w_proj, final_norm_w=m_final_norm_w)
    V = dict(ev_norm_w=v_ev_norm_w, ev_w_in=v_ev_w_in, ev_conv_w=v_ev_conv_w, ev_conv_b=v_ev_conv_b, ev_dt_bias_f=v_ev_dt_bias_f,
             ev_dt_bias_b=v_ev_dt_bias_b, ev_a_log_f=v_ev_a_log_f, ev_a_log_b=v_ev_a_log_b, ev_d_skip=v_ev_d_skip,
             ev_gnorm_w=v_ev_gnorm_w, ev_rpb=v_ev_rpb, ev_w_out=v_ev_w_out, od_norm_w=v_od_norm_w, od_w_in=v_od_w_in,
             od_dw_w=v_od_dw_w, od_dw_b=v_od_dw_b, od_ln_w=v_od_ln_w, od_ln_b=v_od_ln_b, od_w_out=v_od_w_out,
             ple_norm_w=v_ple_norm_w, ple_w_gate=v_ple_w_gate, ple_w_proj=v_ple_w_proj, final_norm_w=v_final_norm_w)

    h0 = x[0]
    L, D = h0.shape
    H = D // HEAD_P
    depth = p.shape[0]
    n_even, n_odd = (depth + 1) // 2, depth // 2
    me = 4 * lax.axis_index("x") + 2 * lax.axis_index("y") + lax.axis_index("c")

    small_shapes = [W[n].shape for n, _ in SMALL]
    (small_g,) = all_gather([_pack([W[n] for n, _ in SMALL], SMALL_COLS, f32)], "gather_small")
    F = {n: _from_slabs(a, ax) for (n, ax), a in zip(SMALL, _unpack(small_g, small_shapes, (N_DEV,)))}

    def layer_big(i):
        kind = "ev" if i % 2 == 0 else "od"
        return [(kind + "_w_in", i // 2, 1), (kind + "_w_out", i // 2, 0), ("ple_w_gate", i, 0), ("ple_w_proj", i, 1)]

    def layer_shards(i):
        return [W[n][k].astype(bf16) for n, k, _ in layer_big(i)]

    def even_params(e, w_in, w_out):
        return dict(
            norm_w=W["ev_norm_w"][e][None], w_main=jnp.concatenate([w_in[:, :3 * D], w_in[:, 3 * D + 2 * H:]], axis=1),
            w_dt=jnp.pad(w_in[:, 3 * D:3 * D + 2 * H], ((0, 0), (0, LANES - 2 * H))),
            conv_w=F["ev_conv_w"][e], conv_b=W["ev_conv_b"][e][None],
            bias128=jnp.pad(jnp.concatenate([W["ev_dt_bias_f"][e], W["ev_dt_bias_b"][e]]), (0, LANES - 2 * H))[None],
            alog128=jnp.pad(jnp.concatenate([W["ev_a_log_f"][e], W["ev_a_log_b"][e]]), (0, LANES - 2 * H))[None],
            dskipx=jnp.repeat(W["ev_d_skip"][e], HEAD_P)[None], gnorm_w=W["ev_gnorm_w"][e][None], rpb=W["ev_rpb"][e],
            w_out=w_out)

    def odd_params(e, w_in, w_out):
        return dict(norm_w=F["od_norm_w"][e][None], w_in=w_in, dw_w=F["od_dw_w"][e], dw_b=F["od_dw_b"][e][None],
                    ln_w=F["od_ln_w"][e][None], ln_b=F["od_ln_b"][e][None], w_out=w_out)

    h = h0
    saved = []
    gathered = all_gather(layer_shards(0), "gather_layer0")
    for i in range(depth):
        e = i // 2
        tag = str(i)
        w_in, w_out, w_gate, w_proj = [_from_slabs(a, ax) for a, (_, _, ax) in zip(gathered, layer_big(i))]
        side = ("gather", layer_shards(i + 1)) if i + 1 < depth else None
        if i % 2 == 0:
            P = even_params(e, w_in, w_out)
            h1, res, gathered = _even_fwd(h, P, tag, side)
        else:
            P = odd_params(e, w_in, w_out)
            h1, res, gathered = _odd_fwd(h, P, tag, side)
        h, pres = _ple_fwd(h1, p[i, 0], W["ple_norm_w"][i][None], w_gate, w_proj, tag)
        saved.append((P, res, pres))
    loss_tile, dh, dfinal = final_loss(h, W["final_norm_w"][None], loss_target[0])

    def ev_w_in_grad(g):
        return jnp.concatenate([g["w_main"][:, :3 * D], g["w_dt"][:, :2 * H], g["w_main"][:, 3 * D:]], axis=1)

    ev_g = [None] * n_even
    od_g = [None] * n_odd
    ple_g = [None] * depth
    recv = [None] * depth
    own = [None] * depth
    pending = None
    for i in reversed(range(depth)):
        P, res, pres = saved[i]
        tag = str(i)
        dh1, dnw, dwg, dwp = _ple_bwd(pres, dh, tag)
        ple_g[i] = dnw.reshape(D)
        side = ("a2a", pending) if pending is not None else None
        if i % 2 == 0:
            dh, g, exchanged = _even_bwd(P, res, dh1, tag, side)
            ev_g[i // 2] = g
            big_grads = [ev_w_in_grad(g), g["w_out"], dwg, dwp]
        else:
            dh, g, exchanged = _odd_bwd(P, res, dh1, tag, side)
            od_g[i // 2] = g
            big_grads = [g["w_in"], g["w_out"], dwg, dwp]
        if pending is not None:
            recv[i + 1] = exchanged
        pending = [_to_slabs(gr.astype(bf16), ax) for gr, (_, _, ax) in zip(big_grads, layer_big(i))]
        own[i] = [lax.dynamic_slice_in_dim(gr, me * (gr.shape[ax] // N_DEV), gr.shape[ax] // N_DEV, axis=ax)
                  for gr, (_, _, ax) in zip(big_grads, layer_big(i))]
    recv[0] = all_to_all(pending, "scatter_layer0")
    grad_x = dh[None]

    full = dict(
        ev_conv_w=jnp.stack([g["conv_w"] for g in ev_g]), od_norm_w=jnp.stack([g["norm_w"] for g in od_g]),
        od_dw_w=jnp.stack([g["dw_w"] for g in od_g]), od_dw_b=jnp.stack([g["dw_b"] for g in od_g]),
        od_ln_w=jnp.stack([g["ln_w"] for g in od_g]), od_ln_b=jnp.stack([g["ln_b"] for g in od_g]),
        ev_norm_w=jnp.stack([g["norm_w"] for g in ev_g]), ev_conv_b=jnp.stack([g["conv_b"] for g in ev_g]),
        ev_dt_bias_f=jnp.stack([g["dt_bias_f"] for g in ev_g]), ev_dt_bias_b=jnp.stack([g["dt_bias_b"] for g in ev_g]),
        ev_a_log_f=jnp.stack([g["a_log_f"] for g in ev_g]), ev_a_log_b=jnp.stack([g["a_log_b"] for g in ev_g]),
        ev_d_skip=jnp.stack([g["d_skip"] for g in ev_g]), ev_gnorm_w=jnp.stack([g["gnorm_w"] for g in ev_g]),
        ev_rpb=jnp.stack([g["rpb"] for g in ev_g]), ple_norm_w=jnp.stack(ple_g), final_norm_w=dfinal.reshape(D))

    out = {}
    where = {}
    for i in range(depth):
        for slot, (n, k, _) in enumerate(layer_big(i)):
            where.setdefault(n, {})[k] = (i, slot)
    for n, _ in BIG:
        shard = W[n].shape
        two_d = (int(np.prod(shard[:-1])), shard[-1])
        layers = [where[n][k] for k in range(shard[0])]
        slabs = jnp.stack([recv[i][slot] for i, slot in layers], axis=1)
        part = jnp.stack([own[i][slot] for i, slot in layers], axis=0)
        res = adamw(W[n].reshape(two_d), slabs.reshape((N_DEV,) + two_d), M[n].reshape(two_d), V[n].reshape(two_d),
                    (me, part.reshape(two_d)), "adamw_" + n)
        for kind, arr in zip(("grad", "delta", "new_m", "new_v"), res):
            out[kind + "_" + n] = arr.reshape(shard)
    small_full = _pack([_to_slabs(full[n], ax) for n, ax in SMALL], SMALL_COLS, f32, (N_DEV,))
    (small_recv,) = all_to_all([small_full], "scatter_small")
    res_small = adamw(_pack([W[n] for n, _ in SMALL], SMALL_COLS, f32), small_recv, _pack([M[n] for n, _ in SMALL], SMALL_COLS, f32),
                      _pack([V[n] for n, _ in SMALL], SMALL_COLS, f32), None, "adamw_small")
    for kind, arr in zip(("grad", "delta", "new_m", "new_v"), res_small):
        for (n, _), a in zip(SMALL, _unpack(arr, small_shapes)):
            out[kind + "_" + n] = a
    repl_shapes = [W[n].shape for n in REPL] + [(1,)]
    zero1 = jnp.zeros((1,), f32)
    repl_part = _pack([full[n] for n in REPL] + [loss_tile[0, :1]], SMALL_COLS, f32)
    (repl_all,) = all_gather([repl_part], "gather_repl")
    res_repl = adamw(_pack([W[n] for n in REPL] + [zero1], SMALL_COLS, f32), repl_all, _pack([M[n] for n in REPL] + [zero1], SMALL_COLS, f32),
                     _pack([V[n] for n in REPL] + [zero1], SMALL_COLS, f32), None, "adamw_repl")
    for kind, arr in zip(("grad", "delta", "new_m", "new_v"), res_repl):
        parts = _unpack(arr, repl_shapes)
        for n, a in zip(REPL, parts[:-1]):
            out[kind + "_" + n] = a
        if kind == "grad":
            loss = parts[-1].reshape(())

    return (loss, grad_x, *[out["grad_" + n] for n in WEIGHTS], *[out["delta_" + n] for n in WEIGHTS],
            *[out["new_m_" + n] for n in WEIGHTS], *[out["new_v_" + n] for n in WEIGHTS])
```

```python
import functools

import numpy as np
import jax
import jax.numpy as jnp
from jax import lax
from jax.experimental import pallas as pl
from jax.experimental.pallas import tpu as pltpu

f32 = jnp.float32
bf16 = jnp.bfloat16
HIGHEST = lax.Precision.HIGHEST

N_DEV = 8
EPS = 1e-6
GRID_W = 64
HEAD_P = 64
HEADS_PER_GROUP = 4
GROUP_W = HEAD_P * HEADS_PER_GROUP
N_STATE = 128
CHUNK = 128
K_SSM = 5
K_CONV = 31
NA_D = 128
NA_WR = 8
NA_WC = 16
PLE_DIM = 256
LANES = 128
CONV_PAD = 16
VMEM_LIMIT = 56 * 1024 * 1024
MM_VMEM_BUDGET = 40 * 1024 * 1024
MM_FULL_K = 2048
NA_ROWS_PER_STEP = 16
ADAMW_TILE_BYTES = 1024 * 1024
SSD_GROUPS_PER_STEP = 2

ADAM_LR = 0.001
ADAM_B1 = 0.9
ADAM_B2 = 0.999
ADAM_EPS = 1e-08
ADAM_WD = 0.01
ADAM_STEP = 10

NEG_BIG = -1e30


def _params(sem=None):
    if sem is None:
        return pltpu.CompilerParams(vmem_limit_bytes=VMEM_LIMIT)
    return pltpu.CompilerParams(vmem_limit_bytes=VMEM_LIMIT, dimension_semantics=sem)


def _pick(n, prefs):
    for t in prefs:
        if n % t == 0:
            return t
    return n


def _sigmoid(x):
    return jax.nn.sigmoid(x)


def _silu(x):
    return x * jax.nn.sigmoid(x)


_DN = {"nn": (((1,), (0,)), ((), ())), "nt": (((1,), (1,)), ((), ())), "tn": (((0,), (0,)), ((), ()))}


def mm(a, b, mode, name, out_dtype=f32, add=None, side=None):
    if mode == "nn":
        (M, K), (K2, N) = a.shape, b.shape
    elif mode == "nt":
        (M, K), (N, K2) = a.shape, b.shape
    else:
        (K, M), (K2, N) = a.shape, b.shape
    assert K == K2, (a.shape, b.shape, mode)
    sa, sb, so = a.dtype.itemsize, b.dtype.itemsize, jnp.dtype(out_dtype).itemsize
    tk = K if K <= MM_FULL_K else _pick(K, (1024, 512, 256, 128))
    nk = K // tk
    tms = [t for t in (1024, 512, 256, 128) if M % t == 0] or [M]
    tns = [t for t in (2048, 1024, 512, 256, 128) if N % t == 0] or [N]

    def vmem_bytes(tm, tn):
        n = 2 * (tm * tk * sa + tk * tn * sb) + 2 * tm * tn * so
        n += tm * tn * 4 if nk > 1 else 0
        n += 2 * tm * tn * add.dtype.itemsize if add is not None else 0
        return n

    ti, tj = 0, 0
    while vmem_bytes(tms[ti], tns[tj]) > MM_VMEM_BUDGET:
        if tj + 1 < len(tns) and (tns[tj] >= tms[ti] or ti + 1 >= len(tms)):
            tj += 1
        elif ti + 1 < len(tms):
            ti += 1
        else:
            break
    tm, tn = tms[ti], tns[tj]
    a_bytes, b_bytes = M * K * sa, K * N * sb
    i_outer = a_bytes + b_bytes * (M // tm) <= b_bytes + a_bytes * (N // tn)
    dn = _DN[mode]

    grid = (M // tm, N // tn, nk) if i_outer else (N // tn, M // tm, nk)
    ns = len(side[1]) if side is not None else 0
    n_add = 1 if add is not None else 0

    def body(a_ref, b_ref, *rest):
        add_ref = rest[0] if add is not None else None
        x_refs = rest[n_add:n_add + ns]
        o_ref = rest[n_add + ns]
        out_refs = rest[n_add + ns + 1:n_add + 2 * ns + 1]
        scratch = rest[n_add + 2 * ns + 1:]
        if side is not None:
            step = (pl.program_id(0) * grid[1] + pl.program_id(1)) * nk + pl.program_id(2)

            @pl.when(step == 0)
            def _():
                _exchange_start(side[0], x_refs, out_refs, scratch[-3:])

        def finish(r):
            if add is not None:
                r = r + add_ref[...].astype(f32)
            o_ref[...] = r.astype(out_dtype)

        part = lax.dot_general(a_ref[...].astype(bf16), b_ref[...].astype(bf16), dn, preferred_element_type=f32)
        if nk == 1:
            finish(part)
        else:
            acc = scratch[0]
            k = pl.program_id(2)

            @pl.when(k == 0)
            def _():
                acc[...] = part

            @pl.when(k > 0)
            def _():
                acc[...] += part

            @pl.when(k == nk - 1)
            def _():
                finish(acc[...])

        if side is not None:
            @pl.when(step == grid[0] * grid[1] * nk - 1)
            def _():
                _exchange_finish(side[0], x_refs, out_refs, scratch[-3:])

    def ij(p, q):
        return (p, q) if i_outer else (q, p)

    if mode == "tn":
        a_spec = pl.BlockSpec((tk, tm), lambda p, q, k: (k, ij(p, q)[0]))
    else:
        a_spec = pl.BlockSpec((tm, tk), lambda p, q, k: (ij(p, q)[0], k))
    if mode == "nt":
        b_spec = pl.BlockSpec((tn, tk), lambda p, q, k: (ij(p, q)[1], k))
    else:
        b_spec = pl.BlockSpec((tk, tn), lambda p, q, k: (k, ij(p, q)[1]))
    o_spec = pl.BlockSpec((tm, tn), lambda p, q, k: ij(p, q))
    in_specs = [a_spec, b_spec] + ([o_spec] if add is not None else [])
    args = (a, b) + ((add,) if add is not None else ())
    acc_scratch = [pltpu.VMEM((tm, tn), f32)] if nk > 1 else []
    o_shape = jax.ShapeDtypeStruct((M, N), out_dtype)
    if side is None:
        return pl.pallas_call(
            body, name=name, grid=grid, in_specs=in_specs, out_specs=o_spec, out_shape=o_shape, scratch_shapes=acc_scratch,
            compiler_params=_params(("parallel", "parallel", "arbitrary")))(*args)
    res = pl.pallas_call(
        body, name=name, grid=grid, in_specs=in_specs + [_HBM] * ns, out_specs=[o_spec] + [_HBM] * ns,
        out_shape=[o_shape] + _exchange_out_shapes(side[0], side[1]), scratch_shapes=acc_scratch + _exchange_scratch(ns),
        compiler_params=_params(("arbitrary", "arbitrary", "arbitrary")))(*args, *side[1])
    return res[0], res[1:]


def _row_spec(T, w, base):
    return pl.BlockSpec((T, w), lambda g, i: (i, base + g))


def _const_spec(r, w, base):
    return pl.BlockSpec((r, w), lambda g, i: (0, base + g))


def rowop_fwd(fn, name, rows, consts, outs, T, ncb=1):
    L = rows[0][0].shape[0]
    nr, nc = len(rows), len(consts)

    def body(*refs):
        ins = [r[...].astype(f32) for r in refs[:nr + nc]]
        res = fn(*ins)
        for o_ref, r in zip(refs[nr + nc:], res):
            o_ref[...] = r.astype(o_ref.dtype)

    in_specs = [_row_spec(T, w, b) for (_, w, b) in rows] + [_const_spec(a.shape[0], w, b) for (a, w, b) in consts]
    out_specs = [_row_spec(T, w, 0) for (w, _) in outs]
    out_shape = [jax.ShapeDtypeStruct((L, ncb * w), dt) for (w, dt) in outs]
    return pl.pallas_call(
        body, name=name, grid=(ncb, L // T), in_specs=in_specs, out_specs=out_specs, out_shape=out_shape,
        compiler_params=_params(("parallel", "parallel")))(*[a for (a, _, _) in rows], *[a for (a, _, _) in consts])


def rowop_bwd(fn, name, rows, consts, cts, row_grads, const_grads, T, ncb=1, add=None):
    L = rows[0][0].shape[0]
    nr, nc, nct = len(rows), len(consts), len(cts)
    n_in = nr + nc + nct + (1 if add is not None else 0)
    rg_idx = [i for i, d in enumerate(row_grads) if d is not None]
    cg_idx = [i for i, d in enumerate(const_grads) if d]

    def body(*refs):
        ins = [r[...].astype(f32) for r in refs[:nr + nc]]
        ct = tuple(r[...].astype(f32) for r in refs[nr + nc:nr + nc + nct])
        _, vjp = jax.vjp(fn, *ins)
        grads = vjp(ct)
        outs = refs[n_in:]
        for n, i in enumerate(rg_idx):
            gi = grads[i]
            if add is not None and n == 0:
                gi = gi + refs[n_in - 1][...].astype(f32)
            outs[n][...] = gi.astype(outs[n].dtype)
        first = pl.program_id(1) == 0
        for n, i in enumerate(cg_idx):
            o = outs[len(rg_idx) + n]

            @pl.when(first)
            def _(o=o):
                o[...] = jnp.zeros_like(o)

            o[...] += grads[nr + i]

    in_specs = ([_row_spec(T, w, b) for (_, w, b) in rows] + [_const_spec(a.shape[0], w, b) for (a, w, b) in consts]
                + [_row_spec(T, w, b) for (_, w, b) in cts] + ([_row_spec(T, add[1], add[2])] if add is not None else []))
    out_specs = ([_row_spec(T, rows[i][1], 0) for i in rg_idx] + [_const_spec(consts[i][0].shape[0], consts[i][1], 0) for i in cg_idx])
    out_shape = ([jax.ShapeDtypeStruct((L, ncb * rows[i][1]), row_grads[i]) for i in rg_idx]
                 + [jax.ShapeDtypeStruct((consts[i][0].shape[0], ncb * consts[i][1]), f32) for i in cg_idx])
    args = [a for (a, _, _) in rows] + [a for (a, _, _) in consts] + [a for (a, _, _) in cts] + ([add[0]] if add is not None else [])
    return pl.pallas_call(
        body, name=name, grid=(ncb, L // T), in_specs=in_specs, out_specs=out_specs, out_shape=out_shape,
        compiler_params=_params(("parallel", "arbitrary")))(*args)


def _rms_fn(x, w):
    return (x * lax.rsqrt(jnp.mean(x * x, axis=-1, keepdims=True) + EPS) * w,)


def _glu_fn(a, ag):
    return (a * _sigmoid(ag),)


def _gate_fn(o, g):
    return (o * _silu(g),)


def _ple_fn(gpre, pp):
    return (_sigmoid(gpre) * pp,)


def _ple_fwd_fn(h1, gpre, pp):
    return (h1 + _sigmoid(gpre) * pp,)


def _lngate_fn(vc, g, w, b):
    mu = jnp.mean(vc, axis=-1, keepdims=True)
    xc = vc - mu
    y = xc * lax.rsqrt(jnp.mean(xc * xc, axis=-1, keepdims=True) + EPS) * w + b
    return (_silu(y) * _silu(g),)


def _post_fn(yf, yb, xs, z, dsk, gw):
    y = (yf + yb + dsk * xs) * _silu(z)
    return (y * lax.rsqrt(jnp.mean(y * y, axis=-1, keepdims=True) + EPS) * gw,)


def _make_prep_fn(n_heads):
    def prep_fn(u, bias, alog):
        lane = lax.broadcasted_iota(jnp.int32, (1, LANES), 1)
        dt = jnp.where(lane < 2 * n_heads, jax.nn.softplus(u + bias), 0.0)
        da = dt * (-jnp.exp(alog))
        li = lax.broadcasted_iota(jnp.int32, (CHUNK, CHUNK), 0)
        si = lax.broadcasted_iota(jnp.int32, (CHUNK, CHUNK), 1)
        tril = (si <= li).astype(f32)
        triu = (si >= li).astype(f32)
        csf = jnp.dot(tril, da, precision=HIGHEST, preferred_element_type=f32)
        csb = jnp.dot(triu, da, precision=HIGHEST, preferred_element_type=f32)
        return dt, jnp.where(lane < n_heads, csf, csb)

    return prep_fn


def final_loss(h, w, tgt, name="final_loss"):
    L, D = h.shape
    T = _pick(L, (256, 128))

    def fn(x, w_, t):
        y = x * lax.rsqrt(jnp.mean(x * x, axis=-1, keepdims=True) + EPS) * w_
        err = jnp.square(y - t)
        return 0.5 * jnp.sum(jnp.mean(err, axis=-1, keepdims=True))

    def body(h_ref, w_ref, t_ref, loss_ref, dh_ref, dw_ref):
        @pl.when(pl.program_id(0) == 0)
        def _():
            loss_ref[...] = jnp.zeros_like(loss_ref)
            dw_ref[...] = jnp.zeros_like(dw_ref)

        t = t_ref[...]
        val, vjp = jax.vjp(lambda x, w_: fn(x, w_, t), h_ref[...], w_ref[...])
        dx, dw = vjp(jnp.ones((), f32))
        dh_ref[...] = dx
        dw_ref[...] += dw
        loss_ref[...] += jnp.broadcast_to(val, loss_ref.shape)

    return pl.pallas_call(
        body, name=name, grid=(L // T,),
        in_specs=[pl.BlockSpec((T, D), lambda i: (i, 0)), pl.BlockSpec((1, D), lambda i: (0, 0)), pl.BlockSpec((T, D), lambda i: (i, 0))],
        out_specs=[pl.BlockSpec((8, LANES), lambda i: (0, 0)), pl.BlockSpec((T, D), lambda i: (i, 0)), pl.BlockSpec((1, D), lambda i: (0, 0))],
        out_shape=[jax.ShapeDtypeStruct((8, LANES), f32), jax.ShapeDtypeStruct((L, D), f32), jax.ShapeDtypeStruct((1, D), f32)],
        compiler_params=_params(("arbitrary",)))(h, w, tgt)


CONV_ROWS = 128


def conv_fwd(x_arr, xbase, C, w, b, K, act, name):
    L = x_arr.shape[0]
    R = CONV_ROWS
    lo = CONV_PAD - K // 2

    def body(x_ref, w_ref, b_ref, *rest):
        xpad = rest[-1]
        outs = rest[:-1]
        xpad[0:CONV_PAD, :] = jnp.zeros((CONV_PAD, LANES), f32)
        xpad[CONV_PAD + L:CONV_PAD + L + CONV_PAD, :] = jnp.zeros((CONV_PAD, LANES), f32)
        xpad[CONV_PAD:CONV_PAD + L, :] = x_ref[...]

        def chunk(ci, carry):
            r0 = pl.multiple_of(ci * R, R)
            acc = jnp.broadcast_to(b_ref[...], (R, LANES))
            for k in range(K):
                acc = acc + w_ref[k:k + 1, :] * xpad[pl.ds(r0 + (lo + k), R), :]
            outs[0][pl.ds(r0, R), :] = acc
            if act:
                outs[1][pl.ds(r0, R), :] = _silu(acc)
            return carry

        lax.fori_loop(0, L // R, chunk, 0)

    col = pl.BlockSpec((L, LANES), lambda j: (0, j))
    n_out = 2 if act else 1
    return pl.pallas_call(
        body, name=name, grid=(C // LANES,),
        in_specs=[pl.BlockSpec((L, LANES), lambda j: (0, xbase + j)), pl.BlockSpec((K, LANES), lambda j: (0, j)),
                  pl.BlockSpec((1, LANES), lambda j: (0, j))],
        out_specs=[col] * n_out, out_shape=[jax.ShapeDtypeStruct((L, C), f32)] * n_out,
        scratch_shapes=[pltpu.VMEM((L + 2 * CONV_PAD, LANES), f32)],
        compiler_params=_params(("parallel",)))(x_arr, w, b)


def conv_bwd(dy, x_arr, xbase, C, w, K, pre, out_dtype, name):
    L = x_arr.shape[0]
    R = CONV_ROWS
    lo = CONV_PAD - K // 2
    act = pre is not None

    def body(*refs):
        if act:
            dy_ref, pre_ref, x_ref, w_ref, dx_ref, dw_ref, db_ref, dpad, xpad, accw, accb = refs
        else:
            dy_ref, x_ref, w_ref, dx_ref, dw_ref, db_ref, dpad, xpad, accw, accb = refs
        zeros = jnp.zeros((CONV_PAD, LANES), f32)
        for pad in (dpad, xpad):
            pad[0:CONV_PAD, :] = zeros
            pad[CONV_PAD + L:CONV_PAD + L + CONV_PAD, :] = zeros
        xpad[CONV_PAD:CONV_PAD + L, :] = x_ref[...]
        if act:
            p = pre_ref[...]
            s = _sigmoid(p)
            dpad[CONV_PAD:CONV_PAD + L, :] = dy_ref[...].astype(f32) * (s * (1.0 + p * (1.0 - s)))
        else:
            dpad[CONV_PAD:CONV_PAD + L, :] = dy_ref[...].astype(f32)
        accw[...] = jnp.zeros_like(accw)
        accb[...] = jnp.zeros_like(accb)

        def chunk(ci, carry):
            r0 = pl.multiple_of(ci * R, R)
            acc = jnp.zeros((R, LANES), f32)
            for k in range(K):
                acc = acc + w_ref[k:k + 1, :] * dpad[pl.ds(r0 + (CONV_PAD + K // 2 - k), R), :]
            dx_ref[pl.ds(r0, R), :] = acc.astype(out_dtype)
            d = dpad[pl.ds(r0 + CONV_PAD, R), :]
            accb[...] += jnp.sum(d.reshape(R // 8, 8, LANES), axis=0)
            for k in range(K):
                prod = d * xpad[pl.ds(r0 + (lo + k), R), :]
                accw[k] += jnp.sum(prod.reshape(R // 8, 8, LANES), axis=0)
            return carry

        lax.fori_loop(0, L // R, chunk, 0)
        dw_ref[...] = jnp.sum(accw[...], axis=1)
        db_ref[...] = jnp.sum(accb[...], axis=0, keepdims=True)

    col = pl.BlockSpec((L, LANES), lambda j: (0, j))
    in_specs = [col] + ([col] if act else []) + [pl.BlockSpec((L, LANES), lambda j: (0, xbase + j)), pl.BlockSpec((K, LANES), lambda j: (0, j))]
    args = (dy,) + ((pre,) if act else ()) + (x_arr, w)
    return pl.pallas_call(
        body, name=name, grid=(C // LANES,), in_specs=in_specs,
        out_specs=[col, pl.BlockSpec((K, LANES), lambda j: (0, j)), pl.BlockSpec((1, LANES), lambda j: (0, j))],
        out_shape=[jax.ShapeDtypeStruct((L, C), out_dtype), jax.ShapeDtypeStruct((K, C), f32), jax.ShapeDtypeStruct((1, C), f32)],
        scratch_shapes=[pltpu.VMEM((L + 2 * CONV_PAD, LANES), f32), pltpu.VMEM((L + 2 * CONV_PAD, LANES), f32),
                        pltpu.VMEM((K, 8, LANES), f32), pltpu.VMEM((8, LANES), f32)],
        compiler_params=_params(("parallel",)))(*args)


def _ssd_chunk(X, Bm, Cm, dtc, csc, csr, S, lane_off, dirn):
    Qn = X.shape[0]
    lane = lax.broadcasted_iota(jnp.int32, (1, LANES), 1)
    head = lax.shift_right_logical(lax.broadcasted_iota(jnp.int32, (1, GROUP_W), 1), 6)
    dtx = jnp.zeros((Qn, GROUP_W), f32)
    csx = jnp.zeros((Qn, GROUP_W), f32)
    cols = []
    for j in range(HEADS_PER_GROUP):
        sel = lane == lane_off + j
        dcol = jnp.sum(jnp.where(sel, dtc, 0.0), axis=1, keepdims=True)
        ccol = jnp.sum(jnp.where(sel, csc, 0.0), axis=1, keepdims=True)
        dtx = jnp.where(head == j, dcol, dtx)
        csx = jnp.where(head == j, ccol, csx)
        cols.append(ccol)
    Xd = X * dtx
    edge = Qn - 1 if dirn == 0 else 0
    cs_edge = csx[edge:edge + 1, :]
    li = lax.broadcasted_iota(jnp.int32, (Qn, Qn), 0)
    si = lax.broadcasted_iota(jnp.int32, (Qn, Qn), 1)
    mask = (si <= li) if dirn == 0 else (si >= li)
    Cb = Cm.astype(bf16)
    Bb = Bm.astype(bf16)
    G = lax.dot_general(Cb, Bb, _DN["nt"], preferred_element_type=f32)
    Y = jnp.exp(csx) * jnp.dot(Cb, S.astype(bf16), preferred_element_type=f32)
    for j in range(HEADS_PER_GROUP):
        row = csr[j:j + 1, :]
        Lm = jnp.exp(jnp.where(mask, cols[j] - row, NEG_BIG))
        Wj = (G * Lm).astype(bf16)
        Xj = jnp.where(head == j, Xd, 0.0).astype(bf16)
        Y = Y + jnp.dot(Wj, Xj, preferred_element_type=f32)
    S_new = S * jnp.exp(cs_edge) + lax.dot_general(Bb, (Xd * jnp.exp(cs_edge - csx)).astype(bf16), _DN["tn"], preferred_element_type=f32)
    return Y, S_new


def _ssd_specs(D, G, nchunk, dirn, descending):
    def cidx(c):
        return nchunk - 1 - c if descending else c

    gps = _ssd_groups_per_step(D, G)
    nb = D // N_STATE
    x_spec = pl.BlockSpec((CHUNK, gps * GROUP_W), lambda c, g: (cidx(c), g))
    b_spec = pl.BlockSpec((CHUNK, gps * N_STATE), lambda c, g: (cidx(c), nb // gps + g))
    c_spec = pl.BlockSpec((CHUNK, gps * N_STATE), lambda c, g: (cidx(c), (nb + G) // gps + g))
    n_spec = pl.BlockSpec((CHUNK, gps * N_STATE), lambda c, g: (cidx(c), g))
    lane_spec = pl.BlockSpec((CHUNK, LANES), lambda c, g: (cidx(c), 0))
    csr_spec = pl.BlockSpec((gps, 8, CHUNK), lambda c, g: (dirn * (G // gps) + g, 0, cidx(c)))
    dcsr_spec = pl.BlockSpec((gps, 8, CHUNK), lambda c, g: (g, 0, cidx(c)))
    s_spec = pl.BlockSpec((gps, None, N_STATE, GROUP_W), lambda c, g: (g, cidx(c), 0, 0))
    return x_spec, b_spec, c_spec, n_spec, lane_spec, csr_spec, dcsr_spec, s_spec


def _ssd_groups_per_step(D, G):
    nb = D // N_STATE
    return SSD_GROUPS_PER_STEP if (G % SSD_GROUPS_PER_STEP == 0 and nb % SSD_GROUPS_PER_STEP == 0) else 1


def ssd_fwd(act, dt128, cs128, csr, dirn, n_heads, name):
    L = act.shape[0]
    D = act.shape[1] // 2
    G = D // GROUP_W
    nchunk = L // CHUNK
    x_spec, b_spec, c_spec, _, lane_spec, csr_spec, _, s_spec = _ssd_specs(D, G, nchunk, dirn, dirn == 1)

    gps = _ssd_groups_per_step(D, G)

    def body(x_ref, b_ref, c_ref, dt_ref, cs_ref, csr_ref, y_ref, ssave_ref, S):
        c = pl.program_id(0)
        for gg in range(gps):
            g = pl.program_id(1) * gps + gg

            @pl.when(c == 0)
            def _():
                S[g] = jnp.zeros((N_STATE, GROUP_W), f32)

            xs_, ns_ = pl.ds(gg * GROUP_W, GROUP_W), pl.ds(gg * N_STATE, N_STATE)
            s_in = S[g]
            ssave_ref[gg] = s_in
            y, s_new = _ssd_chunk(x_ref[:, xs_], b_ref[:, ns_], c_ref[:, ns_], dt_ref[...], cs_ref[...], csr_ref[gg], s_in,
                                  dirn * n_heads + HEADS_PER_GROUP * g, dirn)
            y_ref[:, xs_] = y
            S[g] = s_new

    return pl.pallas_call(
        body, name=name, grid=(nchunk, G // gps),
        in_specs=[x_spec, b_spec, c_spec, lane_spec, lane_spec, csr_spec],
        out_specs=[x_spec, s_spec],
        out_shape=[jax.ShapeDtypeStruct((L, D), f32), jax.ShapeDtypeStruct((G, nchunk, N_STATE, GROUP_W), f32)],
        scratch_shapes=[pltpu.VMEM((G, N_STATE, GROUP_W), f32)],
        compiler_params=_params(("arbitrary", "arbitrary")))(act, act, act, dt128, cs128, csr)


def ssd_bwd(act, dt128, cs128, csr, ssave, dy, dirn, n_heads, name):
    L = act.shape[0]
    D = act.shape[1] // 2
    G = D // GROUP_W
    nchunk = L // CHUNK
    x_spec, b_spec, c_spec, n_spec, lane_spec, csr_in, dcsr_spec, s_spec = _ssd_specs(D, G, nchunk, dirn, dirn == 0)

    gps = _ssd_groups_per_step(D, G)

    def body(x_ref, b_ref, c_ref, dt_ref, cs_ref, csr_ref, s_ref, dy_ref, dx_ref, db_ref, dc_ref, ddt_ref, dcs_ref, dcsr_ref, dS):
        c = pl.program_id(0)

        @pl.when(pl.program_id(1) == 0)
        def _():
            ddt_ref[...] = jnp.zeros_like(ddt_ref)
            dcs_ref[...] = jnp.zeros_like(dcs_ref)

        ddt_sum, dcs_sum = None, None
        for gg in range(gps):
            g = pl.program_id(1) * gps + gg

            @pl.when(c == 0)
            def _():
                dS[g] = jnp.zeros((N_STATE, GROUP_W), f32)

            xs_, ns_ = pl.ds(gg * GROUP_W, GROUP_W), pl.ds(gg * N_STATE, N_STATE)
            lane_off = dirn * n_heads + HEADS_PER_GROUP * g
            _, vjp = jax.vjp(lambda X, Bm, Cm, dtc, csc, csr_, S: _ssd_chunk(X, Bm, Cm, dtc, csc, csr_, S, lane_off, dirn),
                             x_ref[:, xs_], b_ref[:, ns_], c_ref[:, ns_], dt_ref[...], cs_ref[...], csr_ref[gg], s_ref[gg])
            dX, dB, dC, ddt, dcs, dcsr, dS_in = vjp((dy_ref[:, xs_], dS[g]))
            dx_ref[:, xs_] = dX
            db_ref[:, ns_] = dB
            dc_ref[:, ns_] = dC
            dcsr_ref[gg] = dcsr
            dS[g] = dS_in
            ddt_sum = ddt if ddt_sum is None else ddt_sum + ddt
            dcs_sum = dcs if dcs_sum is None else dcs_sum + dcs

        ddt_ref[...] += ddt_sum
        dcs_ref[...] += dcs_sum

    return pl.pallas_call(
        body, name=name, grid=(nchunk, G // gps),
        in_specs=[x_spec, b_spec, c_spec, lane_spec, lane_spec, csr_in, s_spec, x_spec],
        out_specs=[x_spec, n_spec, n_spec, lane_spec, lane_spec, dcsr_spec],
        out_shape=[jax.ShapeDtypeStruct((L, D), f32), jax.ShapeDtypeStruct((L, G * N_STATE), f32), jax.ShapeDtypeStruct((L, G * N_STATE), f32),
                   jax.ShapeDtypeStruct((L, LANES), f32), jax.ShapeDtypeStruct((L, LANES), f32), jax.ShapeDtypeStruct((G, 8, L), f32)],
        scratch_shapes=[pltpu.VMEM((G, N_STATE, GROUP_W), f32)],
        compiler_params=_params(("arbitrary", "arbitrary")))(act, act, act, dt128, cs128, csr, ssave, dy)


def _csr_from_cs(cs128, n_heads):
    L = cs128.shape[0]
    t = cs128[:, :2 * n_heads].T.reshape(2 * n_heads // HEADS_PER_GROUP, HEADS_PER_GROUP, L)
    return jnp.pad(t, ((0, 0), (0, 8 - HEADS_PER_GROUP), (0, 0)))


def _cs_from_dcsr(dcsr_f, dcsr_b, n_heads):
    L = dcsr_f.shape[-1]
    t = jnp.concatenate([dcsr_f, dcsr_b], axis=0)[:, :HEADS_PER_GROUP, :].reshape(2 * n_heads, L).T
    return jnp.pad(t, ((0, 0), (0, LANES - 2 * n_heads)))


def na_bias_table(rpb):
    nh = rpb.shape[0]
    cols = np.arange(GRID_W)
    col_start = np.clip(cols - NA_WC // 2, 0, GRID_W - NA_WC)
    col_valid = (cols[None, :] >= col_start[:, None]) & (cols[None, :] < col_start[:, None] + NA_WC)
    col_off = np.clip(cols[None, :] - cols[:, None], -(NA_WC - 1), NA_WC - 1) + NA_WC - 1
    rm = np.zeros((NA_WR, NA_WR, 2 * NA_WR - 1), np.float32)
    for d in range(NA_WR):
        for i in range(NA_WR):
            rm[d, i, i - d + NA_WR - 1] = 1.0
    cm = np.zeros((GRID_W, GRID_W, 2 * NA_WC - 1), np.float32)
    cm[np.arange(GRID_W)[:, None], np.arange(GRID_W)[None, :], col_off] = 1.0
    bt = jnp.einsum("hrc,dir,qkc->hdqik", rpb, jnp.asarray(rm), jnp.asarray(cm), precision=HIGHEST)
    bt = jnp.where(jnp.asarray(col_valid)[None, None, :, None, :], bt, NEG_BIG)
    return bt.reshape(nh, NA_WR, GRID_W, NA_WR * GRID_W)


def _na_window(r, rows):
    rs = jnp.clip(r - NA_WR // 2, 0, rows - NA_WR)
    return rs, r - rs


def _na_probs(q, kw, bias):
    s = lax.dot_general(q.astype(bf16), kw.astype(bf16), _DN["nt"], preferred_element_type=f32) * (NA_D ** -0.5) + bias
    m = jnp.max(s, axis=-1, keepdims=True)
    p = jnp.exp(s - m)
    return p / jnp.sum(p, axis=-1, keepdims=True)


def na_fwd(um, qbase, kbase, vbase, bt, name):
    L = um.shape[0]
    nh = bt.shape[0]
    rows = L // GRID_W
    WK = NA_WR * GRID_W

    RB = _pick(rows, (NA_ROWS_PER_STEP, 2, 1))

    def body(q_ref, k_ref, v_ref, bt_ref, o_ref):
        win = [_na_window(pl.program_id(1) * RB + rb, rows) for rb in range(RB)]
        t0s = [pl.multiple_of(rs * GRID_W, GRID_W) for rs, _ in win]
        qrows = [pl.ds(rb * GRID_W, GRID_W) for rb in range(RB)]
        ps = [_na_probs(q_ref[qr, :], k_ref[pl.ds(t0, WK), :], bt_ref[delta]) for qr, t0, (_, delta) in zip(qrows, t0s, win)]
        os_ = [jnp.dot(p.astype(bf16), v_ref[pl.ds(t0, WK), :].astype(bf16), preferred_element_type=f32) for p, t0 in zip(ps, t0s)]
        for qr, o in zip(qrows, os_):
            o_ref[qr, :] = o

    kv = lambda base: pl.BlockSpec((L, NA_D), lambda h, r: (0, base + h))
    return pl.pallas_call(
        body, name=name, grid=(nh, rows // RB),
        in_specs=[pl.BlockSpec((RB * GRID_W, NA_D), lambda h, r: (r, qbase + h)), kv(kbase), kv(vbase),
                  pl.BlockSpec((None, NA_WR, GRID_W, WK), lambda h, r: (h, 0, 0, 0))],
        out_specs=pl.BlockSpec((RB * GRID_W, NA_D), lambda h, r: (r, h)),
        out_shape=jax.ShapeDtypeStruct((L, nh * NA_D), f32),
        compiler_params=_params(("parallel", "arbitrary")))(um, um, um, bt)


def na_bwd(um, qbase, kbase, vbase, bt, do, name, side=None):
    L = um.shape[0]
    nh = bt.shape[0]
    rows = L // GRID_W
    WK = NA_WR * GRID_W
    scale = NA_D ** -0.5

    RB = _pick(rows, (NA_ROWS_PER_STEP, 2, 1))

    ns = len(side[1]) if side is not None else 0
    nsteps = rows // RB

    def body(q_ref, k_ref, v_ref, bt_ref, do_ref, *rest):
        x_refs = rest[:ns]
        dq_ref, dk_ref, dv_ref, dbt_ref = rest[ns:ns + 4]
        out_refs = rest[ns + 4:2 * ns + 4]
        sems = rest[2 * ns + 4:]
        if side is not None:
            step = pl.program_id(0) * nsteps + pl.program_id(1)

            @pl.when(step == 0)
            def _():
                _exchange_start(side[0], x_refs, out_refs, sems)

        @pl.when(pl.program_id(1) == 0)
        def _():
            dk_ref[...] = jnp.zeros_like(dk_ref)
            dv_ref[...] = jnp.zeros_like(dv_ref)
            dbt_ref[...] = jnp.zeros_like(dbt_ref)

        win = [_na_window(pl.program_id(1) * RB + rb, rows) for rb in range(RB)]
        t0s = [pl.multiple_of(rs * GRID_W, GRID_W) for rs, _ in win]
        qrows = [pl.ds(rb * GRID_W, GRID_W) for rb in range(RB)]
        qs_ = [q_ref[qr, :] for qr in qrows]
        kws = [k_ref[pl.ds(t0, WK), :].astype(bf16) for t0 in t0s]
        vws = [v_ref[pl.ds(t0, WK), :].astype(bf16) for t0 in t0s]
        dos = [do_ref[qr, :].astype(bf16) for qr in qrows]
        ps = [_na_probs(q, kw, bt_ref[delta]) for q, kw, (_, delta) in zip(qs_, kws, win)]
        dps = [lax.dot_general(do_, vw, _DN["nt"], preferred_element_type=f32) for do_, vw in zip(dos, vws)]
        dss = [p * (dp - jnp.sum(dp * p, axis=-1, keepdims=True)) for p, dp in zip(ps, dps)]
        dsbs = [ds.astype(bf16) for ds in dss]
        dqs = [jnp.dot(dsb, kw, preferred_element_type=f32) * scale for dsb, kw in zip(dsbs, kws)]
        dks = [lax.dot_general(dsb, q.astype(bf16), _DN["tn"], preferred_element_type=f32) * scale for dsb, q in zip(dsbs, qs_)]
        dvs = [lax.dot_general(p.astype(bf16), do_, _DN["tn"], preferred_element_type=f32) for p, do_ in zip(ps, dos)]
        for rb in range(RB):
            dq_ref[qrows[rb], :] = dqs[rb]
            dbt_ref[win[rb][1]] += dss[rb]
            dk_ref[pl.ds(t0s[rb], WK), :] += dks[rb]
            dv_ref[pl.ds(t0s[rb], WK), :] += dvs[rb]

        if side is not None:
            @pl.when(step == nh * nsteps - 1)
            def _():
                _exchange_finish(side[0], x_refs, out_refs, sems)

    kv = lambda base: pl.BlockSpec((L, NA_D), lambda h, r: (0, base + h))
    qs = lambda base: pl.BlockSpec((RB * GRID_W, NA_D), lambda h, r: (r, base + h))
    bts = pl.BlockSpec((None, NA_WR, GRID_W, WK), lambda h, r: (h, 0, 0, 0))
    big = jax.ShapeDtypeStruct((L, nh * NA_D), f32)
    in_specs = [qs(qbase), kv(kbase), kv(vbase), bts, qs(0)]
    out_specs = [qs(0), kv(0), kv(0), bts]
    out_shape = [big, big, big, jax.ShapeDtypeStruct(bt.shape, f32)]
    if side is None:
        return pl.pallas_call(
            body, name=name, grid=(nh, nsteps), in_specs=in_specs, out_specs=out_specs, out_shape=out_shape,
            compiler_params=_params(("parallel", "arbitrary")))(um, um, um, bt, do)
    res = pl.pallas_call(
        body, name=name, grid=(nh, nsteps), in_specs=in_specs + [_HBM] * ns, out_specs=out_specs + [_HBM] * ns,
        out_shape=out_shape + _exchange_out_shapes(side[0], side[1]), scratch_shapes=_exchange_scratch(ns),
        compiler_params=_params(("arbitrary", "arbitrary")))(um, um, um, bt, do, *side[1])
    return tuple(res[:4]) + (res[4:],)


_HBM = pl.BlockSpec(memory_space=pltpu.HBM)


def all_gather(xs, name):
    n = len(xs)

    def body(*refs):
        _exchange_start("gather", refs[:n], refs[n:2 * n], refs[2 * n:])
        _exchange_finish("gather", refs[:n], refs[n:2 * n], refs[2 * n:])

    return pl.pallas_call(
        body, name=name, in_specs=[_HBM] * n, out_specs=[_HBM] * n, out_shape=_exchange_out_shapes("gather", xs),
        scratch_shapes=_exchange_scratch(n))(*xs)


def _exchange_out_shapes(kind, xs):
    return [jax.ShapeDtypeStruct(((N_DEV,) + x.shape) if kind == "gather" else x.shape, x.dtype) for x in xs]


def _exchange_scratch(n):
    return [pltpu.SemaphoreType.DMA((7 * n,)), pltpu.SemaphoreType.DMA((7 * n,)), pltpu.SemaphoreType.DMA((n,))]


def _gather_copies(x_refs, out_refs, sems):
    send_sems, recv_sems, local_sems = sems
    n = len(x_refs)
    mx, my, mc = lax.axis_index("x"), lax.axis_index("y"), lax.axis_index("c")
    me, sibling = (mx, my, mc), (mx, my, 1 - mc)
    chips = [(1 - mx, my), (mx, 1 - my), (1 - mx, 1 - my)]

    def slab(t, px, py, pc):
        return out_refs[t].at[4 * px + 2 * py + pc]

    def copy(t, k, block, to, src=None):
        return pltpu.make_async_remote_copy(
            src_ref=slab(t, *block) if src is None else src, dst_ref=slab(t, *block),
            send_sem=send_sems.at[7 * t + k], recv_sem=recv_sems.at[7 * t + k], device_id=to, device_id_type=pl.DeviceIdType.MESH)

    mine = [pltpu.make_async_copy(x_refs[t], slab(t, *me), local_sems.at[t]) for t in range(n)]
    first = []
    for t in range(n):
        first.append(copy(t, 0, me, sibling, src=x_refs[t]))
        first += [copy(t, 1 + j, me, (*chip, mc), src=x_refs[t]) for j, chip in enumerate(chips)]
    return mine, first, copy, me, sibling, chips, mc


def _exchange_start(kind, x_refs, out_refs, sems):
    if kind == "gather":
        mine, first = _gather_copies(x_refs, out_refs, sems)[:2]
    else:
        mine, first = _a2a_copies(x_refs, out_refs, sems)
    for cp in mine + first:
        cp.start()


def _exchange_finish(kind, x_refs, out_refs, sems):
    n = len(x_refs)
    if kind == "gather":
        mine, first, copy, me, sibling, chips, mc = _gather_copies(x_refs, out_refs, sems)
        passed = []
        for t in range(n):
            for j, chip in enumerate(chips):
                copy(t, 1 + j, (*chip, mc), me).wait_recv()
                fwd = copy(t, 4 + j, (*chip, mc), sibling)
                fwd.start()
                passed.append(fwd)
        for t in range(n):
            copy(t, 0, sibling, me).wait_recv()
            for j, chip in enumerate(chips):
                copy(t, 4 + j, (*chip, 1 - mc), me).wait_recv()
        for cp in first + passed:
            cp.wait_send()
    else:
        mine, first = _a2a_copies(x_refs, out_refs, sems)
        for cp in first:
            cp.wait()
    for cp in mine:
        cp.wait()


def _a2a_copies(x_refs, out_refs, sems):
    send_sems, recv_sems, local_sems = sems
    n = len(x_refs)
    mx, my, mc = lax.axis_index("x"), lax.axis_index("y"), lax.axis_index("c")
    me = 4 * mx + 2 * my + mc
    mine = [pltpu.make_async_copy(x_refs[t].at[me], out_refs[t].at[me], local_sems.at[t]) for t in range(n)]
    copies = []
    for t in range(n):
        for k in range(1, N_DEV):
            px = 1 - mx if k & 4 else mx
            py = 1 - my if k & 2 else my
            pc = 1 - mc if k & 1 else mc
            peer = 4 * px + 2 * py + pc
            copies.append(pltpu.make_async_remote_copy(
                src_ref=x_refs[t].at[peer], dst_ref=out_refs[t].at[me], send_sem=send_sems.at[7 * t + k - 1],
                recv_sem=recv_sems.at[7 * t + k - 1], device_id=(px, py, pc), device_id_type=pl.DeviceIdType.MESH))
    return mine, copies


def all_to_all(xs, name):
    n = len(xs)

    def body(*refs):
        _exchange_start("a2a", refs[:n], refs[n:2 * n], refs[2 * n:])
        _exchange_finish("a2a", refs[:n], refs[n:2 * n], refs[2 * n:])

    return pl.pallas_call(
        body, name=name, in_specs=[_HBM] * n, out_specs=[_HBM] * n, out_shape=_exchange_out_shapes("a2a", xs),
        scratch_shapes=_exchange_scratch(n))(*xs)


def adamw(w, slabs, m, v, own, name):
    R, C = w.shape
    tr = _pick(R, tuple(t for t in (256, 128, 64, 32, 16, 8) if t * C * 4 <= ADAMW_TILE_BYTES) or (8,))
    c1 = 1.0 - ADAM_B1 ** ADAM_STEP
    c2 = 1.0 - ADAM_B2 ** ADAM_STEP

    def body(*refs):
        if own is not None:
            me_ref, w_ref, s_ref, m_ref, v_ref, own_ref, g_ref, d_ref, nm_ref, nv_ref = refs
        else:
            w_ref, s_ref, m_ref, v_ref, g_ref, d_ref, nm_ref, nv_ref = refs
        g = None
        for s in range(N_DEV):
            t = s_ref[s].astype(f32)
            if own is not None:
                t = jnp.where(me_ref[0] == s, own_ref[...], t)
            g = t if g is None else g + t
        wv = w_ref[...]
        mn = ADAM_B1 * m_ref[...] + (1.0 - ADAM_B1) * g
        vn = ADAM_B2 * v_ref[...] + (1.0 - ADAM_B2) * jnp.square(g)
        m_hat = mn / c1
        v_hat = vn / c2
        g_ref[...] = g
        d_ref[...] = -ADAM_LR * (m_hat / (jnp.sqrt(v_hat) + ADAM_EPS) + ADAM_WD * wv)
        nm_ref[...] = mn
        nv_ref[...] = vn

    out_shape = [jax.ShapeDtypeStruct((R, C), f32)] * 4
    if own is not None:
        me, part = own
        blk = pl.BlockSpec((tr, C), lambda i, me_: (i, 0))
        gs = pltpu.PrefetchScalarGridSpec(
            num_scalar_prefetch=1, grid=(R // tr,),
            in_specs=[blk, pl.BlockSpec((N_DEV, tr, C), lambda i, me_: (0, i, 0)), blk, blk, blk],
            out_specs=[blk] * 4)
        return pl.pallas_call(body, name=name, grid_spec=gs, out_shape=out_shape, compiler_params=_params(("parallel",)))(
            me.reshape(1).astype(jnp.int32), w, slabs, m, v, part)
    blk = pl.BlockSpec((tr, C), lambda i: (i, 0))
    return pl.pallas_call(
        body, name=name, grid=(R // tr,), in_specs=[blk, pl.BlockSpec((N_DEV, tr, C), lambda i: (0, i, 0)), blk, blk],
        out_specs=[blk] * 4, out_shape=out_shape, compiler_params=_params(("parallel",)))(w, slabs, m, v)


def _packed_rows(n, cols):
    rows = -(-n // cols)
    mult = 256 if rows > 256 else 16
    return -(-rows // mult) * mult


def _pack(arrs, cols, dtype, lead=()):
    nl = len(lead)
    flat = jnp.concatenate([a.reshape(lead + (-1,)).astype(dtype) for a in arrs], axis=-1)
    n = flat.shape[-1]
    rows = _packed_rows(n, cols)
    flat = jnp.pad(flat, [(0, 0)] * nl + [(0, rows * cols - n)])
    return flat.reshape(lead + (rows, cols))


def _unpack(packed, shapes, lead=()):
    flat = packed.reshape(lead + (-1,))
    out, off = [], 0
    for s in shapes:
        n = int(np.prod(s))
        out.append(flat[..., off:off + n].reshape(lead + tuple(s)))
        off += n
    return out


def _to_slabs(full, axis):
    s = full.shape
    r = full.reshape(s[:axis] + (N_DEV, s[axis] // N_DEV) + s[axis + 1:])
    return jnp.moveaxis(r, axis, 0)


def _from_slabs(slabs, axis):
    r = jnp.moveaxis(slabs, 0, axis)
    s = r.shape
    return r.reshape(s[:axis] + (s[axis] * s[axis + 1],) + s[axis + 2:])


BIG = [("ev_w_in", 2), ("ev_w_out", 1), ("od_w_in", 2), ("od_w_out", 1), ("ple_w_gate", 1), ("ple_w_proj", 2)]
SMALL = [("ev_conv_w", 2), ("od_norm_w", 1), ("od_dw_w", 2), ("od_dw_b", 1), ("od_ln_w", 1), ("od_ln_b", 1)]
REPL = ["ev_norm_w", "ev_conv_b", "ev_dt_bias_f", "ev_dt_bias_b", "ev_a_log_f", "ev_a_log_b", "ev_d_skip", "ev_gnorm_w", "ev_rpb",
        "ple_norm_w", "final_norm_w"]
WEIGHTS = ["ev_norm_w", "ev_w_in", "ev_conv_w", "ev_conv_b", "ev_dt_bias_f", "ev_dt_bias_b", "ev_a_log_f", "ev_a_log_b", "ev_d_skip",
           "ev_gnorm_w", "ev_rpb", "ev_w_out", "od_norm_w", "od_w_in", "od_dw_w", "od_dw_b", "od_ln_w", "od_ln_b", "od_w_out",
           "ple_norm_w", "ple_w_gate", "ple_w_proj", "final_norm_w"]
BIG_COLS = 1024
SMALL_COLS = 128


def _row_tile(L, width):
    return _pick(L, (256, 128)) if width <= 2048 else _pick(L, (128,))


def _ple_fwd(h1, p_i, nw, wg, wp, tag):
    L, D = h1.shape
    T = _row_tile(L, D)
    (hn2,) = rowop_fwd(_rms_fn, "ple_rms_" + tag, [(h1, D, 0)], [(nw, D, 0)], [(D, bf16)], T)
    gpre = mm(hn2, wg, "nn", "ple_gate_mm_" + tag)
    pp = mm(p_i, wp, "nn", "ple_proj_mm_" + tag)
    (h2,) = rowop_fwd(_ple_fwd_fn, "ple_comb_" + tag, [(h1, D, 0), (gpre, D, 0), (pp, D, 0)], [], [(D, f32)], T)
    return h2, (h1, p_i, nw, wg, wp, hn2, gpre, pp)


def _ple_bwd(res, dh2, tag):
    h1, p_i, nw, wg, wp, hn2, gpre, pp = res
    L, D = h1.shape
    T = _row_tile(L, D)
    dgpre, dpp = rowop_bwd(_ple_fn, "ple_comb_bwd_" + tag, [(gpre, D, 0), (pp, D, 0)], [], [(dh2, D, 0)], [bf16, bf16], [], T)
    dwg = mm(hn2, dgpre, "tn", "ple_gate_dw_" + tag)
    dwp = mm(p_i, dpp, "tn", "ple_proj_dw_" + tag)
    dhn2 = mm(dgpre, wg, "nt", "ple_gate_dx_" + tag)
    dh1, dnw = rowop_bwd(_rms_fn, "ple_rms_bwd_" + tag, [(h1, D, 0)], [(nw, D, 0)], [(dhn2, D, 0)], [f32], [True], T, add=(dh2, D, 0))
    return dh1, dnw, dwg, dwp


def _even_fwd(h, P, tag, side=None):
    L, D = h.shape
    H = D // HEAD_P
    T = _row_tile(L, D)
    DB = D // LANES
    (hn,) = rowop_fwd(_rms_fn, "ev_rms_" + tag, [(h, D, 0)], [(P["norm_w"], D, 0)], [(D, bf16)], T)
    um = mm(hn, P["w_main"], "nn", "ev_in_mm_" + tag, side=side)
    um, exchanged = um if side is not None else (um, None)
    udt = mm(hn, P["w_dt"], "nn", "ev_dt_mm_" + tag)
    pre, act = conv_fwd(um, DB, 2 * D, P["conv_w"], P["conv_b"], K_SSM, True, "ev_conv_" + tag)
    prep = _make_prep_fn(H)
    dt128, cs128 = rowop_fwd(prep, "ev_prep_" + tag, [(udt, LANES, 0)], [(P["bias128"], LANES, 0), (P["alog128"], LANES, 0)],
                             [(LANES, f32), (LANES, f32)], CHUNK)
    csr = _csr_from_cs(cs128, H)
    yf, sf = ssd_fwd(act, dt128, cs128, csr, 0, H, "ev_ssd_f_" + tag)
    yb, sb = ssd_fwd(act, dt128, cs128, csr, 1, H, "ev_ssd_b_" + tag)
    G = D // GROUP_W
    (yssd,) = rowop_fwd(_post_fn, "ev_post_" + tag, [(yf, GROUP_W, 0), (yb, GROUP_W, 0), (act, GROUP_W, 0), (um, GROUP_W, 0)],
                        [(P["dskipx"], GROUP_W, 0), (P["gnorm_w"], GROUP_W, 0)], [(GROUP_W, bf16)], T, ncb=G)
    bt = na_bias_table(P["rpb"])
    o = na_fwd(um, 3 * DB, 4 * DB, 5 * DB, bt, "ev_na_" + tag)
    (yna,) = rowop_fwd(_gate_fn, "ev_nagate_" + tag, [(o, D, 0), (um, D, 6)], [], [(D, bf16)], T)
    cat = jnp.concatenate([yssd, yna], axis=-1)
    if P["w_out"] is None:
        P["w_out"] = _from_slabs(exchanged[0], 0)
    h1 = mm(cat, P["w_out"], "nn", "ev_out_mm_" + tag, add=h)
    return h1, (h, hn, um, udt, pre, act, dt128, cs128, csr, yf, yb, sf, sb, bt, o, cat), exchanged


def _even_bwd(P, res, dh1, tag, side=None, early=None):
    h, hn, um, udt, pre, act, dt128, cs128, csr, yf, yb, sf, sb, bt, o, cat = res
    L, D = h.shape
    H = D // HEAD_P
    G = D // GROUP_W
    T = _row_tile(L, D)
    DB = D // LANES
    g = {}
    dcat = mm(dh1, P["w_out"], "nt", "ev_out_dx_" + tag)
    g["w_out"] = mm(cat, dh1, "tn", "ev_out_dw_" + tag)
    do, dg = rowop_bwd(_gate_fn, "ev_nagate_bwd_" + tag, [(o, D, 0), (um, D, 6)], [], [(dcat, D, 1)], [f32, bf16], [], T)
    dq, dk, dv, dbt, *exchanged = na_bwd(um, 3 * DB, 4 * DB, 5 * DB, bt, do, "ev_na_bwd_" + tag, side=side)
    _, rpb_vjp = jax.vjp(na_bias_table, P["rpb"])
    (g["rpb"],) = rpb_vjp(dbt)
    dy, dxs_a, dz, ddsk, dgn = rowop_bwd(
        _post_fn, "ev_post_bwd_" + tag, [(yf, GROUP_W, 0), (yb, GROUP_W, 0), (act, GROUP_W, 0), (um, GROUP_W, 0)],
        [(P["dskipx"], GROUP_W, 0), (P["gnorm_w"], GROUP_W, 0)], [(dcat, GROUP_W, 0)], [f32, None, f32, bf16], [True, True], T, ncb=G)
    g["d_skip"] = ddsk.reshape(H, HEAD_P).sum(axis=-1)
    g["gnorm_w"] = dgn.reshape(D)
    dxf, dbf, dcf, ddtf, dcsf, dcsrf = ssd_bwd(act, dt128, cs128, csr, sf, dy, 0, H, "ev_ssd_f_bwd_" + tag)
    dxb, dbb, dcb, ddtb, dcsb, dcsrb = ssd_bwd(act, dt128, cs128, csr, sb, dy, 1, H, "ev_ssd_b_bwd_" + tag)
    dact = jnp.concatenate([dxf + dxb + dxs_a, dbf + dbb, dcf + dcb], axis=-1)
    ddt128 = ddtf + ddtb
    dcs128 = dcsf + dcsb + _cs_from_dcsr(dcsrf, dcsrb, H)
    prep = _make_prep_fn(H)
    dudt, dbias, dalog = rowop_bwd(prep, "ev_prep_bwd_" + tag, [(udt, LANES, 0)], [(P["bias128"], LANES, 0), (P["alog128"], LANES, 0)],
                                   [(ddt128, LANES, 0), (dcs128, LANES, 0)], [bf16], [True, True], CHUNK)
    g["dt_bias_f"], g["dt_bias_b"] = dbias[0, :H], dbias[0, H:2 * H]
    g["a_log_f"], g["a_log_b"] = dalog[0, :H], dalog[0, H:2 * H]
    dxbc, g["conv_w"], dcb_ = conv_bwd(dact, um, DB, 2 * D, P["conv_w"], K_SSM, pre, bf16, "ev_conv_bwd_" + tag)
    g["conv_b"] = dcb_.reshape(2 * D)
    dum = jnp.concatenate([dz, dxbc, dq.astype(bf16), dk.astype(bf16), dv.astype(bf16), dg], axis=-1)
    early_recv = None
    if early is None:
        dhn = mm(dum, P["w_main"], "nt", "ev_in_dx_" + tag)
    else:
        slabs = [_to_slabs(gr.astype(bf16), ax) for gr, ax in zip([g["w_out"]] + early[0], [0] + early[1])]
        dhn, early_recv = mm(dum, P["w_main"], "nt", "ev_in_dx_" + tag, side=("a2a", slabs))
    dhn = mm(dudt, P["w_dt"], "nt", "ev_dt_dx_" + tag, add=dhn)
    g["w_main"] = mm(hn, dum, "tn", "ev_in_dw_" + tag)
    g["w_dt"] = mm(hn, dudt, "tn", "ev_dt_dw_" + tag)
    dh, dnw = rowop_bwd(_rms_fn, "ev_rms_bwd_" + tag, [(h, D, 0)], [(P["norm_w"], D, 0)], [(dhn, D, 0)], [f32], [True], T, add=(dh1, D, 0))
    g["norm_w"] = dnw.reshape(D)
    return dh, g, (exchanged[0] if exchanged else None), early_recv


def _odd_fwd(h, P, tag, side=None):
    L, D = h.shape
    C = 2 * D
    T = _row_tile(L, D)
    (hn,) = rowop_fwd(_rms_fn, "od_rms_" + tag, [(h, D, 0)], [(P["norm_w"], D, 0)], [(D, bf16)], T)
    u = mm(hn, P["w_in"], "nn", "od_in_mm_" + tag, side=side)
    u, exchanged = u if side is not None else (u, None)
    (vglu,) = rowop_fwd(_glu_fn, "od_glu_" + tag, [(u, D, 0), (u, D, 2)], [], [(D, f32)], T, ncb=2)
    (vc,) = conv_fwd(vglu, 0, C, P["dw_w"], P["dw_b"], K_CONV, False, "od_conv_" + tag)
    TL = _row_tile(L, C)
    (t,) = rowop_fwd(_lngate_fn, "od_lngate_" + tag, [(vc, C, 0), (u, C, 2)], [(P["ln_w"], C, 0), (P["ln_b"], C, 0)], [(C, bf16)], TL)
    h1 = mm(t, P["w_out"], "nn", "od_out_mm_" + tag, add=h)
    return h1, (h, hn, u, vglu, vc, t), exchanged


def _odd_bwd(P, res, dh1, tag, side=None):
    h, hn, u, vglu, vc, t = res
    L, D = h.shape
    C = 2 * D
    T = _row_tile(L, D)
    TL = _row_tile(L, C)
    g = {}
    dt_ = mm(dh1, P["w_out"], "nt", "od_out_dx_" + tag)
    g["w_out"] = mm(t, dh1, "tn", "od_out_dw_" + tag)
    dvc, dg, dlnw, dlnb = rowop_bwd(_lngate_fn, "od_lngate_bwd_" + tag, [(vc, C, 0), (u, C, 2)], [(P["ln_w"], C, 0), (P["ln_b"], C, 0)],
                                    [(dt_, C, 0)], [f32, bf16], [True, True], TL)
    g["ln_w"], g["ln_b"] = dlnw.reshape(C), dlnb.reshape(C)
    dvglu, g["dw_w"], ddwb = conv_bwd(dvc, vglu, 0, C, P["dw_w"], K_CONV, None, f32, "od_conv_bwd_" + tag)
    g["dw_b"] = ddwb.reshape(C)
    da, dag = rowop_bwd(_glu_fn, "od_glu_bwd_" + tag, [(u, D, 0), (u, D, 2)], [], [(dvglu, D, 0)], [bf16, bf16], [], T, ncb=2)
    du = jnp.concatenate([da, dag, dg], axis=-1)
    exchanged = None
    if side is None:
        dhn = mm(du, P["w_in"], "nt", "od_in_dx_" + tag)
        g["w_in"] = mm(hn, du, "tn", "od_in_dw_" + tag)
    else:
        dhn, ex_a = mm(du, P["w_in"], "nt", "od_in_dx_" + tag, side=(side[0], side[1][:1]))
        g["w_in"], ex_b = mm(hn, du, "tn", "od_in_dw_" + tag, side=(side[0], side[1][1:]))
        exchanged = list(ex_a) + list(ex_b)
    dh, dnw = rowop_bwd(_rms_fn, "od_rms_bwd_" + tag, [(h, D, 0)], [(P["norm_w"], D, 0)], [(dhn, D, 0)], [f32], [True], T, add=(dh1, D, 0))
    g["norm_w"] = dnw.reshape(D)
    return dh, g, exchanged


def kernel(x, p, ev_norm_w, ev_w_in, ev_conv_w, ev_conv_b, ev_dt_bias_f, ev_dt_bias_b, ev_a_log_f, ev_a_log_b, ev_d_skip, ev_gnorm_w, ev_rpb, ev_w_out, od_norm_w, od_w_in, od_dw_w, od_dw_b, od_ln_w, od_ln_b, od_w_out, ple_norm_w, ple_w_gate, ple_w_proj, final_norm_w, loss_target, m_ev_norm_w, m_ev_w_in, m_ev_conv_w, m_ev_conv_b, m_ev_dt_bias_f, m_ev_dt_bias_b, m_ev_a_log_f, m_ev_a_log_b, m_ev_d_skip, m_ev_gnorm_w, m_ev_rpb, m_ev_w_out, m_od_norm_w, m_od_w_in, m_od_dw_w, m_od_dw_b, m_od_ln_w, m_od_ln_b, m_od_w_out, m_ple_norm_w, m_ple_w_gate, m_ple_w_proj, m_final_norm_w, v_ev_norm_w, v_ev_w_in, v_ev_conv_w, v_ev_conv_b, v_ev_dt_bias_f, v_ev_dt_bias_b, v_ev_a_log_f, v_ev_a_log_b, v_ev_d_skip, v_ev_gnorm_w, v_ev_rpb, v_ev_w_out, v_od_norm_w, v_od_w_in, v_od_dw_w, v_od_dw_b, v_od_ln_w, v_od_ln_b, v_od_w_out, v_ple_norm_w, v_ple_w_gate, v_ple_w_proj, v_final_norm_w):
    W = dict(ev_norm_w=ev_norm_w, ev_w_in=ev_w_in, ev_conv_w=ev_conv_w, ev_conv_b=ev_conv_b, ev_dt_bias_f=ev_dt_bias_f,
             ev_dt_bias_b=ev_dt_bias_b, ev_a_log_f=ev_a_log_f, ev_a_log_b=ev_a_log_b, ev_d_skip=ev_d_skip, ev_gnorm_w=ev_gnorm_w,
             ev_rpb=ev_rpb, ev_w_out=ev_w_out, od_norm_w=od_norm_w, od_w_in=od_w_in, od_dw_w=od_dw_w, od_dw_b=od_dw_b,
             od_ln_w=od_ln_w, od_ln_b=od_ln_b, od_w_out=od_w_out, ple_norm_w=ple_norm_w, ple_w_gate=ple_w_gate,
             ple_w_proj=ple_w_proj, final_norm_w=final_norm_w)
    M = dict(ev_norm_w=m_ev_norm_w, ev_w_in=m_ev_w_in, ev_conv_w=m_ev_conv_w, ev_conv_b=m_ev_conv_b, ev_dt_bias_f=m_ev_dt_bias_f,
             ev_dt_bias_b=m_ev_dt_bias_b, ev_a_log_f=m_ev_a_log_f, ev_a_log_b=m_ev_a_log_b, ev_d_skip=m_ev_d_skip,
             ev_gnorm_w=m_ev_gnorm_w, ev_rpb=m_ev_rpb, ev_w_out=m_ev_w_out, od_norm_w=m_od_norm_w, od_w_in=m_od_w_in,
             od_dw_w=m_od_dw_w, od_dw_b=m_od_dw_b, od_ln_w=m_od_ln_w, od_ln_b=m_od_ln_b, od_w_out=m_od_w_out,
             ple_norm_w=m_ple_norm_w, ple_w_gate=m_ple_w_gate, ple_w_proj=m_ple_w_proj, final_norm_w=m_final_norm_w)
    V = dict(ev_norm_w=v_ev_norm_w, ev_w_in=v_ev_w_in, ev_conv_w=v_ev_conv_w, ev_conv_b=v_ev_conv_b, ev_dt_bias_f=v_ev_dt_bias_f,
             ev_dt_bias_b=v_ev_dt_bias_b, ev_a_log_f=v_ev_a_log_f, ev_a_log_b=v_ev_a_log_b, ev_d_skip=v_ev_d_skip,
             ev_gnorm_w=v_ev_gnorm_w, ev_rpb=v_ev_rpb, ev_w_out=v_ev_w_out, od_norm_w=v_od_norm_w, od_w_in=v_od_w_in,
             od_dw_w=v_od_dw_w, od_dw_b=v_od_dw_b, od_ln_w=v_od_ln_w, od_ln_b=v_od_ln_b, od_w_out=v_od_w_out,
             ple_norm_w=v_ple_norm_w, ple_w_gate=v_ple_w_gate, ple_w_proj=v_ple_w_proj, final_norm_w=v_final_norm_w)

    h0 = x[0]
    L, D = h0.shape
    H = D // HEAD_P
    depth = p.shape[0]
    n_even, n_odd = (depth + 1) // 2, depth // 2
    me = 4 * lax.axis_index("x") + 2 * lax.axis_index("y") + lax.axis_index("c")

    small_shapes = [W[n].shape for n, _ in SMALL]
    (small_g,) = all_gather([_pack([W[n] for n, _ in SMALL], SMALL_COLS, f32)], "gather_small")
    F = {n: _from_slabs(a, ax) for (n, ax), a in zip(SMALL, _unpack(small_g, small_shapes, (N_DEV,)))}

    def layer_big(i):
        kind = "ev" if i % 2 == 0 else "od"
        return [(kind + "_w_in", i // 2, 1), (kind + "_w_out", i // 2, 0), ("ple_w_gate", i, 0), ("ple_w_proj", i, 1)]

    def layer_shards(i):
        return [W[n][k].astype(bf16) for n, k, _ in layer_big(i)]

    def even_params(e, w_in, w_out):
        return dict(
            norm_w=W["ev_norm_w"][e][None], w_main=jnp.concatenate([w_in[:, :3 * D], w_in[:, 3 * D + 2 * H:]], axis=1),
            w_dt=jnp.pad(w_in[:, 3 * D:3 * D + 2 * H], ((0, 0), (0, LANES - 2 * H))),
            conv_w=F["ev_conv_w"][e], conv_b=W["ev_conv_b"][e][None],
            bias128=jnp.pad(jnp.concatenate([W["ev_dt_bias_f"][e], W["ev_dt_bias_b"][e]]), (0, LANES - 2 * H))[None],
            alog128=jnp.pad(jnp.concatenate([W["ev_a_log_f"][e], W["ev_a_log_b"][e]]), (0, LANES - 2 * H))[None],
            dskipx=jnp.repeat(W["ev_d_skip"][e], HEAD_P)[None], gnorm_w=W["ev_gnorm_w"][e][None], rpb=W["ev_rpb"][e],
            w_out=w_out)

    def odd_params(e, w_in, w_out):
        return dict(norm_w=F["od_norm_w"][e][None], w_in=w_in, dw_w=F["od_dw_w"][e], dw_b=F["od_dw_b"][e][None],
                    ln_w=F["od_ln_w"][e][None], ln_b=F["od_ln_b"][e][None], w_out=w_out)

    h = h0
    saved = []
    gathered = all_gather(layer_shards(0)[:1], "gather_layer0")
    for i in range(depth):
        e = i // 2
        tag = str(i)
        nxt = layer_shards(i + 1) if i + 1 < depth else []
        if i == 0:
            w_in = _from_slabs(gathered[0], layer_big(0)[0][2])
            P = even_params(e, w_in, None)
            h1, res, exchanged = _even_fwd(h, P, tag, ("gather", layer_shards(0)[1:] + nxt))
            w_gate, w_proj = [_from_slabs(a, ax) for a, (_, _, ax) in zip(exchanged[1:3], layer_big(0)[2:])]
            gathered = exchanged[3:]
        else:
            w_in, w_out, w_gate, w_proj = [_from_slabs(a, ax) for a, (_, _, ax) in zip(gathered, layer_big(i))]
            side = ("gather", nxt) if nxt else None
            if i % 2 == 0:
                P = even_params(e, w_in, w_out)
                h1, res, gathered = _even_fwd(h, P, tag, side)
            else:
                P = odd_params(e, w_in, w_out)
                h1, res, gathered = _odd_fwd(h, P, tag, side)
        h, pres = _ple_fwd(h1, p[i, 0], W["ple_norm_w"][i][None], w_gate, w_proj, tag)
        saved.append((P, res, pres))
    loss_tile, dh, dfinal = final_loss(h, W["final_norm_w"][None], loss_target[0])

    def ev_w_in_grad(g):
        return jnp.concatenate([g["w_main"][:, :3 * D], g["w_dt"][:, :2 * H], g["w_main"][:, 3 * D:]], axis=1)

    ev_g = [None] * n_even
    od_g = [None] * n_odd
    ple_g = [None] * depth
    recv = [None] * depth
    own = [None] * depth
    pending = None
    for i in reversed(range(depth)):
        P, res, pres = saved[i]
        tag = str(i)
        dh1, dnw, dwg, dwp = _ple_bwd(pres, dh, tag)
        ple_g[i] = dnw.reshape(D)
        side = ("a2a", pending) if pending is not None else None
        if i % 2 == 0:
            early = ([dwg, dwp], [ax for _, _, ax in layer_big(0)[2:]]) if i == 0 else None
            dh, g, exchanged, early_recv = _even_bwd(P, res, dh1, tag, side, early)
            ev_g[i // 2] = g
            big_grads = [ev_w_in_grad(g), g["w_out"], dwg, dwp]
        else:
            dh, g, exchanged = _odd_bwd(P, res, dh1, tag, side)
            od_g[i // 2] = g
            big_grads = [g["w_in"], g["w_out"], dwg, dwp]
        if pending is not None:
            recv[i + 1] = exchanged
        pending = [_to_slabs(gr.astype(bf16), ax) for gr, (_, _, ax) in zip(big_grads, layer_big(i))]
        own[i] = [lax.dynamic_slice_in_dim(gr, me * (gr.shape[ax] // N_DEV), gr.shape[ax] // N_DEV, axis=ax)
                  for gr, (_, _, ax) in zip(big_grads, layer_big(i))]
    recv[0] = list(all_to_all(pending[:1], "scatter_layer0")) + list(early_recv)
    grad_x = dh[None]

    full = dict(
        ev_conv_w=jnp.stack([g["conv_w"] for g in ev_g]), od_norm_w=jnp.stack([g["norm_w"] for g in od_g]),
        od_dw_w=jnp.stack([g["dw_w"] for g in od_g]), od_dw_b=jnp.stack([g["dw_b"] for g in od_g]),
        od_ln_w=jnp.stack([g["ln_w"] for g in od_g]), od_ln_b=jnp.stack([g["ln_b"] for g in od_g]),
        ev_norm_w=jnp.stack([g["norm_w"] for g in ev_g]), ev_conv_b=jnp.stack([g["conv_b"] for g in ev_g]),
        ev_dt_bias_f=jnp.stack([g["dt_bias_f"] for g in ev_g]), ev_dt_bias_b=jnp.stack([g["dt_bias_b"] for g in ev_g]),
        ev_a_log_f=jnp.stack([g["a_log_f"] for g in ev_g]), ev_a_log_b=jnp.stack([g["a_log_b"] for g in ev_g]),
        ev_d_skip=jnp.stack([g["d_skip"] for g in ev_g]), ev_gnorm_w=jnp.stack([g["gnorm_w"] for g in ev_g]),
        ev_rpb=jnp.stack([g["rpb"] for g in ev_g]), ple_norm_w=jnp.stack(ple_g), final_norm_w=dfinal.reshape(D))

    out = {}
    where = {}
    for i in range(depth):
        for slot, (n, k, _) in enumerate(layer_big(i)):
            where.setdefault(n, {})[k] = (i, slot)
    for n, _ in BIG:
        shard = W[n].shape
        two_d = (int(np.prod(shard[:-1])), shard[-1])
        layers = [where[n][k] for k in range(shard[0])]
        slabs = jnp.stack([recv[i][slot] for i, slot in layers], axis=1)
        part = jnp.stack([own[i][slot] for i, slot in layers], axis=0)
        res = adamw(W[n].reshape(two_d), slabs.reshape((N_DEV,) + two_d), M[n].reshape(two_d), V[n].reshape(two_d),
                    (me, part.reshape(two_d)), "adamw_" + n)
        for kind, arr in zip(("grad", "delta", "new_m", "new_v"), res):
            out[kind + "_" + n] = arr.reshape(shard)
    small_full = _pack([_to_slabs(full[n], ax) for n, ax in SMALL], SMALL_COLS, f32, (N_DEV,))
    (small_recv,) = all_to_all([small_full], "scatter_small")
    res_small = adamw(_pack([W[n] for n, _ in SMALL], SMALL_COLS, f32), small_recv, _pack([M[n] for n, _ in SMALL], SMALL_COLS, f32),
                      _pack([V[n] for n, _ in SMALL], SMALL_COLS, f32), None, "adamw_small")
    for kind, arr in zip(("grad", "delta", "new_m", "new_v"), res_small):
        for (n, _), a in zip(SMALL, _unpack(arr, small_shapes)):
            out[kind + "_" + n] = a
    repl_shapes = [W[n].shape for n in REPL] + [(1,)]
    zero1 = jnp.zeros((1,), f32)
    repl_part = _pack([full[n] for n in REPL] + [loss_tile[0, :1]], SMALL_COLS, f32)
    (repl_all,) = all_gather([repl_part], "gather_repl")
    res_repl = adamw(_pack([W[n] for n in REPL] + [zero1], SMALL_COLS, f32), repl_all, _pack([M[n] for n in REPL] + [zero1], SMALL_COLS, f32),
                     _pack([V[n] for n in REPL] + [zero1], SMALL_COLS, f32), None, "adamw_repl")
    for kind, arr in zip(("grad", "delta", "new_m", "new_v"), res_repl):
        parts = _unpack(arr, repl_shapes)
        for n, a in zip(REPL, parts[:-1]):
            out[kind + "_" + n] = a
        if kind == "grad":
            loss = parts[-1].reshape(())

    return (loss, grad_x, *[out["grad_" + n] for n in WEIGHTS], *[out["delta_" + n] for n in WEIGHTS],
            *[out["new_m_" + n] for n in WEIGHTS], *[out["new_v_" + n] for n in WEIGHTS])
```

```python
import functools

import numpy as np
import jax
import jax.numpy as jnp
from jax import lax
from jax.experimental import pallas as pl
from jax.experimental.pallas import tpu as pltpu

f32 = jnp.float32
bf16 = jnp.bfloat16
HIGHEST = lax.Precision.HIGHEST

N_DEV = 8
EPS = 1e-6
GRID_W = 64
HEAD_P = 64
HEADS_PER_GROUP = 4
GROUP_W = HEAD_P * HEADS_PER_GROUP
N_STATE = 128
CHUNK = 128
K_SSM = 5
K_CONV = 31
NA_D = 128
NA_WR = 8
NA_WC = 16
PLE_DIM = 256
LANES = 128
CONV_PAD = 16
VMEM_LIMIT = 56 * 1024 * 1024
MM_VMEM_BUDGET = 40 * 1024 * 1024
MM_FULL_K = 2048
NA_ROWS_PER_STEP = 16
ADAMW_TILE_BYTES = 1024 * 1024
SSD_GROUPS_PER_STEP = 4
POST_ROW_TILES = (1024, 512, 256, 128)

ADAM_LR = 0.001
ADAM_B1 = 0.9
ADAM_B2 = 0.999
ADAM_EPS = 1e-08
ADAM_WD = 0.01
ADAM_STEP = 10

NEG_BIG = -1e30


def _params(sem=None):
    if sem is None:
        return pltpu.CompilerParams(vmem_limit_bytes=VMEM_LIMIT)
    return pltpu.CompilerParams(vmem_limit_bytes=VMEM_LIMIT, dimension_semantics=sem)


def _pick(n, prefs):
    for t in prefs:
        if n % t == 0:
            return t
    return n


def _sigmoid(x):
    return jax.nn.sigmoid(x)


def _silu(x):
    return x * jax.nn.sigmoid(x)


_DN = {"nn": (((1,), (0,)), ((), ())), "nt": (((1,), (1,)), ((), ())), "tn": (((0,), (0,)), ((), ()))}


def mm(a, b, mode, name, out_dtype=f32, add=None, side=None):
    if mode == "nn":
        (M, K), (K2, N) = a.shape, b.shape
    elif mode == "nt":
        (M, K), (N, K2) = a.shape, b.shape
    else:
        (K, M), (K2, N) = a.shape, b.shape
    assert K == K2, (a.shape, b.shape, mode)
    sa, sb, so = a.dtype.itemsize, b.dtype.itemsize, jnp.dtype(out_dtype).itemsize
    tk = K if K <= MM_FULL_K else _pick(K, (1024, 512, 256, 128))
    nk = K // tk
    tms = [t for t in (1024, 512, 256, 128) if M % t == 0] or [M]
    tns = [t for t in (2048, 1024, 512, 256, 128) if N % t == 0] or [N]

    def vmem_bytes(tm, tn):
        n = 2 * (tm * tk * sa + tk * tn * sb) + 2 * tm * tn * so
        n += tm * tn * 4 if nk > 1 else 0
        n += 2 * tm * tn * add.dtype.itemsize if add is not None else 0
        return n

    ti, tj = 0, 0
    while vmem_bytes(tms[ti], tns[tj]) > MM_VMEM_BUDGET:
        if tj + 1 < len(tns) and (tns[tj] >= tms[ti] or ti + 1 >= len(tms)):
            tj += 1
        elif ti + 1 < len(tms):
            ti += 1
        else:
            break
    tm, tn = tms[ti], tns[tj]
    a_bytes, b_bytes = M * K * sa, K * N * sb
    i_outer = a_bytes + b_bytes * (M // tm) <= b_bytes + a_bytes * (N // tn)
    dn = _DN[mode]

    grid = (M // tm, N // tn, nk) if i_outer else (N // tn, M // tm, nk)
    ns = len(side[1]) if side is not None else 0
    n_add = 1 if add is not None else 0

    def body(a_ref, b_ref, *rest):
        add_ref = rest[0] if add is not None else None
        x_refs = rest[n_add:n_add + ns]
        o_ref = rest[n_add + ns]
        out_refs = rest[n_add + ns + 1:n_add + 2 * ns + 1]
        scratch = rest[n_add + 2 * ns + 1:]
        if side is not None:
            step = (pl.program_id(0) * grid[1] + pl.program_id(1)) * nk + pl.program_id(2)

            @pl.when(step == 0)
            def _():
                _exchange_start(side[0], x_refs, out_refs, scratch[-3:])

        def finish(r):
            if add is not None:
                r = r + add_ref[...].astype(f32)
            o_ref[...] = r.astype(out_dtype)

        part = lax.dot_general(a_ref[...].astype(bf16), b_ref[...].astype(bf16), dn, preferred_element_type=f32)
        if nk == 1:
            finish(part)
        else:
            acc = scratch[0]
            k = pl.program_id(2)

            @pl.when(k == 0)
            def _():
                acc[...] = part

            @pl.when(k > 0)
            def _():
                acc[...] += part

            @pl.when(k == nk - 1)
            def _():
                finish(acc[...])

        if side is not None:
            @pl.when(step == grid[0] * grid[1] * nk - 1)
            def _():
                _exchange_finish(side[0], x_refs, out_refs, scratch[-3:])

    def ij(p, q):
        return (p, q) if i_outer else (q, p)

    if mode == "tn":
        a_spec = pl.BlockSpec((tk, tm), lambda p, q, k: (k, ij(p, q)[0]))
    else:
        a_spec = pl.BlockSpec((tm, tk), lambda p, q, k: (ij(p, q)[0], k))
    if mode == "nt":
        b_spec = pl.BlockSpec((tn, tk), lambda p, q, k: (ij(p, q)[1], k))
    else:
        b_spec = pl.BlockSpec((tk, tn), lambda p, q, k: (k, ij(p, q)[1]))
    o_spec = pl.BlockSpec((tm, tn), lambda p, q, k: ij(p, q))
    in_specs = [a_spec, b_spec] + ([o_spec] if add is not None else [])
    args = (a, b) + ((add,) if add is not None else ())
    acc_scratch = [pltpu.VMEM((tm, tn), f32)] if nk > 1 else []
    o_shape = jax.ShapeDtypeStruct((M, N), out_dtype)
    if side is None:
        return pl.pallas_call(
            body, name=name, grid=grid, in_specs=in_specs, out_specs=o_spec, out_shape=o_shape, scratch_shapes=acc_scratch,
            compiler_params=_params(("parallel", "parallel", "arbitrary")))(*args)
    res = pl.pallas_call(
        body, name=name, grid=grid, in_specs=in_specs + [_HBM] * ns, out_specs=[o_spec] + [_HBM] * ns,
        out_shape=[o_shape] + _exchange_out_shapes(side[0], side[1]), scratch_shapes=acc_scratch + _exchange_scratch(ns),
        compiler_params=_params(("arbitrary", "arbitrary", "arbitrary")))(*args, *side[1])
    return res[0], res[1:]


def _row_spec(T, w, base):
    return pl.BlockSpec((T, w), lambda g, i: (i, base + g))


def _const_spec(r, w, base):
    return pl.BlockSpec((r, w), lambda g, i: (0, base + g))


def rowop_fwd(fn, name, rows, consts, outs, T, ncb=1):
    L = rows[0][0].shape[0]
    nr, nc = len(rows), len(consts)

    def body(*refs):
        ins = [r[...].astype(f32) for r in refs[:nr + nc]]
        res = fn(*ins)
        for o_ref, r in zip(refs[nr + nc:], res):
            o_ref[...] = r.astype(o_ref.dtype)

    in_specs = [_row_spec(T, w, b) for (_, w, b) in rows] + [_const_spec(a.shape[0], w, b) for (a, w, b) in consts]
    out_specs = [_row_spec(T, w, 0) for (w, _) in outs]
    out_shape = [jax.ShapeDtypeStruct((L, ncb * w), dt) for (w, dt) in outs]
    return pl.pallas_call(
        body, name=name, grid=(ncb, L // T), in_specs=in_specs, out_specs=out_specs, out_shape=out_shape,
        compiler_params=_params(("parallel", "parallel")))(*[a for (a, _, _) in rows], *[a for (a, _, _) in consts])


def rowop_bwd(fn, name, rows, consts, cts, row_grads, const_grads, T, ncb=1, add=None):
    L = rows[0][0].shape[0]
    nr, nc, nct = len(rows), len(consts), len(cts)
    n_in = nr + nc + nct + (1 if add is not None else 0)
    rg_idx = [i for i, d in enumerate(row_grads) if d is not None]
    cg_idx = [i for i, d in enumerate(const_grads) if d]

    def body(*refs):
        ins = [r[...].astype(f32) for r in refs[:nr + nc]]
        ct = tuple(r[...].astype(f32) for r in refs[nr + nc:nr + nc + nct])
        _, vjp = jax.vjp(fn, *ins)
        grads = vjp(ct)
        outs = refs[n_in:]
        for n, i in enumerate(rg_idx):
            gi = grads[i]
            if add is not None and n == 0:
                gi = gi + refs[n_in - 1][...].astype(f32)
            outs[n][...] = gi.astype(outs[n].dtype)
        first = pl.program_id(1) == 0
        for n, i in enumerate(cg_idx):
            o = outs[len(rg_idx) + n]

            @pl.when(first)
            def _(o=o):
                o[...] = jnp.zeros_like(o)

            o[...] += grads[nr + i]

    in_specs = ([_row_spec(T, w, b) for (_, w, b) in rows] + [_const_spec(a.shape[0], w, b) for (a, w, b) in consts]
                + [_row_spec(T, w, b) for (_, w, b) in cts] + ([_row_spec(T, add[1], add[2])] if add is not None else []))
    out_specs = ([_row_spec(T, rows[i][1], 0) for i in rg_idx] + [_const_spec(consts[i][0].shape[0], consts[i][1], 0) for i in cg_idx])
    out_shape = ([jax.ShapeDtypeStruct((L, ncb * rows[i][1]), row_grads[i]) for i in rg_idx]
                 + [jax.ShapeDtypeStruct((consts[i][0].shape[0], ncb * consts[i][1]), f32) for i in cg_idx])
    args = [a for (a, _, _) in rows] + [a for (a, _, _) in consts] + [a for (a, _, _) in cts] + ([add[0]] if add is not None else [])
    return pl.pallas_call(
        body, name=name, grid=(ncb, L // T), in_specs=in_specs, out_specs=out_specs, out_shape=out_shape,
        compiler_params=_params(("parallel", "arbitrary")))(*args)


def _rms_fn(x, w):
    return (x * lax.rsqrt(jnp.mean(x * x, axis=-1, keepdims=True) + EPS) * w,)


def _glu_fn(a, ag):
    return (a * _sigmoid(ag),)


def _gate_fn(o, g):
    return (o * _silu(g),)


def _ple_fn(gpre, pp):
    return (_sigmoid(gpre) * pp,)


def _ple_fwd_fn(h1, gpre, pp):
    return (h1 + _sigmoid(gpre) * pp,)


def _lngate_fn(vc, g, w, b):
    mu = jnp.mean(vc, axis=-1, keepdims=True)
    xc = vc - mu
    y = xc * lax.rsqrt(jnp.mean(xc * xc, axis=-1, keepdims=True) + EPS) * w + b
    return (_silu(y) * _silu(g),)


def _post_fn(yf, yb, xs, z, dsk, gw):
    y = (yf + yb + dsk * xs) * _silu(z)
    return (y * lax.rsqrt(jnp.mean(y * y, axis=-1, keepdims=True) + EPS) * gw,)


def _make_prep_fn(n_heads):
    def prep_fn(u, bias, alog):
        lane = lax.broadcasted_iota(jnp.int32, (1, LANES), 1)
        dt = jnp.where(lane < 2 * n_heads, jax.nn.softplus(u + bias), 0.0)
        da = dt * (-jnp.exp(alog))
        li = lax.broadcasted_iota(jnp.int32, (CHUNK, CHUNK), 0)
        si = lax.broadcasted_iota(jnp.int32, (CHUNK, CHUNK), 1)
        tril = (si <= li).astype(f32)
        triu = (si >= li).astype(f32)
        csf = jnp.dot(tril, da, precision=HIGHEST, preferred_element_type=f32)
        csb = jnp.dot(triu, da, precision=HIGHEST, preferred_element_type=f32)
        return dt, jnp.where(lane < n_heads, csf, csb)

    return prep_fn


def final_loss(h, w, tgt, name="final_loss"):
    L, D = h.shape
    T = _pick(L, (256, 128))

    def fn(x, w_, t):
        y = x * lax.rsqrt(jnp.mean(x * x, axis=-1, keepdims=True) + EPS) * w_
        err = jnp.square(y - t)
        return 0.5 * jnp.sum(jnp.mean(err, axis=-1, keepdims=True))

    def body(h_ref, w_ref, t_ref, loss_ref, dh_ref, dw_ref):
        @pl.when(pl.program_id(0) == 0)
        def _():
            loss_ref[...] = jnp.zeros_like(loss_ref)
            dw_ref[...] = jnp.zeros_like(dw_ref)

        t = t_ref[...]
        val, vjp = jax.vjp(lambda x, w_: fn(x, w_, t), h_ref[...], w_ref[...])
        dx, dw = vjp(jnp.ones((), f32))
        dh_ref[...] = dx
        dw_ref[...] += dw
        loss_ref[...] += jnp.broadcast_to(val, loss_ref.shape)

    return pl.pallas_call(
        body, name=name, grid=(L // T,),
        in_specs=[pl.BlockSpec((T, D), lambda i: (i, 0)), pl.BlockSpec((1, D), lambda i: (0, 0)), pl.BlockSpec((T, D), lambda i: (i, 0))],
        out_specs=[pl.BlockSpec((8, LANES), lambda i: (0, 0)), pl.BlockSpec((T, D), lambda i: (i, 0)), pl.BlockSpec((1, D), lambda i: (0, 0))],
        out_shape=[jax.ShapeDtypeStruct((8, LANES), f32), jax.ShapeDtypeStruct((L, D), f32), jax.ShapeDtypeStruct((1, D), f32)],
        compiler_params=_params(("arbitrary",)))(h, w, tgt)


CONV_ROWS = 128


def conv_fwd(x_arr, xbase, C, w, b, K, act, name):
    L = x_arr.shape[0]
    R = CONV_ROWS
    lo = CONV_PAD - K // 2

    def body(x_ref, w_ref, b_ref, *rest):
        xpad = rest[-1]
        outs = rest[:-1]
        xpad[0:CONV_PAD, :] = jnp.zeros((CONV_PAD, LANES), f32)
        xpad[CONV_PAD + L:CONV_PAD + L + CONV_PAD, :] = jnp.zeros((CONV_PAD, LANES), f32)
        xpad[CONV_PAD:CONV_PAD + L, :] = x_ref[...]

        def chunk(ci, carry):
            r0 = pl.multiple_of(ci * R, R)
            acc = jnp.broadcast_to(b_ref[...], (R, LANES))
            for k in range(K):
                acc = acc + w_ref[k:k + 1, :] * xpad[pl.ds(r0 + (lo + k), R), :]
            outs[0][pl.ds(r0, R), :] = acc
            if act:
                outs[1][pl.ds(r0, R), :] = _silu(acc)
            return carry

        lax.fori_loop(0, L // R, chunk, 0)

    col = pl.BlockSpec((L, LANES), lambda j: (0, j))
    n_out = 2 if act else 1
    return pl.pallas_call(
        body, name=name, grid=(C // LANES,),
        in_specs=[pl.BlockSpec((L, LANES), lambda j: (0, xbase + j)), pl.BlockSpec((K, LANES), lambda j: (0, j)),
                  pl.BlockSpec((1, LANES), lambda j: (0, j))],
        out_specs=[col] * n_out, out_shape=[jax.ShapeDtypeStruct((L, C), f32)] * n_out,
        scratch_shapes=[pltpu.VMEM((L + 2 * CONV_PAD, LANES), f32)],
        compiler_params=_params(("parallel",)))(x_arr, w, b)


def conv_bwd(dy, x_arr, xbase, C, w, K, pre, out_dtype, name):
    L = x_arr.shape[0]
    R = CONV_ROWS
    lo = CONV_PAD - K // 2
    act = pre is not None

    def body(*refs):
        if act:
            dy_ref, pre_ref, x_ref, w_ref, dx_ref, dw_ref, db_ref, dpad, xpad, accw, accb = refs
        else:
            dy_ref, x_ref, w_ref, dx_ref, dw_ref, db_ref, dpad, xpad, accw, accb = refs
        zeros = jnp.zeros((CONV_PAD, LANES), f32)
        for pad in (dpad, xpad):
            pad[0:CONV_PAD, :] = zeros
            pad[CONV_PAD + L:CONV_PAD + L + CONV_PAD, :] = zeros
        xpad[CONV_PAD:CONV_PAD + L, :] = x_ref[...]
        if act:
            p = pre_ref[...]
            s = _sigmoid(p)
            dpad[CONV_PAD:CONV_PAD + L, :] = dy_ref[...].astype(f32) * (s * (1.0 + p * (1.0 - s)))
        else:
            dpad[CONV_PAD:CONV_PAD + L, :] = dy_ref[...].astype(f32)
        accw[...] = jnp.zeros_like(accw)
        accb[...] = jnp.zeros_like(accb)

        def chunk(ci, carry):
            r0 = pl.multiple_of(ci * R, R)
            acc = jnp.zeros((R, LANES), f32)
            for k in range(K):
                acc = acc + w_ref[k:k + 1, :] * dpad[pl.ds(r0 + (CONV_PAD + K // 2 - k), R), :]
            dx_ref[pl.ds(r0, R), :] = acc.astype(out_dtype)
            d = dpad[pl.ds(r0 + CONV_PAD, R), :]
            accb[...] += jnp.sum(d.reshape(R // 8, 8, LANES), axis=0)
            for k in range(K):
                prod = d * xpad[pl.ds(r0 + (lo + k), R), :]
                accw[k] += jnp.sum(prod.reshape(R // 8, 8, LANES), axis=0)
            return carry

        lax.fori_loop(0, L // R, chunk, 0)
        dw_ref[...] = jnp.sum(accw[...], axis=1)
        db_ref[...] = jnp.sum(accb[...], axis=0, keepdims=True)

    col = pl.BlockSpec((L, LANES), lambda j: (0, j))
    in_specs = [col] + ([col] if act else []) + [pl.BlockSpec((L, LANES), lambda j: (0, xbase + j)), pl.BlockSpec((K, LANES), lambda j: (0, j))]
    args = (dy,) + ((pre,) if act else ()) + (x_arr, w)
    return pl.pallas_call(
        body, name=name, grid=(C // LANES,), in_specs=in_specs,
        out_specs=[col, pl.BlockSpec((K, LANES), lambda j: (0, j)), pl.BlockSpec((1, LANES), lambda j: (0, j))],
        out_shape=[jax.ShapeDtypeStruct((L, C), out_dtype), jax.ShapeDtypeStruct((K, C), f32), jax.ShapeDtypeStruct((1, C), f32)],
        scratch_shapes=[pltpu.VMEM((L + 2 * CONV_PAD, LANES), f32), pltpu.VMEM((L + 2 * CONV_PAD, LANES), f32),
                        pltpu.VMEM((K, 8, LANES), f32), pltpu.VMEM((8, LANES), f32)],
        compiler_params=_params(("parallel",)))(*args)


def _ssd_chunk(X, Bm, Cm, dtc, csc, csr, S, lane_off, dirn):
    Qn = X.shape[0]
    lane = lax.broadcasted_iota(jnp.int32, (1, LANES), 1)
    head = lax.shift_right_logical(lax.broadcasted_iota(jnp.int32, (1, GROUP_W), 1), 6)
    dtx = jnp.zeros((Qn, GROUP_W), f32)
    csx = jnp.zeros((Qn, GROUP_W), f32)
    cols = []
    for j in range(HEADS_PER_GROUP):
        sel = lane == lane_off + j
        dcol = jnp.sum(jnp.where(sel, dtc, 0.0), axis=1, keepdims=True)
        ccol = jnp.sum(jnp.where(sel, csc, 0.0), axis=1, keepdims=True)
        dtx = jnp.where(head == j, dcol, dtx)
        csx = jnp.where(head == j, ccol, csx)
        cols.append(ccol)
    Xd = X * dtx
    edge = Qn - 1 if dirn == 0 else 0
    cs_edge = csx[edge:edge + 1, :]
    li = lax.broadcasted_iota(jnp.int32, (Qn, Qn), 0)
    si = lax.broadcasted_iota(jnp.int32, (Qn, Qn), 1)
    mask = (si <= li) if dirn == 0 else (si >= li)
    Cb = Cm.astype(bf16)
    Bb = Bm.astype(bf16)
    G = lax.dot_general(Cb, Bb, _DN["nt"], preferred_element_type=f32)
    Y = jnp.exp(csx) * jnp.dot(Cb, S.astype(bf16), preferred_element_type=f32)
    for j in range(HEADS_PER_GROUP):
        row = csr[j:j + 1, :]
        Lm = jnp.exp(jnp.where(mask, cols[j] - row, NEG_BIG))
        Wj = (G * Lm).astype(bf16)
        Xj = jnp.where(head == j, Xd, 0.0).astype(bf16)
        Y = Y + jnp.dot(Wj, Xj, preferred_element_type=f32)
    S_new = S * jnp.exp(cs_edge) + lax.dot_general(Bb, (Xd * jnp.exp(cs_edge - csx)).astype(bf16), _DN["tn"], preferred_element_type=f32)
    return Y, S_new


def _ssd_specs(D, G, nchunk, dirn, descending):
    def cidx(c):
        return nchunk - 1 - c if descending else c

    gps = _ssd_groups_per_step(D, G)
    nb = D // N_STATE
    x_spec = pl.BlockSpec((CHUNK, gps * GROUP_W), lambda c, g: (cidx(c), g))
    b_spec = pl.BlockSpec((CHUNK, gps * N_STATE), lambda c, g: (cidx(c), nb // gps + g))
    c_spec = pl.BlockSpec((CHUNK, gps * N_STATE), lambda c, g: (cidx(c), (nb + G) // gps + g))
    n_spec = pl.BlockSpec((CHUNK, gps * N_STATE), lambda c, g: (cidx(c), g))
    lane_spec = pl.BlockSpec((CHUNK, LANES), lambda c, g: (cidx(c), 0))
    csr_spec = pl.BlockSpec((gps, 8, CHUNK), lambda c, g: (dirn * (G // gps) + g, 0, cidx(c)))
    dcsr_spec = pl.BlockSpec((gps, 8, CHUNK), lambda c, g: (g, 0, cidx(c)))
    s_spec = pl.BlockSpec((gps, None, N_STATE, GROUP_W), lambda c, g: (g, cidx(c), 0, 0))
    return x_spec, b_spec, c_spec, n_spec, lane_spec, csr_spec, dcsr_spec, s_spec


def _ssd_groups_per_step(D, G):
    nb = D // N_STATE
    for n in (SSD_GROUPS_PER_STEP, 2):
        if G % n == 0 and nb % n == 0:
            return n
    return 1


def ssd_fwd(act, dt128, cs128, csr, dirn, n_heads, name):
    L = act.shape[0]
    D = act.shape[1] // 2
    G = D // GROUP_W
    nchunk = L // CHUNK
    x_spec, b_spec, c_spec, _, lane_spec, csr_spec, _, s_spec = _ssd_specs(D, G, nchunk, dirn, dirn == 1)

    gps = _ssd_groups_per_step(D, G)

    def body(x_ref, b_ref, c_ref, dt_ref, cs_ref, csr_ref, y_ref, ssave_ref, S):
        c = pl.program_id(0)
        for gg in range(gps):
            g = pl.program_id(1) * gps + gg

            @pl.when(c == 0)
            def _():
                S[g] = jnp.zeros((N_STATE, GROUP_W), f32)

            xs_, ns_ = pl.ds(gg * GROUP_W, GROUP_W), pl.ds(gg * N_STATE, N_STATE)
            s_in = S[g]
            ssave_ref[gg] = s_in
            y, s_new = _ssd_chunk(x_ref[:, xs_], b_ref[:, ns_], c_ref[:, ns_], dt_ref[...], cs_ref[...], csr_ref[gg], s_in,
                                  dirn * n_heads + HEADS_PER_GROUP * g, dirn)
            y_ref[:, xs_] = y
            S[g] = s_new

    return pl.pallas_call(
        body, name=name, grid=(nchunk, G // gps),
        in_specs=[x_spec, b_spec, c_spec, lane_spec, lane_spec, csr_spec],
        out_specs=[x_spec, s_spec],
        out_shape=[jax.ShapeDtypeStruct((L, D), f32), jax.ShapeDtypeStruct((G, nchunk, N_STATE, GROUP_W), f32)],
        scratch_shapes=[pltpu.VMEM((G, N_STATE, GROUP_W), f32)],
        compiler_params=_params(("arbitrary", "arbitrary")))(act, act, act, dt128, cs128, csr)


def ssd_bwd(act, dt128, cs128, csr, ssave, dy, dirn, n_heads, name):
    L = act.shape[0]
    D = act.shape[1] // 2
    G = D // GROUP_W
    nchunk = L // CHUNK
    x_spec, b_spec, c_spec, n_spec, lane_spec, csr_in, dcsr_spec, s_spec = _ssd_specs(D, G, nchunk, dirn, dirn == 0)

    gps = _ssd_groups_per_step(D, G)

    def body(x_ref, b_ref, c_ref, dt_ref, cs_ref, csr_ref, s_ref, dy_ref, dx_ref, db_ref, dc_ref, ddt_ref, dcs_ref, dcsr_ref, dS):
        c = pl.program_id(0)

        @pl.when(pl.program_id(1) == 0)
        def _():
            ddt_ref[...] = jnp.zeros_like(ddt_ref)
            dcs_ref[...] = jnp.zeros_like(dcs_ref)

        ddt_sum, dcs_sum = None, None
        for gg in range(gps):
            g = pl.program_id(1) * gps + gg

            @pl.when(c == 0)
            def _():
                dS[g] = jnp.zeros((N_STATE, GROUP_W), f32)

            xs_, ns_ = pl.ds(gg * GROUP_W, GROUP_W), pl.ds(gg * N_STATE, N_STATE)
            lane_off = dirn * n_heads + HEADS_PER_GROUP * g
            _, vjp = jax.vjp(lambda X, Bm, Cm, dtc, csc, csr_, S: _ssd_chunk(X, Bm, Cm, dtc, csc, csr_, S, lane_off, dirn),
                             x_ref[:, xs_], b_ref[:, ns_], c_ref[:, ns_], dt_ref[...], cs_ref[...], csr_ref[gg], s_ref[gg])
            dX, dB, dC, ddt, dcs, dcsr, dS_in = vjp((dy_ref[:, xs_], dS[g]))
            dx_ref[:, xs_] = dX
            db_ref[:, ns_] = dB
            dc_ref[:, ns_] = dC
            dcsr_ref[gg] = dcsr
            dS[g] = dS_in
            ddt_sum = ddt if ddt_sum is None else ddt_sum + ddt
            dcs_sum = dcs if dcs_sum is None else dcs_sum + dcs

        ddt_ref[...] += ddt_sum
        dcs_ref[...] += dcs_sum

    return pl.pallas_call(
        body, name=name, grid=(nchunk, G // gps),
        in_specs=[x_spec, b_spec, c_spec, lane_spec, lane_spec, csr_in, s_spec, x_spec],
        out_specs=[x_spec, n_spec, n_spec, lane_spec, lane_spec, dcsr_spec],
        out_shape=[jax.ShapeDtypeStruct((L, D), f32), jax.ShapeDtypeStruct((L, G * N_STATE), f32), jax.ShapeDtypeStruct((L, G * N_STATE), f32),
                   jax.ShapeDtypeStruct((L, LANES), f32), jax.ShapeDtypeStruct((L, LANES), f32), jax.ShapeDtypeStruct((G, 8, L), f32)],
        scratch_shapes=[pltpu.VMEM((G, N_STATE, GROUP_W), f32)],
        compiler_params=_params(("arbitrary", "arbitrary")))(act, act, act, dt128, cs128, csr, ssave, dy)


def _csr_from_cs(cs128, n_heads):
    L = cs128.shape[0]
    t = cs128[:, :2 * n_heads].T.reshape(2 * n_heads // HEADS_PER_GROUP, HEADS_PER_GROUP, L)
    return jnp.pad(t, ((0, 0), (0, 8 - HEADS_PER_GROUP), (0, 0)))


def _cs_from_dcsr(dcsr_f, dcsr_b, n_heads):
    L = dcsr_f.shape[-1]
    t = jnp.concatenate([dcsr_f, dcsr_b], axis=0)[:, :HEADS_PER_GROUP, :].reshape(2 * n_heads, L).T
    return jnp.pad(t, ((0, 0), (0, LANES - 2 * n_heads)))


def na_bias_table(rpb):
    nh = rpb.shape[0]
    cols = np.arange(GRID_W)
    col_start = np.clip(cols - NA_WC // 2, 0, GRID_W - NA_WC)
    col_valid = (cols[None, :] >= col_start[:, None]) & (cols[None, :] < col_start[:, None] + NA_WC)
    col_off = np.clip(cols[None, :] - cols[:, None], -(NA_WC - 1), NA_WC - 1) + NA_WC - 1
    rm = np.zeros((NA_WR, NA_WR, 2 * NA_WR - 1), np.float32)
    for d in range(NA_WR):
        for i in range(NA_WR):
            rm[d, i, i - d + NA_WR - 1] = 1.0
    cm = np.zeros((GRID_W, GRID_W, 2 * NA_WC - 1), np.float32)
    cm[np.arange(GRID_W)[:, None], np.arange(GRID_W)[None, :], col_off] = 1.0
    bt = jnp.einsum("hrc,dir,qkc->hdqik", rpb, jnp.asarray(rm), jnp.asarray(cm), precision=HIGHEST)
    bt = jnp.where(jnp.asarray(col_valid)[None, None, :, None, :], bt, NEG_BIG)
    return bt.reshape(nh, NA_WR, GRID_W, NA_WR * GRID_W)


def _na_window(r, rows):
    rs = jnp.clip(r - NA_WR // 2, 0, rows - NA_WR)
    return rs, r - rs


def _na_probs(q, kw, bias):
    s = lax.dot_general(q.astype(bf16), kw.astype(bf16), _DN["nt"], preferred_element_type=f32) * (NA_D ** -0.5) + bias
    m = jnp.max(s, axis=-1, keepdims=True)
    p = jnp.exp(s - m)
    return p / jnp.sum(p, axis=-1, keepdims=True)


def na_fwd(um, qbase, kbase, vbase, bt, name):
    L = um.shape[0]
    nh = bt.shape[0]
    rows = L // GRID_W
    WK = NA_WR * GRID_W

    RB = _pick(rows, (NA_ROWS_PER_STEP, 2, 1))

    def body(q_ref, k_ref, v_ref, bt_ref, o_ref):
        win = [_na_window(pl.program_id(1) * RB + rb, rows) for rb in range(RB)]
        t0s = [pl.multiple_of(rs * GRID_W, GRID_W) for rs, _ in win]
        qrows = [pl.ds(rb * GRID_W, GRID_W) for rb in range(RB)]
        ps = [_na_probs(q_ref[qr, :], k_ref[pl.ds(t0, WK), :], bt_ref[delta]) for qr, t0, (_, delta) in zip(qrows, t0s, win)]
        os_ = [jnp.dot(p.astype(bf16), v_ref[pl.ds(t0, WK), :].astype(bf16), preferred_element_type=f32) for p, t0 in zip(ps, t0s)]
        for qr, o in zip(qrows, os_):
            o_ref[qr, :] = o

    kv = lambda base: pl.BlockSpec((L, NA_D), lambda h, r: (0, base + h))
    return pl.pallas_call(
        body, name=name, grid=(nh, rows // RB),
        in_specs=[pl.BlockSpec((RB * GRID_W, NA_D), lambda h, r: (r, qbase + h)), kv(kbase), kv(vbase),
                  pl.BlockSpec((None, NA_WR, GRID_W, WK), lambda h, r: (h, 0, 0, 0))],
        out_specs=pl.BlockSpec((RB * GRID_W, NA_D), lambda h, r: (r, h)),
        out_shape=jax.ShapeDtypeStruct((L, nh * NA_D), f32),
        compiler_params=_params(("parallel", "arbitrary")))(um, um, um, bt)


def na_bwd(um, qbase, kbase, vbase, bt, do, name, side=None):
    L = um.shape[0]
    nh = bt.shape[0]
    rows = L // GRID_W
    WK = NA_WR * GRID_W
    scale = NA_D ** -0.5

    RB = _pick(rows, (NA_ROWS_PER_STEP, 2, 1))

    ns = len(side[1]) if side is not None else 0
    nsteps = rows // RB

    def body(q_ref, k_ref, v_ref, bt_ref, do_ref, *rest):
        x_refs = rest[:ns]
        dq_ref, dk_ref, dv_ref, dbt_ref = rest[ns:ns + 4]
        out_refs = rest[ns + 4:2 * ns + 4]
        sems = rest[2 * ns + 4:]
        if side is not None:
            step = pl.program_id(0) * nsteps + pl.program_id(1)

            @pl.when(step == 0)
            def _():
                _exchange_start(side[0], x_refs, out_refs, sems)

        @pl.when(pl.program_id(1) == 0)
        def _():
            dk_ref[...] = jnp.zeros_like(dk_ref)
            dv_ref[...] = jnp.zeros_like(dv_ref)
            dbt_ref[...] = jnp.zeros_like(dbt_ref)

        win = [_na_window(pl.program_id(1) * RB + rb, rows) for rb in range(RB)]
        t0s = [pl.multiple_of(rs * GRID_W, GRID_W) for rs, _ in win]
        qrows = [pl.ds(rb * GRID_W, GRID_W) for rb in range(RB)]
        qs_ = [q_ref[qr, :] for qr in qrows]
        kws = [k_ref[pl.ds(t0, WK), :].astype(bf16) for t0 in t0s]
        vws = [v_ref[pl.ds(t0, WK), :].astype(bf16) for t0 in t0s]
        dos = [do_ref[qr, :].astype(bf16) for qr in qrows]
        ps = [_na_probs(q, kw, bt_ref[delta]) for q, kw, (_, delta) in zip(qs_, kws, win)]
        dps = [lax.dot_general(do_, vw, _DN["nt"], preferred_element_type=f32) for do_, vw in zip(dos, vws)]
        dss = [p * (dp - jnp.sum(dp * p, axis=-1, keepdims=True)) for p, dp in zip(ps, dps)]
        dsbs = [ds.astype(bf16) for ds in dss]
        dqs = [jnp.dot(dsb, kw, preferred_element_type=f32) * scale for dsb, kw in zip(dsbs, kws)]
        dks = [lax.dot_general(dsb, q.astype(bf16), _DN["tn"], preferred_element_type=f32) * scale for dsb, q in zip(dsbs, qs_)]
        dvs = [lax.dot_general(p.astype(bf16), do_, _DN["tn"], preferred_element_type=f32) for p, do_ in zip(ps, dos)]
        for rb in range(RB):
            dq_ref[qrows[rb], :] = dqs[rb]
            dbt_ref[win[rb][1]] += dss[rb]
            dk_ref[pl.ds(t0s[rb], WK), :] += dks[rb]
            dv_ref[pl.ds(t0s[rb], WK), :] += dvs[rb]

        if side is not None:
            @pl.when(step == nh * nsteps - 1)
            def _():
                _exchange_finish(side[0], x_refs, out_refs, sems)

    kv = lambda base: pl.BlockSpec((L, NA_D), lambda h, r: (0, base + h))
    qs = lambda base: pl.BlockSpec((RB * GRID_W, NA_D), lambda h, r: (r, base + h))
    bts = pl.BlockSpec((None, NA_WR, GRID_W, WK), lambda h, r: (h, 0, 0, 0))
    big = jax.ShapeDtypeStruct((L, nh * NA_D), f32)
    in_specs = [qs(qbase), kv(kbase), kv(vbase), bts, qs(0)]
    out_specs = [qs(0), kv(0), kv(0), bts]
    out_shape = [big, big, big, jax.ShapeDtypeStruct(bt.shape, f32)]
    if side is None:
        return pl.pallas_call(
            body, name=name, grid=(nh, nsteps), in_specs=in_specs, out_specs=out_specs, out_shape=out_shape,
            compiler_params=_params(("parallel", "arbitrary")))(um, um, um, bt, do)
    res = pl.pallas_call(
        body, name=name, grid=(nh, nsteps), in_specs=in_specs + [_HBM] * ns, out_specs=out_specs + [_HBM] * ns,
        out_shape=out_shape + _exchange_out_shapes(side[0], side[1]), scratch_shapes=_exchange_scratch(ns),
        compiler_params=_params(("arbitrary", "arbitrary")))(um, um, um, bt, do, *side[1])
    return tuple(res[:4]) + (res[4:],)


_HBM = pl.BlockSpec(memory_space=pltpu.HBM)


def all_gather(xs, name):
    n = len(xs)

    def body(*refs):
        _exchange_start("gather", refs[:n], refs[n:2 * n], refs[2 * n:])
        _exchange_finish("gather", refs[:n], refs[n:2 * n], refs[2 * n:])

    return pl.pallas_call(
        body, name=name, in_specs=[_HBM] * n, out_specs=[_HBM] * n, out_shape=_exchange_out_shapes("gather", xs),
        scratch_shapes=_exchange_scratch(n))(*xs)


def _exchange_out_shapes(kind, xs):
    return [jax.ShapeDtypeStruct(((N_DEV,) + x.shape) if kind == "gather" else x.shape, x.dtype) for x in xs]


def _exchange_scratch(n):
    return [pltpu.SemaphoreType.DMA((7 * n,)), pltpu.SemaphoreType.DMA((7 * n,)), pltpu.SemaphoreType.DMA((n,))]


def _gather_copies(x_refs, out_refs, sems):
    send_sems, recv_sems, local_sems = sems
    n = len(x_refs)
    mx, my, mc = lax.axis_index("x"), lax.axis_index("y"), lax.axis_index("c")
    me, sibling = (mx, my, mc), (mx, my, 1 - mc)
    chips = [(1 - mx, my), (mx, 1 - my), (1 - mx, 1 - my)]

    def slab(t, px, py, pc):
        return out_refs[t].at[4 * px + 2 * py + pc]

    def copy(t, k, block, to, src=None):
        return pltpu.make_async_remote_copy(
            src_ref=slab(t, *block) if src is None else src, dst_ref=slab(t, *block),
            send_sem=send_sems.at[7 * t + k], recv_sem=recv_sems.at[7 * t + k], device_id=to, device_id_type=pl.DeviceIdType.MESH)

    mine = [pltpu.make_async_copy(x_refs[t], slab(t, *me), local_sems.at[t]) for t in range(n)]
    first = []
    for t in range(n):
        first.append(copy(t, 0, me, sibling, src=x_refs[t]))
        first += [copy(t, 1 + j, me, (*chip, mc), src=x_refs[t]) for j, chip in enumerate(chips)]
    return mine, first, copy, me, sibling, chips, mc


def _exchange_start(kind, x_refs, out_refs, sems):
    if kind == "gather":
        mine, first = _gather_copies(x_refs, out_refs, sems)[:2]
    else:
        mine, first = _a2a_copies(x_refs, out_refs, sems)
    for cp in mine + first:
        cp.start()


def _exchange_finish(kind, x_refs, out_refs, sems):
    n = len(x_refs)
    if kind == "gather":
        mine, first, copy, me, sibling, chips, mc = _gather_copies(x_refs, out_refs, sems)
        passed = []
        for t in range(n):
            for j, chip in enumerate(chips):
                copy(t, 1 + j, (*chip, mc), me).wait_recv()
                fwd = copy(t, 4 + j, (*chip, mc), sibling)
                fwd.start()
                passed.append(fwd)
        for t in range(n):
            copy(t, 0, sibling, me).wait_recv()
            for j, chip in enumerate(chips):
                copy(t, 4 + j, (*chip, 1 - mc), me).wait_recv()
        for cp in first + passed:
            cp.wait_send()
    else:
        mine, first = _a2a_copies(x_refs, out_refs, sems)
        for cp in first:
            cp.wait()
    for cp in mine:
        cp.wait()


def _a2a_copies(x_refs, out_refs, sems):
    send_sems, recv_sems, local_sems = sems
    n = len(x_refs)
    mx, my, mc = lax.axis_index("x"), lax.axis_index("y"), lax.axis_index("c")
    me = 4 * mx + 2 * my + mc
    mine = [pltpu.make_async_copy(x_refs[t].at[me], out_refs[t].at[me], local_sems.at[t]) for t in range(n)]
    copies = []
    for t in range(n):
        for k in range(1, N_DEV):
            px = 1 - mx if k & 4 else mx
            py = 1 - my if k & 2 else my
            pc = 1 - mc if k & 1 else mc
            peer = 4 * px + 2 * py + pc
            copies.append(pltpu.make_async_remote_copy(
                src_ref=x_refs[t].at[peer], dst_ref=out_refs[t].at[me], send_sem=send_sems.at[7 * t + k - 1],
                recv_sem=recv_sems.at[7 * t + k - 1], device_id=(px, py, pc), device_id_type=pl.DeviceIdType.MESH))
    return mine, copies


def all_to_all(xs, name):
    n = len(xs)

    def body(*refs):
        _exchange_start("a2a", refs[:n], refs[n:2 * n], refs[2 * n:])
        _exchange_finish("a2a", refs[:n], refs[n:2 * n], refs[2 * n:])

    return pl.pallas_call(
        body, name=name, in_specs=[_HBM] * n, out_specs=[_HBM] * n, out_shape=_exchange_out_shapes("a2a", xs),
        scratch_shapes=_exchange_scratch(n))(*xs)


def adamw(w, slabs, m, v, own, name):
    R, C = w.shape
    tr = _pick(R, tuple(t for t in (256, 128, 64, 32, 16, 8) if t * C * 4 <= ADAMW_TILE_BYTES) or (8,))
    c1 = 1.0 - ADAM_B1 ** ADAM_STEP
    c2 = 1.0 - ADAM_B2 ** ADAM_STEP

    def body(*refs):
        if own is not None:
            me_ref, w_ref, s_ref, m_ref, v_ref, own_ref, g_ref, d_ref, nm_ref, nv_ref = refs
        else:
            w_ref, s_ref, m_ref, v_ref, g_ref, d_ref, nm_ref, nv_ref = refs
        g = None
        for s in range(N_DEV):
            t = s_ref[s].astype(f32)
            if own is not None:
                t = jnp.where(me_ref[0] == s, own_ref[...], t)
            g = t if g is None else g + t
        wv = w_ref[...]
        mn = ADAM_B1 * m_ref[...] + (1.0 - ADAM_B1) * g
        vn = ADAM_B2 * v_ref[...] + (1.0 - ADAM_B2) * jnp.square(g)
        m_hat = mn / c1
        v_hat = vn / c2
        g_ref[...] = g
        d_ref[...] = -ADAM_LR * (m_hat / (jnp.sqrt(v_hat) + ADAM_EPS) + ADAM_WD * wv)
        nm_ref[...] = mn
        nv_ref[...] = vn

    out_shape = [jax.ShapeDtypeStruct((R, C), f32)] * 4
    if own is not None:
        me, part = own
        blk = pl.BlockSpec((tr, C), lambda i, me_: (i, 0))
        gs = pltpu.PrefetchScalarGridSpec(
            num_scalar_prefetch=1, grid=(R // tr,),
            in_specs=[blk, pl.BlockSpec((N_DEV, tr, C), lambda i, me_: (0, i, 0)), blk, blk, blk],
            out_specs=[blk] * 4)
        return pl.pallas_call(body, name=name, grid_spec=gs, out_shape=out_shape, compiler_params=_params(("parallel",)))(
            me.reshape(1).astype(jnp.int32), w, slabs, m, v, part)
    blk = pl.BlockSpec((tr, C), lambda i: (i, 0))
    return pl.pallas_call(
        body, name=name, grid=(R // tr,), in_specs=[blk, pl.BlockSpec((N_DEV, tr, C), lambda i: (0, i, 0)), blk, blk],
        out_specs=[blk] * 4, out_shape=out_shape, compiler_params=_params(("parallel",)))(w, slabs, m, v)


def _packed_rows(n, cols):
    rows = -(-n // cols)
    mult = 256 if rows > 256 else 16
    return -(-rows // mult) * mult


def _pack(arrs, cols, dtype, lead=()):
    nl = len(lead)
    flat = jnp.concatenate([a.reshape(lead + (-1,)).astype(dtype) for a in arrs], axis=-1)
    n = flat.shape[-1]
    rows = _packed_rows(n, cols)
    flat = jnp.pad(flat, [(0, 0)] * nl + [(0, rows * cols - n)])
    return flat.reshape(lead + (rows, cols))


def _unpack(packed, shapes, lead=()):
    flat = packed.reshape(lead + (-1,))
    out, off = [], 0
    for s in shapes:
        n = int(np.prod(s))
        out.append(flat[..., off:off + n].reshape(lead + tuple(s)))
        off += n
    return out


def _to_slabs(full, axis):
    s = full.shape
    r = full.reshape(s[:axis] + (N_DEV, s[axis] // N_DEV) + s[axis + 1:])
    return jnp.moveaxis(r, axis, 0)


def _from_slabs(slabs, axis):
    r = jnp.moveaxis(slabs, 0, axis)
    s = r.shape
    return r.reshape(s[:axis] + (s[axis] * s[axis + 1],) + s[axis + 2:])


BIG = [("ev_w_in", 2), ("ev_w_out", 1), ("od_w_in", 2), ("od_w_out", 1), ("ple_w_gate", 1), ("ple_w_proj", 2)]
SMALL = [("ev_conv_w", 2), ("od_norm_w", 1), ("od_dw_w", 2), ("od_dw_b", 1), ("od_ln_w", 1), ("od_ln_b", 1)]
REPL = ["ev_norm_w", "ev_conv_b", "ev_dt_bias_f", "ev_dt_bias_b", "ev_a_log_f", "ev_a_log_b", "ev_d_skip", "ev_gnorm_w", "ev_rpb",
        "ple_norm_w", "final_norm_w"]
WEIGHTS = ["ev_norm_w", "ev_w_in", "ev_conv_w", "ev_conv_b", "ev_dt_bias_f", "ev_dt_bias_b", "ev_a_log_f", "ev_a_log_b", "ev_d_skip",
           "ev_gnorm_w", "ev_rpb", "ev_w_out", "od_norm_w", "od_w_in", "od_dw_w", "od_dw_b", "od_ln_w", "od_ln_b", "od_w_out",
           "ple_norm_w", "ple_w_gate", "ple_w_proj", "final_norm_w"]
BIG_COLS = 1024
SMALL_COLS = 128


def _row_tile(L, width):
    return _pick(L, (256, 128)) if width <= 2048 else _pick(L, (128,))


def _ple_fwd(h1, p_i, nw, wg, wp, tag):
    L, D = h1.shape
    T = _row_tile(L, D)
    (hn2,) = rowop_fwd(_rms_fn, "ple_rms_" + tag, [(h1, D, 0)], [(nw, D, 0)], [(D, bf16)], T)
    gpre = mm(hn2, wg, "nn", "ple_gate_mm_" + tag)
    pp = mm(p_i, wp, "nn", "ple_proj_mm_" + tag)
    (h2,) = rowop_fwd(_ple_fwd_fn, "ple_comb_" + tag, [(h1, D, 0), (gpre, D, 0), (pp, D, 0)], [], [(D, f32)], T)
    return h2, (h1, p_i, nw, wg, wp, hn2, gpre, pp)


def _ple_bwd(res, dh2, tag):
    h1, p_i, nw, wg, wp, hn2, gpre, pp = res
    L, D = h1.shape
    T = _row_tile(L, D)
    dgpre, dpp = rowop_bwd(_ple_fn, "ple_comb_bwd_" + tag, [(gpre, D, 0), (pp, D, 0)], [], [(dh2, D, 0)], [bf16, bf16], [], T)
    dwg = mm(hn2, dgpre, "tn", "ple_gate_dw_" + tag)
    dwp = mm(p_i, dpp, "tn", "ple_proj_dw_" + tag)
    dhn2 = mm(dgpre, wg, "nt", "ple_gate_dx_" + tag)
    dh1, dnw = rowop_bwd(_rms_fn, "ple_rms_bwd_" + tag, [(h1, D, 0)], [(nw, D, 0)], [(dhn2, D, 0)], [f32], [True], T, add=(dh2, D, 0))
    return dh1, dnw, dwg, dwp


def _even_fwd(h, P, tag, side=None):
    L, D = h.shape
    H = D // HEAD_P
    T = _row_tile(L, D)
    DB = D // LANES
    (hn,) = rowop_fwd(_rms_fn, "ev_rms_" + tag, [(h, D, 0)], [(P["norm_w"], D, 0)], [(D, bf16)], T)
    um = mm(hn, P["w_main"], "nn", "ev_in_mm_" + tag, side=side)
    um, exchanged = um if side is not None else (um, None)
    udt = mm(hn, P["w_dt"], "nn", "ev_dt_mm_" + tag)
    pre, act = conv_fwd(um, DB, 2 * D, P["conv_w"], P["conv_b"], K_SSM, True, "ev_conv_" + tag)
    prep = _make_prep_fn(H)
    dt128, cs128 = rowop_fwd(prep, "ev_prep_" + tag, [(udt, LANES, 0)], [(P["bias128"], LANES, 0), (P["alog128"], LANES, 0)],
                             [(LANES, f32), (LANES, f32)], CHUNK)
    csr = _csr_from_cs(cs128, H)
    yf, sf = ssd_fwd(act, dt128, cs128, csr, 0, H, "ev_ssd_f_" + tag)
    yb, sb = ssd_fwd(act, dt128, cs128, csr, 1, H, "ev_ssd_b_" + tag)
    G = D // GROUP_W
    (yssd,) = rowop_fwd(_post_fn, "ev_post_" + tag, [(yf, GROUP_W, 0), (yb, GROUP_W, 0), (act, GROUP_W, 0), (um, GROUP_W, 0)],
                        [(P["dskipx"], GROUP_W, 0), (P["gnorm_w"], GROUP_W, 0)], [(GROUP_W, bf16)], _pick(L, POST_ROW_TILES), ncb=G)
    bt = na_bias_table(P["rpb"])
    o = na_fwd(um, 3 * DB, 4 * DB, 5 * DB, bt, "ev_na_" + tag)
    (yna,) = rowop_fwd(_gate_fn, "ev_nagate_" + tag, [(o, D, 0), (um, D, 6)], [], [(D, bf16)], T)
    cat = jnp.concatenate([yssd, yna], axis=-1)
    if P["w_out"] is None:
        P["w_out"] = _from_slabs(exchanged[0], 0)
    h1 = mm(cat, P["w_out"], "nn", "ev_out_mm_" + tag, add=h)
    return h1, (h, hn, um, udt, pre, act, dt128, cs128, csr, yf, yb, sf, sb, bt, o, cat), exchanged


def _even_bwd(P, res, dh1, tag, side=None, early=None):
    h, hn, um, udt, pre, act, dt128, cs128, csr, yf, yb, sf, sb, bt, o, cat = res
    L, D = h.shape
    H = D // HEAD_P
    G = D // GROUP_W
    T = _row_tile(L, D)
    DB = D // LANES
    g = {}
    dcat = mm(dh1, P["w_out"], "nt", "ev_out_dx_" + tag)
    g["w_out"] = mm(cat, dh1, "tn", "ev_out_dw_" + tag)
    do, dg = rowop_bwd(_gate_fn, "ev_nagate_bwd_" + tag, [(o, D, 0), (um, D, 6)], [], [(dcat, D, 1)], [f32, bf16], [], T)
    side_na = (side[0], side[1][1:]) if side is not None else None
    dq, dk, dv, dbt, *exchanged = na_bwd(um, 3 * DB, 4 * DB, 5 * DB, bt, do, "ev_na_bwd_" + tag, side=side_na)
    _, rpb_vjp = jax.vjp(na_bias_table, P["rpb"])
    (g["rpb"],) = rpb_vjp(dbt)
    dy, dxs_a, dz, ddsk, dgn = rowop_bwd(
        _post_fn, "ev_post_bwd_" + tag, [(yf, GROUP_W, 0), (yb, GROUP_W, 0), (act, GROUP_W, 0), (um, GROUP_W, 0)],
        [(P["dskipx"], GROUP_W, 0), (P["gnorm_w"], GROUP_W, 0)], [(dcat, GROUP_W, 0)], [f32, None, f32, bf16], [True, True],
        _pick(L, POST_ROW_TILES), ncb=G)
    g["d_skip"] = ddsk.reshape(H, HEAD_P).sum(axis=-1)
    g["gnorm_w"] = dgn.reshape(D)
    dxf, dbf, dcf, ddtf, dcsf, dcsrf = ssd_bwd(act, dt128, cs128, csr, sf, dy, 0, H, "ev_ssd_f_bwd_" + tag)
    dxb, dbb, dcb, ddtb, dcsb, dcsrb = ssd_bwd(act, dt128, cs128, csr, sb, dy, 1, H, "ev_ssd_b_bwd_" + tag)
    dact = jnp.concatenate([dxf + dxb + dxs_a, dbf + dbb, dcf + dcb], axis=-1)
    ddt128 = ddtf + ddtb
    dcs128 = dcsf + dcsb + _cs_from_dcsr(dcsrf, dcsrb, H)
    prep = _make_prep_fn(H)
    dudt, dbias, dalog = rowop_bwd(prep, "ev_prep_bwd_" + tag, [(udt, LANES, 0)], [(P["bias128"], LANES, 0), (P["alog128"], LANES, 0)],
                                   [(ddt128, LANES, 0), (dcs128, LANES, 0)], [bf16], [True, True], CHUNK)
    g["dt_bias_f"], g["dt_bias_b"] = dbias[0, :H], dbias[0, H:2 * H]
    g["a_log_f"], g["a_log_b"] = dalog[0, :H], dalog[0, H:2 * H]
    dxbc, g["conv_w"], dcb_ = conv_bwd(dact, um, DB, 2 * D, P["conv_w"], K_SSM, pre, bf16, "ev_conv_bwd_" + tag)
    g["conv_b"] = dcb_.reshape(2 * D)
    dum = jnp.concatenate([dz, dxbc, dq.astype(bf16), dk.astype(bf16), dv.astype(bf16), dg], axis=-1)
    if side is None:
        dhn = mm(dum, P["w_main"], "nt", "ev_in_dx_" + tag)
        exchanged = None
    else:
        dhn, first = mm(dum, P["w_main"], "nt", "ev_in_dx_" + tag, side=(side[0], side[1][:1]))
        exchanged = list(first) + list(exchanged[0])
    dhn = mm(dudt, P["w_dt"], "nt", "ev_dt_dx_" + tag, add=dhn)
    early_recv = None
    if early is None:
        g["w_main"] = mm(hn, dum, "tn", "ev_in_dw_" + tag)
    else:
        slabs = [_to_slabs(gr.astype(bf16), ax) for gr, ax in zip([g["w_out"]] + early[0], [0] + early[1])]
        g["w_main"], early_recv = mm(hn, dum, "tn", "ev_in_dw_" + tag, side=("a2a", slabs))
    g["w_dt"] = mm(hn, dudt, "tn", "ev_dt_dw_" + tag)
    dh, dnw = rowop_bwd(_rms_fn, "ev_rms_bwd_" + tag, [(h, D, 0)], [(P["norm_w"], D, 0)], [(dhn, D, 0)], [f32], [True], T, add=(dh1, D, 0))
    g["norm_w"] = dnw.reshape(D)
    return dh, g, exchanged, early_recv


def _odd_fwd(h, P, tag, side=None):
    L, D = h.shape
    C = 2 * D
    T = _row_tile(L, D)
    (hn,) = rowop_fwd(_rms_fn, "od_rms_" + tag, [(h, D, 0)], [(P["norm_w"], D, 0)], [(D, bf16)], T)
    u = mm(hn, P["w_in"], "nn", "od_in_mm_" + tag, side=side)
    u, exchanged = u if side is not None else (u, None)
    (vglu,) = rowop_fwd(_glu_fn, "od_glu_" + tag, [(u, D, 0), (u, D, 2)], [], [(D, f32)], T, ncb=2)
    (vc,) = conv_fwd(vglu, 0, C, P["dw_w"], P["dw_b"], K_CONV, False, "od_conv_" + tag)
    TL = _row_tile(L, C)
    (t,) = rowop_fwd(_lngate_fn, "od_lngate_" + tag, [(vc, C, 0), (u, C, 2)], [(P["ln_w"], C, 0), (P["ln_b"], C, 0)], [(C, bf16)], TL)
    h1 = mm(t, P["w_out"], "nn", "od_out_mm_" + tag, add=h)
    return h1, (h, hn, u, vglu, vc, t), exchanged


def _odd_bwd(P, res, dh1, tag, side=None):
    h, hn, u, vglu, vc, t = res
    L, D = h.shape
    C = 2 * D
    T = _row_tile(L, D)
    TL = _row_tile(L, C)
    g = {}
    dt_ = mm(dh1, P["w_out"], "nt", "od_out_dx_" + tag)
    g["w_out"] = mm(t, dh1, "tn", "od_out_dw_" + tag)
    dvc, dg, dlnw, dlnb = rowop_bwd(_lngate_fn, "od_lngate_bwd_" + tag, [(vc, C, 0), (u, C, 2)], [(P["ln_w"], C, 0), (P["ln_b"], C, 0)],
                                    [(dt_, C, 0)], [f32, bf16], [True, True], TL)
    g["ln_w"], g["ln_b"] = dlnw.reshape(C), dlnb.reshape(C)
    dvglu, g["dw_w"], ddwb = conv_bwd(dvc, vglu, 0, C, P["dw_w"], K_CONV, None, f32, "od_conv_bwd_" + tag)
    g["dw_b"] = ddwb.reshape(C)
    da, dag = rowop_bwd(_glu_fn, "od_glu_bwd_" + tag, [(u, D, 0), (u, D, 2)], [], [(dvglu, D, 0)], [bf16, bf16], [], T, ncb=2)
    du = jnp.concatenate([da, dag, dg], axis=-1)
    exchanged = None
    if side is None:
        dhn = mm(du, P["w_in"], "nt", "od_in_dx_" + tag)
        g["w_in"] = mm(hn, du, "tn", "od_in_dw_" + tag)
    else:
        dhn, ex_a = mm(du, P["w_in"], "nt", "od_in_dx_" + tag, side=(side[0], side[1][:1]))
        g["w_in"], ex_b = mm(hn, du, "tn", "od_in_dw_" + tag, side=(side[0], side[1][1:]))
        exchanged = list(ex_a) + list(ex_b)
    dh, dnw = rowop_bwd(_rms_fn, "od_rms_bwd_" + tag, [(h, D, 0)], [(P["norm_w"], D, 0)], [(dhn, D, 0)], [f32], [True], T, add=(dh1, D, 0))
    g["norm_w"] = dnw.reshape(D)
    return dh, g, exchanged


def kernel(x, p, ev_norm_w, ev_w_in, ev_conv_w, ev_conv_b, ev_dt_bias_f, ev_dt_bias_b, ev_a_log_f, ev_a_log_b, ev_d_skip, ev_gnorm_w, ev_rpb, ev_w_out, od_norm_w, od_w_in, od_dw_w, od_dw_b, od_ln_w, od_ln_b, od_w_out, ple_norm_w, ple_w_gate, ple_w_proj, final_norm_w, loss_target, m_ev_norm_w, m_ev_w_in, m_ev_conv_w, m_ev_conv_b, m_ev_dt_bias_f, m_ev_dt_bias_b, m_ev_a_log_f, m_ev_a_log_b, m_ev_d_skip, m_ev_gnorm_w, m_ev_rpb, m_ev_w_out, m_od_norm_w, m_od_w_in, m_od_dw_w, m_od_dw_b, m_od_ln_w, m_od_ln_b, m_od_w_out, m_ple_norm_w, m_ple_w_gate, m_ple_w_proj, m_final_norm_w, v_ev_norm_w, v_ev_w_in, v_ev_conv_w, v_ev_conv_b, v_ev_dt_bias_f, v_ev_dt_bias_b, v_ev_a_log_f, v_ev_a_log_b, v_ev_d_skip, v_ev_gnorm_w, v_ev_rpb, v_ev_w_out, v_od_norm_w, v_od_w_in, v_od_dw_w, v_od_dw_b, v_od_ln_w, v_od_ln_b, v_od_w_out, v_ple_norm_w, v_ple_w_gate, v_ple_w_proj, v_final_norm_w):
    W = dict(ev_norm_w=ev_norm_w, ev_w_in=ev_w_in, ev_conv_w=ev_conv_w, ev_conv_b=ev_conv_b, ev_dt_bias_f=ev_dt_bias_f,
             ev_dt_bias_b=ev_dt_bias_b, ev_a_log_f=ev_a_log_f, ev_a_log_b=ev_a_log_b, ev_d_skip=ev_d_skip, ev_gnorm_w=ev_gnorm_w,
             ev_rpb=ev_rpb, ev_w_out=ev_w_out, od_norm_w=od_norm_w, od_w_in=od_w_in, od_dw_w=od_dw_w, od_dw_b=od_dw_b,
             od_ln_w=od_ln_w, od_ln_b=od_ln_b, od_w_out=od_w_out, ple_norm_w=ple_norm_w, ple_w_gate=ple_w_gate,
             ple_w_proj=ple_w_proj, final_norm_w=final_norm_w)
    M = dict(ev_norm_w=m_ev_norm_w, ev_w_in=m_ev_w_in, ev_conv_w=m_ev_conv_w, ev_conv_b=m_ev_conv_b, ev_dt_bias_f=m_ev_dt_bias_f,
             ev_dt_bias_b=m_ev_dt_bias_b, ev_a_log_f=m_ev_a_log_f, ev_a_log_b=m_ev_a_log_b, ev_d_skip=m_ev_d_skip,
             ev_gnorm_w=m_ev_gnorm_w, ev_rpb=m_ev_rpb, ev_w_out=m_ev_w_out, od_norm_w=m_od_norm_w, od_w_in=m_od_w_in,
             od_dw_w=m_od_dw_w, od_dw_b=m_od_dw_b, od_ln_w=m_od_ln_w, od_ln_b=m_od_ln_b, od_w_out=m_od_w_out,
             ple_norm_w=m_ple_norm_w, ple_w_gate=m_ple_w_gate, ple_w_proj=m_ple_w_proj, final_norm_w=m_final_norm_w)
    V = dict(ev_norm_w=v_ev_norm_w, ev_w_in=v_ev_w_in, ev_conv_w=v_ev_conv_w, ev_conv_b=v_ev_conv_b, ev_dt_bias_f=v_ev_dt_bias_f,
             ev_dt_bias_b=v_ev_dt_bias_b, ev_a_log_f=v_ev_a_log_f, ev_a_log_b=v_ev_a_log_b, ev_d_skip=v_ev_d_skip,
             ev_gnorm_w=v_ev_gnorm_w, ev_rpb=v_ev_rpb, ev_w_out=v_ev_w_out, od_norm_w=v_od_norm_w, od_w_in=v_od_w_in,
             od_dw_w=v_od_dw_w, od_dw_b=v_od_dw_b, od_ln_w=v_od_ln_w, od_ln_b=v_od_ln_b, od_w_out=v_od_w_out,
             ple_norm_w=v_ple_norm_w, ple_w_gate=v_ple_w_gate, ple_w_proj=v_ple_w_proj, final_norm_w=v_final_norm_w)

    h0 = x[0]
    L, D = h0.shape
    H = D // HEAD_P
    depth = p.shape[0]
    n_even, n_odd = (depth + 1) // 2, depth // 2
    me = 4 * lax.axis_index("x") + 2 * lax.axis_index("y") + lax.axis_index("c")

    small_shapes = [W[n].shape for n, _ in SMALL]
    (small_g,) = all_gather([_pack([W[n] for n, _ in SMALL], SMALL_COLS, f32)], "gather_small")
    F = {n: _from_slabs(a, ax) for (n, ax), a in zip(SMALL, _unpack(small_g, small_shapes, (N_DEV,)))}

    def layer_big(i):
        kind = "ev" if i % 2 == 0 else "od"
        return [(kind + "_w_in", i // 2, 1), (kind + "_w_out", i // 2, 0), ("ple_w_gate", i, 0), ("ple_w_proj", i, 1)]

    def layer_shards(i):
        return [W[n][k].astype(bf16) for n, k, _ in layer_big(i)]

    def even_params(e, w_in, w_out):
        return dict(
            norm_w=W["ev_norm_w"][e][None], w_main=jnp.concatenate([w_in[:, :3 * D], w_in[:, 3 * D + 2 * H:]], axis=1),
            w_dt=jnp.pad(w_in[:, 3 * D:3 * D + 2 * H], ((0, 0), (0, LANES - 2 * H))),
            conv_w=F["ev_conv_w"][e], conv_b=W["ev_conv_b"][e][None],
            bias128=jnp.pad(jnp.concatenate([W["ev_dt_bias_f"][e], W["ev_dt_bias_b"][e]]), (0, LANES - 2 * H))[None],
            alog128=jnp.pad(jnp.concatenate([W["ev_a_log_f"][e], W["ev_a_log_b"][e]]), (0, LANES - 2 * H))[None],
            dskipx=jnp.repeat(W["ev_d_skip"][e], HEAD_P)[None], gnorm_w=W["ev_gnorm_w"][e][None], rpb=W["ev_rpb"][e],
            w_out=w_out)

    def odd_params(e, w_in, w_out):
        return dict(norm_w=F["od_norm_w"][e][None], w_in=w_in, dw_w=F["od_dw_w"][e], dw_b=F["od_dw_b"][e][None],
                    ln_w=F["od_ln_w"][e][None], ln_b=F["od_ln_b"][e][None], w_out=w_out)

    h = h0
    saved = []
    gathered = all_gather(layer_shards(0)[:1], "gather_layer0")
    for i in range(depth):
        e = i // 2
        tag = str(i)
        nxt = layer_shards(i + 1) if i + 1 < depth else []
        if i == 0:
            w_in = _from_slabs(gathered[0], layer_big(0)[0][2])
            P = even_params(e, w_in, None)
            h1, res, exchanged = _even_fwd(h, P, tag, ("gather", layer_shards(0)[1:] + nxt))
            w_gate, w_proj = [_from_slabs(a, ax) for a, (_, _, ax) in zip(exchanged[1:3], layer_big(0)[2:])]
            gathered = exchanged[3:]
        else:
            w_in, w_out, w_gate, w_proj = [_from_slabs(a, ax) for a, (_, _, ax) in zip(gathered, layer_big(i))]
            side = ("gather", nxt) if nxt else None
            if i % 2 == 0:
                P = even_params(e, w_in, w_out)
                h1, res, gathered = _even_fwd(h, P, tag, side)
            else:
                P = odd_params(e, w_in, w_out)
                h1, res, gathered = _odd_fwd(h, P, tag, side)
        h, pres = _ple_fwd(h1, p[i, 0], W["ple_norm_w"][i][None], w_gate, w_proj, tag)
        saved.append((P, res, pres))
    loss_tile, dh, dfinal = final_loss(h, W["final_norm_w"][None], loss_target[0])

    def ev_w_in_grad(g):
        return jnp.concatenate([g["w_main"][:, :3 * D], g["w_dt"][:, :2 * H], g["w_main"][:, 3 * D:]], axis=1)

    ev_g = [None] * n_even
    od_g = [None] * n_odd
    ple_g = [None] * depth
    recv = [None] * depth
    own = [None] * depth
    pending = None
    for i in reversed(range(depth)):
        P, res, pres = saved[i]
        tag = str(i)
        dh1, dnw, dwg, dwp = _ple_bwd(pres, dh, tag)
        ple_g[i] = dnw.reshape(D)
        side = ("a2a", pending) if pending is not None else None
        if i % 2 == 0:
            early = ([dwg, dwp], [ax for _, _, ax in layer_big(0)[2:]]) if i == 0 else None
            dh, g, exchanged, early_recv = _even_bwd(P, res, dh1, tag, side, early)
            ev_g[i // 2] = g
            big_grads = [ev_w_in_grad(g), g["w_out"], dwg, dwp]
        else:
            dh, g, exchanged = _odd_bwd(P, res, dh1, tag, side)
            od_g[i // 2] = g
            big_grads = [g["w_in"], g["w_out"], dwg, dwp]
        if pending is not None:
            recv[i + 1] = exchanged
        pending = [_to_slabs(gr.astype(bf16), ax) for gr, (_, _, ax) in zip(big_grads, layer_big(i))]
        own[i] = [lax.dynamic_slice_in_dim(gr, me * (gr.shape[ax] // N_DEV), gr.shape[ax] // N_DEV, axis=ax)
                  for gr, (_, _, ax) in zip(big_grads, layer_big(i))]
    recv[0] = list(all_to_all(pending[:1], "scatter_layer0")) + list(early_recv)
    grad_x = dh[None]

    full = dict(
        ev_conv_w=jnp.stack([g["conv_w"] for g in ev_g]), od_norm_w=jnp.stack([g["norm_w"] for g in od_g]),
        od_dw_w=jnp.stack([g["dw_w"] for g in od_g]), od_dw_b=jnp.stack([g["dw_b"] for g in od_g]),
        od_ln_w=jnp.stack([g["ln_w"] for g in od_g]), od_ln_b=jnp.stack([g["ln_b"] for g in od_g]),
        ev_norm_w=jnp.stack([g["norm_w"] for g in ev_g]), ev_conv_b=jnp.stack([g["conv_b"] for g in ev_g]),
        ev_dt_bias_f=jnp.stack([g["dt_bias_f"] for g in ev_g]), ev_dt_bias_b=jnp.stack([g["dt_bias_b"] for g in ev_g]),
        ev_a_log_f=jnp.stack([g["a_log_f"] for g in ev_g]), ev_a_log_b=jnp.stack([g["a_log_b"] for g in ev_g]),
        ev_d_skip=jnp.stack([g["d_skip"] for g in ev_g]), ev_gnorm_w=jnp.stack([g["gnorm_w"] for g in ev_g]),
        ev_rpb=jnp.stack([g["rpb"] for g in ev_g]), ple_norm_w=jnp.stack(ple_g), final_norm_w=dfinal.reshape(D))

    out = {}
    where = {}
    for i in range(depth):
        for slot, (n, k, _) in enumerate(layer_big(i)):
            where.setdefault(n, {})[k] = (i, slot)
    for n, _ in BIG:
        shard = W[n].shape
        two_d = (int(np.prod(shard[:-1])), shard[-1])
        layers = [where[n][k] for k in range(shard[0])]
        slabs = jnp.stack([recv[i][slot] for i, slot in layers], axis=1)
        part = jnp.stack([own[i][slot] for i, slot in layers], axis=0)
        res = adamw(W[n].reshape(two_d), slabs.reshape((N_DEV,) + two_d), M[n].reshape(two_d), V[n].reshape(two_d),
                    (me, part.reshape(two_d)), "adamw_" + n)
        for kind, arr in zip(("grad", "delta", "new_m", "new_v"), res):
            out[kind + "_" + n] = arr.reshape(shard)
    small_full = _pack([_to_slabs(full[n], ax) for n, ax in SMALL], SMALL_COLS, f32, (N_DEV,))
    (small_recv,) = all_to_all([small_full], "scatter_small")
    res_small = adamw(_pack([W[n] for n, _ in SMALL], SMALL_COLS, f32), small_recv, _pack([M[n] for n, _ in SMALL], SMALL_COLS, f32),
                      _pack([V[n] for n, _ in SMALL], SMALL_COLS, f32), None, "adamw_small")
    for kind, arr in zip(("grad", "delta", "new_m", "new_v"), res_small):
        for (n, _), a in zip(SMALL, _unpack(arr, small_shapes)):
            out[kind + "_" + n] = a
    repl_shapes = [W[n].shape for n in REPL] + [(1,)]
    zero1 = jnp.zeros((1,), f32)
    repl_part = _pack([full[n] for n in REPL] + [loss_tile[0, :1]], SMALL_COLS, f32)
    (repl_all,) = all_gather([repl_part], "gather_repl")
    res_repl = adamw(_pack([W[n] for n in REPL] + [zero1], SMALL_COLS, f32), repl_all, _pack([M[n] for n in REPL] + [zero1], SMALL_COLS, f32),
                     _pack([V[n] for n in REPL] + [zero1], SMALL_COLS, f32), None, "adamw_repl")
    for kind, arr in zip(("grad", "delta", "new_m", "new_v"), res_repl):
        parts = _unpack(arr, repl_shapes)
        for n, a in zip(REPL, parts[:-1]):
            out[kind + "_" + n] = a
        if kind == "grad":
            loss = parts[-1].reshape(())

    return (loss, grad_x, *[out["grad_" + n] for n in WEIGHTS], *[out["delta_" + n] for n in WEIGHTS],
            *[out["new_m_" + n] for n in WEIGHTS], *[out["new_v_" + n] for n in WEIGHTS])
```

```python
import functools

import numpy as np
import jax
import jax.numpy as jnp
from jax import lax
from jax.experimental import pallas as pl
from jax.experimental.pallas import tpu as pltpu

f32 = jnp.float32
bf16 = jnp.bfloat16
HIGHEST = lax.Precision.HIGHEST

N_DEV = 8
EPS = 1e-6
GRID_W = 64
HEAD_P = 64
HEADS_PER_GROUP = 4
GROUP_W = HEAD_P * HEADS_PER_GROUP
N_STATE = 128
CHUNK = 128
K_SSM = 5
K_CONV = 31
NA_D = 128
NA_WR = 8
NA_WC = 16
PLE_DIM = 256
LANES = 128
CONV_PAD = 16
VMEM_LIMIT = 56 * 1024 * 1024
MM_VMEM_BUDGET = 40 * 1024 * 1024
MM_FULL_K = 2048
NA_ROWS_PER_STEP = 16
ADAMW_TILE_BYTES = 1024 * 1024
SSD_GROUPS_PER_STEP = 4
POST_ROW_TILES = (1024, 512, 256, 128)

ADAM_LR = 0.001
ADAM_B1 = 0.9
ADAM_B2 = 0.999
ADAM_EPS = 1e-08
ADAM_WD = 0.01
ADAM_STEP = 10

NEG_BIG = -1e30


def _params(sem=None):
    if sem is None:
        return pltpu.CompilerParams(vmem_limit_bytes=VMEM_LIMIT)
    return pltpu.CompilerParams(vmem_limit_bytes=VMEM_LIMIT, dimension_semantics=sem)


def _pick(n, prefs):
    for t in prefs:
        if n % t == 0:
            return t
    return n


def _sigmoid(x):
    return jax.nn.sigmoid(x)


def _silu(x):
    return x * jax.nn.sigmoid(x)


_DN = {"nn": (((1,), (0,)), ((), ())), "nt": (((1,), (1,)), ((), ())), "tn": (((0,), (0,)), ((), ()))}


def mm(a, b, mode, name, out_dtype=f32, add=None, side=None):
    if mode == "nn":
        (M, K), (K2, N) = a.shape, b.shape
    elif mode == "nt":
        (M, K), (N, K2) = a.shape, b.shape
    else:
        (K, M), (K2, N) = a.shape, b.shape
    assert K == K2, (a.shape, b.shape, mode)
    sa, sb, so = a.dtype.itemsize, b.dtype.itemsize, jnp.dtype(out_dtype).itemsize
    tk = K if K <= MM_FULL_K else _pick(K, (1024, 512, 256, 128))
    nk = K // tk
    tms = [t for t in (1024, 512, 256, 128) if M % t == 0] or [M]
    tns = [t for t in (2048, 1024, 512, 256, 128) if N % t == 0] or [N]

    def vmem_bytes(tm, tn):
        n = 2 * (tm * tk * sa + tk * tn * sb) + 2 * tm * tn * so
        n += tm * tn * 4 if nk > 1 else 0
        n += 2 * tm * tn * add.dtype.itemsize if add is not None else 0
        return n

    ti, tj = 0, 0
    while vmem_bytes(tms[ti], tns[tj]) > MM_VMEM_BUDGET:
        if tj + 1 < len(tns) and (tns[tj] >= tms[ti] or ti + 1 >= len(tms)):
            tj += 1
        elif ti + 1 < len(tms):
            ti += 1
        else:
            break
    tm, tn = tms[ti], tns[tj]
    a_bytes, b_bytes = M * K * sa, K * N * sb
    i_outer = a_bytes + b_bytes * (M // tm) <= b_bytes + a_bytes * (N // tn)
    dn = _DN[mode]

    grid = (M // tm, N // tn, nk) if i_outer else (N // tn, M // tm, nk)
    ns = len(side[1]) if side is not None else 0
    n_add = 1 if add is not None else 0

    def body(a_ref, b_ref, *rest):
        add_ref = rest[0] if add is not None else None
        x_refs = rest[n_add:n_add + ns]
        o_ref = rest[n_add + ns]
        out_refs = rest[n_add + ns + 1:n_add + 2 * ns + 1]
        scratch = rest[n_add + 2 * ns + 1:]
        if side is not None:
            step = (pl.program_id(0) * grid[1] + pl.program_id(1)) * nk + pl.program_id(2)

            @pl.when(step == 0)
            def _():
                _exchange_start(side[0], x_refs, out_refs, scratch[-3:])

        def finish(r):
            if add is not None:
                r = r + add_ref[...].astype(f32)
            o_ref[...] = r.astype(out_dtype)

        part = lax.dot_general(a_ref[...].astype(bf16), b_ref[...].astype(bf16), dn, preferred_element_type=f32)
        if nk == 1:
            finish(part)
        else:
            acc = scratch[0]
            k = pl.program_id(2)

            @pl.when(k == 0)
            def _():
                acc[...] = part

            @pl.when(k > 0)
            def _():
                acc[...] += part

            @pl.when(k == nk - 1)
            def _():
                finish(acc[...])

        if side is not None:
            @pl.when(step == grid[0] * grid[1] * nk - 1)
            def _():
                _exchange_finish(side[0], x_refs, out_refs, scratch[-3:])

    def ij(p, q):
        return (p, q) if i_outer else (q, p)

    if mode == "tn":
        a_spec = pl.BlockSpec((tk, tm), lambda p, q, k: (k, ij(p, q)[0]))
    else:
        a_spec = pl.BlockSpec((tm, tk), lambda p, q, k: (ij(p, q)[0], k))
    if mode == "nt":
        b_spec = pl.BlockSpec((tn, tk), lambda p, q, k: (ij(p, q)[1], k))
    else:
        b_spec = pl.BlockSpec((tk, tn), lambda p, q, k: (k, ij(p, q)[1]))
    o_spec = pl.BlockSpec((tm, tn), lambda p, q, k: ij(p, q))
    in_specs = [a_spec, b_spec] + ([o_spec] if add is not None else [])
    args = (a, b) + ((add,) if add is not None else ())
    acc_scratch = [pltpu.VMEM((tm, tn), f32)] if nk > 1 else []
    o_shape = jax.ShapeDtypeStruct((M, N), out_dtype)
    if side is None:
        return pl.pallas_call(
            body, name=name, grid=grid, in_specs=in_specs, out_specs=o_spec, out_shape=o_shape, scratch_shapes=acc_scratch,
            compiler_params=_params(("parallel", "parallel", "arbitrary")))(*args)
    res = pl.pallas_call(
        body, name=name, grid=grid, in_specs=in_specs + [_HBM] * ns, out_specs=[o_spec] + [_HBM] * ns,
        out_shape=[o_shape] + _exchange_out_shapes(side[0], side[1]), scratch_shapes=acc_scratch + _exchange_scratch(ns),
        compiler_params=_params(("arbitrary", "arbitrary", "arbitrary")))(*args, *side[1])
    return res[0], res[1:]


def _row_spec(T, w, base):
    return pl.BlockSpec((T, w), lambda g, i: (i, base + g))


def _const_spec(r, w, base):
    return pl.BlockSpec((r, w), lambda g, i: (0, base + g))


def rowop_fwd(fn, name, rows, consts, outs, T, ncb=1):
    L = rows[0][0].shape[0]
    nr, nc = len(rows), len(consts)

    def body(*refs):
        ins = [r[...].astype(f32) for r in refs[:nr + nc]]
        res = fn(*ins)
        for o_ref, r in zip(refs[nr + nc:], res):
            o_ref[...] = r.astype(o_ref.dtype)

    in_specs = [_row_spec(T, w, b) for (_, w, b) in rows] + [_const_spec(a.shape[0], w, b) for (a, w, b) in consts]
    out_specs = [_row_spec(T, w, 0) for (w, _) in outs]
    out_shape = [jax.ShapeDtypeStruct((L, ncb * w), dt) for (w, dt) in outs]
    return pl.pallas_call(
        body, name=name, grid=(ncb, L // T), in_specs=in_specs, out_specs=out_specs, out_shape=out_shape,
        compiler_params=_params(("parallel", "parallel")))(*[a for (a, _, _) in rows], *[a for (a, _, _) in consts])


def rowop_bwd(fn, name, rows, consts, cts, row_grads, const_grads, T, ncb=1, add=None):
    L = rows[0][0].shape[0]
    nr, nc, nct = len(rows), len(consts), len(cts)
    n_in = nr + nc + nct + (1 if add is not None else 0)
    rg_idx = [i for i, d in enumerate(row_grads) if d is not None]
    cg_idx = [i for i, d in enumerate(const_grads) if d]

    def body(*refs):
        ins = [r[...].astype(f32) for r in refs[:nr + nc]]
        ct = tuple(r[...].astype(f32) for r in refs[nr + nc:nr + nc + nct])
        _, vjp = jax.vjp(fn, *ins)
        grads = vjp(ct)
        outs = refs[n_in:]
        for n, i in enumerate(rg_idx):
            gi = grads[i]
            if add is not None and n == 0:
                gi = gi + refs[n_in - 1][...].astype(f32)
            outs[n][...] = gi.astype(outs[n].dtype)
        first = pl.program_id(1) == 0
        for n, i in enumerate(cg_idx):
            o = outs[len(rg_idx) + n]

            @pl.when(first)
            def _(o=o):
                o[...] = jnp.zeros_like(o)

            o[...] += grads[nr + i]

    in_specs = ([_row_spec(T, w, b) for (_, w, b) in rows] + [_const_spec(a.shape[0], w, b) for (a, w, b) in consts]
                + [_row_spec(T, w, b) for (_, w, b) in cts] + ([_row_spec(T, add[1], add[2])] if add is not None else []))
    out_specs = ([_row_spec(T, rows[i][1], 0) for i in rg_idx] + [_const_spec(consts[i][0].shape[0], consts[i][1], 0) for i in cg_idx])
    out_shape = ([jax.ShapeDtypeStruct((L, ncb * rows[i][1]), row_grads[i]) for i in rg_idx]
                 + [jax.ShapeDtypeStruct((consts[i][0].shape[0], ncb * consts[i][1]), f32) for i in cg_idx])
    args = [a for (a, _, _) in rows] + [a for (a, _, _) in consts] + [a for (a, _, _) in cts] + ([add[0]] if add is not None else [])
    return pl.pallas_call(
        body, name=name, grid=(ncb, L // T), in_specs=in_specs, out_specs=out_specs, out_shape=out_shape,
        compiler_params=_params(("parallel", "arbitrary")))(*args)


def _rms_fn(x, w):
    return (x * lax.rsqrt(jnp.mean(x * x, axis=-1, keepdims=True) + EPS) * w,)


def _glu_fn(a, ag):
    return (a * _sigmoid(ag),)


def _gate_fn(o, g):
    return (o * _silu(g),)


def _ple_fn(gpre, pp):
    return (_sigmoid(gpre) * pp,)


def _ple_fwd_fn(h1, gpre, pp):
    return (h1 + _sigmoid(gpre) * pp,)


def _lngate_fn(vc, g, w, b):
    mu = jnp.mean(vc, axis=-1, keepdims=True)
    xc = vc - mu
    y = xc * lax.rsqrt(jnp.mean(xc * xc, axis=-1, keepdims=True) + EPS) * w + b
    return (_silu(y) * _silu(g),)


def _post_fn(yf, yb, xs, z, dsk, gw):
    y = (yf + yb + dsk * xs) * _silu(z)
    return (y * lax.rsqrt(jnp.mean(y * y, axis=-1, keepdims=True) + EPS) * gw,)


def _make_prep_fn(n_heads):
    def prep_fn(u, bias, alog):
        lane = lax.broadcasted_iota(jnp.int32, (1, LANES), 1)
        dt = jnp.where(lane < 2 * n_heads, jax.nn.softplus(u + bias), 0.0)
        da = dt * (-jnp.exp(alog))
        li = lax.broadcasted_iota(jnp.int32, (CHUNK, CHUNK), 0)
        si = lax.broadcasted_iota(jnp.int32, (CHUNK, CHUNK), 1)
        tril = (si <= li).astype(f32)
        triu = (si >= li).astype(f32)
        csf = jnp.dot(tril, da, precision=HIGHEST, preferred_element_type=f32)
        csb = jnp.dot(triu, da, precision=HIGHEST, preferred_element_type=f32)
        return dt, jnp.where(lane < n_heads, csf, csb)

    return prep_fn


def final_loss(h, w, tgt, name="final_loss"):
    L, D = h.shape
    T = _pick(L, (256, 128))

    def fn(x, w_, t):
        y = x * lax.rsqrt(jnp.mean(x * x, axis=-1, keepdims=True) + EPS) * w_
        err = jnp.square(y - t)
        return 0.5 * jnp.sum(jnp.mean(err, axis=-1, keepdims=True))

    def body(h_ref, w_ref, t_ref, loss_ref, dh_ref, dw_ref):
        @pl.when(pl.program_id(0) == 0)
        def _():
            loss_ref[...] = jnp.zeros_like(loss_ref)
            dw_ref[...] = jnp.zeros_like(dw_ref)

        t = t_ref[...]
        val, vjp = jax.vjp(lambda x, w_: fn(x, w_, t), h_ref[...], w_ref[...])
        dx, dw = vjp(jnp.ones((), f32))
        dh_ref[...] = dx
        dw_ref[...] += dw
        loss_ref[...] += jnp.broadcast_to(val, loss_ref.shape)

    return pl.pallas_call(
        body, name=name, grid=(L // T,),
        in_specs=[pl.BlockSpec((T, D), lambda i: (i, 0)), pl.BlockSpec((1, D), lambda i: (0, 0)), pl.BlockSpec((T, D), lambda i: (i, 0))],
        out_specs=[pl.BlockSpec((8, LANES), lambda i: (0, 0)), pl.BlockSpec((T, D), lambda i: (i, 0)), pl.BlockSpec((1, D), lambda i: (0, 0))],
        out_shape=[jax.ShapeDtypeStruct((8, LANES), f32), jax.ShapeDtypeStruct((L, D), f32), jax.ShapeDtypeStruct((1, D), f32)],
        compiler_params=_params(("arbitrary",)))(h, w, tgt)


CONV_ROWS = 128


def conv_fwd(x_arr, xbase, C, w, b, K, act, name):
    L = x_arr.shape[0]
    R = CONV_ROWS
    lo = CONV_PAD - K // 2

    def body(x_ref, w_ref, b_ref, *rest):
        xpad = rest[-1]
        outs = rest[:-1]
        xpad[0:CONV_PAD, :] = jnp.zeros((CONV_PAD, LANES), f32)
        xpad[CONV_PAD + L:CONV_PAD + L + CONV_PAD, :] = jnp.zeros((CONV_PAD, LANES), f32)
        xpad[CONV_PAD:CONV_PAD + L, :] = x_ref[...]

        def chunk(ci, carry):
            r0 = pl.multiple_of(ci * R, R)
            acc = jnp.broadcast_to(b_ref[...], (R, LANES))
            for k in range(K):
                acc = acc + w_ref[k:k + 1, :] * xpad[pl.ds(r0 + (lo + k), R), :]
            outs[0][pl.ds(r0, R), :] = acc
            if act:
                outs[1][pl.ds(r0, R), :] = _silu(acc)
            return carry

        lax.fori_loop(0, L // R, chunk, 0)

    col = pl.BlockSpec((L, LANES), lambda j: (0, j))
    n_out = 2 if act else 1
    return pl.pallas_call(
        body, name=name, grid=(C // LANES,),
        in_specs=[pl.BlockSpec((L, LANES), lambda j: (0, xbase + j)), pl.BlockSpec((K, LANES), lambda j: (0, j)),
                  pl.BlockSpec((1, LANES), lambda j: (0, j))],
        out_specs=[col] * n_out, out_shape=[jax.ShapeDtypeStruct((L, C), f32)] * n_out,
        scratch_shapes=[pltpu.VMEM((L + 2 * CONV_PAD, LANES), f32)],
        compiler_params=_params(("parallel",)))(x_arr, w, b)


def conv_bwd(dy, x_arr, xbase, C, w, K, pre, out_dtype, name):
    L = x_arr.shape[0]
    R = CONV_ROWS
    lo = CONV_PAD - K // 2
    act = pre is not None

    def body(*refs):
        if act:
            dy_ref, pre_ref, x_ref, w_ref, dx_ref, dw_ref, db_ref, dpad, xpad, accw, accb = refs
        else:
            dy_ref, x_ref, w_ref, dx_ref, dw_ref, db_ref, dpad, xpad, accw, accb = refs
        zeros = jnp.zeros((CONV_PAD, LANES), f32)
        for pad in (dpad, xpad):
            pad[0:CONV_PAD, :] = zeros
            pad[CONV_PAD + L:CONV_PAD + L + CONV_PAD, :] = zeros
        xpad[CONV_PAD:CONV_PAD + L, :] = x_ref[...]
        if act:
            p = pre_ref[...]
            s = _sigmoid(p)
            dpad[CONV_PAD:CONV_PAD + L, :] = dy_ref[...].astype(f32) * (s * (1.0 + p * (1.0 - s)))
        else:
            dpad[CONV_PAD:CONV_PAD + L, :] = dy_ref[...].astype(f32)
        accw[...] = jnp.zeros_like(accw)
        accb[...] = jnp.zeros_like(accb)

        def chunk(ci, carry):
            r0 = pl.multiple_of(ci * R, R)
            acc = jnp.zeros((R, LANES), f32)
            for k in range(K):
                acc = acc + w_ref[k:k + 1, :] * dpad[pl.ds(r0 + (CONV_PAD + K // 2 - k), R), :]
            dx_ref[pl.ds(r0, R), :] = acc.astype(out_dtype)
            d = dpad[pl.ds(r0 + CONV_PAD, R), :]
            accb[...] += jnp.sum(d.reshape(R // 8, 8, LANES), axis=0)
            for k in range(K):
                prod = d * xpad[pl.ds(r0 + (lo + k), R), :]
                accw[k] += jnp.sum(prod.reshape(R // 8, 8, LANES), axis=0)
            return carry

        lax.fori_loop(0, L // R, chunk, 0)
        dw_ref[...] = jnp.sum(accw[...], axis=1)
        db_ref[...] = jnp.sum(accb[...], axis=0, keepdims=True)

    col = pl.BlockSpec((L, LANES), lambda j: (0, j))
    in_specs = [col] + ([col] if act else []) + [pl.BlockSpec((L, LANES), lambda j: (0, xbase + j)), pl.BlockSpec((K, LANES), lambda j: (0, j))]
    args = (dy,) + ((pre,) if act else ()) + (x_arr, w)
    return pl.pallas_call(
        body, name=name, grid=(C // LANES,), in_specs=in_specs,
        out_specs=[col, pl.BlockSpec((K, LANES), lambda j: (0, j)), pl.BlockSpec((1, LANES), lambda j: (0, j))],
        out_shape=[jax.ShapeDtypeStruct((L, C), out_dtype), jax.ShapeDtypeStruct((K, C), f32), jax.ShapeDtypeStruct((1, C), f32)],
        scratch_shapes=[pltpu.VMEM((L + 2 * CONV_PAD, LANES), f32), pltpu.VMEM((L + 2 * CONV_PAD, LANES), f32),
                        pltpu.VMEM((K, 8, LANES), f32), pltpu.VMEM((8, LANES), f32)],
        compiler_params=_params(("parallel",)))(*args)


def _ssd_chunk(X, Bm, Cm, dtc, csc, csr, S, lane_off, dirn):
    Qn = X.shape[0]
    lane = lax.broadcasted_iota(jnp.int32, (1, LANES), 1)
    head = lax.shift_right_logical(lax.broadcasted_iota(jnp.int32, (1, GROUP_W), 1), 6)
    dtx = jnp.zeros((Qn, GROUP_W), f32)
    csx = jnp.zeros((Qn, GROUP_W), f32)
    cols = []
    for j in range(HEADS_PER_GROUP):
        sel = lane == lane_off + j
        dcol = jnp.sum(jnp.where(sel, dtc, 0.0), axis=1, keepdims=True)
        ccol = jnp.sum(jnp.where(sel, csc, 0.0), axis=1, keepdims=True)
        dtx = jnp.where(head == j, dcol, dtx)
        csx = jnp.where(head == j, ccol, csx)
        cols.append(ccol)
    Xd = X * dtx
    edge = Qn - 1 if dirn == 0 else 0
    cs_edge = csx[edge:edge + 1, :]
    li = lax.broadcasted_iota(jnp.int32, (Qn, Qn), 0)
    si = lax.broadcasted_iota(jnp.int32, (Qn, Qn), 1)
    mask = (si <= li) if dirn == 0 else (si >= li)
    Cb = Cm.astype(bf16)
    Bb = Bm.astype(bf16)
    G = lax.dot_general(Cb, Bb, _DN["nt"], preferred_element_type=f32)
    Y = jnp.exp(csx) * jnp.dot(Cb, S.astype(bf16), preferred_element_type=f32)
    for j in range(HEADS_PER_GROUP):
        row = csr[j:j + 1, :]
        Lm = jnp.exp(jnp.where(mask, cols[j] - row, NEG_BIG))
        Wj = (G * Lm).astype(bf16)
        Xj = jnp.where(head == j, Xd, 0.0).astype(bf16)
        Y = Y + jnp.dot(Wj, Xj, preferred_element_type=f32)
    S_new = S * jnp.exp(cs_edge) + lax.dot_general(Bb, (Xd * jnp.exp(cs_edge - csx)).astype(bf16), _DN["tn"], preferred_element_type=f32)
    return Y, S_new


def _ssd_specs(D, G, nchunk, dirn, descending):
    def cidx(c):
        return nchunk - 1 - c if descending else c

    gps = _ssd_groups_per_step(D, G)
    nb = D // N_STATE
    x_spec = pl.BlockSpec((CHUNK, gps * GROUP_W), lambda c, g: (cidx(c), g))
    b_spec = pl.BlockSpec((CHUNK, gps * N_STATE), lambda c, g: (cidx(c), nb // gps + g))
    c_spec = pl.BlockSpec((CHUNK, gps * N_STATE), lambda c, g: (cidx(c), (nb + G) // gps + g))
    n_spec = pl.BlockSpec((CHUNK, gps * N_STATE), lambda c, g: (cidx(c), g))
    lane_spec = pl.BlockSpec((CHUNK, LANES), lambda c, g: (cidx(c), 0))
    csr_spec = pl.BlockSpec((gps, 8, CHUNK), lambda c, g: (dirn * (G // gps) + g, 0, cidx(c)))
    dcsr_spec = pl.BlockSpec((gps, 8, CHUNK), lambda c, g: (g, 0, cidx(c)))
    s_spec = pl.BlockSpec((gps, None, N_STATE, GROUP_W), lambda c, g: (g, cidx(c), 0, 0))
    return x_spec, b_spec, c_spec, n_spec, lane_spec, csr_spec, dcsr_spec, s_spec


def _ssd_groups_per_step(D, G):
    nb = D // N_STATE
    for n in (SSD_GROUPS_PER_STEP, 2):
        if G % n == 0 and nb % n == 0:
            return n
    return 1


def ssd_fwd(act, dt128, cs128, csr, dirn, n_heads, name):
    L = act.shape[0]
    D = act.shape[1] // 2
    G = D // GROUP_W
    nchunk = L // CHUNK
    x_spec, b_spec, c_spec, _, lane_spec, csr_spec, _, s_spec = _ssd_specs(D, G, nchunk, dirn, dirn == 1)

    gps = _ssd_groups_per_step(D, G)

    def body(x_ref, b_ref, c_ref, dt_ref, cs_ref, csr_ref, y_ref, ssave_ref, S):
        c = pl.program_id(0)
        for gg in range(gps):
            g = pl.program_id(1) * gps + gg

            @pl.when(c == 0)
            def _():
                S[g] = jnp.zeros((N_STATE, GROUP_W), f32)

            xs_, ns_ = pl.ds(gg * GROUP_W, GROUP_W), pl.ds(gg * N_STATE, N_STATE)
            s_in = S[g]
            ssave_ref[gg] = s_in
            y, s_new = _ssd_chunk(x_ref[:, xs_], b_ref[:, ns_], c_ref[:, ns_], dt_ref[...], cs_ref[...], csr_ref[gg], s_in,
                                  dirn * n_heads + HEADS_PER_GROUP * g, dirn)
            y_ref[:, xs_] = y
            S[g] = s_new

    return pl.pallas_call(
        body, name=name, grid=(nchunk, G // gps),
        in_specs=[x_spec, b_spec, c_spec, lane_spec, lane_spec, csr_spec],
        out_specs=[x_spec, s_spec],
        out_shape=[jax.ShapeDtypeStruct((L, D), f32), jax.ShapeDtypeStruct((G, nchunk, N_STATE, GROUP_W), f32)],
        scratch_shapes=[pltpu.VMEM((G, N_STATE, GROUP_W), f32)],
        compiler_params=_params(("arbitrary", "arbitrary")))(act, act, act, dt128, cs128, csr)


def ssd_bwd(act, dt128, cs128, csr, ssave, dy, dirn, n_heads, name):
    L = act.shape[0]
    D = act.shape[1] // 2
    G = D // GROUP_W
    nchunk = L // CHUNK
    x_spec, b_spec, c_spec, n_spec, lane_spec, csr_in, dcsr_spec, s_spec = _ssd_specs(D, G, nchunk, dirn, dirn == 0)

    gps = _ssd_groups_per_step(D, G)

    def body(x_ref, b_ref, c_ref, dt_ref, cs_ref, csr_ref, s_ref, dy_ref, dx_ref, db_ref, dc_ref, ddt_ref, dcs_ref, dcsr_ref, dS):
        c = pl.program_id(0)

        @pl.when(pl.program_id(1) == 0)
        def _():
            ddt_ref[...] = jnp.zeros_like(ddt_ref)
            dcs_ref[...] = jnp.zeros_like(dcs_ref)

        ddt_sum, dcs_sum = None, None
        for gg in range(gps):
            g = pl.program_id(1) * gps + gg

            @pl.when(c == 0)
            def _():
                dS[g] = jnp.zeros((N_STATE, GROUP_W), f32)

            xs_, ns_ = pl.ds(gg * GROUP_W, GROUP_W), pl.ds(gg * N_STATE, N_STATE)
            lane_off = dirn * n_heads + HEADS_PER_GROUP * g
            _, vjp = jax.vjp(lambda X, Bm, Cm, dtc, csc, csr_, S: _ssd_chunk(X, Bm, Cm, dtc, csc, csr_, S, lane_off, dirn),
                             x_ref[:, xs_], b_ref[:, ns_], c_ref[:, ns_], dt_ref[...], cs_ref[...], csr_ref[gg], s_ref[gg])
            dX, dB, dC, ddt, dcs, dcsr, dS_in = vjp((dy_ref[:, xs_], dS[g]))
            dx_ref[:, xs_] = dX
            db_ref[:, ns_] = dB
            dc_ref[:, ns_] = dC
            dcsr_ref[gg] = dcsr
            dS[g] = dS_in
            ddt_sum = ddt if ddt_sum is None else ddt_sum + ddt
            dcs_sum = dcs if dcs_sum is None else dcs_sum + dcs

        ddt_ref[...] += ddt_sum
        dcs_ref[...] += dcs_sum

    return pl.pallas_call(
        body, name=name, grid=(nchunk, G // gps),
        in_specs=[x_spec, b_spec, c_spec, lane_spec, lane_spec, csr_in, s_spec, x_spec],
        out_specs=[x_spec, n_spec, n_spec, lane_spec, lane_spec, dcsr_spec],
        out_shape=[jax.ShapeDtypeStruct((L, D), f32), jax.ShapeDtypeStruct((L, G * N_STATE), f32), jax.ShapeDtypeStruct((L, G * N_STATE), f32),
                   jax.ShapeDtypeStruct((L, LANES), f32), jax.ShapeDtypeStruct((L, LANES), f32), jax.ShapeDtypeStruct((G, 8, L), f32)],
        scratch_shapes=[pltpu.VMEM((G, N_STATE, GROUP_W), f32)],
        compiler_params=_params(("arbitrary", "arbitrary")))(act, act, act, dt128, cs128, csr, ssave, dy)


def _csr_from_cs(cs128, n_heads):
    L = cs128.shape[0]
    t = cs128[:, :2 * n_heads].T.reshape(2 * n_heads // HEADS_PER_GROUP, HEADS_PER_GROUP, L)
    return jnp.pad(t, ((0, 0), (0, 8 - HEADS_PER_GROUP), (0, 0)))


def _cs_from_dcsr(dcsr_f, dcsr_b, n_heads):
    L = dcsr_f.shape[-1]
    t = jnp.concatenate([dcsr_f, dcsr_b], axis=0)[:, :HEADS_PER_GROUP, :].reshape(2 * n_heads, L).T
    return jnp.pad(t, ((0, 0), (0, LANES - 2 * n_heads)))


def na_bias_table(rpb):
    nh = rpb.shape[0]
    cols = np.arange(GRID_W)
    col_start = np.clip(cols - NA_WC // 2, 0, GRID_W - NA_WC)
    col_valid = (cols[None, :] >= col_start[:, None]) & (cols[None, :] < col_start[:, None] + NA_WC)
    col_off = np.clip(cols[None, :] - cols[:, None], -(NA_WC - 1), NA_WC - 1) + NA_WC - 1
    rm = np.zeros((NA_WR, NA_WR, 2 * NA_WR - 1), np.float32)
    for d in range(NA_WR):
        for i in range(NA_WR):
            rm[d, i, i - d + NA_WR - 1] = 1.0
    cm = np.zeros((GRID_W, GRID_W, 2 * NA_WC - 1), np.float32)
    cm[np.arange(GRID_W)[:, None], np.arange(GRID_W)[None, :], col_off] = 1.0
    bt = jnp.einsum("hrc,dir,qkc->hdqik", rpb, jnp.asarray(rm), jnp.asarray(cm), precision=HIGHEST)
    bt = jnp.where(jnp.asarray(col_valid)[None, None, :, None, :], bt, NEG_BIG)
    return bt.reshape(nh, NA_WR, GRID_W, NA_WR * GRID_W)


def _na_window(r, rows):
    rs = jnp.clip(r - NA_WR // 2, 0, rows - NA_WR)
    return rs, r - rs


def _na_probs(q, kw, bias):
    s = lax.dot_general(q.astype(bf16), kw.astype(bf16), _DN["nt"], preferred_element_type=f32) * (NA_D ** -0.5) + bias
    m = jnp.max(s, axis=-1, keepdims=True)
    p = jnp.exp(s - m)
    return p / jnp.sum(p, axis=-1, keepdims=True)


def na_fwd(um, qbase, kbase, vbase, bt, name):
    L = um.shape[0]
    nh = bt.shape[0]
    rows = L // GRID_W
    WK = NA_WR * GRID_W

    RB = _pick(rows, (NA_ROWS_PER_STEP, 2, 1))

    def body(q_ref, k_ref, v_ref, bt_ref, o_ref):
        win = [_na_window(pl.program_id(1) * RB + rb, rows) for rb in range(RB)]
        t0s = [pl.multiple_of(rs * GRID_W, GRID_W) for rs, _ in win]
        qrows = [pl.ds(rb * GRID_W, GRID_W) for rb in range(RB)]
        ps = [_na_probs(q_ref[qr, :], k_ref[pl.ds(t0, WK), :], bt_ref[delta]) for qr, t0, (_, delta) in zip(qrows, t0s, win)]
        os_ = [jnp.dot(p.astype(bf16), v_ref[pl.ds(t0, WK), :].astype(bf16), preferred_element_type=f32) for p, t0 in zip(ps, t0s)]
        for qr, o in zip(qrows, os_):
            o_ref[qr, :] = o

    kv = lambda base: pl.BlockSpec((L, NA_D), lambda h, r: (0, base + h))
    return pl.pallas_call(
        body, name=name, grid=(nh, rows // RB),
        in_specs=[pl.BlockSpec((RB * GRID_W, NA_D), lambda h, r: (r, qbase + h)), kv(kbase), kv(vbase),
                  pl.BlockSpec((None, NA_WR, GRID_W, WK), lambda h, r: (h, 0, 0, 0))],
        out_specs=pl.BlockSpec((RB * GRID_W, NA_D), lambda h, r: (r, h)),
        out_shape=jax.ShapeDtypeStruct((L, nh * NA_D), f32),
        compiler_params=_params(("parallel", "arbitrary")))(um, um, um, bt)


def na_bwd(um, qbase, kbase, vbase, bt, do, name, side=None):
    L = um.shape[0]
    nh = bt.shape[0]
    rows = L // GRID_W
    WK = NA_WR * GRID_W
    scale = NA_D ** -0.5

    RB = _pick(rows, (NA_ROWS_PER_STEP, 2, 1))

    ns = len(side[1]) if side is not None else 0
    nsteps = rows // RB

    def body(q_ref, k_ref, v_ref, bt_ref, do_ref, *rest):
        x_refs = rest[:ns]
        dq_ref, dk_ref, dv_ref, dbt_ref = rest[ns:ns + 4]
        out_refs = rest[ns + 4:2 * ns + 4]
        sems = rest[2 * ns + 4:]
        if side is not None:
            step = pl.program_id(0) * nsteps + pl.program_id(1)

            @pl.when(step == 0)
            def _():
                _exchange_start(side[0], x_refs, out_refs, sems)

        @pl.when(pl.program_id(1) == 0)
        def _():
            dk_ref[...] = jnp.zeros_like(dk_ref)
            dv_ref[...] = jnp.zeros_like(dv_ref)
            dbt_ref[...] = jnp.zeros_like(dbt_ref)

        win = [_na_window(pl.program_id(1) * RB + rb, rows) for rb in range(RB)]
        t0s = [pl.multiple_of(rs * GRID_W, GRID_W) for rs, _ in win]
        qrows = [pl.ds(rb * GRID_W, GRID_W) for rb in range(RB)]
        qs_ = [q_ref[qr, :] for qr in qrows]
        kws = [k_ref[pl.ds(t0, WK), :].astype(bf16) for t0 in t0s]
        vws = [v_ref[pl.ds(t0, WK), :].astype(bf16) for t0 in t0s]
        dos = [do_ref[qr, :].astype(bf16) for qr in qrows]
        ps = [_na_probs(q, kw, bt_ref[delta]) for q, kw, (_, delta) in zip(qs_, kws, win)]
        dps = [lax.dot_general(do_, vw, _DN["nt"], preferred_element_type=f32) for do_, vw in zip(dos, vws)]
        dss = [p * (dp - jnp.sum(dp * p, axis=-1, keepdims=True)) for p, dp in zip(ps, dps)]
        dsbs = [ds.astype(bf16) for ds in dss]
        dqs = [jnp.dot(dsb, kw, preferred_element_type=f32) * scale for dsb, kw in zip(dsbs, kws)]
        dks = [lax.dot_general(dsb, q.astype(bf16), _DN["tn"], preferred_element_type=f32) * scale for dsb, q in zip(dsbs, qs_)]
        dvs = [lax.dot_general(p.astype(bf16), do_, _DN["tn"], preferred_element_type=f32) for p, do_ in zip(ps, dos)]
        for rb in range(RB):
            dq_ref[qrows[rb], :] = dqs[rb]
            dbt_ref[win[rb][1]] += dss[rb]
            dk_ref[pl.ds(t0s[rb], WK), :] += dks[rb]
            dv_ref[pl.ds(t0s[rb], WK), :] += dvs[rb]

        if side is not None:
            @pl.when(step == nh * nsteps - 1)
            def _():
                _exchange_finish(side[0], x_refs, out_refs, sems)

    kv = lambda base: pl.BlockSpec((L, NA_D), lambda h, r: (0, base + h))
    qs = lambda base: pl.BlockSpec((RB * GRID_W, NA_D), lambda h, r: (r, base + h))
    bts = pl.BlockSpec((None, NA_WR, GRID_W, WK), lambda h, r: (h, 0, 0, 0))
    big = jax.ShapeDtypeStruct((L, nh * NA_D), f32)
    in_specs = [qs(qbase), kv(kbase), kv(vbase), bts, qs(0)]
    out_specs = [qs(0), kv(0), kv(0), bts]
    out_shape = [big, big, big, jax.ShapeDtypeStruct(bt.shape, f32)]
    if side is None:
        return pl.pallas_call(
            body, name=name, grid=(nh, nsteps), in_specs=in_specs, out_specs=out_specs, out_shape=out_shape,
            compiler_params=_params(("parallel", "arbitrary")))(um, um, um, bt, do)
    res = pl.pallas_call(
        body, name=name, grid=(nh, nsteps), in_specs=in_specs + [_HBM] * ns, out_specs=out_specs + [_HBM] * ns,
        out_shape=out_shape + _exchange_out_shapes(side[0], side[1]), scratch_shapes=_exchange_scratch(ns),
        compiler_params=_params(("arbitrary", "arbitrary")))(um, um, um, bt, do, *side[1])
    return tuple(res[:4]) + (res[4:],)


_HBM = pl.BlockSpec(memory_space=pltpu.HBM)


def all_gather(xs, name):
    n = len(xs)

    def body(*refs):
        _exchange_start("gather", refs[:n], refs[n:2 * n], refs[2 * n:])
        _exchange_finish("gather", refs[:n], refs[n:2 * n], refs[2 * n:])

    return pl.pallas_call(
        body, name=name, in_specs=[_HBM] * n, out_specs=[_HBM] * n, out_shape=_exchange_out_shapes("gather", xs),
        scratch_shapes=_exchange_scratch(n))(*xs)


def _exchange_out_shapes(kind, xs):
    return [jax.ShapeDtypeStruct(((N_DEV,) + x.shape) if kind == "gather" else x.shape, x.dtype) for x in xs]


def _exchange_scratch(n):
    return [pltpu.SemaphoreType.DMA((7 * n,)), pltpu.SemaphoreType.DMA((7 * n,)), pltpu.SemaphoreType.DMA((n,))]


def _gather_copies(x_refs, out_refs, sems):
    send_sems, recv_sems, local_sems = sems
    n = len(x_refs)
    mx, my, mc = lax.axis_index("x"), lax.axis_index("y"), lax.axis_index("c")
    me, sibling = (mx, my, mc), (mx, my, 1 - mc)
    chips = [(1 - mx, my), (mx, 1 - my), (1 - mx, 1 - my)]

    def slab(t, px, py, pc):
        return out_refs[t].at[4 * px + 2 * py + pc]

    def copy(t, k, block, to, src=None):
        return pltpu.make_async_remote_copy(
            src_ref=slab(t, *block) if src is None else src, dst_ref=slab(t, *block),
            send_sem=send_sems.at[7 * t + k], recv_sem=recv_sems.at[7 * t + k], device_id=to, device_id_type=pl.DeviceIdType.MESH)

    mine = [pltpu.make_async_copy(x_refs[t], slab(t, *me), local_sems.at[t]) for t in range(n)]
    first = []
    for t in range(n):
        first.append(copy(t, 0, me, sibling, src=x_refs[t]))
        first += [copy(t, 1 + j, me, (*chip, mc), src=x_refs[t]) for j, chip in enumerate(chips)]
    return mine, first, copy, me, sibling, chips, mc


def _exchange_start(kind, x_refs, out_refs, sems):
    if kind == "gather":
        mine, first = _gather_copies(x_refs, out_refs, sems)[:2]
    else:
        mine, first = _a2a_copies(x_refs, out_refs, sems)
    for cp in mine + first:
        cp.start()


def _exchange_finish(kind, x_refs, out_refs, sems):
    n = len(x_refs)
    if kind == "gather":
        mine, first, copy, me, sibling, chips, mc = _gather_copies(x_refs, out_refs, sems)
        passed = []
        for t in range(n):
            for j, chip in enumerate(chips):
                copy(t, 1 + j, (*chip, mc), me).wait_recv()
                fwd = copy(t, 4 + j, (*chip, mc), sibling)
                fwd.start()
                passed.append(fwd)
        for t in range(n):
            copy(t, 0, sibling, me).wait_recv()
            for j, chip in enumerate(chips):
                copy(t, 4 + j, (*chip, 1 - mc), me).wait_recv()
        for cp in first + passed:
            cp.wait_send()
    else:
        mine, first = _a2a_copies(x_refs, out_refs, sems)
        for cp in first:
            cp.wait()
    for cp in mine:
        cp.wait()


def _a2a_copies(x_refs, out_refs, sems):
    send_sems, recv_sems, local_sems = sems
    n = len(x_refs)
    mx, my, mc = lax.axis_index("x"), lax.axis_index("y"), lax.axis_index("c")
    me = 4 * mx + 2 * my + mc
    mine = [pltpu.make_async_copy(x_refs[t].at[me], out_refs[t].at[me], local_sems.at[t]) for t in range(n)]
    copies = []
    for t in range(n):
        for k in range(1, N_DEV):
            px = 1 - mx if k & 4 else mx
            py = 1 - my if k & 2 else my
            pc = 1 - mc if k & 1 else mc
            peer = 4 * px + 2 * py + pc
            copies.append(pltpu.make_async_remote_copy(
                src_ref=x_refs[t].at[peer], dst_ref=out_refs[t].at[me], send_sem=send_sems.at[7 * t + k - 1],
                recv_sem=recv_sems.at[7 * t + k - 1], device_id=(px, py, pc), device_id_type=pl.DeviceIdType.MESH))
    return mine, copies


def all_to_all(xs, name):
    n = len(xs)

    def body(*refs):
        _exchange_start("a2a", refs[:n], refs[n:2 * n], refs[2 * n:])
        _exchange_finish("a2a", refs[:n], refs[n:2 * n], refs[2 * n:])

    return pl.pallas_call(
        body, name=name, in_specs=[_HBM] * n, out_specs=[_HBM] * n, out_shape=_exchange_out_shapes("a2a", xs),
        scratch_shapes=_exchange_scratch(n))(*xs)


def adamw(w, slabs, m, v, own, name):
    R, C = w.shape
    tr = _pick(R, tuple(t for t in (256, 128, 64, 32, 16, 8) if t * C * 4 <= ADAMW_TILE_BYTES) or (8,))
    c1 = 1.0 - ADAM_B1 ** ADAM_STEP
    c2 = 1.0 - ADAM_B2 ** ADAM_STEP

    def body(*refs):
        if own is not None:
            me_ref, w_ref, s_ref, m_ref, v_ref, own_ref, g_ref, d_ref, nm_ref, nv_ref = refs
        else:
            w_ref, s_ref, m_ref, v_ref, g_ref, d_ref, nm_ref, nv_ref = refs
        g = None
        for s in range(N_DEV):
            t = s_ref[s].astype(f32)
            if own is not None:
                t = jnp.where(me_ref[0] == s, own_ref[...], t)
            g = t if g is None else g + t
        wv = w_ref[...]
        mn = ADAM_B1 * m_ref[...] + (1.0 - ADAM_B1) * g
        vn = ADAM_B2 * v_ref[...] + (1.0 - ADAM_B2) * jnp.square(g)
        m_hat = mn / c1
        v_hat = vn / c2
        g_ref[...] = g
        d_ref[...] = -ADAM_LR * (m_hat / (jnp.sqrt(v_hat) + ADAM_EPS) + ADAM_WD * wv)
        nm_ref[...] = mn
        nv_ref[...] = vn

    out_shape = [jax.ShapeDtypeStruct((R, C), f32)] * 4
    if own is not None:
        me, part = own
        blk = pl.BlockSpec((tr, C), lambda i, me_: (i, 0))
        gs = pltpu.PrefetchScalarGridSpec(
            num_scalar_prefetch=1, grid=(R // tr,),
            in_specs=[blk, pl.BlockSpec((N_DEV, tr, C), lambda i, me_: (0, i, 0)), blk, blk, blk],
            out_specs=[blk] * 4)
        return pl.pallas_call(body, name=name, grid_spec=gs, out_shape=out_shape, compiler_params=_params(("parallel",)))(
            me.reshape(1).astype(jnp.int32), w, slabs, m, v, part)
    blk = pl.BlockSpec((tr, C), lambda i: (i, 0))
    return pl.pallas_call(
        body, name=name, grid=(R // tr,), in_specs=[blk, pl.BlockSpec((N_DEV, tr, C), lambda i: (0, i, 0)), blk, blk],
        out_specs=[blk] * 4, out_shape=out_shape, compiler_params=_params(("parallel",)))(w, slabs, m, v)


def _packed_rows(n, cols):
    rows = -(-n // cols)
    mult = 256 if rows > 256 else 16
    return -(-rows // mult) * mult


def _pack(arrs, cols, dtype, lead=()):
    nl = len(lead)
    flat = jnp.concatenate([a.reshape(lead + (-1,)).astype(dtype) for a in arrs], axis=-1)
    n = flat.shape[-1]
    rows = _packed_rows(n, cols)
    flat = jnp.pad(flat, [(0, 0)] * nl + [(0, rows * cols - n)])
    return flat.reshape(lead + (rows, cols))


def _unpack(packed, shapes, lead=()):
    flat = packed.reshape(lead + (-1,))
    out, off = [], 0
    for s in shapes:
        n = int(np.prod(s))
        out.append(flat[..., off:off + n].reshape(lead + tuple(s)))
        off += n
    return out


def _to_slabs(full, axis):
    s = full.shape
    r = full.reshape(s[:axis] + (N_DEV, s[axis] // N_DEV) + s[axis + 1:])
    return jnp.moveaxis(r, axis, 0)


def _from_slabs(slabs, axis):
    r = jnp.moveaxis(slabs, 0, axis)
    s = r.shape
    return r.reshape(s[:axis] + (s[axis] * s[axis + 1],) + s[axis + 2:])


BIG = [("ev_w_in", 2), ("ev_w_out", 1), ("od_w_in", 2), ("od_w_out", 1), ("ple_w_gate", 1), ("ple_w_proj", 2)]
SMALL = [("ev_conv_w", 2), ("od_norm_w", 1), ("od_dw_w", 2), ("od_dw_b", 1), ("od_ln_w", 1), ("od_ln_b", 1)]
REPL = ["ev_norm_w", "ev_conv_b", "ev_dt_bias_f", "ev_dt_bias_b", "ev_a_log_f", "ev_a_log_b", "ev_d_skip", "ev_gnorm_w", "ev_rpb",
        "ple_norm_w", "final_norm_w"]
WEIGHTS = ["ev_norm_w", "ev_w_in", "ev_conv_w", "ev_conv_b", "ev_dt_bias_f", "ev_dt_bias_b", "ev_a_log_f", "ev_a_log_b", "ev_d_skip",
           "ev_gnorm_w", "ev_rpb", "ev_w_out", "od_norm_w", "od_w_in", "od_dw_w", "od_dw_b", "od_ln_w", "od_ln_b", "od_w_out",
           "ple_norm_w", "ple_w_gate", "ple_w_proj", "final_norm_w"]
BIG_COLS = 1024
SMALL_COLS = 128


def _row_tile(L, width):
    return _pick(L, (256, 128)) if width <= 2048 else _pick(L, (128,))


def _ple_fwd(h1, p_i, nw, wg, wp, tag):
    L, D = h1.shape
    T = _row_tile(L, D)
    (hn2,) = rowop_fwd(_rms_fn, "ple_rms_" + tag, [(h1, D, 0)], [(nw, D, 0)], [(D, bf16)], T)
    gpre = mm(hn2, wg, "nn", "ple_gate_mm_" + tag)
    pp = mm(p_i, wp, "nn", "ple_proj_mm_" + tag)
    (h2,) = rowop_fwd(_ple_fwd_fn, "ple_comb_" + tag, [(h1, D, 0), (gpre, D, 0), (pp, D, 0)], [], [(D, f32)], T)
    return h2, (h1, p_i, nw, wg, wp, hn2, gpre, pp)


def _ple_bwd(res, dh2, tag):
    h1, p_i, nw, wg, wp, hn2, gpre, pp = res
    L, D = h1.shape
    T = _row_tile(L, D)
    dgpre, dpp = rowop_bwd(_ple_fn, "ple_comb_bwd_" + tag, [(gpre, D, 0), (pp, D, 0)], [], [(dh2, D, 0)], [bf16, bf16], [], T)
    dwg = mm(hn2, dgpre, "tn", "ple_gate_dw_" + tag)
    dwp = mm(p_i, dpp, "tn", "ple_proj_dw_" + tag)
    dhn2 = mm(dgpre, wg, "nt", "ple_gate_dx_" + tag)
    dh1, dnw = rowop_bwd(_rms_fn, "ple_rms_bwd_" + tag, [(h1, D, 0)], [(nw, D, 0)], [(dhn2, D, 0)], [f32], [True], T, add=(dh2, D, 0))
    return dh1, dnw, dwg, dwp


def _even_fwd(h, P, tag, side=None):
    L, D = h.shape
    H = D // HEAD_P
    T = _row_tile(L, D)
    DB = D // LANES
    (hn,) = rowop_fwd(_rms_fn, "ev_rms_" + tag, [(h, D, 0)], [(P["norm_w"], D, 0)], [(D, bf16)], T)
    um = mm(hn, P["w_main"], "nn", "ev_in_mm_" + tag, side=side)
    um, exchanged = um if side is not None else (um, None)
    udt = mm(hn, P["w_dt"], "nn", "ev_dt_mm_" + tag)
    pre, act = conv_fwd(um, DB, 2 * D, P["conv_w"], P["conv_b"], K_SSM, True, "ev_conv_" + tag)
    prep = _make_prep_fn(H)
    dt128, cs128 = rowop_fwd(prep, "ev_prep_" + tag, [(udt, LANES, 0)], [(P["bias128"], LANES, 0), (P["alog128"], LANES, 0)],
                             [(LANES, f32), (LANES, f32)], CHUNK)
    csr = _csr_from_cs(cs128, H)
    yf, sf = ssd_fwd(act, dt128, cs128, csr, 0, H, "ev_ssd_f_" + tag)
    yb, sb = ssd_fwd(act, dt128, cs128, csr, 1, H, "ev_ssd_b_" + tag)
    G = D // GROUP_W
    (yssd,) = rowop_fwd(_post_fn, "ev_post_" + tag, [(yf, GROUP_W, 0), (yb, GROUP_W, 0), (act, GROUP_W, 0), (um, GROUP_W, 0)],
                        [(P["dskipx"], GROUP_W, 0), (P["gnorm_w"], GROUP_W, 0)], [(GROUP_W, bf16)], _pick(L, POST_ROW_TILES), ncb=G)
    bt = na_bias_table(P["rpb"])
    o = na_fwd(um, 3 * DB, 4 * DB, 5 * DB, bt, "ev_na_" + tag)
    (yna,) = rowop_fwd(_gate_fn, "ev_nagate_" + tag, [(o, D, 0), (um, D, 6)], [], [(D, bf16)], T)
    cat = jnp.concatenate([yssd, yna], axis=-1)
    if P["w_out"] is None:
        P["w_out"] = _from_slabs(exchanged[0], 0)
    h1 = mm(cat, P["w_out"], "nn", "ev_out_mm_" + tag, add=h)
    return h1, (h, hn, um, udt, pre, act, dt128, cs128, csr, yf, yb, sf, sb, bt, o, cat), exchanged


def _even_bwd(P, res, dh1, tag, side=None, early=None):
    h, hn, um, udt, pre, act, dt128, cs128, csr, yf, yb, sf, sb, bt, o, cat = res
    L, D = h.shape
    H = D // HEAD_P
    G = D // GROUP_W
    T = _row_tile(L, D)
    DB = D // LANES
    g = {}
    dcat = mm(dh1, P["w_out"], "nt", "ev_out_dx_" + tag)
    g["w_out"] = mm(cat, dh1, "tn", "ev_out_dw_" + tag)
    do, dg = rowop_bwd(_gate_fn, "ev_nagate_bwd_" + tag, [(o, D, 0), (um, D, 6)], [], [(dcat, D, 1)], [f32, bf16], [], T)
    n_side = len(side[1]) - 1 if side is not None else 0
    na_arrays = list(side[1][1:]) if side is not None else []
    if early is not None:
        na_arrays += [_to_slabs(gr.astype(bf16), ax) for gr, ax in zip([g["w_out"]] + early[0], [0] + early[1])]
    dq, dk, dv, dbt, *na_recv = na_bwd(um, 3 * DB, 4 * DB, 5 * DB, bt, do, "ev_na_bwd_" + tag,
                                       side=("a2a", na_arrays) if na_arrays else None)
    na_recv = list(na_recv[0]) if na_recv else []
    _, rpb_vjp = jax.vjp(na_bias_table, P["rpb"])
    (g["rpb"],) = rpb_vjp(dbt)
    dy, dxs_a, dz, ddsk, dgn = rowop_bwd(
        _post_fn, "ev_post_bwd_" + tag, [(yf, GROUP_W, 0), (yb, GROUP_W, 0), (act, GROUP_W, 0), (um, GROUP_W, 0)],
        [(P["dskipx"], GROUP_W, 0), (P["gnorm_w"], GROUP_W, 0)], [(dcat, GROUP_W, 0)], [f32, None, f32, bf16], [True, True],
        _pick(L, POST_ROW_TILES), ncb=G)
    g["d_skip"] = ddsk.reshape(H, HEAD_P).sum(axis=-1)
    g["gnorm_w"] = dgn.reshape(D)
    dxf, dbf, dcf, ddtf, dcsf, dcsrf = ssd_bwd(act, dt128, cs128, csr, sf, dy, 0, H, "ev_ssd_f_bwd_" + tag)
    dxb, dbb, dcb, ddtb, dcsb, dcsrb = ssd_bwd(act, dt128, cs128, csr, sb, dy, 1, H, "ev_ssd_b_bwd_" + tag)
    dact = jnp.concatenate([dxf + dxb + dxs_a, dbf + dbb, dcf + dcb], axis=-1)
    ddt128 = ddtf + ddtb
    dcs128 = dcsf + dcsb + _cs_from_dcsr(dcsrf, dcsrb, H)
    prep = _make_prep_fn(H)
    dudt, dbias, dalog = rowop_bwd(prep, "ev_prep_bwd_" + tag, [(udt, LANES, 0)], [(P["bias128"], LANES, 0), (P["alog128"], LANES, 0)],
                                   [(ddt128, LANES, 0), (dcs128, LANES, 0)], [bf16], [True, True], CHUNK)
    g["dt_bias_f"], g["dt_bias_b"] = dbias[0, :H], dbias[0, H:2 * H]
    g["a_log_f"], g["a_log_b"] = dalog[0, :H], dalog[0, H:2 * H]
    dxbc, g["conv_w"], dcb_ = conv_bwd(dact, um, DB, 2 * D, P["conv_w"], K_SSM, pre, bf16, "ev_conv_bwd_" + tag)
    g["conv_b"] = dcb_.reshape(2 * D)
    dum = jnp.concatenate([dz, dxbc, dq.astype(bf16), dk.astype(bf16), dv.astype(bf16), dg], axis=-1)
    first_side = (side[0], side[1][:1]) if side is not None else None
    first, own_recv = [], None
    if early is None:
        dhn = mm(dum, P["w_main"], "nt", "ev_in_dx_" + tag, side=first_side)
        dhn, first = dhn if side is not None else (dhn, [])
        g["w_main"] = mm(hn, dum, "tn", "ev_in_dw_" + tag)
        g["w_dt"] = mm(hn, dudt, "tn", "ev_dt_dw_" + tag)
    else:
        gw = mm(hn, dum, "tn", "ev_in_dw_" + tag, side=first_side)
        g["w_main"], first = gw if side is not None else (gw, [])
        g["w_dt"] = mm(hn, dudt, "tn", "ev_dt_dw_" + tag)
        own_slabs = [_to_slabs(_ev_w_in_grad(g, D, H).astype(bf16), 1)]
        dhn, own_recv = mm(dum, P["w_main"], "nt", "ev_in_dx_" + tag, side=("a2a", own_slabs))
    dhn = mm(dudt, P["w_dt"], "nt", "ev_dt_dx_" + tag, add=dhn)
    dh, dnw = rowop_bwd(_rms_fn, "ev_rms_bwd_" + tag, [(h, D, 0)], [(P["norm_w"], D, 0)], [(dhn, D, 0)], [f32], [True], T, add=(dh1, D, 0))
    g["norm_w"] = dnw.reshape(D)
    exchanged = (list(first) + na_recv[:n_side]) if side is not None else None
    early_recv = na_recv[n_side:] if early is not None else None
    return dh, g, exchanged, early_recv, own_recv


def _ev_w_in_grad(g, D, H):
    return jnp.concatenate([g["w_main"][:, :3 * D], g["w_dt"][:, :2 * H], g["w_main"][:, 3 * D:]], axis=1)


def _odd_fwd(h, P, tag, side=None):
    L, D = h.shape
    C = 2 * D
    T = _row_tile(L, D)
    (hn,) = rowop_fwd(_rms_fn, "od_rms_" + tag, [(h, D, 0)], [(P["norm_w"], D, 0)], [(D, bf16)], T)
    u = mm(hn, P["w_in"], "nn", "od_in_mm_" + tag, side=side)
    u, exchanged = u if side is not None else (u, None)
    (vglu,) = rowop_fwd(_glu_fn, "od_glu_" + tag, [(u, D, 0), (u, D, 2)], [], [(D, f32)], T, ncb=2)
    (vc,) = conv_fwd(vglu, 0, C, P["dw_w"], P["dw_b"], K_CONV, False, "od_conv_" + tag)
    TL = _row_tile(L, C)
    (t,) = rowop_fwd(_lngate_fn, "od_lngate_" + tag, [(vc, C, 0), (u, C, 2)], [(P["ln_w"], C, 0), (P["ln_b"], C, 0)], [(C, bf16)], TL)
    h1 = mm(t, P["w_out"], "nn", "od_out_mm_" + tag, add=h)
    return h1, (h, hn, u, vglu, vc, t), exchanged


def _odd_bwd(P, res, dh1, tag, side=None):
    h, hn, u, vglu, vc, t = res
    L, D = h.shape
    C = 2 * D
    T = _row_tile(L, D)
    TL = _row_tile(L, C)
    g = {}
    dt_ = mm(dh1, P["w_out"], "nt", "od_out_dx_" + tag)
    g["w_out"] = mm(t, dh1, "tn", "od_out_dw_" + tag)
    dvc, dg, dlnw, dlnb = rowop_bwd(_lngate_fn, "od_lngate_bwd_" + tag, [(vc, C, 0), (u, C, 2)], [(P["ln_w"], C, 0), (P["ln_b"], C, 0)],
                                    [(dt_, C, 0)], [f32, bf16], [True, True], TL)
    g["ln_w"], g["ln_b"] = dlnw.reshape(C), dlnb.reshape(C)
    dvglu, g["dw_w"], ddwb = conv_bwd(dvc, vglu, 0, C, P["dw_w"], K_CONV, None, f32, "od_conv_bwd_" + tag)
    g["dw_b"] = ddwb.reshape(C)
    da, dag = rowop_bwd(_glu_fn, "od_glu_bwd_" + tag, [(u, D, 0), (u, D, 2)], [], [(dvglu, D, 0)], [bf16, bf16], [], T, ncb=2)
    du = jnp.concatenate([da, dag, dg], axis=-1)
    exchanged = None
    if side is None:
        dhn = mm(du, P["w_in"], "nt", "od_in_dx_" + tag)
        g["w_in"] = mm(hn, du, "tn", "od_in_dw_" + tag)
    else:
        dhn, ex_a = mm(du, P["w_in"], "nt", "od_in_dx_" + tag, side=(side[0], side[1][:1]))
        g["w_in"], ex_b = mm(hn, du, "tn", "od_in_dw_" + tag, side=(side[0], side[1][1:]))
        exchanged = list(ex_a) + list(ex_b)
    dh, dnw = rowop_bwd(_rms_fn, "od_rms_bwd_" + tag, [(h, D, 0)], [(P["norm_w"], D, 0)], [(dhn, D, 0)], [f32], [True], T, add=(dh1, D, 0))
    g["norm_w"] = dnw.reshape(D)
    return dh, g, exchanged


def kernel(x, p, ev_norm_w, ev_w_in, ev_conv_w, ev_conv_b, ev_dt_bias_f, ev_dt_bias_b, ev_a_log_f, ev_a_log_b, ev_d_skip, ev_gnorm_w, ev_rpb, ev_w_out, od_norm_w, od_w_in, od_dw_w, od_dw_b, od_ln_w, od_ln_b, od_w_out, ple_norm_w, ple_w_gate, ple_w_proj, final_norm_w, loss_target, m_ev_norm_w, m_ev_w_in, m_ev_conv_w, m_ev_conv_b, m_ev_dt_bias_f, m_ev_dt_bias_b, m_ev_a_log_f, m_ev_a_log_b, m_ev_d_skip, m_ev_gnorm_w, m_ev_rpb, m_ev_w_out, m_od_norm_w, m_od_w_in, m_od_dw_w, m_od_dw_b, m_od_ln_w, m_od_ln_b, m_od_w_out, m_ple_norm_w, m_ple_w_gate, m_ple_w_proj, m_final_norm_w, v_ev_norm_w, v_ev_w_in, v_ev_conv_w, v_ev_conv_b, v_ev_dt_bias_f, v_ev_dt_bias_b, v_ev_a_log_f, v_ev_a_log_b, v_ev_d_skip, v_ev_gnorm_w, v_ev_rpb, v_ev_w_out, v_od_norm_w, v_od_w_in, v_od_dw_w, v_od_dw_b, v_od_ln_w, v_od_ln_b, v_od_w_out, v_ple_norm_w, v_ple_w_gate, v_ple_w_proj, v_final_norm_w):
    W = dict(ev_norm_w=ev_norm_w, ev_w_in=ev_w_in, ev_conv_w=ev_conv_w, ev_conv_b=ev_conv_b, ev_dt_bias_f=ev_dt_bias_f,
             ev_dt_bias_b=ev_dt_bias_b, ev_a_log_f=ev_a_log_f, ev_a_log_b=ev_a_log_b, ev_d_skip=ev_d_skip, ev_gnorm_w=ev_gnorm_w,
             ev_rpb=ev_rpb, ev_w_out=ev_w_out, od_norm_w=od_norm_w, od_w_in=od_w_in, od_dw_w=od_dw_w, od_dw_b=od_dw_b,
             od_ln_w=od_ln_w, od_ln_b=od_ln_b, od_w_out=od_w_out, ple_norm_w=ple_norm_w, ple_w_gate=ple_w_gate,
             ple_w_proj=ple_w_proj, final_norm_w=final_norm_w)
    M = dict(ev_norm_w=m_ev_norm_w, ev_w_in=m_ev_w_in, ev_conv_w=m_ev_conv_w, ev_conv_b=m_ev_conv_b, ev_dt_bias_f=m_ev_dt_bias_f,
             ev_dt_bias_b=m_ev_dt_bias_b, ev_a_log_f=m_ev_a_log_f, ev_a_log_b=m_ev_a_log_b, ev_d_skip=m_ev_d_skip,
             ev_gnorm_w=m_ev_gnorm_w, ev_rpb=m_ev_rpb, ev_w_out=m_ev_w_out, od_norm_w=m_od_norm_w, od_w_in=m_od_w_in,
             od_dw_w=m_od_dw_w, od_dw_b=m_od_dw_b, od_ln_w=m_od_ln_w, od_ln_b=m_od_ln_b, od_w_out=m_od_w_out,
             ple_norm_w=m_ple_norm_w, ple_w_gate=m_ple_w_gate, ple_w_proj=m_ple_w_proj, final_norm_w=m_final_norm_w)
    V = dict(ev_norm_w=v_ev_norm_w, ev_w_in=v_ev_w_in, ev_conv_w=v_ev_conv_w, ev_conv_b=v_ev_conv_b, ev_dt_bias_f=v_ev_dt_bias_f,
             ev_dt_bias_b=v_ev_dt_bias_b, ev_a_log_f=v_ev_a_log_f, ev_a_log_b=v_ev_a_log_b, ev_d_skip=v_ev_d_skip,
             ev_gnorm_w=v_ev_gnorm_w, ev_rpb=v_ev_rpb, ev_w_out=v_ev_w_out, od_norm_w=v_od_norm_w, od_w_in=v_od_w_in,
             od_dw_w=v_od_dw_w, od_dw_b=v_od_dw_b, od_ln_w=v_od_ln_w, od_ln_b=v_od_ln_b, od_w_out=v_od_w_out,
             ple_norm_w=v_ple_norm_w, ple_w_gate=v_ple_w_gate, ple_w_proj=v_ple_w_proj, final_norm_w=v_final_norm_w)

    h0 = x[0]
    L, D = h0.shape
    H = D // HEAD_P
    depth = p.shape[0]
    n_even, n_odd = (depth + 1) // 2, depth // 2
    me = 4 * lax.axis_index("x") + 2 * lax.axis_index("y") + lax.axis_index("c")

    small_shapes = [W[n].shape for n, _ in SMALL]
    (small_g,) = all_gather([_pack([W[n] for n, _ in SMALL], SMALL_COLS, f32)], "gather_small")
    F = {n: _from_slabs(a, ax) for (n, ax), a in zip(SMALL, _unpack(small_g, small_shapes, (N_DEV,)))}

    def layer_big(i):
        kind = "ev" if i % 2 == 0 else "od"
        return [(kind + "_w_in", i // 2, 1), (kind + "_w_out", i // 2, 0), ("ple_w_gate", i, 0), ("ple_w_proj", i, 1)]

    def layer_shards(i):
        return [W[n][k].astype(bf16) for n, k, _ in layer_big(i)]

    def even_params(e, w_in, w_out):
        return dict(
            norm_w=W["ev_norm_w"][e][None], w_main=jnp.concatenate([w_in[:, :3 * D], w_in[:, 3 * D + 2 * H:]], axis=1),
            w_dt=jnp.pad(w_in[:, 3 * D:3 * D + 2 * H], ((0, 0), (0, LANES - 2 * H))),
            conv_w=F["ev_conv_w"][e], conv_b=W["ev_conv_b"][e][None],
            bias128=jnp.pad(jnp.concatenate([W["ev_dt_bias_f"][e], W["ev_dt_bias_b"][e]]), (0, LANES - 2 * H))[None],
            alog128=jnp.pad(jnp.concatenate([W["ev_a_log_f"][e], W["ev_a_log_b"][e]]), (0, LANES - 2 * H))[None],
            dskipx=jnp.repeat(W["ev_d_skip"][e], HEAD_P)[None], gnorm_w=W["ev_gnorm_w"][e][None], rpb=W["ev_rpb"][e],
            w_out=w_out)

    def odd_params(e, w_in, w_out):
        return dict(norm_w=F["od_norm_w"][e][None], w_in=w_in, dw_w=F["od_dw_w"][e], dw_b=F["od_dw_b"][e][None],
                    ln_w=F["od_ln_w"][e][None], ln_b=F["od_ln_b"][e][None], w_out=w_out)

    h = h0
    saved = []
    gathered = all_gather(layer_shards(0)[:1], "gather_layer0")
    for i in range(depth):
        e = i // 2
        tag = str(i)
        nxt = layer_shards(i + 1) if i + 1 < depth else []
        if i == 0:
            w_in = _from_slabs(gathered[0], layer_big(0)[0][2])
            P = even_params(e, w_in, None)
            h1, res, exchanged = _even_fwd(h, P, tag, ("gather", layer_shards(0)[1:] + nxt))
            w_gate, w_proj = [_from_slabs(a, ax) for a, (_, _, ax) in zip(exchanged[1:3], layer_big(0)[2:])]
            gathered = exchanged[3:]
        else:
            w_in, w_out, w_gate, w_proj = [_from_slabs(a, ax) for a, (_, _, ax) in zip(gathered, layer_big(i))]
            side = ("gather", nxt) if nxt else None
            if i % 2 == 0:
                P = even_params(e, w_in, w_out)
                h1, res, gathered = _even_fwd(h, P, tag, side)
            else:
                P = odd_params(e, w_in, w_out)
                h1, res, gathered = _odd_fwd(h, P, tag, side)
        h, pres = _ple_fwd(h1, p[i, 0], W["ple_norm_w"][i][None], w_gate, w_proj, tag)
        saved.append((P, res, pres))
    loss_tile, dh, dfinal = final_loss(h, W["final_norm_w"][None], loss_target[0])

    ev_g = [None] * n_even
    od_g = [None] * n_odd
    ple_g = [None] * depth
    recv = [None] * depth
    own = [None] * depth
    pending = None
    for i in reversed(range(depth)):
        P, res, pres = saved[i]
        tag = str(i)
        dh1, dnw, dwg, dwp = _ple_bwd(pres, dh, tag)
        ple_g[i] = dnw.reshape(D)
        side = ("a2a", pending) if pending is not None else None
        if i % 2 == 0:
            early = ([dwg, dwp], [ax for _, _, ax in layer_big(0)[2:]]) if i == 0 else None
            dh, g, exchanged, early_recv, own_recv = _even_bwd(P, res, dh1, tag, side, early)
            ev_g[i // 2] = g
            big_grads = [_ev_w_in_grad(g, D, H), g["w_out"], dwg, dwp]
        else:
            dh, g, exchanged = _odd_bwd(P, res, dh1, tag, side)
            od_g[i // 2] = g
            big_grads = [g["w_in"], g["w_out"], dwg, dwp]
        if pending is not None:
            recv[i + 1] = exchanged
        pending = [_to_slabs(gr.astype(bf16), ax) for gr, (_, _, ax) in zip(big_grads, layer_big(i))]
        own[i] = [lax.dynamic_slice_in_dim(gr, me * (gr.shape[ax] // N_DEV), gr.shape[ax] // N_DEV, axis=ax)
                  for gr, (_, _, ax) in zip(big_grads, layer_big(i))]
    recv[0] = list(own_recv) + list(early_recv)
    grad_x = dh[None]

    full = dict(
        ev_conv_w=jnp.stack([g["conv_w"] for g in ev_g]), od_norm_w=jnp.stack([g["norm_w"] for g in od_g]),
        od_dw_w=jnp.stack([g["dw_w"] for g in od_g]), od_dw_b=jnp.stack([g["dw_b"] for g in od_g]),
        od_ln_w=jnp.stack([g["ln_w"] for g in od_g]), od_ln_b=jnp.stack([g["ln_b"] for g in od_g]),
        ev_norm_w=jnp.stack([g["norm_w"] for g in ev_g]), ev_conv_b=jnp.stack([g["conv_b"] for g in ev_g]),
        ev_dt_bias_f=jnp.stack([g["dt_bias_f"] for g in ev_g]), ev_dt_bias_b=jnp.stack([g["dt_bias_b"] for g in ev_g]),
        ev_a_log_f=jnp.stack([g["a_log_f"] for g in ev_g]), ev_a_log_b=jnp.stack([g["a_log_b"] for g in ev_g]),
        ev_d_skip=jnp.stack([g["d_skip"] for g in ev_g]), ev_gnorm_w=jnp.stack([g["gnorm_w"] for g in ev_g]),
        ev_rpb=jnp.stack([g["rpb"] for g in ev_g]), ple_norm_w=jnp.stack(ple_g), final_norm_w=dfinal.reshape(D))

    out = {}
    where = {}
    for i in range(depth):
        for slot, (n, k, _) in enumerate(layer_big(i)):
            where.setdefault(n, {})[k] = (i, slot)
    for n, _ in BIG:
        shard = W[n].shape
        two_d = (int(np.prod(shard[:-1])), shard[-1])
        layers = [where[n][k] for k in range(shard[0])]
        slabs = jnp.stack([recv[i][slot] for i, slot in layers], axis=1)
        part = jnp.stack([own[i][slot] for i, slot in layers], axis=0)
        res = adamw(W[n].reshape(two_d), slabs.reshape((N_DEV,) + two_d), M[n].reshape(two_d), V[n].reshape(two_d),
                    (me, part.reshape(two_d)), "adamw_" + n)
        for kind, arr in zip(("grad", "delta", "new_m", "new_v"), res):
            out[kind + "_" + n] = arr.reshape(shard)
    small_full = _pack([_to_slabs(full[n], ax) for n, ax in SMALL], SMALL_COLS, f32, (N_DEV,))
    (small_recv,) = all_to_all([small_full], "scatter_small")
    res_small = adamw(_pack([W[n] for n, _ in SMALL], SMALL_COLS, f32), small_recv, _pack([M[n] for n, _ in SMALL], SMALL_COLS, f32),
                      _pack([V[n] for n, _ in SMALL], SMALL_COLS, f32), None, "adamw_small")
    for kind, arr in zip(("grad", "delta", "new_m", "new_v"), res_small):
        for (n, _), a in zip(SMALL, _unpack(arr, small_shapes)):
            out[kind + "_" + n] = a
    repl_shapes = [W[n].shape for n in REPL] + [(1,)]
    zero1 = jnp.zeros((1,), f32)
    repl_part = _pack([full[n] for n in REPL] + [loss_tile[0, :1]], SMALL_COLS, f32)
    (repl_all,) = all_gather([repl_part], "gather_repl")
    res_repl = adamw(_pack([W[n] for n in REPL] + [zero1], SMALL_COLS, f32), repl_all, _pack([M[n] for n in REPL] + [zero1], SMALL_COLS, f32),
                     _pack([V[n] for n in REPL] + [zero1], SMALL_COLS, f32), None, "adamw_repl")
    for kind, arr in zip(("grad", "delta", "new_m", "new_v"), res_repl):
        parts = _unpack(arr, repl_shapes)
        for n, a in zip(REPL, parts[:-1]):
            out[kind + "_" + n] = a
        if kind == "grad":
            loss = parts[-1].reshape(())

    return (loss, grad_x, *[out["grad_" + n] for n in WEIGHTS], *[out["delta_" + n] for n in WEIGHTS],
            *[out["new_m_" + n] for n in WEIGHTS], *[out["new_v_" + n] for n in WEIGHTS])
```

```python
import functools

import numpy as np
import jax
import jax.numpy as jnp
from jax import lax
from jax.experimental import pallas as pl
from jax.experimental.pallas import tpu as pltpu

f32 = jnp.float32
bf16 = jnp.bfloat16
HIGHEST = lax.Precision.HIGHEST

N_DEV = 8
EPS = 1e-6
GRID_W = 64
HEAD_P = 64
HEADS_PER_GROUP = 4
GROUP_W = HEAD_P * HEADS_PER_GROUP
N_STATE = 128
CHUNK = 128
K_SSM = 5
K_CONV = 31
NA_D = 128
NA_WR = 8
NA_WC = 16
PLE_DIM = 256
LANES = 128
CONV_PAD = 16
VMEM_LIMIT = 56 * 1024 * 1024
MM_VMEM_BUDGET = 40 * 1024 * 1024
MM_FULL_K = 2048
NA_ROWS_PER_STEP = 16
ADAMW_TILE_BYTES = 1024 * 1024
SSD_GROUPS_PER_STEP = 8
POST_ROW_TILES = (1024, 512, 256, 128)

ADAM_LR = 0.001
ADAM_B1 = 0.9
ADAM_B2 = 0.999
ADAM_EPS = 1e-08
ADAM_WD = 0.01
ADAM_STEP = 10

NEG_BIG = -1e30


def _params(sem=None):
    if sem is None:
        return pltpu.CompilerParams(vmem_limit_bytes=VMEM_LIMIT)
    return pltpu.CompilerParams(vmem_limit_bytes=VMEM_LIMIT, dimension_semantics=sem)


def _pick(n, prefs):
    for t in prefs:
        if n % t == 0:
            return t
    return n


def _sigmoid(x):
    return jax.nn.sigmoid(x)


def _silu(x):
    return x * jax.nn.sigmoid(x)


_DN = {"nn": (((1,), (0,)), ((), ())), "nt": (((1,), (1,)), ((), ())), "tn": (((0,), (0,)), ((), ()))}


def mm(a, b, mode, name, out_dtype=f32, add=None, side=None):
    if mode == "nn":
        (M, K), (K2, N) = a.shape, b.shape
    elif mode == "nt":
        (M, K), (N, K2) = a.shape, b.shape
    else:
        (K, M), (K2, N) = a.shape, b.shape
    assert K == K2, (a.shape, b.shape, mode)
    sa, sb, so = a.dtype.itemsize, b.dtype.itemsize, jnp.dtype(out_dtype).itemsize
    tk = K if K <= MM_FULL_K else _pick(K, (1024, 512, 256, 128))
    nk = K // tk
    tms = [t for t in (1024, 512, 256, 128) if M % t == 0] or [M]
    tns = [t for t in (2048, 1024, 512, 256, 128) if N % t == 0] or [N]

    def vmem_bytes(tm, tn):
        n = 2 * (tm * tk * sa + tk * tn * sb) + 2 * tm * tn * so
        n += tm * tn * 4 if nk > 1 else 0
        n += 2 * tm * tn * add.dtype.itemsize if add is not None else 0
        return n

    ti, tj = 0, 0
    while vmem_bytes(tms[ti], tns[tj]) > MM_VMEM_BUDGET:
        if tj + 1 < len(tns) and (tns[tj] >= tms[ti] or ti + 1 >= len(tms)):
            tj += 1
        elif ti + 1 < len(tms):
            ti += 1
        else:
            break
    tm, tn = tms[ti], tns[tj]
    a_bytes, b_bytes = M * K * sa, K * N * sb
    i_outer = a_bytes + b_bytes * (M // tm) <= b_bytes + a_bytes * (N // tn)
    dn = _DN[mode]

    grid = (M // tm, N // tn, nk) if i_outer else (N // tn, M // tm, nk)
    ns = len(side[1]) if side is not None else 0
    n_add = 1 if add is not None else 0

    def body(a_ref, b_ref, *rest):
        add_ref = rest[0] if add is not None else None
        x_refs = rest[n_add:n_add + ns]
        o_ref = rest[n_add + ns]
        out_refs = rest[n_add + ns + 1:n_add + 2 * ns + 1]
        scratch = rest[n_add + 2 * ns + 1:]
        if side is not None:
            step = (pl.program_id(0) * grid[1] + pl.program_id(1)) * nk + pl.program_id(2)

            @pl.when(step == 0)
            def _():
                _exchange_start(side[0], x_refs, out_refs, scratch[-3:])

        def finish(r):
            if add is not None:
                r = r + add_ref[...].astype(f32)
            o_ref[...] = r.astype(out_dtype)

        part = lax.dot_general(a_ref[...].astype(bf16), b_ref[...].astype(bf16), dn, preferred_element_type=f32)
        if nk == 1:
            finish(part)
        else:
            acc = scratch[0]
            k = pl.program_id(2)

            @pl.when(k == 0)
            def _():
                acc[...] = part

            @pl.when(k > 0)
            def _():
                acc[...] += part

            @pl.when(k == nk - 1)
            def _():
                finish(acc[...])

        if side is not None:
            @pl.when(step == grid[0] * grid[1] * nk - 1)
            def _():
                _exchange_finish(side[0], x_refs, out_refs, scratch[-3:])

    def ij(p, q):
        return (p, q) if i_outer else (q, p)

    if mode == "tn":
        a_spec = pl.BlockSpec((tk, tm), lambda p, q, k: (k, ij(p, q)[0]))
    else:
        a_spec = pl.BlockSpec((tm, tk), lambda p, q, k: (ij(p, q)[0], k))
    if mode == "nt":
        b_spec = pl.BlockSpec((tn, tk), lambda p, q, k: (ij(p, q)[1], k))
    else:
        b_spec = pl.BlockSpec((tk, tn), lambda p, q, k: (k, ij(p, q)[1]))
    o_spec = pl.BlockSpec((tm, tn), lambda p, q, k: ij(p, q))
    in_specs = [a_spec, b_spec] + ([o_spec] if add is not None else [])
    args = (a, b) + ((add,) if add is not None else ())
    acc_scratch = [pltpu.VMEM((tm, tn), f32)] if nk > 1 else []
    o_shape = jax.ShapeDtypeStruct((M, N), out_dtype)
    if side is None:
        return pl.pallas_call(
            body, name=name, grid=grid, in_specs=in_specs, out_specs=o_spec, out_shape=o_shape, scratch_shapes=acc_scratch,
            compiler_params=_params(("parallel", "parallel", "arbitrary")))(*args)
    res = pl.pallas_call(
        body, name=name, grid=grid, in_specs=in_specs + [_HBM] * ns, out_specs=[o_spec] + [_HBM] * ns,
        out_shape=[o_shape] + _exchange_out_shapes(side[0], side[1]), scratch_shapes=acc_scratch + _exchange_scratch(ns),
        compiler_params=_params(("arbitrary", "arbitrary", "arbitrary")))(*args, *side[1])
    return res[0], res[1:]


def _row_spec(T, w, base):
    return pl.BlockSpec((T, w), lambda g, i: (i, base + g))


def _const_spec(r, w, base):
    return pl.BlockSpec((r, w), lambda g, i: (0, base + g))


def rowop_fwd(fn, name, rows, consts, outs, T, ncb=1):
    L = rows[0][0].shape[0]
    nr, nc = len(rows), len(consts)

    def body(*refs):
        ins = [r[...].astype(f32) for r in refs[:nr + nc]]
        res = fn(*ins)
        for o_ref, r in zip(refs[nr + nc:], res):
            o_ref[...] = r.astype(o_ref.dtype)

    in_specs = [_row_spec(T, w, b) for (_, w, b) in rows] + [_const_spec(a.shape[0], w, b) for (a, w, b) in consts]
    out_specs = [_row_spec(T, w, 0) for (w, _) in outs]
    out_shape = [jax.ShapeDtypeStruct((L, ncb * w), dt) for (w, dt) in outs]
    return pl.pallas_call(
        body, name=name, grid=(ncb, L // T), in_specs=in_specs, out_specs=out_specs, out_shape=out_shape,
        compiler_params=_params(("parallel", "parallel")))(*[a for (a, _, _) in rows], *[a for (a, _, _) in consts])


def rowop_bwd(fn, name, rows, consts, cts, row_grads, const_grads, T, ncb=1, add=None):
    L = rows[0][0].shape[0]
    nr, nc, nct = len(rows), len(consts), len(cts)
    n_in = nr + nc + nct + (1 if add is not None else 0)
    rg_idx = [i for i, d in enumerate(row_grads) if d is not None]
    cg_idx = [i for i, d in enumerate(const_grads) if d]

    def body(*refs):
        ins = [r[...].astype(f32) for r in refs[:nr + nc]]
        ct = tuple(r[...].astype(f32) for r in refs[nr + nc:nr + nc + nct])
        _, vjp = jax.vjp(fn, *ins)
        grads = vjp(ct)
        outs = refs[n_in:]
        for n, i in enumerate(rg_idx):
            gi = grads[i]
            if add is not None and n == 0:
                gi = gi + refs[n_in - 1][...].astype(f32)
            outs[n][...] = gi.astype(outs[n].dtype)
        first = pl.program_id(1) == 0
        for n, i in enumerate(cg_idx):
            o = outs[len(rg_idx) + n]

            @pl.when(first)
            def _(o=o):
                o[...] = jnp.zeros_like(o)

            o[...] += grads[nr + i]

    in_specs = ([_row_spec(T, w, b) for (_, w, b) in rows] + [_const_spec(a.shape[0], w, b) for (a, w, b) in consts]
                + [_row_spec(T, w, b) for (_, w, b) in cts] + ([_row_spec(T, add[1], add[2])] if add is not None else []))
    out_specs = ([_row_spec(T, rows[i][1], 0) for i in rg_idx] + [_const_spec(consts[i][0].shape[0], consts[i][1], 0) for i in cg_idx])
    out_shape = ([jax.ShapeDtypeStruct((L, ncb * rows[i][1]), row_grads[i]) for i in rg_idx]
                 + [jax.ShapeDtypeStruct((consts[i][0].shape[0], ncb * consts[i][1]), f32) for i in cg_idx])
    args = [a for (a, _, _) in rows] + [a for (a, _, _) in consts] + [a for (a, _, _) in cts] + ([add[0]] if add is not None else [])
    return pl.pallas_call(
        body, name=name, grid=(ncb, L // T), in_specs=in_specs, out_specs=out_specs, out_shape=out_shape,
        compiler_params=_params(("parallel", "arbitrary")))(*args)


def _rms_fn(x, w):
    return (x * lax.rsqrt(jnp.mean(x * x, axis=-1, keepdims=True) + EPS) * w,)


def _glu_fn(a, ag):
    return (a * _sigmoid(ag),)


def _gate_fn(o, g):
    return (o * _silu(g),)


def _ple_fn(gpre, pp):
    return (_sigmoid(gpre) * pp,)


def _ple_fwd_fn(h1, gpre, pp):
    return (h1 + _sigmoid(gpre) * pp,)


def _lngate_fn(vc, g, w, b):
    mu = jnp.mean(vc, axis=-1, keepdims=True)
    xc = vc - mu
    y = xc * lax.rsqrt(jnp.mean(xc * xc, axis=-1, keepdims=True) + EPS) * w + b
    return (_silu(y) * _silu(g),)


def _post_fn(yf, yb, xs, z, dsk, gw):
    y = (yf + yb + dsk * xs) * _silu(z)
    return (y * lax.rsqrt(jnp.mean(y * y, axis=-1, keepdims=True) + EPS) * gw,)


def _make_prep_fn(n_heads):
    def prep_fn(u, bias, alog):
        lane = lax.broadcasted_iota(jnp.int32, (1, LANES), 1)
        dt = jnp.where(lane < 2 * n_heads, jax.nn.softplus(u + bias), 0.0)
        da = dt * (-jnp.exp(alog))
        li = lax.broadcasted_iota(jnp.int32, (CHUNK, CHUNK), 0)
        si = lax.broadcasted_iota(jnp.int32, (CHUNK, CHUNK), 1)
        tril = (si <= li).astype(f32)
        triu = (si >= li).astype(f32)
        csf = jnp.dot(tril, da, precision=HIGHEST, preferred_element_type=f32)
        csb = jnp.dot(triu, da, precision=HIGHEST, preferred_element_type=f32)
        return dt, jnp.where(lane < n_heads, csf, csb)

    return prep_fn


def final_loss(h, w, tgt, name="final_loss"):
    L, D = h.shape
    T = _pick(L, (256, 128))

    def fn(x, w_, t):
        y = x * lax.rsqrt(jnp.mean(x * x, axis=-1, keepdims=True) + EPS) * w_
        err = jnp.square(y - t)
        return 0.5 * jnp.sum(jnp.mean(err, axis=-1, keepdims=True))

    def body(h_ref, w_ref, t_ref, loss_ref, dh_ref, dw_ref):
        @pl.when(pl.program_id(0) == 0)
        def _():
            loss_ref[...] = jnp.zeros_like(loss_ref)
            dw_ref[...] = jnp.zeros_like(dw_ref)

        t = t_ref[...]
        val, vjp = jax.vjp(lambda x, w_: fn(x, w_, t), h_ref[...], w_ref[...])
        dx, dw = vjp(jnp.ones((), f32))
        dh_ref[...] = dx
        dw_ref[...] += dw
        loss_ref[...] += jnp.broadcast_to(val, loss_ref.shape)

    return pl.pallas_call(
        body, name=name, grid=(L // T,),
        in_specs=[pl.BlockSpec((T, D), lambda i: (i, 0)), pl.BlockSpec((1, D), lambda i: (0, 0)), pl.BlockSpec((T, D), lambda i: (i, 0))],
        out_specs=[pl.BlockSpec((8, LANES), lambda i: (0, 0)), pl.BlockSpec((T, D), lambda i: (i, 0)), pl.BlockSpec((1, D), lambda i: (0, 0))],
        out_shape=[jax.ShapeDtypeStruct((8, LANES), f32), jax.ShapeDtypeStruct((L, D), f32), jax.ShapeDtypeStruct((1, D), f32)],
        compiler_params=_params(("arbitrary",)))(h, w, tgt)


CONV_ROWS = 128


def conv_fwd(x_arr, xbase, C, w, b, K, act, name):
    L = x_arr.shape[0]
    R = CONV_ROWS
    lo = CONV_PAD - K // 2

    def body(x_ref, w_ref, b_ref, *rest):
        xpad = rest[-1]
        outs = rest[:-1]
        xpad[0:CONV_PAD, :] = jnp.zeros((CONV_PAD, LANES), f32)
        xpad[CONV_PAD + L:CONV_PAD + L + CONV_PAD, :] = jnp.zeros((CONV_PAD, LANES), f32)
        xpad[CONV_PAD:CONV_PAD + L, :] = x_ref[...]

        def chunk(ci, carry):
            r0 = pl.multiple_of(ci * R, R)
            acc = jnp.broadcast_to(b_ref[...], (R, LANES))
            for k in range(K):
                acc = acc + w_ref[k:k + 1, :] * xpad[pl.ds(r0 + (lo + k), R), :]
            outs[0][pl.ds(r0, R), :] = acc
            if act:
                outs[1][pl.ds(r0, R), :] = _silu(acc)
            return carry

        lax.fori_loop(0, L // R, chunk, 0)

    col = pl.BlockSpec((L, LANES), lambda j: (0, j))
    n_out = 2 if act else 1
    return pl.pallas_call(
        body, name=name, grid=(C // LANES,),
        in_specs=[pl.BlockSpec((L, LANES), lambda j: (0, xbase + j)), pl.BlockSpec((K, LANES), lambda j: (0, j)),
                  pl.BlockSpec((1, LANES), lambda j: (0, j))],
        out_specs=[col] * n_out, out_shape=[jax.ShapeDtypeStruct((L, C), f32)] * n_out,
        scratch_shapes=[pltpu.VMEM((L + 2 * CONV_PAD, LANES), f32)],
        compiler_params=_params(("parallel",)))(x_arr, w, b)


def conv_bwd(dy, x_arr, xbase, C, w, K, pre, out_dtype, name):
    L = x_arr.shape[0]
    R = CONV_ROWS
    lo = CONV_PAD - K // 2
    act = pre is not None

    def body(*refs):
        if act:
            dy_ref, pre_ref, x_ref, w_ref, dx_ref, dw_ref, db_ref, dpad, xpad, accw, accb = refs
        else:
            dy_ref, x_ref, w_ref, dx_ref, dw_ref, db_ref, dpad, xpad, accw, accb = refs
        zeros = jnp.zeros((CONV_PAD, LANES), f32)
        for pad in (dpad, xpad):
            pad[0:CONV_PAD, :] = zeros
            pad[CONV_PAD + L:CONV_PAD + L + CONV_PAD, :] = zeros
        xpad[CONV_PAD:CONV_PAD + L, :] = x_ref[...]
        if act:
            p = pre_ref[...]
            s = _sigmoid(p)
            dpad[CONV_PAD:CONV_PAD + L, :] = dy_ref[...].astype(f32) * (s * (1.0 + p * (1.0 - s)))
        else:
            dpad[CONV_PAD:CONV_PAD + L, :] = dy_ref[...].astype(f32)
        accw[...] = jnp.zeros_like(accw)
        accb[...] = jnp.zeros_like(accb)

        def chunk(ci, carry):
            r0 = pl.multiple_of(ci * R, R)
            acc = jnp.zeros((R, LANES), f32)
            for k in range(K):
                acc = acc + w_ref[k:k + 1, :] * dpad[pl.ds(r0 + (CONV_PAD + K // 2 - k), R), :]
            dx_ref[pl.ds(r0, R), :] = acc.astype(out_dtype)
            d = dpad[pl.ds(r0 + CONV_PAD, R), :]
            accb[...] += jnp.sum(d.reshape(R // 8, 8, LANES), axis=0)
            for k in range(K):
                prod = d * xpad[pl.ds(r0 + (lo + k), R), :]
                accw[k] += jnp.sum(prod.reshape(R // 8, 8, LANES), axis=0)
            return carry

        lax.fori_loop(0, L // R, chunk, 0)
        dw_ref[...] = jnp.sum(accw[...], axis=1)
        db_ref[...] = jnp.sum(accb[...], axis=0, keepdims=True)

    col = pl.BlockSpec((L, LANES), lambda j: (0, j))
    in_specs = [col] + ([col] if act else []) + [pl.BlockSpec((L, LANES), lambda j: (0, xbase + j)), pl.BlockSpec((K, LANES), lambda j: (0, j))]
    args = (dy,) + ((pre,) if act else ()) + (x_arr, w)
    return pl.pallas_call(
        body, name=name, grid=(C // LANES,), in_specs=in_specs,
        out_specs=[col, pl.BlockSpec((K, LANES), lambda j: (0, j)), pl.BlockSpec((1, LANES), lambda j: (0, j))],
        out_shape=[jax.ShapeDtypeStruct((L, C), out_dtype), jax.ShapeDtypeStruct((K, C), f32), jax.ShapeDtypeStruct((1, C), f32)],
        scratch_shapes=[pltpu.VMEM((L + 2 * CONV_PAD, LANES), f32), pltpu.VMEM((L + 2 * CONV_PAD, LANES), f32),
                        pltpu.VMEM((K, 8, LANES), f32), pltpu.VMEM((8, LANES), f32)],
        compiler_params=_params(("parallel",)))(*args)


def _ssd_chunk(X, Bm, Cm, dtc, csc, csr, S, lane_off, dirn):
    Qn = X.shape[0]
    lane = lax.broadcasted_iota(jnp.int32, (1, LANES), 1)
    head = lax.shift_right_logical(lax.broadcasted_iota(jnp.int32, (1, GROUP_W), 1), 6)
    dtx = jnp.zeros((Qn, GROUP_W), f32)
    csx = jnp.zeros((Qn, GROUP_W), f32)
    cols = []
    for j in range(HEADS_PER_GROUP):
        sel = lane == lane_off + j
        dcol = jnp.sum(jnp.where(sel, dtc, 0.0), axis=1, keepdims=True)
        ccol = jnp.sum(jnp.where(sel, csc, 0.0), axis=1, keepdims=True)
        dtx = jnp.where(head == j, dcol, dtx)
        csx = jnp.where(head == j, ccol, csx)
        cols.append(ccol)
    Xd = X * dtx
    edge = Qn - 1 if dirn == 0 else 0
    cs_edge = csx[edge:edge + 1, :]
    li = lax.broadcasted_iota(jnp.int32, (Qn, Qn), 0)
    si = lax.broadcasted_iota(jnp.int32, (Qn, Qn), 1)
    mask = (si <= li) if dirn == 0 else (si >= li)
    Cb = Cm.astype(bf16)
    Bb = Bm.astype(bf16)
    G = lax.dot_general(Cb, Bb, _DN["nt"], preferred_element_type=f32)
    Y = jnp.exp(csx) * jnp.dot(Cb, S.astype(bf16), preferred_element_type=f32)
    for j in range(HEADS_PER_GROUP):
        row = csr[j:j + 1, :]
        Lm = jnp.exp(jnp.where(mask, cols[j] - row, NEG_BIG))
        Wj = (G * Lm).astype(bf16)
        Xj = jnp.where(head == j, Xd, 0.0).astype(bf16)
        Y = Y + jnp.dot(Wj, Xj, preferred_element_type=f32)
    S_new = S * jnp.exp(cs_edge) + lax.dot_general(Bb, (Xd * jnp.exp(cs_edge - csx)).astype(bf16), _DN["tn"], preferred_element_type=f32)
    return Y, S_new


def _ssd_specs(D, G, nchunk, dirn, descending):
    def cidx(c):
        return nchunk - 1 - c if descending else c

    gps = _ssd_groups_per_step(D, G)
    nb = D // N_STATE
    x_spec = pl.BlockSpec((CHUNK, gps * GROUP_W), lambda c, g: (cidx(c), g))
    b_spec = pl.BlockSpec((CHUNK, gps * N_STATE), lambda c, g: (cidx(c), nb // gps + g))
    c_spec = pl.BlockSpec((CHUNK, gps * N_STATE), lambda c, g: (cidx(c), (nb + G) // gps + g))
    n_spec = pl.BlockSpec((CHUNK, gps * N_STATE), lambda c, g: (cidx(c), g))
    lane_spec = pl.BlockSpec((CHUNK, LANES), lambda c, g: (cidx(c), 0))
    csr_spec = pl.BlockSpec((gps, 8, CHUNK), lambda c, g: (dirn * (G // gps) + g, 0, cidx(c)))
    dcsr_spec = pl.BlockSpec((gps, 8, CHUNK), lambda c, g: (g, 0, cidx(c)))
    s_spec = pl.BlockSpec((gps, None, N_STATE, GROUP_W), lambda c, g: (g, cidx(c), 0, 0))
    return x_spec, b_spec, c_spec, n_spec, lane_spec, csr_spec, dcsr_spec, s_spec


def _ssd_groups_per_step(D, G):
    nb = D // N_STATE
    for n in (SSD_GROUPS_PER_STEP, 2):
        if G % n == 0 and nb % n == 0:
            return n
    return 1


def ssd_fwd(act, dt128, cs128, csr, dirn, n_heads, name):
    L = act.shape[0]
    D = act.shape[1] // 2
    G = D // GROUP_W
    nchunk = L // CHUNK
    x_spec, b_spec, c_spec, _, lane_spec, csr_spec, _, s_spec = _ssd_specs(D, G, nchunk, dirn, dirn == 1)

    gps = _ssd_groups_per_step(D, G)

    def body(x_ref, b_ref, c_ref, dt_ref, cs_ref, csr_ref, y_ref, ssave_ref, S):
        c = pl.program_id(0)
        for gg in range(gps):
            g = pl.program_id(1) * gps + gg

            @pl.when(c == 0)
            def _():
                S[g] = jnp.zeros((N_STATE, GROUP_W), f32)

            xs_, ns_ = pl.ds(gg * GROUP_W, GROUP_W), pl.ds(gg * N_STATE, N_STATE)
            s_in = S[g]
            ssave_ref[gg] = s_in
            y, s_new = _ssd_chunk(x_ref[:, xs_], b_ref[:, ns_], c_ref[:, ns_], dt_ref[...], cs_ref[...], csr_ref[gg], s_in,
                                  dirn * n_heads + HEADS_PER_GROUP * g, dirn)
            y_ref[:, xs_] = y
            S[g] = s_new

    return pl.pallas_call(
        body, name=name, grid=(nchunk, G // gps),
        in_specs=[x_spec, b_spec, c_spec, lane_spec, lane_spec, csr_spec],
        out_specs=[x_spec, s_spec],
        out_shape=[jax.ShapeDtypeStruct((L, D), f32), jax.ShapeDtypeStruct((G, nchunk, N_STATE, GROUP_W), f32)],
        scratch_shapes=[pltpu.VMEM((G, N_STATE, GROUP_W), f32)],
        compiler_params=_params(("arbitrary", "arbitrary")))(act, act, act, dt128, cs128, csr)


def ssd_bwd(act, dt128, cs128, csr, ssave, dy, dirn, n_heads, name):
    L = act.shape[0]
    D = act.shape[1] // 2
    G = D // GROUP_W
    nchunk = L // CHUNK
    x_spec, b_spec, c_spec, n_spec, lane_spec, csr_in, dcsr_spec, s_spec = _ssd_specs(D, G, nchunk, dirn, dirn == 0)

    gps = _ssd_groups_per_step(D, G)

    def body(x_ref, b_ref, c_ref, dt_ref, cs_ref, csr_ref, s_ref, dy_ref, dx_ref, db_ref, dc_ref, ddt_ref, dcs_ref, dcsr_ref, dS):
        c = pl.program_id(0)

        @pl.when(pl.program_id(1) == 0)
        def _():
            ddt_ref[...] = jnp.zeros_like(ddt_ref)
            dcs_ref[...] = jnp.zeros_like(dcs_ref)

        ddt_sum, dcs_sum = None, None
        for gg in range(gps):
            g = pl.program_id(1) * gps + gg

            @pl.when(c == 0)
            def _():
                dS[g] = jnp.zeros((N_STATE, GROUP_W), f32)

            xs_, ns_ = pl.ds(gg * GROUP_W, GROUP_W), pl.ds(gg * N_STATE, N_STATE)
            lane_off = dirn * n_heads + HEADS_PER_GROUP * g
            _, vjp = jax.vjp(lambda X, Bm, Cm, dtc, csc, csr_, S: _ssd_chunk(X, Bm, Cm, dtc, csc, csr_, S, lane_off, dirn),
                             x_ref[:, xs_], b_ref[:, ns_], c_ref[:, ns_], dt_ref[...], cs_ref[...], csr_ref[gg], s_ref[gg])
            dX, dB, dC, ddt, dcs, dcsr, dS_in = vjp((dy_ref[:, xs_], dS[g]))
            dx_ref[:, xs_] = dX
            db_ref[:, ns_] = dB
            dc_ref[:, ns_] = dC
            dcsr_ref[gg] = dcsr
            dS[g] = dS_in
            ddt_sum = ddt if ddt_sum is None else ddt_sum + ddt
            dcs_sum = dcs if dcs_sum is None else dcs_sum + dcs

        ddt_ref[...] += ddt_sum
        dcs_ref[...] += dcs_sum

    return pl.pallas_call(
        body, name=name, grid=(nchunk, G // gps),
        in_specs=[x_spec, b_spec, c_spec, lane_spec, lane_spec, csr_in, s_spec, x_spec],
        out_specs=[x_spec, n_spec, n_spec, lane_spec, lane_spec, dcsr_spec],
        out_shape=[jax.ShapeDtypeStruct((L, D), f32), jax.ShapeDtypeStruct((L, G * N_STATE), f32), jax.ShapeDtypeStruct((L, G * N_STATE), f32),
                   jax.ShapeDtypeStruct((L, LANES), f32), jax.ShapeDtypeStruct((L, LANES), f32), jax.ShapeDtypeStruct((G, 8, L), f32)],
        scratch_shapes=[pltpu.VMEM((G, N_STATE, GROUP_W), f32)],
        compiler_params=_params(("arbitrary", "arbitrary")))(act, act, act, dt128, cs128, csr, ssave, dy)


def _csr_from_cs(cs128, n_heads):
    L = cs128.shape[0]
    t = cs128[:, :2 * n_heads].T.reshape(2 * n_heads // HEADS_PER_GROUP, HEADS_PER_GROUP, L)
    return jnp.pad(t, ((0, 0), (0, 8 - HEADS_PER_GROUP), (0, 0)))


def _cs_from_dcsr(dcsr_f, dcsr_b, n_heads):
    L = dcsr_f.shape[-1]
    t = jnp.concatenate([dcsr_f, dcsr_b], axis=0)[:, :HEADS_PER_GROUP, :].reshape(2 * n_heads, L).T
    return jnp.pad(t, ((0, 0), (0, LANES - 2 * n_heads)))


def na_bias_table(rpb):
    nh = rpb.shape[0]
    cols = np.arange(GRID_W)
    col_start = np.clip(cols - NA_WC // 2, 0, GRID_W - NA_WC)
    col_valid = (cols[None, :] >= col_start[:, None]) & (cols[None, :] < col_start[:, None] + NA_WC)
    col_off = np.clip(cols[None, :] - cols[:, None], -(NA_WC - 1), NA_WC - 1) + NA_WC - 1
    rm = np.zeros((NA_WR, NA_WR, 2 * NA_WR - 1), np.float32)
    for d in range(NA_WR):
        for i in range(NA_WR):
            rm[d, i, i - d + NA_WR - 1] = 1.0
    cm = np.zeros((GRID_W, GRID_W, 2 * NA_WC - 1), np.float32)
    cm[np.arange(GRID_W)[:, None], np.arange(GRID_W)[None, :], col_off] = 1.0
    bt = jnp.einsum("hrc,dir,qkc->hdqik", rpb, jnp.asarray(rm), jnp.asarray(cm), precision=HIGHEST)
    bt = jnp.where(jnp.asarray(col_valid)[None, None, :, None, :], bt, NEG_BIG)
    return bt.reshape(nh, NA_WR, GRID_W, NA_WR * GRID_W)


def _na_window(r, rows):
    rs = jnp.clip(r - NA_WR // 2, 0, rows - NA_WR)
    return rs, r - rs


def _na_probs(q, kw, bias):
    s = lax.dot_general(q.astype(bf16), kw.astype(bf16), _DN["nt"], preferred_element_type=f32) * (NA_D ** -0.5) + bias
    m = jnp.max(s, axis=-1, keepdims=True)
    p = jnp.exp(s - m)
    return p / jnp.sum(p, axis=-1, keepdims=True)


def na_fwd(um, qbase, kbase, vbase, bt, name):
    L = um.shape[0]
    nh = bt.shape[0]
    rows = L // GRID_W
    WK = NA_WR * GRID_W

    RB = _pick(rows, (NA_ROWS_PER_STEP, 2, 1))

    def body(q_ref, k_ref, v_ref, bt_ref, o_ref):
        win = [_na_window(pl.program_id(1) * RB + rb, rows) for rb in range(RB)]
        t0s = [pl.multiple_of(rs * GRID_W, GRID_W) for rs, _ in win]
        qrows = [pl.ds(rb * GRID_W, GRID_W) for rb in range(RB)]
        ps = [_na_probs(q_ref[qr, :], k_ref[pl.ds(t0, WK), :], bt_ref[delta]) for qr, t0, (_, delta) in zip(qrows, t0s, win)]
        os_ = [jnp.dot(p.astype(bf16), v_ref[pl.ds(t0, WK), :].astype(bf16), preferred_element_type=f32) for p, t0 in zip(ps, t0s)]
        for qr, o in zip(qrows, os_):
            o_ref[qr, :] = o

    kv = lambda base: pl.BlockSpec((L, NA_D), lambda h, r: (0, base + h))
    return pl.pallas_call(
        body, name=name, grid=(nh, rows // RB),
        in_specs=[pl.BlockSpec((RB * GRID_W, NA_D), lambda h, r: (r, qbase + h)), kv(kbase), kv(vbase),
                  pl.BlockSpec((None, NA_WR, GRID_W, WK), lambda h, r: (h, 0, 0, 0))],
        out_specs=pl.BlockSpec((RB * GRID_W, NA_D), lambda h, r: (r, h)),
        out_shape=jax.ShapeDtypeStruct((L, nh * NA_D), f32),
        compiler_params=_params(("parallel", "arbitrary")))(um, um, um, bt)


def na_bwd(um, qbase, kbase, vbase, bt, do, name, side=None):
    L = um.shape[0]
    nh = bt.shape[0]
    rows = L // GRID_W
    WK = NA_WR * GRID_W
    scale = NA_D ** -0.5

    RB = _pick(rows, (NA_ROWS_PER_STEP, 2, 1))

    ns = len(side[1]) if side is not None else 0
    nsteps = rows // RB

    def body(q_ref, k_ref, v_ref, bt_ref, do_ref, *rest):
        x_refs = rest[:ns]
        dq_ref, dk_ref, dv_ref, dbt_ref = rest[ns:ns + 4]
        out_refs = rest[ns + 4:2 * ns + 4]
        sems = rest[2 * ns + 4:]
        if side is not None:
            step = pl.program_id(0) * nsteps + pl.program_id(1)

            @pl.when(step == 0)
            def _():
                _exchange_start(side[0], x_refs, out_refs, sems)

        @pl.when(pl.program_id(1) == 0)
        def _():
            dk_ref[...] = jnp.zeros_like(dk_ref)
            dv_ref[...] = jnp.zeros_like(dv_ref)
            dbt_ref[...] = jnp.zeros_like(dbt_ref)

        win = [_na_window(pl.program_id(1) * RB + rb, rows) for rb in range(RB)]
        t0s = [pl.multiple_of(rs * GRID_W, GRID_W) for rs, _ in win]
        qrows = [pl.ds(rb * GRID_W, GRID_W) for rb in range(RB)]
        qs_ = [q_ref[qr, :] for qr in qrows]
        kws = [k_ref[pl.ds(t0, WK), :].astype(bf16) for t0 in t0s]
        vws = [v_ref[pl.ds(t0, WK), :].astype(bf16) for t0 in t0s]
        dos = [do_ref[qr, :].astype(bf16) for qr in qrows]
        ps = [_na_probs(q, kw, bt_ref[delta]) for q, kw, (_, delta) in zip(qs_, kws, win)]
        dps = [lax.dot_general(do_, vw, _DN["nt"], preferred_element_type=f32) for do_, vw in zip(dos, vws)]
        dss = [p * (dp - jnp.sum(dp * p, axis=-1, keepdims=True)) for p, dp in zip(ps, dps)]
        dsbs = [ds.astype(bf16) for ds in dss]
        dqs = [jnp.dot(dsb, kw, preferred_element_type=f32) * scale for dsb, kw in zip(dsbs, kws)]
        dks = [lax.dot_general(dsb, q.astype(bf16), _DN["tn"], preferred_element_type=f32) * scale for dsb, q in zip(dsbs, qs_)]
        dvs = [lax.dot_general(p.astype(bf16), do_, _DN["tn"], preferred_element_type=f32) for p, do_ in zip(ps, dos)]
        for rb in range(RB):
            dq_ref[qrows[rb], :] = dqs[rb]
            dbt_ref[win[rb][1]] += dss[rb]
            dk_ref[pl.ds(t0s[rb], WK), :] += dks[rb]
            dv_ref[pl.ds(t0s[rb], WK), :] += dvs[rb]

        if side is not None:
            @pl.when(step == nh * nsteps - 1)
            def _():
                _exchange_finish(side[0], x_refs, out_refs, sems)

    kv = lambda base: pl.BlockSpec((L, NA_D), lambda h, r: (0, base + h))
    qs = lambda base: pl.BlockSpec((RB * GRID_W, NA_D), lambda h, r: (r, base + h))
    bts = pl.BlockSpec((None, NA_WR, GRID_W, WK), lambda h, r: (h, 0, 0, 0))
    big = jax.ShapeDtypeStruct((L, nh * NA_D), f32)
    in_specs = [qs(qbase), kv(kbase), kv(vbase), bts, qs(0)]
    out_specs = [qs(0), kv(0), kv(0), bts]
    out_shape = [big, big, big, jax.ShapeDtypeStruct(bt.shape, f32)]
    if side is None:
        return pl.pallas_call(
            body, name=name, grid=(nh, nsteps), in_specs=in_specs, out_specs=out_specs, out_shape=out_shape,
            compiler_params=_params(("parallel", "arbitrary")))(um, um, um, bt, do)
    res = pl.pallas_call(
        body, name=name, grid=(nh, nsteps), in_specs=in_specs + [_HBM] * ns, out_specs=out_specs + [_HBM] * ns,
        out_shape=out_shape + _exchange_out_shapes(side[0], side[1]), scratch_shapes=_exchange_scratch(ns),
        compiler_params=_params(("arbitrary", "arbitrary")))(um, um, um, bt, do, *side[1])
    return tuple(res[:4]) + (res[4:],)


_HBM = pl.BlockSpec(memory_space=pltpu.HBM)


def all_gather(xs, name):
    n = len(xs)

    def body(*refs):
        _exchange_start("gather", refs[:n], refs[n:2 * n], refs[2 * n:])
        _exchange_finish("gather", refs[:n], refs[n:2 * n], refs[2 * n:])

    return pl.pallas_call(
        body, name=name, in_specs=[_HBM] * n, out_specs=[_HBM] * n, out_shape=_exchange_out_shapes("gather", xs),
        scratch_shapes=_exchange_scratch(n))(*xs)


def _exchange_out_shapes(kind, xs):
    return [jax.ShapeDtypeStruct(((N_DEV,) + x.shape) if kind == "gather" else x.shape, x.dtype) for x in xs]


def _exchange_scratch(n):
    return [pltpu.SemaphoreType.DMA((7 * n,)), pltpu.SemaphoreType.DMA((7 * n,)), pltpu.SemaphoreType.DMA((n,))]


def _gather_copies(x_refs, out_refs, sems):
    send_sems, recv_sems, local_sems = sems
    n = len(x_refs)
    mx, my, mc = lax.axis_index("x"), lax.axis_index("y"), lax.axis_index("c")
    me, sibling = (mx, my, mc), (mx, my, 1 - mc)
    chips = [(1 - mx, my), (mx, 1 - my), (1 - mx, 1 - my)]

    def slab(t, px, py, pc):
        return out_refs[t].at[4 * px + 2 * py + pc]

    def copy(t, k, block, to, src=None):
        return pltpu.make_async_remote_copy(
            src_ref=slab(t, *block) if src is None else src, dst_ref=slab(t, *block),
            send_sem=send_sems.at[7 * t + k], recv_sem=recv_sems.at[7 * t + k], device_id=to, device_id_type=pl.DeviceIdType.MESH)

    mine = [pltpu.make_async_copy(x_refs[t], slab(t, *me), local_sems.at[t]) for t in range(n)]
    first = []
    for t in range(n):
        first.append(copy(t, 0, me, sibling, src=x_refs[t]))
        first += [copy(t, 1 + j, me, (*chip, mc), src=x_refs[t]) for j, chip in enumerate(chips)]
    return mine, first, copy, me, sibling, chips, mc


def _exchange_start(kind, x_refs, out_refs, sems):
    if kind == "gather":
        mine, first = _gather_copies(x_refs, out_refs, sems)[:2]
    else:
        mine, first = _a2a_copies(x_refs, out_refs, sems)
    for cp in mine + first:
        cp.start()


def _exchange_finish(kind, x_refs, out_refs, sems):
    n = len(x_refs)
    if kind == "gather":
        mine, first, copy, me, sibling, chips, mc = _gather_copies(x_refs, out_refs, sems)
        passed = []
        for t in range(n):
            for j, chip in enumerate(chips):
                copy(t, 1 + j, (*chip, mc), me).wait_recv()
                fwd = copy(t, 4 + j, (*chip, mc), sibling)
                fwd.start()
                passed.append(fwd)
        for t in range(n):
            copy(t, 0, sibling, me).wait_recv()
            for j, chip in enumerate(chips):
                copy(t, 4 + j, (*chip, 1 - mc), me).wait_recv()
        for cp in first + passed:
            cp.wait_send()
    else:
        mine, first = _a2a_copies(x_refs, out_refs, sems)
        for cp in first:
            cp.wait()
    for cp in mine:
        cp.wait()


def _a2a_copies(x_refs, out_refs, sems):
    send_sems, recv_sems, local_sems = sems
    n = len(x_refs)
    mx, my, mc = lax.axis_index("x"), lax.axis_index("y"), lax.axis_index("c")
    me = 4 * mx + 2 * my + mc
    mine = [pltpu.make_async_copy(x_refs[t].at[me], out_refs[t].at[me], local_sems.at[t]) for t in range(n)]
    copies = []
    for t in range(n):
        for k in range(1, N_DEV):
            px = 1 - mx if k & 4 else mx
            py = 1 - my if k & 2 else my
            pc = 1 - mc if k & 1 else mc
            peer = 4 * px + 2 * py + pc
            copies.append(pltpu.make_async_remote_copy(
                src_ref=x_refs[t].at[peer], dst_ref=out_refs[t].at[me], send_sem=send_sems.at[7 * t + k - 1],
                recv_sem=recv_sems.at[7 * t + k - 1], device_id=(px, py, pc), device_id_type=pl.DeviceIdType.MESH))
    return mine, copies


def all_to_all(xs, name):
    n = len(xs)

    def body(*refs):
        _exchange_start("a2a", refs[:n], refs[n:2 * n], refs[2 * n:])
        _exchange_finish("a2a", refs[:n], refs[n:2 * n], refs[2 * n:])

    return pl.pallas_call(
        body, name=name, in_specs=[_HBM] * n, out_specs=[_HBM] * n, out_shape=_exchange_out_shapes("a2a", xs),
        scratch_shapes=_exchange_scratch(n))(*xs)


def adamw(w, slabs, m, v, own, name):
    R, C = w.shape
    tr = _pick(R, tuple(t for t in (256, 128, 64, 32, 16, 8) if t * C * 4 <= ADAMW_TILE_BYTES) or (8,))
    c1 = 1.0 - ADAM_B1 ** ADAM_STEP
    c2 = 1.0 - ADAM_B2 ** ADAM_STEP

    def body(*refs):
        if own is not None:
            me_ref, w_ref, s_ref, m_ref, v_ref, own_ref, g_ref, d_ref, nm_ref, nv_ref = refs
        else:
            w_ref, s_ref, m_ref, v_ref, g_ref, d_ref, nm_ref, nv_ref = refs
        g = None
        for s in range(N_DEV):
            t = s_ref[s].astype(f32)
            if own is not None:
                t = jnp.where(me_ref[0] == s, own_ref[...], t)
            g = t if g is None else g + t
        wv = w_ref[...]
        mn = ADAM_B1 * m_ref[...] + (1.0 - ADAM_B1) * g
        vn = ADAM_B2 * v_ref[...] + (1.0 - ADAM_B2) * jnp.square(g)
        m_hat = mn / c1
        v_hat = vn / c2
        g_ref[...] = g
        d_ref[...] = -ADAM_LR * (m_hat / (jnp.sqrt(v_hat) + ADAM_EPS) + ADAM_WD * wv)
        nm_ref[...] = mn
        nv_ref[...] = vn

    out_shape = [jax.ShapeDtypeStruct((R, C), f32)] * 4
    if own is not None:
        me, part = own
        blk = pl.BlockSpec((tr, C), lambda i, me_: (i, 0))
        gs = pltpu.PrefetchScalarGridSpec(
            num_scalar_prefetch=1, grid=(R // tr,),
            in_specs=[blk, pl.BlockSpec((N_DEV, tr, C), lambda i, me_: (0, i, 0)), blk, blk, blk],
            out_specs=[blk] * 4)
        return pl.pallas_call(body, name=name, grid_spec=gs, out_shape=out_shape, compiler_params=_params(("parallel",)))(
            me.reshape(1).astype(jnp.int32), w, slabs, m, v, part)
    blk = pl.BlockSpec((tr, C), lambda i: (i, 0))
    return pl.pallas_call(
        body, name=name, grid=(R // tr,), in_specs=[blk, pl.BlockSpec((N_DEV, tr, C), lambda i: (0, i, 0)), blk, blk],
        out_specs=[blk] * 4, out_shape=out_shape, compiler_params=_params(("parallel",)))(w, slabs, m, v)


def _packed_rows(n, cols):
    rows = -(-n // cols)
    mult = 256 if rows > 256 else 16
    return -(-rows // mult) * mult


def _pack(arrs, cols, dtype, lead=()):
    nl = len(lead)
    flat = jnp.concatenate([a.reshape(lead + (-1,)).astype(dtype) for a in arrs], axis=-1)
    n = flat.shape[-1]
    rows = _packed_rows(n, cols)
    flat = jnp.pad(flat, [(0, 0)] * nl + [(0, rows * cols - n)])
    return flat.reshape(lead + (rows, cols))


def _unpack(packed, shapes, lead=()):
    flat = packed.reshape(lead + (-1,))
    out, off = [], 0
    for s in shapes:
        n = int(np.prod(s))
        out.append(flat[..., off:off + n].reshape(lead + tuple(s)))
        off += n
    return out


def _to_slabs(full, axis):
    s = full.shape
    r = full.reshape(s[:axis] + (N_DEV, s[axis] // N_DEV) + s[axis + 1:])
    return jnp.moveaxis(r, axis, 0)


def _from_slabs(slabs, axis):
    r = jnp.moveaxis(slabs, 0, axis)
    s = r.shape
    return r.reshape(s[:axis] + (s[axis] * s[axis + 1],) + s[axis + 2:])


BIG = [("ev_w_in", 2), ("ev_w_out", 1), ("od_w_in", 2), ("od_w_out", 1), ("ple_w_gate", 1), ("ple_w_proj", 2)]
SMALL = [("ev_conv_w", 2), ("od_norm_w", 1), ("od_dw_w", 2), ("od_dw_b", 1), ("od_ln_w", 1), ("od_ln_b", 1)]
REPL = ["ev_norm_w", "ev_conv_b", "ev_dt_bias_f", "ev_dt_bias_b", "ev_a_log_f", "ev_a_log_b", "ev_d_skip", "ev_gnorm_w", "ev_rpb",
        "ple_norm_w", "final_norm_w"]
WEIGHTS = ["ev_norm_w", "ev_w_in", "ev_conv_w", "ev_conv_b", "ev_dt_bias_f", "ev_dt_bias_b", "ev_a_log_f", "ev_a_log_b", "ev_d_skip",
           "ev_gnorm_w", "ev_rpb", "ev_w_out", "od_norm_w", "od_w_in", "od_dw_w", "od_dw_b", "od_ln_w", "od_ln_b", "od_w_out",
           "ple_norm_w", "ple_w_gate", "ple_w_proj", "final_norm_w"]
BIG_COLS = 1024
SMALL_COLS = 128


def _row_tile(L, width):
    return _pick(L, (256, 128)) if width <= 2048 else _pick(L, (128,))


def _ple_fwd(h1, p_i, nw, wg, wp, tag):
    L, D = h1.shape
    T = _row_tile(L, D)
    (hn2,) = rowop_fwd(_rms_fn, "ple_rms_" + tag, [(h1, D, 0)], [(nw, D, 0)], [(D, bf16)], T)
    gpre = mm(hn2, wg, "nn", "ple_gate_mm_" + tag)
    pp = mm(p_i, wp, "nn", "ple_proj_mm_" + tag)
    (h2,) = rowop_fwd(_ple_fwd_fn, "ple_comb_" + tag, [(h1, D, 0), (gpre, D, 0), (pp, D, 0)], [], [(D, f32)], T)
    return h2, (h1, p_i, nw, wg, wp, hn2, gpre, pp)


def _ple_bwd(res, dh2, tag):
    h1, p_i, nw, wg, wp, hn2, gpre, pp = res
    L, D = h1.shape
    T = _row_tile(L, D)
    dgpre, dpp = rowop_bwd(_ple_fn, "ple_comb_bwd_" + tag, [(gpre, D, 0), (pp, D, 0)], [], [(dh2, D, 0)], [bf16, bf16], [], T)
    dwg = mm(hn2, dgpre, "tn", "ple_gate_dw_" + tag)
    dwp = mm(p_i, dpp, "tn", "ple_proj_dw_" + tag)
    dhn2 = mm(dgpre, wg, "nt", "ple_gate_dx_" + tag)
    dh1, dnw = rowop_bwd(_rms_fn, "ple_rms_bwd_" + tag, [(h1, D, 0)], [(nw, D, 0)], [(dhn2, D, 0)], [f32], [True], T, add=(dh2, D, 0))
    return dh1, dnw, dwg, dwp


def _even_fwd(h, P, tag, side=None):
    L, D = h.shape
    H = D // HEAD_P
    T = _row_tile(L, D)
    DB = D // LANES
    (hn,) = rowop_fwd(_rms_fn, "ev_rms_" + tag, [(h, D, 0)], [(P["norm_w"], D, 0)], [(D, bf16)], T)
    um = mm(hn, P["w_main"], "nn", "ev_in_mm_" + tag, side=side)
    um, exchanged = um if side is not None else (um, None)
    udt = mm(hn, P["w_dt"], "nn", "ev_dt_mm_" + tag)
    pre, act = conv_fwd(um, DB, 2 * D, P["conv_w"], P["conv_b"], K_SSM, True, "ev_conv_" + tag)
    prep = _make_prep_fn(H)
    dt128, cs128 = rowop_fwd(prep, "ev_prep_" + tag, [(udt, LANES, 0)], [(P["bias128"], LANES, 0), (P["alog128"], LANES, 0)],
                             [(LANES, f32), (LANES, f32)], CHUNK)
    csr = _csr_from_cs(cs128, H)
    yf, sf = ssd_fwd(act, dt128, cs128, csr, 0, H, "ev_ssd_f_" + tag)
    yb, sb = ssd_fwd(act, dt128, cs128, csr, 1, H, "ev_ssd_b_" + tag)
    G = D // GROUP_W
    (yssd,) = rowop_fwd(_post_fn, "ev_post_" + tag, [(yf, GROUP_W, 0), (yb, GROUP_W, 0), (act, GROUP_W, 0), (um, GROUP_W, 0)],
                        [(P["dskipx"], GROUP_W, 0), (P["gnorm_w"], GROUP_W, 0)], [(GROUP_W, bf16)], _pick(L, POST_ROW_TILES), ncb=G)
    bt = na_bias_table(P["rpb"])
    o = na_fwd(um, 3 * DB, 4 * DB, 5 * DB, bt, "ev_na_" + tag)
    (yna,) = rowop_fwd(_gate_fn, "ev_nagate_" + tag, [(o, D, 0), (um, D, 6)], [], [(D, bf16)], T)
    cat = jnp.concatenate([yssd, yna], axis=-1)
    if P["w_out"] is None:
        P["w_out"] = _from_slabs(exchanged[0], 0)
    h1 = mm(cat, P["w_out"], "nn", "ev_out_mm_" + tag, add=h)
    return h1, (h, hn, um, udt, pre, act, dt128, cs128, csr, yf, yb, sf, sb, bt, o, cat), exchanged


def _even_bwd(P, res, dh1, tag, side=None, early=None):
    h, hn, um, udt, pre, act, dt128, cs128, csr, yf, yb, sf, sb, bt, o, cat = res
    L, D = h.shape
    H = D // HEAD_P
    G = D // GROUP_W
    T = _row_tile(L, D)
    DB = D // LANES
    g = {}
    dcat = mm(dh1, P["w_out"], "nt", "ev_out_dx_" + tag)
    g["w_out"] = mm(cat, dh1, "tn", "ev_out_dw_" + tag)
    do, dg = rowop_bwd(_gate_fn, "ev_nagate_bwd_" + tag, [(o, D, 0), (um, D, 6)], [], [(dcat, D, 1)], [f32, bf16], [], T)
    n_side = len(side[1]) - 1 if side is not None else 0
    na_arrays = list(side[1][1:]) if side is not None else []
    if early is not None:
        na_arrays += [_to_slabs(gr.astype(bf16), ax) for gr, ax in zip([g["w_out"]] + early[0], [0] + early[1])]
    dq, dk, dv, dbt, *na_recv = na_bwd(um, 3 * DB, 4 * DB, 5 * DB, bt, do, "ev_na_bwd_" + tag,
                                       side=("a2a", na_arrays) if na_arrays else None)
    na_recv = list(na_recv[0]) if na_recv else []
    _, rpb_vjp = jax.vjp(na_bias_table, P["rpb"])
    (g["rpb"],) = rpb_vjp(dbt)
    dy, dxs_a, dz, ddsk, dgn = rowop_bwd(
        _post_fn, "ev_post_bwd_" + tag, [(yf, GROUP_W, 0), (yb, GROUP_W, 0), (act, GROUP_W, 0), (um, GROUP_W, 0)],
        [(P["dskipx"], GROUP_W, 0), (P["gnorm_w"], GROUP_W, 0)], [(dcat, GROUP_W, 0)], [f32, None, f32, bf16], [True, True],
        _pick(L, POST_ROW_TILES), ncb=G)
    g["d_skip"] = ddsk.reshape(H, HEAD_P).sum(axis=-1)
    g["gnorm_w"] = dgn.reshape(D)
    dxf, dbf, dcf, ddtf, dcsf, dcsrf = ssd_bwd(act, dt128, cs128, csr, sf, dy, 0, H, "ev_ssd_f_bwd_" + tag)
    dxb, dbb, dcb, ddtb, dcsb, dcsrb = ssd_bwd(act, dt128, cs128, csr, sb, dy, 1, H, "ev_ssd_b_bwd_" + tag)
    dact = jnp.concatenate([dxf + dxb + dxs_a, dbf + dbb, dcf + dcb], axis=-1)
    ddt128 = ddtf + ddtb
    dcs128 = dcsf + dcsb + _cs_from_dcsr(dcsrf, dcsrb, H)
    prep = _make_prep_fn(H)
    dudt, dbias, dalog = rowop_bwd(prep, "ev_prep_bwd_" + tag, [(udt, LANES, 0)], [(P["bias128"], LANES, 0), (P["alog128"], LANES, 0)],
                                   [(ddt128, LANES, 0), (dcs128, LANES, 0)], [bf16], [True, True], CHUNK)
    g["dt_bias_f"], g["dt_bias_b"] = dbias[0, :H], dbias[0, H:2 * H]
    g["a_log_f"], g["a_log_b"] = dalog[0, :H], dalog[0, H:2 * H]
    dxbc, g["conv_w"], dcb_ = conv_bwd(dact, um, DB, 2 * D, P["conv_w"], K_SSM, pre, bf16, "ev_conv_bwd_" + tag)
    g["conv_b"] = dcb_.reshape(2 * D)
    dum = jnp.concatenate([dz, dxbc, dq.astype(bf16), dk.astype(bf16), dv.astype(bf16), dg], axis=-1)
    first_side = (side[0], side[1][:1]) if side is not None else None
    first, own_recv = [], None
    if early is None:
        dhn = mm(dum, P["w_main"], "nt", "ev_in_dx_" + tag, side=first_side)
        dhn, first = dhn if side is not None else (dhn, [])
        g["w_main"] = mm(hn, dum, "tn", "ev_in_dw_" + tag)
        g["w_dt"] = mm(hn, dudt, "tn", "ev_dt_dw_" + tag)
    else:
        gw = mm(hn, dum, "tn", "ev_in_dw_" + tag, side=first_side)
        g["w_main"], first = gw if side is not None else (gw, [])
        g["w_dt"] = mm(hn, dudt, "tn", "ev_dt_dw_" + tag)
        own_slabs = [_to_slabs(_ev_w_in_grad(g, D, H).astype(bf16), 1)]
        dhn, own_recv = mm(dum, P["w_main"], "nt", "ev_in_dx_" + tag, side=("a2a", own_slabs))
    dhn = mm(dudt, P["w_dt"], "nt", "ev_dt_dx_" + tag, add=dhn)
    dh, dnw = rowop_bwd(_rms_fn, "ev_rms_bwd_" + tag, [(h, D, 0)], [(P["norm_w"], D, 0)], [(dhn, D, 0)], [f32], [True], T, add=(dh1, D, 0))
    g["norm_w"] = dnw.reshape(D)
    exchanged = (list(first) + na_recv[:n_side]) if side is not None else None
    early_recv = na_recv[n_side:] if early is not None else None
    return dh, g, exchanged, early_recv, own_recv


def _ev_w_in_grad(g, D, H):
    return jnp.concatenate([g["w_main"][:, :3 * D], g["w_dt"][:, :2 * H], g["w_main"][:, 3 * D:]], axis=1)


def _odd_fwd(h, P, tag, side=None):
    L, D = h.shape
    C = 2 * D
    T = _row_tile(L, D)
    (hn,) = rowop_fwd(_rms_fn, "od_rms_" + tag, [(h, D, 0)], [(P["norm_w"], D, 0)], [(D, bf16)], T)
    u = mm(hn, P["w_in"], "nn", "od_in_mm_" + tag, side=side)
    u, exchanged = u if side is not None else (u, None)
    (vglu,) = rowop_fwd(_glu_fn, "od_glu_" + tag, [(u, D, 0), (u, D, 2)], [], [(D, f32)], T, ncb=2)
    (vc,) = conv_fwd(vglu, 0, C, P["dw_w"], P["dw_b"], K_CONV, False, "od_conv_" + tag)
    TL = _row_tile(L, C)
    (t,) = rowop_fwd(_lngate_fn, "od_lngate_" + tag, [(vc, C, 0), (u, C, 2)], [(P["ln_w"], C, 0), (P["ln_b"], C, 0)], [(C, bf16)], TL)
    h1 = mm(t, P["w_out"], "nn", "od_out_mm_" + tag, add=h)
    return h1, (h, hn, u, vglu, vc, t), exchanged


def _odd_bwd(P, res, dh1, tag, side=None):
    h, hn, u, vglu, vc, t = res
    L, D = h.shape
    C = 2 * D
    T = _row_tile(L, D)
    TL = _row_tile(L, C)
    g = {}
    dt_ = mm(dh1, P["w_out"], "nt", "od_out_dx_" + tag)
    g["w_out"] = mm(t, dh1, "tn", "od_out_dw_" + tag)
    dvc, dg, dlnw, dlnb = rowop_bwd(_lngate_fn, "od_lngate_bwd_" + tag, [(vc, C, 0), (u, C, 2)], [(P["ln_w"], C, 0), (P["ln_b"], C, 0)],
                                    [(dt_, C, 0)], [f32, bf16], [True, True], TL)
    g["ln_w"], g["ln_b"] = dlnw.reshape(C), dlnb.reshape(C)
    dvglu, g["dw_w"], ddwb = conv_bwd(dvc, vglu, 0, C, P["dw_w"], K_CONV, None, f32, "od_conv_bwd_" + tag)
    g["dw_b"] = ddwb.reshape(C)
    da, dag = rowop_bwd(_glu_fn, "od_glu_bwd_" + tag, [(u, D, 0), (u, D, 2)], [], [(dvglu, D, 0)], [bf16, bf16], [], T, ncb=2)
    du = jnp.concatenate([da, dag, dg], axis=-1)
    exchanged = None
    if side is None:
        dhn = mm(du, P["w_in"], "nt", "od_in_dx_" + tag)
        g["w_in"] = mm(hn, du, "tn", "od_in_dw_" + tag)
    else:
        dhn, ex_a = mm(du, P["w_in"], "nt", "od_in_dx_" + tag, side=(side[0], side[1][:1]))
        g["w_in"], ex_b = mm(hn, du, "tn", "od_in_dw_" + tag, side=(side[0], side[1][1:]))
        exchanged = list(ex_a) + list(ex_b)
    dh, dnw = rowop_bwd(_rms_fn, "od_rms_bwd_" + tag, [(h, D, 0)], [(P["norm_w"], D, 0)], [(dhn, D, 0)], [f32], [True], T, add=(dh1, D, 0))
    g["norm_w"] = dnw.reshape(D)
    return dh, g, exchanged


def kernel(x, p, ev_norm_w, ev_w_in, ev_conv_w, ev_conv_b, ev_dt_bias_f, ev_dt_bias_b, ev_a_log_f, ev_a_log_b, ev_d_skip, ev_gnorm_w, ev_rpb, ev_w_out, od_norm_w, od_w_in, od_dw_w, od_dw_b, od_ln_w, od_ln_b, od_w_out, ple_norm_w, ple_w_gate, ple_w_proj, final_norm_w, loss_target, m_ev_norm_w, m_ev_w_in, m_ev_conv_w, m_ev_conv_b, m_ev_dt_bias_f, m_ev_dt_bias_b, m_ev_a_log_f, m_ev_a_log_b, m_ev_d_skip, m_ev_gnorm_w, m_ev_rpb, m_ev_w_out, m_od_norm_w, m_od_w_in, m_od_dw_w, m_od_dw_b, m_od_ln_w, m_od_ln_b, m_od_w_out, m_ple_norm_w, m_ple_w_gate, m_ple_w_proj, m_final_norm_w, v_ev_norm_w, v_ev_w_in, v_ev_conv_w, v_ev_conv_b, v_ev_dt_bias_f, v_ev_dt_bias_b, v_ev_a_log_f, v_ev_a_log_b, v_ev_d_skip, v_ev_gnorm_w, v_ev_rpb, v_ev_w_out, v_od_norm_w, v_od_w_in, v_od_dw_w, v_od_dw_b, v_od_ln_w, v_od_ln_b, v_od_w_out, v_ple_norm_w, v_ple_w_gate, v_ple_w_proj, v_final_norm_w):
    W = dict(ev_norm_w=ev_norm_w, ev_w_in=ev_w_in, ev_conv_w=ev_conv_w, ev_conv_b=ev_conv_b, ev_dt_bias_f=ev_dt_bias_f,
             ev_dt_bias_b=ev_dt_bias_b, ev_a_log_f=ev_a_log_f, ev_a_log_b=ev_a_log_b, ev_d_skip=ev_d_skip, ev_gnorm_w=ev_gnorm_w,
             ev_rpb=ev_rpb, ev_w_out=ev_w_out, od_norm_w=od_norm_w, od_w_in=od_w_in, od_dw_w=od_dw_w, od_dw_b=od_dw_b,
             od_ln_w=od_ln_w, od_ln_b=od_ln_b, od_w_out=od_w_out, ple_norm_w=ple_norm_w, ple_w_gate=ple_w_gate,
             ple_w_proj=ple_w_proj, final_norm_w=final_norm_w)
    M = dict(ev_norm_w=m_ev_norm_w, ev_w_in=m_ev_w_in, ev_conv_w=m_ev_conv_w, ev_conv_b=m_ev_conv_b, ev_dt_bias_f=m_ev_dt_bias_f,
             ev_dt_bias_b=m_ev_dt_bias_b, ev_a_log_f=m_ev_a_log_f, ev_a_log_b=m_ev_a_log_b, ev_d_skip=m_ev_d_skip,
             ev_gnorm_w=m_ev_gnorm_w, ev_rpb=m_ev_rpb, ev_w_out=m_ev_w_out, od_norm_w=m_od_norm_w, od_w_in=m_od_w_in,
             od_dw_w=m_od_dw_w, od_dw_b=m_od_dw_b, od_ln_w=m_od_ln_w, od_ln_b=m_od_ln_b, od_w_out=m_od_w_out,
             ple_norm_w=m_ple_norm_w, ple_w_gate=m_ple_w_gate, ple_w_proj=m_ple_w_proj, final_norm_w=m_final_norm_w)
    V = dict(ev_norm_w=v_ev_norm_w, ev_w_in=v_ev_w_in, ev_conv_w=v_ev_conv_w, ev_conv_b=v_ev_conv_b, ev_dt_bias_f=v_ev_dt_bias_f,
             ev_dt_bias_b=v_ev_dt_bias_b, ev_a_log_f=v_ev_a_log_f, ev_a_log_b=v_ev_a_log_b, ev_d_skip=v_ev_d_skip,
             ev_gnorm_w=v_ev_gnorm_w, ev_rpb=v_ev_rpb, ev_w_out=v_ev_w_out, od_norm_w=v_od_norm_w, od_w_in=v_od_w_in,
             od_dw_w=v_od_dw_w, od_dw_b=v_od_dw_b, od_ln_w=v_od_ln_w, od_ln_b=v_od_ln_b, od_w_out=v_od_w_out,
             ple_norm_w=v_ple_norm_w, ple_w_gate=v_ple_w_gate, ple_w_proj=v_ple_w_proj, final_norm_w=v_final_norm_w)

    h0 = x[0]
    L, D = h0.shape
    H = D // HEAD_P
    depth = p.shape[0]
    n_even, n_odd = (depth + 1) // 2, depth // 2
    me = 4 * lax.axis_index("x") + 2 * lax.axis_index("y") + lax.axis_index("c")

    small_shapes = [W[n].shape for n, _ in SMALL]
    (small_g,) = all_gather([_pack([W[n] for n, _ in SMALL], SMALL_COLS, f32)], "gather_small")
    F = {n: _from_slabs(a, ax) for (n, ax), a in zip(SMALL, _unpack(small_g, small_shapes, (N_DEV,)))}

    def layer_big(i):
        kind = "ev" if i % 2 == 0 else "od"
        return [(kind + "_w_in", i // 2, 1), (kind + "_w_out", i // 2, 0), ("ple_w_gate", i, 0), ("ple_w_proj", i, 1)]

    def layer_shards(i):
        return [W[n][k].astype(bf16) for n, k, _ in layer_big(i)]

    def even_params(e, w_in, w_out):
        return dict(
            norm_w=W["ev_norm_w"][e][None], w_main=jnp.concatenate([w_in[:, :3 * D], w_in[:, 3 * D + 2 * H:]], axis=1),
            w_dt=jnp.pad(w_in[:, 3 * D:3 * D + 2 * H], ((0, 0), (0, LANES - 2 * H))),
            conv_w=F["ev_conv_w"][e], conv_b=W["ev_conv_b"][e][None],
            bias128=jnp.pad(jnp.concatenate([W["ev_dt_bias_f"][e], W["ev_dt_bias_b"][e]]), (0, LANES - 2 * H))[None],
            alog128=jnp.pad(jnp.concatenate([W["ev_a_log_f"][e], W["ev_a_log_b"][e]]), (0, LANES - 2 * H))[None],
            dskipx=jnp.repeat(W["ev_d_skip"][e], HEAD_P)[None], gnorm_w=W["ev_gnorm_w"][e][None], rpb=W["ev_rpb"][e],
            w_out=w_out)

    def odd_params(e, w_in, w_out):
        return dict(norm_w=F["od_norm_w"][e][None], w_in=w_in, dw_w=F["od_dw_w"][e], dw_b=F["od_dw_b"][e][None],
                    ln_w=F["od_ln_w"][e][None], ln_b=F["od_ln_b"][e][None], w_out=w_out)

    h = h0
    saved = []
    gathered = all_gather(layer_shards(0)[:1], "gather_layer0")
    for i in range(depth):
        e = i // 2
        tag = str(i)
        nxt = layer_shards(i + 1) if i + 1 < depth else []
        if i == 0:
            w_in = _from_slabs(gathered[0], layer_big(0)[0][2])
            P = even_params(e, w_in, None)
            h1, res, exchanged = _even_fwd(h, P, tag, ("gather", layer_shards(0)[1:] + nxt))
            w_gate, w_proj = [_from_slabs(a, ax) for a, (_, _, ax) in zip(exchanged[1:3], layer_big(0)[2:])]
            gathered = exchanged[3:]
        else:
            w_in, w_out, w_gate, w_proj = [_from_slabs(a, ax) for a, (_, _, ax) in zip(gathered, layer_big(i))]
            side = ("gather", nxt) if nxt else None
            if i % 2 == 0:
                P = even_params(e, w_in, w_out)
                h1, res, gathered = _even_fwd(h, P, tag, side)
            else:
                P = odd_params(e, w_in, w_out)
                h1, res, gathered = _odd_fwd(h, P, tag, side)
        h, pres = _ple_fwd(h1, p[i, 0], W["ple_norm_w"][i][None], w_gate, w_proj, tag)
        saved.append((P, res, pres))
    loss_tile, dh, dfinal = final_loss(h, W["final_norm_w"][None], loss_target[0])

    ev_g = [None] * n_even
    od_g = [None] * n_odd
    ple_g = [None] * depth
    recv = [None] * depth
    own = [None] * depth
    pending = None
    for i in reversed(range(depth)):
        P, res, pres = saved[i]
        tag = str(i)
        dh1, dnw, dwg, dwp = _ple_bwd(pres, dh, tag)
        ple_g[i] = dnw.reshape(D)
        side = ("a2a", pending) if pending is not None else None
        if i % 2 == 0:
            early = ([dwg, dwp], [ax for _, _, ax in layer_big(0)[2:]]) if i == 0 else None
            dh, g, exchanged, early_recv, own_recv = _even_bwd(P, res, dh1, tag, side, early)
            ev_g[i // 2] = g
            big_grads = [_ev_w_in_grad(g, D, H), g["w_out"], dwg, dwp]
        else:
            dh, g, exchanged = _odd_bwd(P, res, dh1, tag, side)
            od_g[i // 2] = g
            big_grads = [g["w_in"], g["w_out"], dwg, dwp]
        if pending is not None:
            recv[i + 1] = exchanged
        pending = [_to_slabs(gr.astype(bf16), ax) for gr, (_, _, ax) in zip(big_grads, layer_big(i))]
        own[i] = [lax.dynamic_slice_in_dim(gr, me * (gr.shape[ax] // N_DEV), gr.shape[ax] // N_DEV, axis=ax)
                  for gr, (_, _, ax) in zip(big_grads, layer_big(i))]
    recv[0] = list(own_recv) + list(early_recv)
    grad_x = dh[None]

    full = dict(
        ev_conv_w=jnp.stack([g["conv_w"] for g in ev_g]), od_norm_w=jnp.stack([g["norm_w"] for g in od_g]),
        od_dw_w=jnp.stack([g["dw_w"] for g in od_g]), od_dw_b=jnp.stack([g["dw_b"] for g in od_g]),
        od_ln_w=jnp.stack([g["ln_w"] for g in od_g]), od_ln_b=jnp.stack([g["ln_b"] for g in od_g]),
        ev_norm_w=jnp.stack([g["norm_w"] for g in ev_g]), ev_conv_b=jnp.stack([g["conv_b"] for g in ev_g]),
        ev_dt_bias_f=jnp.stack([g["dt_bias_f"] for g in ev_g]), ev_dt_bias_b=jnp.stack([g["dt_bias_b"] for g in ev_g]),
        ev_a_log_f=jnp.stack([g["a_log_f"] for g in ev_g]), ev_a_log_b=jnp.stack([g["a_log_b"] for g in ev_g]),
        ev_d_skip=jnp.stack([g["d_skip"] for g in ev_g]), ev_gnorm_w=jnp.stack([g["gnorm_w"] for g in ev_g]),
        ev_rpb=jnp.stack([g["rpb"] for g in ev_g]), ple_norm_w=jnp.stack(ple_g), final_norm_w=dfinal.reshape(D))

    out = {}
    where = {}
    for i in range(depth):
        for slot, (n, k, _) in enumerate(layer_big(i)):
            where.setdefault(n, {})[k] = (i, slot)
    for n, _ in BIG:
        shard = W[n].shape
        two_d = (int(np.prod(shard[:-1])), shard[-1])
        layers = [where[n][k] for k in range(shard[0])]
        slabs = jnp.stack([recv[i][slot] for i, slot in layers], axis=1)
        part = jnp.stack([own[i][slot] for i, slot in layers], axis=0)
        res = adamw(W[n].reshape(two_d), slabs.reshape((N_DEV,) + two_d), M[n].reshape(two_d), V[n].reshape(two_d),
                    (me, part.reshape(two_d)), "adamw_" + n)
        for kind, arr in zip(("grad", "delta", "new_m", "new_v"), res):
            out[kind + "_" + n] = arr.reshape(shard)
    small_full = _pack([_to_slabs(full[n], ax) for n, ax in SMALL], SMALL_COLS, f32, (N_DEV,))
    (small_recv,) = all_to_all([small_full], "scatter_small")
    res_small = adamw(_pack([W[n] for n, _ in SMALL], SMALL_COLS, f32), small_recv, _pack([M[n] for n, _ in SMALL], SMALL_COLS, f32),
                      _pack([V[n] for n, _ in SMALL], SMALL_COLS, f32), None, "adamw_small")
    for kind, arr in zip(("grad", "delta", "new_m", "new_v"), res_small):
        for (n, _), a in zip(SMALL, _unpack(arr, small_shapes)):
            out[kind + "_" + n] = a
    repl_shapes = [W[n].shape for n in REPL] + [(1,)]
    zero1 = jnp.zeros((1,), f32)
    repl_part = _pack([full[n] for n in REPL] + [loss_tile[0, :1]], SMALL_COLS, f32)
    (repl_all,) = all_gather([repl_part], "gather_repl")
    res_repl = adamw(_pack([W[n] for n in REPL] + [zero1], SMALL_COLS, f32), repl_all, _pack([M[n] for n in REPL] + [zero1], SMALL_COLS, f32),
                     _pack([V[n] for n in REPL] + [zero1], SMALL_COLS, f32), None, "adamw_repl")
    for kind, arr in zip(("grad", "delta", "new_m", "new_v"), res_repl):
        parts = _unpack(arr, repl_shapes)
        for n, a in zip(REPL, parts[:-1]):
            out[kind + "_" + n] = a
        if kind == "grad":
            loss = parts[-1].reshape(())

    return (loss, grad_x, *[out["grad_" + n] for n in WEIGHTS], *[out["delta_" + n] for n in WEIGHTS],
            *[out["new_m_" + n] for n in WEIGHTS], *[out["new_v_" + n] for n in WEIGHTS])
```

```python
import functools

import numpy as np
import jax
import jax.numpy as jnp
from jax import lax
from jax.experimental import pallas as pl
from jax.experimental.pallas import tpu as pltpu

f32 = jnp.float32
bf16 = jnp.bfloat16
HIGHEST = lax.Precision.HIGHEST

N_DEV = 8
EPS = 1e-6
GRID_W = 64
HEAD_P = 64
HEADS_PER_GROUP = 4
GROUP_W = HEAD_P * HEADS_PER_GROUP
N_STATE = 128
CHUNK = 128
K_SSM = 5
K_CONV = 31
NA_D = 128
NA_WR = 8
NA_WC = 16
PLE_DIM = 256
LANES = 128
CONV_PAD = 16
VMEM_LIMIT = 56 * 1024 * 1024
MM_VMEM_BUDGET = 40 * 1024 * 1024
MM_FULL_K = 2048
NA_ROWS_PER_STEP = 16
ADAMW_TILE_BYTES = 1024 * 1024
SSD_GROUPS_PER_STEP = 8
POST_ROW_TILES = (1024, 512, 256, 128)

ADAM_LR = 0.001
ADAM_B1 = 0.9
ADAM_B2 = 0.999
ADAM_EPS = 1e-08
ADAM_WD = 0.01
ADAM_STEP = 10

NEG_BIG = -1e30


def _params(sem=None):
    if sem is None:
        return pltpu.CompilerParams(vmem_limit_bytes=VMEM_LIMIT)
    return pltpu.CompilerParams(vmem_limit_bytes=VMEM_LIMIT, dimension_semantics=sem)


def _pick(n, prefs):
    for t in prefs:
        if n % t == 0:
            return t
    return n


def _sigmoid(x):
    return jax.nn.sigmoid(x)


def _silu(x):
    return x * jax.nn.sigmoid(x)


_DN = {"nn": (((1,), (0,)), ((), ())), "nt": (((1,), (1,)), ((), ())), "tn": (((0,), (0,)), ((), ()))}


def mm(a, b, mode, name, out_dtype=f32, add=None, side=None):
    if mode == "nn":
        (M, K), (K2, N) = a.shape, b.shape
    elif mode == "nt":
        (M, K), (N, K2) = a.shape, b.shape
    else:
        (K, M), (K2, N) = a.shape, b.shape
    assert K == K2, (a.shape, b.shape, mode)
    sa, sb, so = a.dtype.itemsize, b.dtype.itemsize, jnp.dtype(out_dtype).itemsize
    tk = K if K <= MM_FULL_K else _pick(K, (1024, 512, 256, 128))
    nk = K // tk
    tms = [t for t in (1024, 512, 256, 128) if M % t == 0] or [M]
    tns = [t for t in (2048, 1024, 512, 256, 128) if N % t == 0] or [N]

    def vmem_bytes(tm, tn):
        n = 2 * (tm * tk * sa + tk * tn * sb) + 2 * tm * tn * so
        n += tm * tn * 4 if nk > 1 else 0
        n += 2 * tm * tn * add.dtype.itemsize if add is not None else 0
        return n

    ti, tj = 0, 0
    while vmem_bytes(tms[ti], tns[tj]) > MM_VMEM_BUDGET:
        if tj + 1 < len(tns) and (tns[tj] >= tms[ti] or ti + 1 >= len(tms)):
            tj += 1
        elif ti + 1 < len(tms):
            ti += 1
        else:
            break
    tm, tn = tms[ti], tns[tj]
    a_bytes, b_bytes = M * K * sa, K * N * sb
    i_outer = a_bytes + b_bytes * (M // tm) <= b_bytes + a_bytes * (N // tn)
    dn = _DN[mode]

    grid = (M // tm, N // tn, nk) if i_outer else (N // tn, M // tm, nk)
    ns = len(side[1]) if side is not None else 0
    n_add = 1 if add is not None else 0

    def body(a_ref, b_ref, *rest):
        add_ref = rest[0] if add is not None else None
        x_refs = rest[n_add:n_add + ns]
        o_ref = rest[n_add + ns]
        out_refs = rest[n_add + ns + 1:n_add + 2 * ns + 1]
        scratch = rest[n_add + 2 * ns + 1:]
        if side is not None:
            step = (pl.program_id(0) * grid[1] + pl.program_id(1)) * nk + pl.program_id(2)

            @pl.when(step == 0)
            def _():
                _exchange_start(side[0], x_refs, out_refs, scratch[-3:])

        def finish(r):
            if add is not None:
                r = r + add_ref[...].astype(f32)
            o_ref[...] = r.astype(out_dtype)

        part = lax.dot_general(a_ref[...].astype(bf16), b_ref[...].astype(bf16), dn, preferred_element_type=f32)
        if nk == 1:
            finish(part)
        else:
            acc = scratch[0]
            k = pl.program_id(2)

            @pl.when(k == 0)
            def _():
                acc[...] = part

            @pl.when(k > 0)
            def _():
                acc[...] += part

            @pl.when(k == nk - 1)
            def _():
                finish(acc[...])

        if side is not None:
            @pl.when(step == grid[0] * grid[1] * nk - 1)
            def _():
                _exchange_finish(side[0], x_refs, out_refs, scratch[-3:])

    def ij(p, q):
        return (p, q) if i_outer else (q, p)

    if mode == "tn":
        a_spec = pl.BlockSpec((tk, tm), lambda p, q, k: (k, ij(p, q)[0]))
    else:
        a_spec = pl.BlockSpec((tm, tk), lambda p, q, k: (ij(p, q)[0], k))
    if mode == "nt":
        b_spec = pl.BlockSpec((tn, tk), lambda p, q, k: (ij(p, q)[1], k))
    else:
        b_spec = pl.BlockSpec((tk, tn), lambda p, q, k: (k, ij(p, q)[1]))
    o_spec = pl.BlockSpec((tm, tn), lambda p, q, k: ij(p, q))
    in_specs = [a_spec, b_spec] + ([o_spec] if add is not None else [])
    args = (a, b) + ((add,) if add is not None else ())
    acc_scratch = [pltpu.VMEM((tm, tn), f32)] if nk > 1 else []
    o_shape = jax.ShapeDtypeStruct((M, N), out_dtype)
    if side is None:
        return pl.pallas_call(
            body, name=name, grid=grid, in_specs=in_specs, out_specs=o_spec, out_shape=o_shape, scratch_shapes=acc_scratch,
            compiler_params=_params(("parallel", "parallel", "arbitrary")))(*args)
    res = pl.pallas_call(
        body, name=name, grid=grid, in_specs=in_specs + [_HBM] * ns, out_specs=[o_spec] + [_HBM] * ns,
        out_shape=[o_shape] + _exchange_out_shapes(side[0], side[1]), scratch_shapes=acc_scratch + _exchange_scratch(ns),
        compiler_params=_params(("arbitrary", "arbitrary", "arbitrary")))(*args, *side[1])
    return res[0], res[1:]


def _row_spec(T, w, base):
    return pl.BlockSpec((T, w), lambda g, i: (i, base + g))


def _const_spec(r, w, base):
    return pl.BlockSpec((r, w), lambda g, i: (0, base + g))


def rowop_fwd(fn, name, rows, consts, outs, T, ncb=1):
    L = rows[0][0].shape[0]
    nr, nc = len(rows), len(consts)

    def body(*refs):
        ins = [r[...].astype(f32) for r in refs[:nr + nc]]
        res = fn(*ins)
        for o_ref, r in zip(refs[nr + nc:], res):
            o_ref[...] = r.astype(o_ref.dtype)

    in_specs = [_row_spec(T, w, b) for (_, w, b) in rows] + [_const_spec(a.shape[0], w, b) for (a, w, b) in consts]
    out_specs = [_row_spec(T, w, 0) for (w, _) in outs]
    out_shape = [jax.ShapeDtypeStruct((L, ncb * w), dt) for (w, dt) in outs]
    return pl.pallas_call(
        body, name=name, grid=(ncb, L // T), in_specs=in_specs, out_specs=out_specs, out_shape=out_shape,
        compiler_params=_params(("parallel", "parallel")))(*[a for (a, _, _) in rows], *[a for (a, _, _) in consts])


def rowop_bwd(fn, name, rows, consts, cts, row_grads, const_grads, T, ncb=1, add=None):
    L = rows[0][0].shape[0]
    nr, nc, nct = len(rows), len(consts), len(cts)
    n_in = nr + nc + nct + (1 if add is not None else 0)
    rg_idx = [i for i, d in enumerate(row_grads) if d is not None]
    cg_idx = [i for i, d in enumerate(const_grads) if d]

    def body(*refs):
        ins = [r[...].astype(f32) for r in refs[:nr + nc]]
        ct = tuple(r[...].astype(f32) for r in refs[nr + nc:nr + nc + nct])
        _, vjp = jax.vjp(fn, *ins)
        grads = vjp(ct)
        outs = refs[n_in:]
        for n, i in enumerate(rg_idx):
            gi = grads[i]
            if add is not None and n == 0:
                gi = gi + refs[n_in - 1][...].astype(f32)
            outs[n][...] = gi.astype(outs[n].dtype)
        first = pl.program_id(1) == 0
        for n, i in enumerate(cg_idx):
            o = outs[len(rg_idx) + n]

            @pl.when(first)
            def _(o=o):
                o[...] = jnp.zeros_like(o)

            o[...] += grads[nr + i]

    in_specs = ([_row_spec(T, w, b) for (_, w, b) in rows] + [_const_spec(a.shape[0], w, b) for (a, w, b) in consts]
                + [_row_spec(T, w, b) for (_, w, b) in cts] + ([_row_spec(T, add[1], add[2])] if add is not None else []))
    out_specs = ([_row_spec(T, rows[i][1], 0) for i in rg_idx] + [_const_spec(consts[i][0].shape[0], consts[i][1], 0) for i in cg_idx])
    out_shape = ([jax.ShapeDtypeStruct((L, ncb * rows[i][1]), row_grads[i]) for i in rg_idx]
                 + [jax.ShapeDtypeStruct((consts[i][0].shape[0], ncb * consts[i][1]), f32) for i in cg_idx])
    args = [a for (a, _, _) in rows] + [a for (a, _, _) in consts] + [a for (a, _, _) in cts] + ([add[0]] if add is not None else [])
    return pl.pallas_call(
        body, name=name, grid=(ncb, L // T), in_specs=in_specs, out_specs=out_specs, out_shape=out_shape,
        compiler_params=_params(("parallel", "arbitrary")))(*args)


def _rms_fn(x, w):
    return (x * lax.rsqrt(jnp.mean(x * x, axis=-1, keepdims=True) + EPS) * w,)


def _glu_fn(a, ag):
    return (a * _sigmoid(ag),)


def _gate_fn(o, g):
    return (o * _silu(g),)


def _ple_fn(gpre, pp):
    return (_sigmoid(gpre) * pp,)


def _ple_fwd_fn(h1, gpre, pp):
    return (h1 + _sigmoid(gpre) * pp,)


def _lngate_fn(vc, g, w, b):
    mu = jnp.mean(vc, axis=-1, keepdims=True)
    xc = vc - mu
    y = xc * lax.rsqrt(jnp.mean(xc * xc, axis=-1, keepdims=True) + EPS) * w + b
    return (_silu(y) * _silu(g),)


def _post_fn(yf, yb, xs, z, dsk, gw):
    y = (yf + yb + dsk * xs) * _silu(z)
    return (y * lax.rsqrt(jnp.mean(y * y, axis=-1, keepdims=True) + EPS) * gw,)


def _make_prep_fn(n_heads):
    def prep_fn(u, bias, alog):
        lane = lax.broadcasted_iota(jnp.int32, (1, LANES), 1)
        dt = jnp.where(lane < 2 * n_heads, jax.nn.softplus(u + bias), 0.0)
        da = dt * (-jnp.exp(alog))
        li = lax.broadcasted_iota(jnp.int32, (CHUNK, CHUNK), 0)
        si = lax.broadcasted_iota(jnp.int32, (CHUNK, CHUNK), 1)
        tril = (si <= li).astype(f32)
        triu = (si >= li).astype(f32)
        csf = jnp.dot(tril, da, precision=HIGHEST, preferred_element_type=f32)
        csb = jnp.dot(triu, da, precision=HIGHEST, preferred_element_type=f32)
        return dt, jnp.where(lane < n_heads, csf, csb)

    return prep_fn


def final_loss(h, w, tgt, name="final_loss"):
    L, D = h.shape
    T = _pick(L, (256, 128))

    def fn(x, w_, t):
        y = x * lax.rsqrt(jnp.mean(x * x, axis=-1, keepdims=True) + EPS) * w_
        err = jnp.square(y - t)
        return 0.5 * jnp.sum(jnp.mean(err, axis=-1, keepdims=True))

    def body(h_ref, w_ref, t_ref, loss_ref, dh_ref, dw_ref):
        @pl.when(pl.program_id(0) == 0)
        def _():
            loss_ref[...] = jnp.zeros_like(loss_ref)
            dw_ref[...] = jnp.zeros_like(dw_ref)

        t = t_ref[...]
        val, vjp = jax.vjp(lambda x, w_: fn(x, w_, t), h_ref[...], w_ref[...])
        dx, dw = vjp(jnp.ones((), f32))
        dh_ref[...] = dx
        dw_ref[...] += dw
        loss_ref[...] += jnp.broadcast_to(val, loss_ref.shape)

    return pl.pallas_call(
        body, name=name, grid=(L // T,),
        in_specs=[pl.BlockSpec((T, D), lambda i: (i, 0)), pl.BlockSpec((1, D), lambda i: (0, 0)), pl.BlockSpec((T, D), lambda i: (i, 0))],
        out_specs=[pl.BlockSpec((8, LANES), lambda i: (0, 0)), pl.BlockSpec((T, D), lambda i: (i, 0)), pl.BlockSpec((1, D), lambda i: (0, 0))],
        out_shape=[jax.ShapeDtypeStruct((8, LANES), f32), jax.ShapeDtypeStruct((L, D), f32), jax.ShapeDtypeStruct((1, D), f32)],
        compiler_params=_params(("arbitrary",)))(h, w, tgt)


CONV_ROWS = 128


def conv_fwd(x_arr, xbase, C, w, b, K, act, name):
    L = x_arr.shape[0]
    R = CONV_ROWS
    lo = CONV_PAD - K // 2

    def body(x_ref, w_ref, b_ref, *rest):
        xpad = rest[-1]
        outs = rest[:-1]
        xpad[0:CONV_PAD, :] = jnp.zeros((CONV_PAD, LANES), f32)
        xpad[CONV_PAD + L:CONV_PAD + L + CONV_PAD, :] = jnp.zeros((CONV_PAD, LANES), f32)
        xpad[CONV_PAD:CONV_PAD + L, :] = x_ref[...]

        def chunk(ci, carry):
            r0 = pl.multiple_of(ci * R, R)
            acc = jnp.broadcast_to(b_ref[...], (R, LANES))
            for k in range(K):
                acc = acc + w_ref[k:k + 1, :] * xpad[pl.ds(r0 + (lo + k), R), :]
            outs[0][pl.ds(r0, R), :] = acc
            if act:
                outs[1][pl.ds(r0, R), :] = _silu(acc)
            return carry

        lax.fori_loop(0, L // R, chunk, 0)

    col = pl.BlockSpec((L, LANES), lambda j: (0, j))
    n_out = 2 if act else 1
    return pl.pallas_call(
        body, name=name, grid=(C // LANES,),
        in_specs=[pl.BlockSpec((L, LANES), lambda j: (0, xbase + j)), pl.BlockSpec((K, LANES), lambda j: (0, j)),
                  pl.BlockSpec((1, LANES), lambda j: (0, j))],
        out_specs=[col] * n_out, out_shape=[jax.ShapeDtypeStruct((L, C), f32)] * n_out,
        scratch_shapes=[pltpu.VMEM((L + 2 * CONV_PAD, LANES), f32)],
        compiler_params=_params(("parallel",)))(x_arr, w, b)


def conv_bwd(dy, x_arr, xbase, C, w, K, pre, out_dtype, name):
    L = x_arr.shape[0]
    R = CONV_ROWS
    lo = CONV_PAD - K // 2
    act = pre is not None

    def body(*refs):
        if act:
            dy_ref, pre_ref, x_ref, w_ref, dx_ref, dw_ref, db_ref, dpad, xpad, accw, accb = refs
        else:
            dy_ref, x_ref, w_ref, dx_ref, dw_ref, db_ref, dpad, xpad, accw, accb = refs
        zeros = jnp.zeros((CONV_PAD, LANES), f32)
        for pad in (dpad, xpad):
            pad[0:CONV_PAD, :] = zeros
            pad[CONV_PAD + L:CONV_PAD + L + CONV_PAD, :] = zeros
        xpad[CONV_PAD:CONV_PAD + L, :] = x_ref[...]
        if act:
            p = pre_ref[...]
            s = _sigmoid(p)
            dpad[CONV_PAD:CONV_PAD + L, :] = dy_ref[...].astype(f32) * (s * (1.0 + p * (1.0 - s)))
        else:
            dpad[CONV_PAD:CONV_PAD + L, :] = dy_ref[...].astype(f32)
        accw[...] = jnp.zeros_like(accw)
        accb[...] = jnp.zeros_like(accb)

        def chunk(ci, carry):
            r0 = pl.multiple_of(ci * R, R)
            acc = jnp.zeros((R, LANES), f32)
            for k in range(K):
                acc = acc + w_ref[k:k + 1, :] * dpad[pl.ds(r0 + (CONV_PAD + K // 2 - k), R), :]
            dx_ref[pl.ds(r0, R), :] = acc.astype(out_dtype)
            d = dpad[pl.ds(r0 + CONV_PAD, R), :]
            accb[...] += jnp.sum(d.reshape(R // 8, 8, LANES), axis=0)
            for k in range(K):
                prod = d * xpad[pl.ds(r0 + (lo + k), R), :]
                accw[k] += jnp.sum(prod.reshape(R // 8, 8, LANES), axis=0)
            return carry

        lax.fori_loop(0, L // R, chunk, 0)
        dw_ref[...] = jnp.sum(accw[...], axis=1)
        db_ref[...] = jnp.sum(accb[...], axis=0, keepdims=True)

    col = pl.BlockSpec((L, LANES), lambda j: (0, j))
    in_specs = [col] + ([col] if act else []) + [pl.BlockSpec((L, LANES), lambda j: (0, xbase + j)), pl.BlockSpec((K, LANES), lambda j: (0, j))]
    args = (dy,) + ((pre,) if act else ()) + (x_arr, w)
    return pl.pallas_call(
        body, name=name, grid=(C // LANES,), in_specs=in_specs,
        out_specs=[col, pl.BlockSpec((K, LANES), lambda j: (0, j)), pl.BlockSpec((1, LANES), lambda j: (0, j))],
        out_shape=[jax.ShapeDtypeStruct((L, C), out_dtype), jax.ShapeDtypeStruct((K, C), f32), jax.ShapeDtypeStruct((1, C), f32)],
        scratch_shapes=[pltpu.VMEM((L + 2 * CONV_PAD, LANES), f32), pltpu.VMEM((L + 2 * CONV_PAD, LANES), f32),
                        pltpu.VMEM((K, 8, LANES), f32), pltpu.VMEM((8, LANES), f32)],
        compiler_params=_params(("parallel",)))(*args)


def _ssd_chunk(X, Bm, Cm, dtc, csc, csr, S, lane_off, dirn):
    Qn = X.shape[0]
    lane = lax.broadcasted_iota(jnp.int32, (1, LANES), 1)
    head = lax.shift_right_logical(lax.broadcasted_iota(jnp.int32, (1, GROUP_W), 1), 6)
    dtx = jnp.zeros((Qn, GROUP_W), f32)
    csx = jnp.zeros((Qn, GROUP_W), f32)
    cols = []
    for j in range(HEADS_PER_GROUP):
        sel = lane == lane_off + j
        dcol = jnp.sum(jnp.where(sel, dtc, 0.0), axis=1, keepdims=True)
        ccol = jnp.sum(jnp.where(sel, csc, 0.0), axis=1, keepdims=True)
        dtx = jnp.where(head == j, dcol, dtx)
        csx = jnp.where(head == j, ccol, csx)
        cols.append(ccol)
    Xd = X * dtx
    edge = Qn - 1 if dirn == 0 else 0
    cs_edge = csx[edge:edge + 1, :]
    li = lax.broadcasted_iota(jnp.int32, (Qn, Qn), 0)
    si = lax.broadcasted_iota(jnp.int32, (Qn, Qn), 1)
    mask = (si <= li) if dirn == 0 else (si >= li)
    Cb = Cm.astype(bf16)
    Bb = Bm.astype(bf16)
    G = lax.dot_general(Cb, Bb, _DN["nt"], preferred_element_type=f32)
    Y = jnp.exp(csx) * jnp.dot(Cb, S.astype(bf16), preferred_element_type=f32)
    for j in range(HEADS_PER_GROUP):
        row = csr[j:j + 1, :]
        Lm = jnp.exp(jnp.where(mask, cols[j] - row, NEG_BIG))
        Wj = (G * Lm).astype(bf16)
        Xj = jnp.where(head == j, Xd, 0.0).astype(bf16)
        Y = Y + jnp.dot(Wj, Xj, preferred_element_type=f32)
    S_new = S * jnp.exp(cs_edge) + lax.dot_general(Bb, (Xd * jnp.exp(cs_edge - csx)).astype(bf16), _DN["tn"], preferred_element_type=f32)
    return Y, S_new


def _ssd_specs(D, G, nchunk, dirn, descending):
    def cidx(c):
        return nchunk - 1 - c if descending else c

    gps = _ssd_groups_per_step(D, G)
    nb = D // N_STATE
    x_spec = pl.BlockSpec((CHUNK, gps * GROUP_W), lambda c, g: (cidx(c), g))
    b_spec = pl.BlockSpec((CHUNK, gps * N_STATE), lambda c, g: (cidx(c), nb // gps + g))
    c_spec = pl.BlockSpec((CHUNK, gps * N_STATE), lambda c, g: (cidx(c), (nb + G) // gps + g))
    n_spec = pl.BlockSpec((CHUNK, gps * N_STATE), lambda c, g: (cidx(c), g))
    lane_spec = pl.BlockSpec((CHUNK, LANES), lambda c, g: (cidx(c), 0))
    csr_spec = pl.BlockSpec((gps, 8, CHUNK), lambda c, g: (dirn * (G // gps) + g, 0, cidx(c)))
    dcsr_spec = pl.BlockSpec((gps, 8, CHUNK), lambda c, g: (g, 0, cidx(c)))
    s_spec = pl.BlockSpec((gps, None, N_STATE, GROUP_W), lambda c, g: (g, cidx(c), 0, 0))
    return x_spec, b_spec, c_spec, n_spec, lane_spec, csr_spec, dcsr_spec, s_spec


def _ssd_groups_per_step(D, G):
    nb = D // N_STATE
    for n in (SSD_GROUPS_PER_STEP, 2):
        if G % n == 0 and nb % n == 0:
            return n
    return 1


def ssd_fwd(act, dt128, cs128, csr, dirn, n_heads, name):
    L = act.shape[0]
    D = act.shape[1] // 2
    G = D // GROUP_W
    nchunk = L // CHUNK
    x_spec, b_spec, c_spec, _, lane_spec, csr_spec, _, s_spec = _ssd_specs(D, G, nchunk, dirn, dirn == 1)

    gps = _ssd_groups_per_step(D, G)

    def body(x_ref, b_ref, c_ref, dt_ref, cs_ref, csr_ref, y_ref, ssave_ref, S):
        c = pl.program_id(0)
        for gg in range(gps):
            g = pl.program_id(1) * gps + gg

            @pl.when(c == 0)
            def _():
                S[g] = jnp.zeros((N_STATE, GROUP_W), f32)

            xs_, ns_ = pl.ds(gg * GROUP_W, GROUP_W), pl.ds(gg * N_STATE, N_STATE)
            s_in = S[g]
            ssave_ref[gg] = s_in
            y, s_new = _ssd_chunk(x_ref[:, xs_], b_ref[:, ns_], c_ref[:, ns_], dt_ref[...], cs_ref[...], csr_ref[gg], s_in,
                                  dirn * n_heads + HEADS_PER_GROUP * g, dirn)
            y_ref[:, xs_] = y
            S[g] = s_new

    return pl.pallas_call(
        body, name=name, grid=(nchunk, G // gps),
        in_specs=[x_spec, b_spec, c_spec, lane_spec, lane_spec, csr_spec],
        out_specs=[x_spec, s_spec],
        out_shape=[jax.ShapeDtypeStruct((L, D), f32), jax.ShapeDtypeStruct((G, nchunk, N_STATE, GROUP_W), f32)],
        scratch_shapes=[pltpu.VMEM((G, N_STATE, GROUP_W), f32)],
        compiler_params=_params(("arbitrary", "arbitrary")))(act, act, act, dt128, cs128, csr)


def ssd_bwd(act, dt128, cs128, csr, ssave, dy, dirn, n_heads, name):
    L = act.shape[0]
    D = act.shape[1] // 2
    G = D // GROUP_W
    nchunk = L // CHUNK
    x_spec, b_spec, c_spec, n_spec, lane_spec, csr_in, dcsr_spec, s_spec = _ssd_specs(D, G, nchunk, dirn, dirn == 0)

    gps = _ssd_groups_per_step(D, G)

    def body(x_ref, b_ref, c_ref, dt_ref, cs_ref, csr_ref, s_ref, dy_ref, dx_ref, db_ref, dc_ref, ddt_ref, dcs_ref, dcsr_ref, dS):
        c = pl.program_id(0)

        @pl.when(pl.program_id(1) == 0)
        def _():
            ddt_ref[...] = jnp.zeros_like(ddt_ref)
            dcs_ref[...] = jnp.zeros_like(dcs_ref)

        ddt_sum, dcs_sum = None, None
        for gg in range(gps):
            g = pl.program_id(1) * gps + gg

            @pl.when(c == 0)
            def _():
                dS[g] = jnp.zeros((N_STATE, GROUP_W), f32)

            xs_, ns_ = pl.ds(gg * GROUP_W, GROUP_W), pl.ds(gg * N_STATE, N_STATE)
            lane_off = dirn * n_heads + HEADS_PER_GROUP * g
            _, vjp = jax.vjp(lambda X, Bm, Cm, dtc, csc, csr_, S: _ssd_chunk(X, Bm, Cm, dtc, csc, csr_, S, lane_off, dirn),
                             x_ref[:, xs_], b_ref[:, ns_], c_ref[:, ns_], dt_ref[...], cs_ref[...], csr_ref[gg], s_ref[gg])
            dX, dB, dC, ddt, dcs, dcsr, dS_in = vjp((dy_ref[:, xs_], dS[g]))
            dx_ref[:, xs_] = dX
            db_ref[:, ns_] = dB
            dc_ref[:, ns_] = dC
            dcsr_ref[gg] = dcsr
            dS[g] = dS_in
            ddt_sum = ddt if ddt_sum is None else ddt_sum + ddt
            dcs_sum = dcs if dcs_sum is None else dcs_sum + dcs

        ddt_ref[...] += ddt_sum
        dcs_ref[...] += dcs_sum

    return pl.pallas_call(
        body, name=name, grid=(nchunk, G // gps),
        in_specs=[x_spec, b_spec, c_spec, lane_spec, lane_spec, csr_in, s_spec, x_spec],
        out_specs=[x_spec, n_spec, n_spec, lane_spec, lane_spec, dcsr_spec],
        out_shape=[jax.ShapeDtypeStruct((L, D), f32), jax.ShapeDtypeStruct((L, G * N_STATE), f32), jax.ShapeDtypeStruct((L, G * N_STATE), f32),
                   jax.ShapeDtypeStruct((L, LANES), f32), jax.ShapeDtypeStruct((L, LANES), f32), jax.ShapeDtypeStruct((G, 8, L), f32)],
        scratch_shapes=[pltpu.VMEM((G, N_STATE, GROUP_W), f32)],
        compiler_params=_params(("arbitrary", "arbitrary")))(act, act, act, dt128, cs128, csr, ssave, dy)


def _csr_from_cs(cs128, n_heads):
    L = cs128.shape[0]
    t = cs128[:, :2 * n_heads].T.reshape(2 * n_heads // HEADS_PER_GROUP, HEADS_PER_GROUP, L)
    return jnp.pad(t, ((0, 0), (0, 8 - HEADS_PER_GROUP), (0, 0)))


def _cs_from_dcsr(dcsr_f, dcsr_b, n_heads):
    L = dcsr_f.shape[-1]
    t = jnp.concatenate([dcsr_f, dcsr_b], axis=0)[:, :HEADS_PER_GROUP, :].reshape(2 * n_heads, L).T
    return jnp.pad(t, ((0, 0), (0, LANES - 2 * n_heads)))


def na_bias_table(rpb):
    nh = rpb.shape[0]
    cols = np.arange(GRID_W)
    col_start = np.clip(cols - NA_WC // 2, 0, GRID_W - NA_WC)
    col_valid = (cols[None, :] >= col_start[:, None]) & (cols[None, :] < col_start[:, None] + NA_WC)
    col_off = np.clip(cols[None, :] - cols[:, None], -(NA_WC - 1), NA_WC - 1) + NA_WC - 1
    rm = np.zeros((NA_WR, NA_WR, 2 * NA_WR - 1), np.float32)
    for d in range(NA_WR):
        for i in range(NA_WR):
            rm[d, i, i - d + NA_WR - 1] = 1.0
    cm = np.zeros((GRID_W, GRID_W, 2 * NA_WC - 1), np.float32)
    cm[np.arange(GRID_W)[:, None], np.arange(GRID_W)[None, :], col_off] = 1.0
    bt = jnp.einsum("hrc,dir,qkc->hdqik", rpb, jnp.asarray(rm), jnp.asarray(cm), precision=HIGHEST)
    bt = jnp.where(jnp.asarray(col_valid)[None, None, :, None, :], bt, NEG_BIG)
    return bt.reshape(nh, NA_WR, GRID_W, NA_WR * GRID_W)


def _na_window(r, rows):
    rs = jnp.clip(r - NA_WR // 2, 0, rows - NA_WR)
    return rs, r - rs


def _na_probs(q, kw, bias):
    s = lax.dot_general(q.astype(bf16), kw.astype(bf16), _DN["nt"], preferred_element_type=f32) * (NA_D ** -0.5) + bias
    m = jnp.max(s, axis=-1, keepdims=True)
    p = jnp.exp(s - m)
    return p / jnp.sum(p, axis=-1, keepdims=True)


def na_fwd(um, qbase, kbase, vbase, bt, name):
    L = um.shape[0]
    nh = bt.shape[0]
    rows = L // GRID_W
    WK = NA_WR * GRID_W

    RB = _pick(rows, (NA_ROWS_PER_STEP, 2, 1))

    def body(q_ref, k_ref, v_ref, bt_ref, o_ref):
        win = [_na_window(pl.program_id(1) * RB + rb, rows) for rb in range(RB)]
        t0s = [pl.multiple_of(rs * GRID_W, GRID_W) for rs, _ in win]
        qrows = [pl.ds(rb * GRID_W, GRID_W) for rb in range(RB)]
        ps = [_na_probs(q_ref[qr, :], k_ref[pl.ds(t0, WK), :], bt_ref[delta]) for qr, t0, (_, delta) in zip(qrows, t0s, win)]
        os_ = [jnp.dot(p.astype(bf16), v_ref[pl.ds(t0, WK), :].astype(bf16), preferred_element_type=f32) for p, t0 in zip(ps, t0s)]
        for qr, o in zip(qrows, os_):
            o_ref[qr, :] = o

    kv = lambda base: pl.BlockSpec((L, NA_D), lambda h, r: (0, base + h))
    return pl.pallas_call(
        body, name=name, grid=(nh, rows // RB),
        in_specs=[pl.BlockSpec((RB * GRID_W, NA_D), lambda h, r: (r, qbase + h)), kv(kbase), kv(vbase),
                  pl.BlockSpec((None, NA_WR, GRID_W, WK), lambda h, r: (h, 0, 0, 0))],
        out_specs=pl.BlockSpec((RB * GRID_W, NA_D), lambda h, r: (r, h)),
        out_shape=jax.ShapeDtypeStruct((L, nh * NA_D), f32),
        compiler_params=_params(("parallel", "arbitrary")))(um, um, um, bt)


def na_bwd(um, qbase, kbase, vbase, bt, do, name, side=None):
    L = um.shape[0]
    nh = bt.shape[0]
    rows = L // GRID_W
    WK = NA_WR * GRID_W
    scale = NA_D ** -0.5

    RB = _pick(rows, (NA_ROWS_PER_STEP, 2, 1))

    ns = len(side[1]) if side is not None else 0
    nsteps = rows // RB

    def body(q_ref, k_ref, v_ref, bt_ref, do_ref, *rest):
        x_refs = rest[:ns]
        dq_ref, dk_ref, dv_ref, dbt_ref = rest[ns:ns + 4]
        out_refs = rest[ns + 4:2 * ns + 4]
        sems = rest[2 * ns + 4:]
        if side is not None:
            step = pl.program_id(0) * nsteps + pl.program_id(1)

            @pl.when(step == 0)
            def _():
                _exchange_start(side[0], x_refs, out_refs, sems)

        @pl.when(pl.program_id(1) == 0)
        def _():
            dk_ref[...] = jnp.zeros_like(dk_ref)
            dv_ref[...] = jnp.zeros_like(dv_ref)
            dbt_ref[...] = jnp.zeros_like(dbt_ref)

        win = [_na_window(pl.program_id(1) * RB + rb, rows) for rb in range(RB)]
        t0s = [pl.multiple_of(rs * GRID_W, GRID_W) for rs, _ in win]
        qrows = [pl.ds(rb * GRID_W, GRID_W) for rb in range(RB)]
        qs_ = [q_ref[qr, :] for qr in qrows]
        kws = [k_ref[pl.ds(t0, WK), :].astype(bf16) for t0 in t0s]
        vws = [v_ref[pl.ds(t0, WK), :].astype(bf16) for t0 in t0s]
        dos = [do_ref[qr, :].astype(bf16) for qr in qrows]
        ps = [_na_probs(q, kw, bt_ref[delta]) for q, kw, (_, delta) in zip(qs_, kws, win)]
        dps = [lax.dot_general(do_, vw, _DN["nt"], preferred_element_type=f32) for do_, vw in zip(dos, vws)]
        dss = [p * (dp - jnp.sum(dp * p, axis=-1, keepdims=True)) for p, dp in zip(ps, dps)]
        dsbs = [ds.astype(bf16) for ds in dss]
        dqs = [jnp.dot(dsb, kw, preferred_element_type=f32) * scale for dsb, kw in zip(dsbs, kws)]
        dks = [lax.dot_general(dsb, q.astype(bf16), _DN["tn"], preferred_element_type=f32) * scale for dsb, q in zip(dsbs, qs_)]
        dvs = [lax.dot_general(p.astype(bf16), do_, _DN["tn"], preferred_element_type=f32) for p, do_ in zip(ps, dos)]
        for rb in range(RB):
            dq_ref[qrows[rb], :] = dqs[rb]
            dbt_ref[win[rb][1]] += dss[rb]
            dk_ref[pl.ds(t0s[rb], WK), :] += dks[rb]
            dv_ref[pl.ds(t0s[rb], WK), :] += dvs[rb]

        if side is not None:
            @pl.when(step == nh * nsteps - 1)
            def _():
                _exchange_finish(side[0], x_refs, out_refs, sems)

    kv = lambda base: pl.BlockSpec((L, NA_D), lambda h, r: (0, base + h))
    qs = lambda base: pl.BlockSpec((RB * GRID_W, NA_D), lambda h, r: (r, base + h))
    bts = pl.BlockSpec((None, NA_WR, GRID_W, WK), lambda h, r: (h, 0, 0, 0))
    big = jax.ShapeDtypeStruct((L, nh * NA_D), f32)
    in_specs = [qs(qbase), kv(kbase), kv(vbase), bts, qs(0)]
    out_specs = [qs(0), kv(0), kv(0), bts]
    out_shape = [big, big, big, jax.ShapeDtypeStruct(bt.shape, f32)]
    if side is None:
        return pl.pallas_call(
            body, name=name, grid=(nh, nsteps), in_specs=in_specs, out_specs=out_specs, out_shape=out_shape,
            compiler_params=_params(("parallel", "arbitrary")))(um, um, um, bt, do)
    res = pl.pallas_call(
        body, name=name, grid=(nh, nsteps), in_specs=in_specs + [_HBM] * ns, out_specs=out_specs + [_HBM] * ns,
        out_shape=out_shape + _exchange_out_shapes(side[0], side[1]), scratch_shapes=_exchange_scratch(ns),
        compiler_params=_params(("arbitrary", "arbitrary")))(um, um, um, bt, do, *side[1])
    return tuple(res[:4]) + (res[4:],)


_HBM = pl.BlockSpec(memory_space=pltpu.HBM)


def all_gather(xs, name):
    n = len(xs)

    def body(*refs):
        _exchange_start("gather", refs[:n], refs[n:2 * n], refs[2 * n:])
        _exchange_finish("gather", refs[:n], refs[n:2 * n], refs[2 * n:])

    return pl.pallas_call(
        body, name=name, in_specs=[_HBM] * n, out_specs=[_HBM] * n, out_shape=_exchange_out_shapes("gather", xs),
        scratch_shapes=_exchange_scratch(n))(*xs)


def _exchange_out_shapes(kind, xs):
    return [jax.ShapeDtypeStruct(((N_DEV,) + x.shape) if kind == "gather" else x.shape, x.dtype) for x in xs]


def _exchange_scratch(n):
    return [pltpu.SemaphoreType.DMA((7 * n,)), pltpu.SemaphoreType.DMA((7 * n,)), pltpu.SemaphoreType.DMA((n,))]


def _gather_copies(x_refs, out_refs, sems):
    send_sems, recv_sems, local_sems = sems
    n = len(x_refs)
    mx, my, mc = lax.axis_index("x"), lax.axis_index("y"), lax.axis_index("c")
    me, sibling = (mx, my, mc), (mx, my, 1 - mc)
    chips = [(1 - mx, my), (mx, 1 - my), (1 - mx, 1 - my)]

    def slab(t, px, py, pc):
        return out_refs[t].at[4 * px + 2 * py + pc]

    def copy(t, k, block, to, src=None):
        return pltpu.make_async_remote_copy(
            src_ref=slab(t, *block) if src is None else src, dst_ref=slab(t, *block),
            send_sem=send_sems.at[7 * t + k], recv_sem=recv_sems.at[7 * t + k], device_id=to, device_id_type=pl.DeviceIdType.MESH)

    mine = [pltpu.make_async_copy(x_refs[t], slab(t, *me), local_sems.at[t]) for t in range(n)]
    first = []
    for t in range(n):
        first.append(copy(t, 0, me, sibling, src=x_refs[t]))
        first += [copy(t, 1 + j, me, (*chip, mc), src=x_refs[t]) for j, chip in enumerate(chips)]
    return mine, first, copy, me, sibling, chips, mc


def _exchange_start(kind, x_refs, out_refs, sems):
    if kind == "gather":
        mine, first = _gather_copies(x_refs, out_refs, sems)[:2]
    else:
        mine, first = _a2a_copies(x_refs, out_refs, sems)
    for cp in mine + first:
        cp.start()


def _exchange_finish(kind, x_refs, out_refs, sems):
    n = len(x_refs)
    if kind == "gather":
        mine, first, copy, me, sibling, chips, mc = _gather_copies(x_refs, out_refs, sems)
        passed = []
        for t in range(n):
            for j, chip in enumerate(chips):
                copy(t, 1 + j, (*chip, mc), me).wait_recv()
                fwd = copy(t, 4 + j, (*chip, mc), sibling)
                fwd.start()
                passed.append(fwd)
        for t in range(n):
            copy(t, 0, sibling, me).wait_recv()
            for j, chip in enumerate(chips):
                copy(t, 4 + j, (*chip, 1 - mc), me).wait_recv()
        for cp in first + passed:
            cp.wait_send()
    else:
        mine, first = _a2a_copies(x_refs, out_refs, sems)
        for cp in first:
            cp.wait()
    for cp in mine:
        cp.wait()


def _a2a_copies(x_refs, out_refs, sems):
    send_sems, recv_sems, local_sems = sems
    n = len(x_refs)
    mx, my, mc = lax.axis_index("x"), lax.axis_index("y"), lax.axis_index("c")
    me = 4 * mx + 2 * my + mc
    mine = [pltpu.make_async_copy(x_refs[t].at[me], out_refs[t].at[me], local_sems.at[t]) for t in range(n)]
    copies = []
    for t in range(n):
        for k in range(1, N_DEV):
            px = 1 - mx if k & 4 else mx
            py = 1 - my if k & 2 else my
            pc = 1 - mc if k & 1 else mc
            peer = 4 * px + 2 * py + pc
            copies.append(pltpu.make_async_remote_copy(
                src_ref=x_refs[t].at[peer], dst_ref=out_refs[t].at[me], send_sem=send_sems.at[7 * t + k - 1],
                recv_sem=recv_sems.at[7 * t + k - 1], device_id=(px, py, pc), device_id_type=pl.DeviceIdType.MESH))
    return mine, copies


def all_to_all(xs, name):
    n = len(xs)

    def body(*refs):
        _exchange_start("a2a", refs[:n], refs[n:2 * n], refs[2 * n:])
        _exchange_finish("a2a", refs[:n], refs[n:2 * n], refs[2 * n:])

    return pl.pallas_call(
        body, name=name, in_specs=[_HBM] * n, out_specs=[_HBM] * n, out_shape=_exchange_out_shapes("a2a", xs),
        scratch_shapes=_exchange_scratch(n))(*xs)


def adamw(w, slabs, m, v, own, name):
    R, C = w.shape
    tr = _pick(R, tuple(t for t in (256, 128, 64, 32, 16, 8) if t * C * 4 <= ADAMW_TILE_BYTES) or (8,))
    c1 = 1.0 - ADAM_B1 ** ADAM_STEP
    c2 = 1.0 - ADAM_B2 ** ADAM_STEP

    def body(*refs):
        if own is not None:
            me_ref, w_ref, s_ref, m_ref, v_ref, own_ref, g_ref, d_ref, nm_ref, nv_ref = refs
        else:
            w_ref, s_ref, m_ref, v_ref, g_ref, d_ref, nm_ref, nv_ref = refs
        g = None
        for s in range(N_DEV):
            t = s_ref[s].astype(f32)
            if own is not None:
                t = jnp.where(me_ref[0] == s, own_ref[...], t)
            g = t if g is None else g + t
        wv = w_ref[...]
        mn = ADAM_B1 * m_ref[...] + (1.0 - ADAM_B1) * g
        vn = ADAM_B2 * v_ref[...] + (1.0 - ADAM_B2) * jnp.square(g)
        m_hat = mn / c1
        v_hat = vn / c2
        g_ref[...] = g
        d_ref[...] = -ADAM_LR * (m_hat / (jnp.sqrt(v_hat) + ADAM_EPS) + ADAM_WD * wv)
        nm_ref[...] = mn
        nv_ref[...] = vn

    out_shape = [jax.ShapeDtypeStruct((R, C), f32)] * 4
    if own is not None:
        me, part = own
        blk = pl.BlockSpec((tr, C), lambda i, me_: (i, 0))
        gs = pltpu.PrefetchScalarGridSpec(
            num_scalar_prefetch=1, grid=(R // tr,),
            in_specs=[blk, pl.BlockSpec((N_DEV, tr, C), lambda i, me_: (0, i, 0)), blk, blk, blk],
            out_specs=[blk] * 4)
        return pl.pallas_call(body, name=name, grid_spec=gs, out_shape=out_shape, compiler_params=_params(("parallel",)))(
            me.reshape(1).astype(jnp.int32), w, slabs, m, v, part)
    blk = pl.BlockSpec((tr, C), lambda i: (i, 0))
    return pl.pallas_call(
        body, name=name, grid=(R // tr,), in_specs=[blk, pl.BlockSpec((N_DEV, tr, C), lambda i: (0, i, 0)), blk, blk],
        out_specs=[blk] * 4, out_shape=out_shape, compiler_params=_params(("parallel",)))(w, slabs, m, v)


def _packed_rows(n, cols):
    rows = -(-n // cols)
    mult = 256 if rows > 256 else 16
    return -(-rows // mult) * mult


def _pack(arrs, cols, dtype, lead=()):
    nl = len(lead)
    flat = jnp.concatenate([a.reshape(lead + (-1,)).astype(dtype) for a in arrs], axis=-1)
    n = flat.shape[-1]
    rows = _packed_rows(n, cols)
    flat = jnp.pad(flat, [(0, 0)] * nl + [(0, rows * cols - n)])
    return flat.reshape(lead + (rows, cols))


def _unpack(packed, shapes, lead=()):
    flat = packed.reshape(lead + (-1,))
    out, off = [], 0
    for s in shapes:
        n = int(np.prod(s))
        out.append(flat[..., off:off + n].reshape(lead + tuple(s)))
        off += n
    return out


def _to_slabs(full, axis):
    s = full.shape
    r = full.reshape(s[:axis] + (N_DEV, s[axis] // N_DEV) + s[axis + 1:])
    return jnp.moveaxis(r, axis, 0)


def _from_slabs(slabs, axis):
    r = jnp.moveaxis(slabs, 0, axis)
    s = r.shape
    return r.reshape(s[:axis] + (s[axis] * s[axis + 1],) + s[axis + 2:])


BIG = [("ev_w_in", 2), ("ev_w_out", 1), ("od_w_in", 2), ("od_w_out", 1), ("ple_w_gate", 1), ("ple_w_proj", 2)]
SMALL = [("ev_conv_w", 2), ("od_norm_w", 1), ("od_dw_w", 2), ("od_dw_b", 1), ("od_ln_w", 1), ("od_ln_b", 1)]
REPL = ["ev_norm_w", "ev_conv_b", "ev_dt_bias_f", "ev_dt_bias_b", "ev_a_log_f", "ev_a_log_b", "ev_d_skip", "ev_gnorm_w", "ev_rpb",
        "ple_norm_w", "final_norm_w"]
WEIGHTS = ["ev_norm_w", "ev_w_in", "ev_conv_w", "ev_conv_b", "ev_dt_bias_f", "ev_dt_bias_b", "ev_a_log_f", "ev_a_log_b", "ev_d_skip",
           "ev_gnorm_w", "ev_rpb", "ev_w_out", "od_norm_w", "od_w_in", "od_dw_w", "od_dw_b", "od_ln_w", "od_ln_b", "od_w_out",
           "ple_norm_w", "ple_w_gate", "ple_w_proj", "final_norm_w"]
BIG_COLS = 1024
SMALL_COLS = 128


def _row_tile(L, width):
    return _pick(L, (256, 128)) if width <= 2048 else _pick(L, (128,))


def _ple_fwd(h1, p_i, nw, wg, wp, tag):
    L, D = h1.shape
    T = _row_tile(L, D)
    (hn2,) = rowop_fwd(_rms_fn, "ple_rms_" + tag, [(h1, D, 0)], [(nw, D, 0)], [(D, bf16)], T)
    gpre = mm(hn2, wg, "nn", "ple_gate_mm_" + tag)
    pp = mm(p_i, wp, "nn", "ple_proj_mm_" + tag)
    (h2,) = rowop_fwd(_ple_fwd_fn, "ple_comb_" + tag, [(h1, D, 0), (gpre, D, 0), (pp, D, 0)], [], [(D, f32)], T)
    return h2, (h1, p_i, nw, wg, wp, hn2, gpre, pp)


def _ple_bwd(res, dh2, tag):
    h1, p_i, nw, wg, wp, hn2, gpre, pp = res
    L, D = h1.shape
    T = _row_tile(L, D)
    dgpre, dpp = rowop_bwd(_ple_fn, "ple_comb_bwd_" + tag, [(gpre, D, 0), (pp, D, 0)], [], [(dh2, D, 0)], [bf16, bf16], [], T)
    dwg = mm(hn2, dgpre, "tn", "ple_gate_dw_" + tag)
    dwp = mm(p_i, dpp, "tn", "ple_proj_dw_" + tag)
    dhn2 = mm(dgpre, wg, "nt", "ple_gate_dx_" + tag)
    dh1, dnw = rowop_bwd(_rms_fn, "ple_rms_bwd_" + tag, [(h1, D, 0)], [(nw, D, 0)], [(dhn2, D, 0)], [f32], [True], T, add=(dh2, D, 0))
    return dh1, dnw, dwg, dwp


def _even_fwd(h, P, tag, side=None):
    L, D = h.shape
    H = D // HEAD_P
    T = _row_tile(L, D)
    DB = D // LANES
    (hn,) = rowop_fwd(_rms_fn, "ev_rms_" + tag, [(h, D, 0)], [(P["norm_w"], D, 0)], [(D, bf16)], T)
    um = mm(hn, P["w_main"], "nn", "ev_in_mm_" + tag, side=side)
    um, exchanged = um if side is not None else (um, None)
    udt = mm(hn, P["w_dt"], "nn", "ev_dt_mm_" + tag)
    pre, act = conv_fwd(um, DB, 2 * D, P["conv_w"], P["conv_b"], K_SSM, True, "ev_conv_" + tag)
    prep = _make_prep_fn(H)
    dt128, cs128 = rowop_fwd(prep, "ev_prep_" + tag, [(udt, LANES, 0)], [(P["bias128"], LANES, 0), (P["alog128"], LANES, 0)],
                             [(LANES, f32), (LANES, f32)], CHUNK)
    csr = _csr_from_cs(cs128, H)
    yf, sf = ssd_fwd(act, dt128, cs128, csr, 0, H, "ev_ssd_f_" + tag)
    yb, sb = ssd_fwd(act, dt128, cs128, csr, 1, H, "ev_ssd_b_" + tag)
    G = D // GROUP_W
    (yssd,) = rowop_fwd(_post_fn, "ev_post_" + tag, [(yf, GROUP_W, 0), (yb, GROUP_W, 0), (act, GROUP_W, 0), (um, GROUP_W, 0)],
                        [(P["dskipx"], GROUP_W, 0), (P["gnorm_w"], GROUP_W, 0)], [(GROUP_W, bf16)], _pick(L, POST_ROW_TILES), ncb=G)
    bt = na_bias_table(P["rpb"])
    o = na_fwd(um, 3 * DB, 4 * DB, 5 * DB, bt, "ev_na_" + tag)
    (yna,) = rowop_fwd(_gate_fn, "ev_nagate_" + tag, [(o, D, 0), (um, D, 6)], [], [(D, bf16)], T)
    cat = jnp.concatenate([yssd, yna], axis=-1)
    if P["w_out"] is None:
        P["w_out"] = _from_slabs(exchanged[0], 0)
    h1 = mm(cat, P["w_out"], "nn", "ev_out_mm_" + tag, add=h)
    return h1, (h, hn, um, udt, pre, act, dt128, cs128, csr, yf, yb, sf, sb, bt, o, cat), exchanged


def _even_bwd(P, res, dh1, tag, side=None, early=None):
    h, hn, um, udt, pre, act, dt128, cs128, csr, yf, yb, sf, sb, bt, o, cat = res
    L, D = h.shape
    H = D // HEAD_P
    G = D // GROUP_W
    T = _row_tile(L, D)
    DB = D // LANES
    g = {}
    dcat = mm(dh1, P["w_out"], "nt", "ev_out_dx_" + tag, out_dtype=bf16)
    g["w_out"] = mm(cat, dh1, "tn", "ev_out_dw_" + tag)
    do, dg = rowop_bwd(_gate_fn, "ev_nagate_bwd_" + tag, [(o, D, 0), (um, D, 6)], [], [(dcat, D, 1)], [f32, bf16], [], T)
    n_side = len(side[1]) - 1 if side is not None else 0
    na_arrays = list(side[1][1:]) if side is not None else []
    if early is not None:
        na_arrays += [_to_slabs(gr.astype(bf16), ax) for gr, ax in zip([g["w_out"]] + early[0], [0] + early[1])]
    dq, dk, dv, dbt, *na_recv = na_bwd(um, 3 * DB, 4 * DB, 5 * DB, bt, do, "ev_na_bwd_" + tag,
                                       side=("a2a", na_arrays) if na_arrays else None)
    na_recv = list(na_recv[0]) if na_recv else []
    _, rpb_vjp = jax.vjp(na_bias_table, P["rpb"])
    (g["rpb"],) = rpb_vjp(dbt)
    dy, dxs_a, dz, ddsk, dgn = rowop_bwd(
        _post_fn, "ev_post_bwd_" + tag, [(yf, GROUP_W, 0), (yb, GROUP_W, 0), (act, GROUP_W, 0), (um, GROUP_W, 0)],
        [(P["dskipx"], GROUP_W, 0), (P["gnorm_w"], GROUP_W, 0)], [(dcat, GROUP_W, 0)], [f32, None, f32, bf16], [True, True],
        _pick(L, POST_ROW_TILES), ncb=G)
    g["d_skip"] = ddsk.reshape(H, HEAD_P).sum(axis=-1)
    g["gnorm_w"] = dgn.reshape(D)
    dxf, dbf, dcf, ddtf, dcsf, dcsrf = ssd_bwd(act, dt128, cs128, csr, sf, dy, 0, H, "ev_ssd_f_bwd_" + tag)
    dxb, dbb, dcb, ddtb, dcsb, dcsrb = ssd_bwd(act, dt128, cs128, csr, sb, dy, 1, H, "ev_ssd_b_bwd_" + tag)
    dact = jnp.concatenate([dxf + dxb + dxs_a, dbf + dbb, dcf + dcb], axis=-1)
    ddt128 = ddtf + ddtb
    dcs128 = dcsf + dcsb + _cs_from_dcsr(dcsrf, dcsrb, H)
    prep = _make_prep_fn(H)
    dudt, dbias, dalog = rowop_bwd(prep, "ev_prep_bwd_" + tag, [(udt, LANES, 0)], [(P["bias128"], LANES, 0), (P["alog128"], LANES, 0)],
                                   [(ddt128, LANES, 0), (dcs128, LANES, 0)], [bf16], [True, True], CHUNK)
    g["dt_bias_f"], g["dt_bias_b"] = dbias[0, :H], dbias[0, H:2 * H]
    g["a_log_f"], g["a_log_b"] = dalog[0, :H], dalog[0, H:2 * H]
    dxbc, g["conv_w"], dcb_ = conv_bwd(dact, um, DB, 2 * D, P["conv_w"], K_SSM, pre, bf16, "ev_conv_bwd_" + tag)
    g["conv_b"] = dcb_.reshape(2 * D)
    dum = jnp.concatenate([dz, dxbc, dq.astype(bf16), dk.astype(bf16), dv.astype(bf16), dg], axis=-1)
    first_side = (side[0], side[1][:1]) if side is not None else None
    first, own_recv = [], None
    if early is None:
        dhn = mm(dum, P["w_main"], "nt", "ev_in_dx_" + tag, side=first_side)
        dhn, first = dhn if side is not None else (dhn, [])
        g["w_main"] = mm(hn, dum, "tn", "ev_in_dw_" + tag)
        g["w_dt"] = mm(hn, dudt, "tn", "ev_dt_dw_" + tag)
    else:
        gw = mm(hn, dum, "tn", "ev_in_dw_" + tag, side=first_side)
        g["w_main"], first = gw if side is not None else (gw, [])
        g["w_dt"] = mm(hn, dudt, "tn", "ev_dt_dw_" + tag)
        own_slabs = [_to_slabs(_ev_w_in_grad(g, D, H).astype(bf16), 1)]
        dhn, own_recv = mm(dum, P["w_main"], "nt", "ev_in_dx_" + tag, side=("a2a", own_slabs))
    dhn = mm(dudt, P["w_dt"], "nt", "ev_dt_dx_" + tag, add=dhn)
    dh, dnw = rowop_bwd(_rms_fn, "ev_rms_bwd_" + tag, [(h, D, 0)], [(P["norm_w"], D, 0)], [(dhn, D, 0)], [f32], [True], T, add=(dh1, D, 0))
    g["norm_w"] = dnw.reshape(D)
    exchanged = (list(first) + na_recv[:n_side]) if side is not None else None
    early_recv = na_recv[n_side:] if early is not None else None
    return dh, g, exchanged, early_recv, own_recv


def _ev_w_in_grad(g, D, H):
    return jnp.concatenate([g["w_main"][:, :3 * D], g["w_dt"][:, :2 * H], g["w_main"][:, 3 * D:]], axis=1)


def _odd_fwd(h, P, tag, side=None):
    L, D = h.shape
    C = 2 * D
    T = _row_tile(L, D)
    (hn,) = rowop_fwd(_rms_fn, "od_rms_" + tag, [(h, D, 0)], [(P["norm_w"], D, 0)], [(D, bf16)], T)
    u = mm(hn, P["w_in"], "nn", "od_in_mm_" + tag, side=side)
    u, exchanged = u if side is not None else (u, None)
    (vglu,) = rowop_fwd(_glu_fn, "od_glu_" + tag, [(u, D, 0), (u, D, 2)], [], [(D, f32)], T, ncb=2)
    (vc,) = conv_fwd(vglu, 0, C, P["dw_w"], P["dw_b"], K_CONV, False, "od_conv_" + tag)
    TL = _row_tile(L, C)
    (t,) = rowop_fwd(_lngate_fn, "od_lngate_" + tag, [(vc, C, 0), (u, C, 2)], [(P["ln_w"], C, 0), (P["ln_b"], C, 0)], [(C, bf16)], TL)
    h1 = mm(t, P["w_out"], "nn", "od_out_mm_" + tag, add=h)
    return h1, (h, hn, u, vglu, vc, t), exchanged


def _odd_bwd(P, res, dh1, tag, side=None):
    h, hn, u, vglu, vc, t = res
    L, D = h.shape
    C = 2 * D
    T = _row_tile(L, D)
    TL = _row_tile(L, C)
    g = {}
    dt_ = mm(dh1, P["w_out"], "nt", "od_out_dx_" + tag, out_dtype=bf16)
    g["w_out"] = mm(t, dh1, "tn", "od_out_dw_" + tag)
    dvc, dg, dlnw, dlnb = rowop_bwd(_lngate_fn, "od_lngate_bwd_" + tag, [(vc, C, 0), (u, C, 2)], [(P["ln_w"], C, 0), (P["ln_b"], C, 0)],
                                    [(dt_, C, 0)], [f32, bf16], [True, True], TL)
    g["ln_w"], g["ln_b"] = dlnw.reshape(C), dlnb.reshape(C)
    dvglu, g["dw_w"], ddwb = conv_bwd(dvc, vglu, 0, C, P["dw_w"], K_CONV, None, f32, "od_conv_bwd_" + tag)
    g["dw_b"] = ddwb.reshape(C)
    da, dag = rowop_bwd(_glu_fn, "od_glu_bwd_" + tag, [(u, D, 0), (u, D, 2)], [], [(dvglu, D, 0)], [bf16, bf16], [], T, ncb=2)
    du = jnp.concatenate([da, dag, dg], axis=-1)
    exchanged = None
    if side is None:
        dhn = mm(du, P["w_in"], "nt", "od_in_dx_" + tag)
        g["w_in"] = mm(hn, du, "tn", "od_in_dw_" + tag)
    else:
        dhn, ex_a = mm(du, P["w_in"], "nt", "od_in_dx_" + tag, side=(side[0], side[1][:1]))
        g["w_in"], ex_b = mm(hn, du, "tn", "od_in_dw_" + tag, side=(side[0], side[1][1:]))
        exchanged = list(ex_a) + list(ex_b)
    dh, dnw = rowop_bwd(_rms_fn, "od_rms_bwd_" + tag, [(h, D, 0)], [(P["norm_w"], D, 0)], [(dhn, D, 0)], [f32], [True], T, add=(dh1, D, 0))
    g["norm_w"] = dnw.reshape(D)
    return dh, g, exchanged


def kernel(x, p, ev_norm_w, ev_w_in, ev_conv_w, ev_conv_b, ev_dt_bias_f, ev_dt_bias_b, ev_a_log_f, ev_a_log_b, ev_d_skip, ev_gnorm_w, ev_rpb, ev_w_out, od_norm_w, od_w_in, od_dw_w, od_dw_b, od_ln_w, od_ln_b, od_w_out, ple_norm_w, ple_w_gate, ple_w_proj, final_norm_w, loss_target, m_ev_norm_w, m_ev_w_in, m_ev_conv_w, m_ev_conv_b, m_ev_dt_bias_f, m_ev_dt_bias_b, m_ev_a_log_f, m_ev_a_log_b, m_ev_d_skip, m_ev_gnorm_w, m_ev_rpb, m_ev_w_out, m_od_norm_w, m_od_w_in, m_od_dw_w, m_od_dw_b, m_od_ln_w, m_od_ln_b, m_od_w_out, m_ple_norm_w, m_ple_w_gate, m_ple_w_proj, m_final_norm_w, v_ev_norm_w, v_ev_w_in, v_ev_conv_w, v_ev_conv_b, v_ev_dt_bias_f, v_ev_dt_bias_b, v_ev_a_log_f, v_ev_a_log_b, v_ev_d_skip, v_ev_gnorm_w, v_ev_rpb, v_ev_w_out, v_od_norm_w, v_od_w_in, v_od_dw_w, v_od_dw_b, v_od_ln_w, v_od_ln_b, v_od_w_out, v_ple_norm_w, v_ple_w_gate, v_ple_w_proj, v_final_norm_w):
    W = dict(ev_norm_w=ev_norm_w, ev_w_in=ev_w_in, ev_conv_w=ev_conv_w, ev_conv_b=ev_conv_b, ev_dt_bias_f=ev_dt_bias_f,
             ev_dt_bias_b=ev_dt_bias_b, ev_a_log_f=ev_a_log_f, ev_a_log_b=ev_a_log_b, ev_d_skip=ev_d_skip, ev_gnorm_w=ev_gnorm_w,
             ev_rpb=ev_rpb, ev_w_out=ev_w_out, od_norm_w=od_norm_w, od_w_in=od_w_in, od_dw_w=od_dw_w, od_dw_b=od_dw_b,
             od_ln_w=od_ln_w, od_ln_b=od_ln_b, od_w_out=od_w_out, ple_norm_w=ple_norm_w, ple_w_gate=ple_w_gate,
             ple_w_proj=ple_w_proj, final_norm_w=final_norm_w)
    M = dict(ev_norm_w=m_ev_norm_w, ev_w_in=m_ev_w_in, ev_conv_w=m_ev_conv_w, ev_conv_b=m_ev_conv_b, ev_dt_bias_f=m_ev_dt_bias_f,
             ev_dt_bias_b=m_ev_dt_bias_b, ev_a_log_f=m_ev_a_log_f, ev_a_log_b=m_ev_a_log_b, ev_d_skip=m_ev_d_skip,
             ev_gnorm_w=m_ev_gnorm_w, ev_rpb=m_ev_rpb, ev_w_out=m_ev_w_out, od_norm_w=m_od_norm_w, od_w_in=m_od_w_in,
             od_dw_w=m_od_dw_w, od_dw_b=m_od_dw_b, od_ln_w=m_od_ln_w, od_ln_b=m_od_ln_b, od_w_out=m_od_w_out,
             ple_norm_w=m_ple_norm_w, ple_w_gate=m_ple_w_gate, ple_w_proj=m_ple_w_proj, final_norm_w=m_final_norm_w)
    V = dict(ev_norm_w=v_ev_norm_w, ev_w_in=v_ev_w_in, ev_conv_w=v_ev_conv_w, ev_conv_b=v_ev_conv_b, ev_dt_bias_f=v_ev_dt_bias_f,
             ev_dt_bias_b=v_ev_dt_bias_b, ev_a_log_f=v_ev_a_log_f, ev_a_log_b=v_ev_a_log_b, ev_d_skip=v_ev_d_skip,
             ev_gnorm_w=v_ev_gnorm_w, ev_rpb=v_ev_rpb, ev_w_out=v_ev_w_out, od_norm_w=v_od_norm_w, od_w_in=v_od_w_in,
             od_dw_w=v_od_dw_w, od_dw_b=v_od_dw_b, od_ln_w=v_od_ln_w, od_ln_b=v_od_ln_b, od_w_out=v_od_w_out,
             ple_norm_w=v_ple_norm_w, ple_w_gate=v_ple_w_gate, ple_w_proj=v_ple_w_proj, final_norm_w=v_final_norm_w)

    h0 = x[0]
    L, D = h0.shape
    H = D // HEAD_P
    depth = p.shape[0]
    n_even, n_odd = (depth + 1) // 2, depth // 2
    me = 4 * lax.axis_index("x") + 2 * lax.axis_index("y") + lax.axis_index("c")

    small_shapes = [W[n].shape for n, _ in SMALL]
    (small_g,) = all_gather([_pack([W[n] for n, _ in SMALL], SMALL_COLS, f32)], "gather_small")
    F = {n: _from_slabs(a, ax) for (n, ax), a in zip(SMALL, _unpack(small_g, small_shapes, (N_DEV,)))}

    def layer_big(i):
        kind = "ev" if i % 2 == 0 else "od"
        return [(kind + "_w_in", i // 2, 1), (kind + "_w_out", i // 2, 0), ("ple_w_gate", i, 0), ("ple_w_proj", i, 1)]

    def layer_shards(i):
        return [W[n][k].astype(bf16) for n, k, _ in layer_big(i)]

    def even_params(e, w_in, w_out):
        return dict(
            norm_w=W["ev_norm_w"][e][None], w_main=jnp.concatenate([w_in[:, :3 * D], w_in[:, 3 * D + 2 * H:]], axis=1),
            w_dt=jnp.pad(w_in[:, 3 * D:3 * D + 2 * H], ((0, 0), (0, LANES - 2 * H))),
            conv_w=F["ev_conv_w"][e], conv_b=W["ev_conv_b"][e][None],
            bias128=jnp.pad(jnp.concatenate([W["ev_dt_bias_f"][e], W["ev_dt_bias_b"][e]]), (0, LANES - 2 * H))[None],
            alog128=jnp.pad(jnp.concatenate([W["ev_a_log_f"][e], W["ev_a_log_b"][e]]), (0, LANES - 2 * H))[None],
            dskipx=jnp.repeat(W["ev_d_skip"][e], HEAD_P)[None], gnorm_w=W["ev_gnorm_w"][e][None], rpb=W["ev_rpb"][e],
            w_out=w_out)

    def odd_params(e, w_in, w_out):
        return dict(norm_w=F["od_norm_w"][e][None], w_in=w_in, dw_w=F["od_dw_w"][e], dw_b=F["od_dw_b"][e][None],
                    ln_w=F["od_ln_w"][e][None], ln_b=F["od_ln_b"][e][None], w_out=w_out)

    h = h0
    saved = []
    gathered = all_gather(layer_shards(0)[:1], "gather_layer0")
    for i in range(depth):
        e = i // 2
        tag = str(i)
        nxt = layer_shards(i + 1) if i + 1 < depth else []
        if i == 0:
            w_in = _from_slabs(gathered[0], layer_big(0)[0][2])
            P = even_params(e, w_in, None)
            h1, res, exchanged = _even_fwd(h, P, tag, ("gather", layer_shards(0)[1:] + nxt))
            w_gate, w_proj = [_from_slabs(a, ax) for a, (_, _, ax) in zip(exchanged[1:3], layer_big(0)[2:])]
            gathered = exchanged[3:]
        else:
            w_in, w_out, w_gate, w_proj = [_from_slabs(a, ax) for a, (_, _, ax) in zip(gathered, layer_big(i))]
            side = ("gather", nxt) if nxt else None
            if i % 2 == 0:
                P = even_params(e, w_in, w_out)
                h1, res, gathered = _even_fwd(h, P, tag, side)
            else:
                P = odd_params(e, w_in, w_out)
                h1, res, gathered = _odd_fwd(h, P, tag, side)
        h, pres = _ple_fwd(h1, p[i, 0], W["ple_norm_w"][i][None], w_gate, w_proj, tag)
        saved.append((P, res, pres))
    loss_tile, dh, dfinal = final_loss(h, W["final_norm_w"][None], loss_target[0])

    ev_g = [None] * n_even
    od_g = [None] * n_odd
    ple_g = [None] * depth
    recv = [None] * depth
    own = [None] * depth
    pending = None
    for i in reversed(range(depth)):
        P, res, pres = saved[i]
        tag = str(i)
        dh1, dnw, dwg, dwp = _ple_bwd(pres, dh, tag)
        ple_g[i] = dnw.reshape(D)
        side = ("a2a", pending) if pending is not None else None
        if i % 2 == 0:
            early = ([dwg, dwp], [ax for _, _, ax in layer_big(0)[2:]]) if i == 0 else None
            dh, g, exchanged, early_recv, own_recv = _even_bwd(P, res, dh1, tag, side, early)
            ev_g[i // 2] = g
            big_grads = [_ev_w_in_grad(g, D, H), g["w_out"], dwg, dwp]
        else:
            dh, g, exchanged = _odd_bwd(P, res, dh1, tag, side)
            od_g[i // 2] = g
            big_grads = [g["w_in"], g["w_out"], dwg, dwp]
        if pending is not None:
            recv[i + 1] = exchanged
        pending = [_to_slabs(gr.astype(bf16), ax) for gr, (_, _, ax) in zip(big_grads, layer_big(i))]
        own[i] = [lax.dynamic_slice_in_dim(gr, me * (gr.shape[ax] // N_DEV), gr.shape[ax] // N_DEV, axis=ax)
                  for gr, (_, _, ax) in zip(big_grads, layer_big(i))]
    recv[0] = list(own_recv) + list(early_recv)
    grad_x = dh[None]

    full = dict(
        ev_conv_w=jnp.stack([g["conv_w"] for g in ev_g]), od_norm_w=jnp.stack([g["norm_w"] for g in od_g]),
        od_dw_w=jnp.stack([g["dw_w"] for g in od_g]), od_dw_b=jnp.stack([g["dw_b"] for g in od_g]),
        od_ln_w=jnp.stack([g["ln_w"] for g in od_g]), od_ln_b=jnp.stack([g["ln_b"] for g in od_g]),
        ev_norm_w=jnp.stack([g["norm_w"] for g in ev_g]), ev_conv_b=jnp.stack([g["conv_b"] for g in ev_g]),
        ev_dt_bias_f=jnp.stack([g["dt_bias_f"] for g in ev_g]), ev_dt_bias_b=jnp.stack([g["dt_bias_b"] for g in ev_g]),
        ev_a_log_f=jnp.stack([g["a_log_f"] for g in ev_g]), ev_a_log_b=jnp.stack([g["a_log_b"] for g in ev_g]),
        ev_d_skip=jnp.stack([g["d_skip"] for g in ev_g]), ev_gnorm_w=jnp.stack([g["gnorm_w"] for g in ev_g]),
        ev_rpb=jnp.stack([g["rpb"] for g in ev_g]), ple_norm_w=jnp.stack(ple_g), final_norm_w=dfinal.reshape(D))

    out = {}
    where = {}
    for i in range(depth):
        for slot, (n, k, _) in enumerate(layer_big(i)):
            where.setdefault(n, {})[k] = (i, slot)
    for n, _ in BIG:
        shard = W[n].shape
        two_d = (int(np.prod(shard[:-1])), shard[-1])
        layers = [where[n][k] for k in range(shard[0])]
        slabs = jnp.stack([recv[i][slot] for i, slot in layers], axis=1)
        part = jnp.stack([own[i][slot] for i, slot in layers], axis=0)
        res = adamw(W[n].reshape(two_d), slabs.reshape((N_DEV,) + two_d), M[n].reshape(two_d), V[n].reshape(two_d),
                    (me, part.reshape(two_d)), "adamw_" + n)
        for kind, arr in zip(("grad", "delta", "new_m", "new_v"), res):
            out[kind + "_" + n] = arr.reshape(shard)
    small_full = _pack([_to_slabs(full[n], ax) for n, ax in SMALL], SMALL_COLS, f32, (N_DEV,))
    (small_recv,) = all_to_all([small_full], "scatter_small")
    res_small = adamw(_pack([W[n] for n, _ in SMALL], SMALL_COLS, f32), small_recv, _pack([M[n] for n, _ in SMALL], SMALL_COLS, f32),
                      _pack([V[n] for n, _ in SMALL], SMALL_COLS, f32), None, "adamw_small")
    for kind, arr in zip(("grad", "delta", "new_m", "new_v"), res_small):
        for (n, _), a in zip(SMALL, _unpack(arr, small_shapes)):
            out[kind + "_" + n] = a
    repl_shapes = [W[n].shape for n in REPL] + [(1,)]
    zero1 = jnp.zeros((1,), f32)
    repl_part = _pack([full[n] for n in REPL] + [loss_tile[0, :1]], SMALL_COLS, f32)
    (repl_all,) = all_gather([repl_part], "gather_repl")
    res_repl = adamw(_pack([W[n] for n in REPL] + [zero1], SMALL_COLS, f32), repl_all, _pack([M[n] for n in REPL] + [zero1], SMALL_COLS, f32),
                     _pack([V[n] for n in REPL] + [zero1], SMALL_COLS, f32), None, "adamw_repl")
    for kind, arr in zip(("grad", "delta", "new_m", "new_v"), res_repl):
        parts = _unpack(arr, repl_shapes)
        for n, a in zip(REPL, parts[:-1]):
            out[kind + "_" + n] = a
        if kind == "grad":
            loss = parts[-1].reshape(())

    return (loss, grad_x, *[out["grad_" + n] for n in WEIGHTS], *[out["delta_" + n] for n in WEIGHTS],
            *[out["new_m_" + n] for n in WEIGHTS], *[out["new_v_" + n] for n in WEIGHTS])
```
